```python
import math
import jax, jax.numpy as jnp
from jax import lax
import numpy as np

D_MODEL = 1024
BATCH = 4
SEQ = 4096
DEPTH = 4

MIX_WIDTH = D_MODEL
BRANCH = MIX_WIDTH // 4
CONF_KERNEL = 31
S5_GROUP = 16
S5_GROUPS = BRANCH // S5_GROUP
S5_STATE = 64
SC_KERNEL = 3
DN_HEADS = 4
DN_HEAD_DIM = BRANCH // DN_HEADS
DN_CONV = 4
DN_CHUNK = 64
NORM_EPS = 1e-6
IN_WIDTHS = (BRANCH, BRANCH, BRANCH,
             BRANCH, BRANCH,
             BRANCH, BRANCH, BRANCH, BRANCH,
             BRANCH, BRANCH, BRANCH, DN_HEADS, DN_HEADS, BRANCH)
IN_COLS = 13 * BRANCH + 2 * DN_HEADS

kernel_name = "hybrid_parallel_conv_s5_shortconv_deltanet"


def rms_norm(x, g):
    x32 = x.astype(jnp.float32)
    y = x32 * lax.rsqrt(jnp.mean(x32 * x32, axis=-1, keepdims=True) + NORM_EPS)
    return (y * g.astype(jnp.float32)).astype(x.dtype)


def layer_norm(x, g, b):
    x32 = x.astype(jnp.float32)
    mu = jnp.mean(x32, axis=-1, keepdims=True)
    xc = x32 - mu
    y = xc * lax.rsqrt(jnp.mean(xc * xc, axis=-1, keepdims=True) + NORM_EPS)
    return (y * g.astype(jnp.float32) + b.astype(jnp.float32)).astype(x.dtype)


def l2_normalize(x):
    return x * lax.rsqrt(jnp.sum(x * x, axis=-1, keepdims=True) + NORM_EPS)


def causal_depthwise_conv(x, w):
    K, C = w.shape
    return lax.conv_general_dilated(
        x, w[:, None, :], window_strides=(1,), padding=[(K - 1, 0)],
        dimension_numbers=("NWC", "WIO", "NWC"), feature_group_count=C)


def split_columns(p):
    outs, start = [], 0
    for w in IN_WIDTHS:
        outs.append(p[..., start:start + w])
        start += w
    return outs


def conformer_conv_branch(val, gate, conv_w, conv_b, ln_g, ln_b, pw_w, pw_b):
    a = val * jax.nn.sigmoid(gate)
    a = causal_depthwise_conv(a, conv_w) + conv_b
    a = layer_norm(a, ln_g, ln_b)
    a = jax.nn.silu(a)
    return a @ pw_w + pw_b


def s5_branch(u, lam_re, lam_im, b_re, b_im, c_re, c_im, d_skip, log_dt, glu_w, glu_b):
    bsz, L, _ = u.shape
    f32 = jnp.float32
    u32 = u.astype(f32)
    lam = lax.complex(jnp.minimum(lam_re.astype(f32), -1e-4), lam_im.astype(f32))
    dt = jnp.exp(log_dt.astype(f32))[:, None]
    lam_bar = jnp.exp(lam * dt)
    b = lax.complex(b_re.astype(f32), b_im.astype(f32))
    b_bar = ((lam_bar - 1.0) / lam)[..., None] * b
    ug = u32.reshape(bsz, L, S5_GROUPS, S5_GROUP).astype(jnp.complex64)
    bu = jnp.einsum("blgh,gph->blgp", ug, b_bar)
    a = jnp.broadcast_to(lam_bar, bu.shape)

    def combine(e1, e2):
        a1, s1 = e1
        a2, s2 = e2
        return a1 * a2, a2 * s1 + s2

    _, states = lax.associative_scan(combine, (a, bu), axis=1)
    c = lax.complex(c_re.astype(f32), c_im.astype(f32))
    y = jnp.real(jnp.einsum("blgp,ghp->blgh", states, c)).reshape(bsz, L, BRANCH)
    y = y + d_skip.astype(f32) * u32
    y = jax.nn.gelu(y).astype(u.dtype)
    return y * jax.nn.sigmoid(y @ glu_w + glu_b)


def short_conv_branch(bg, cg, xc, conv_w):
    return bg * causal_depthwise_conv(cg * xc, conv_w)


def gated_delta_rule_chunked(q, k, v, beta, g):
    bsz, L, H, dk = q.shape
    dv = v.shape[-1]
    C = DN_CHUNK
    N = L // C

    def chunk(t):
        t = t.reshape(bsz, N, C, H, t.shape[-1])
        return jnp.transpose(t, (1, 0, 3, 2, 4))

    q, k, v = chunk(q), chunk(k), chunk(v)
    beta = chunk(beta[..., None])[..., 0]
    g = chunk(g[..., None])[..., 0]
    gc = jnp.cumsum(g, axis=-1)
    idx = jnp.arange(C)
    causal = idx[:, None] >= idx[None, :]
    strict = idx[:, None] > idx[None, :]
    decay = jnp.exp(jnp.where(causal, gc[..., :, None] - gc[..., None, :], -jnp.inf))
    k_beta = k * beta[..., None]
    lmat = jnp.where(strict, jnp.einsum("nbhid,nbhjd->nbhij", k_beta, k) * decay, 0.0)
    rhs = jnp.concatenate([v * beta[..., None], k_beta * jnp.exp(gc)[..., None]], axis=-1)
    sol = lax.linalg.triangular_solve(lmat + jnp.eye(C, dtype=lmat.dtype), rhs,
                                      left_side=True, lower=True)
    u, w = sol[..., :dv], sol[..., dv:]
    attn = jnp.einsum("nbhid,nbhjd->nbhij", q, k) * decay
    q_dec = q * jnp.exp(gc)[..., None]
    g_last = gc[..., -1]
    k_dec = k * jnp.exp(g_last[..., None] - gc)[..., None]

    def step(S, inp):
        attn_c, u_c, w_c, qd, kd, gl = inp
        v_new = u_c - jnp.einsum("bhik,bhkv->bhiv", w_c, S)
        o = jnp.einsum("bhik,bhkv->bhiv", qd, S) + jnp.einsum("bhij,bhjv->bhiv", attn_c, v_new)
        S = S * jnp.exp(gl)[..., None, None] + jnp.einsum("bhik,bhiv->bhkv", kd, v_new)
        return S, o

    S0 = jnp.zeros((bsz, H, dk, dv), jnp.float32)
    _, o = lax.scan(step, S0, (attn, u, w, q_dec, k_dec, g_last))
    return jnp.transpose(o, (1, 0, 3, 2, 4)).reshape(bsz, L, H, dv)


def deltanet_branch(q, k, v, alpha, beta_logit, conv_w, a_log, dt_bias, norm_g):
    bsz, L, _ = q.shape
    f32 = jnp.float32
    qkv = jax.nn.silu(causal_depthwise_conv(jnp.concatenate([q, k, v], axis=-1), conv_w))
    q, k, v = qkv[..., :BRANCH], qkv[..., BRANCH:2 * BRANCH], qkv[..., 2 * BRANCH:]
    q = q.astype(f32).reshape(bsz, L, DN_HEADS, DN_HEAD_DIM)
    k = k.astype(f32).reshape(bsz, L, DN_HEADS, DN_HEAD_DIM)
    v = v.astype(f32).reshape(bsz, L, DN_HEADS, DN_HEAD_DIM)
    q = l2_normalize(q) * (DN_HEAD_DIM ** -0.5)
    k = l2_normalize(k)
    beta = jax.nn.sigmoid(beta_logit.astype(f32))
    g = -jnp.exp(a_log.astype(f32)) * jax.nn.softplus(alpha.astype(f32) + dt_bias.astype(f32))
    o = gated_delta_rule_chunked(q, k, v, beta, g)
    o = rms_norm(o, norm_g)
    return o.reshape(bsz, L, BRANCH)


def setup_inputs(seed: int = 0) -> dict:
    key = jax.random.key(seed)
    ks = jax.random.split(key, 32)
    f32 = jnp.float32

    def nrm(k, shape, s):
        return jax.random.normal(k, shape, f32) * s

    G, P, H = S5_GROUPS, S5_STATE, S5_GROUP
    x = nrm(ks[0], (BATCH, SEQ, D_MODEL), 1.0)
    norm_g = 1.0 + nrm(ks[1], (DEPTH, D_MODEL), 0.02)
    w_in = nrm(ks[2], (DEPTH, D_MODEL, IN_COLS), D_MODEL ** -0.5)
    a_conv_w = nrm(ks[3], (DEPTH, CONF_KERNEL, BRANCH), CONF_KERNEL ** -0.5)
    a_conv_b = nrm(ks[4], (DEPTH, BRANCH), 0.02)
    a_ln_g = 1.0 + nrm(ks[5], (DEPTH, BRANCH), 0.02)
    a_ln_b = nrm(ks[6], (DEPTH, BRANCH), 0.02)
    a_pw_w = nrm(ks[7], (DEPTH, BRANCH, BRANCH), BRANCH ** -0.5)
    a_pw_b = nrm(ks[8], (DEPTH, BRANCH), 0.02)
    n_idx = jnp.arange(P, dtype=f32)
    s5_lambda_re = -0.5 + nrm(ks[9], (DEPTH, G, P), 0.01)
    s5_lambda_im = math.pi * n_idx + nrm(ks[10], (DEPTH, G, P), 0.01)
    s5_b_re = nrm(ks[11], (DEPTH, G, P, H), (2.0 * H) ** -0.5)
    s5_b_im = nrm(ks[12], (DEPTH, G, P, H), (2.0 * H) ** -0.5)
    s5_c_re = nrm(ks[13], (DEPTH, G, H, P), (2.0 * P) ** -0.5)
    s5_c_im = nrm(ks[14], (DEPTH, G, H, P), (2.0 * P) ** -0.5)
    s5_d = nrm(ks[15], (DEPTH, BRANCH), 0.5)
    s5_log_dt = jax.random.uniform(ks[16], (DEPTH, G), f32, math.log(1e-3), math.log(1e-1))
    s5_glu_w = nrm(ks[17], (DEPTH, BRANCH, BRANCH), BRANCH ** -0.5)
    s5_glu_b = nrm(ks[18], (DEPTH, BRANCH), 0.02)
    c_conv_w = nrm(ks[19], (DEPTH, SC_KERNEL, BRANCH), SC_KERNEL ** -0.5)
    d_conv_w = nrm(ks[20], (DEPTH, DN_CONV, 3 * BRANCH), DN_CONV ** -0.5)
    d_a_log = jnp.log(jax.random.uniform(ks[21], (DEPTH, DN_HEADS), f32, 1.0, 16.0))
    dt0 = jnp.exp(jax.random.uniform(ks[22], (DEPTH, DN_HEADS), f32, math.log(1e-3), math.log(1e-1)))
    d_dt_bias = dt0 + jnp.log(-jnp.expm1(-dt0))
    d_norm_g = 1.0 + nrm(ks[23], (DEPTH, DN_HEAD_DIM), 0.02)
    w_out = nrm(ks[24], (DEPTH, MIX_WIDTH, D_MODEL), MIX_WIDTH ** -0.5)
    final_g = 1.0 + nrm(ks[25], (D_MODEL,), 0.02)
    return {"x": x, "norm_g": norm_g, "w_in": w_in,
            "a_conv_w": a_conv_w, "a_conv_b": a_conv_b, "a_ln_g": a_ln_g, "a_ln_b": a_ln_b,
            "a_pw_w": a_pw_w, "a_pw_b": a_pw_b,
            "s5_lambda_re": s5_lambda_re, "s5_lambda_im": s5_lambda_im,
            "s5_b_re": s5_b_re, "s5_b_im": s5_b_im, "s5_c_re": s5_c_re, "s5_c_im": s5_c_im,
            "s5_d": s5_d, "s5_log_dt": s5_log_dt, "s5_glu_w": s5_glu_w, "s5_glu_b": s5_glu_b,
            "c_conv_w": c_conv_w,
            "d_conv_w": d_conv_w, "d_a_log": d_a_log, "d_dt_bias": d_dt_bias, "d_norm_g": d_norm_g,
            "w_out": w_out, "final_g": final_g}


def reference(x, norm_g, w_in, a_conv_w, a_conv_b, a_ln_g, a_ln_b, a_pw_w, a_pw_b,
              s5_lambda_re, s5_lambda_im, s5_b_re, s5_b_im, s5_c_re, s5_c_im,
              s5_d, s5_log_dt, s5_glu_w, s5_glu_b, c_conv_w,
              d_conv_w, d_a_log, d_dt_bias, d_norm_g, w_out, final_g):
    for l in range(DEPTH):
        h = rms_norm(x, norm_g[l])
        proj = h @ w_in[l]
        (a_val, a_gate, a_z, b_u, b_z, c_b, c_c, c_x, c_z,
         d_q, d_k, d_v, d_alpha, d_beta, d_z) = split_columns(proj)
        ya = conformer_conv_branch(a_val, a_gate, a_conv_w[l], a_conv_b[l], a_ln_g[l],
                                   a_ln_b[l], a_pw_w[l], a_pw_b[l]) * jax.nn.silu(a_z)
        yb = s5_branch(b_u, s5_lambda_re[l], s5_lambda_im[l], s5_b_re[l], s5_b_im[l],
                       s5_c_re[l], s5_c_im[l], s5_d[l], s5_log_dt[l],
                       s5_glu_w[l], s5_glu_b[l]) * jax.nn.silu(b_z)
        yc = short_conv_branch(c_b, c_c, c_x, c_conv_w[l]) * jax.nn.silu(c_z)
        yd = deltanet_branch(d_q, d_k, d_v, d_alpha, d_beta, d_conv_w[l], d_a_log[l],
                             d_dt_bias[l], d_norm_g[l]).astype(x.dtype) * jax.nn.silu(d_z)
        mixed = jnp.concatenate([ya, yb.astype(x.dtype), yc, yd], axis=-1)
        x = x + mixed @ w_out[l]
    return rms_norm(x, final_g)
```

```python
import functools

import jax
import jax.numpy as jnp
from jax import lax
from jax.experimental import pallas as pl
from jax.experimental.pallas import tpu as pltpu

F32 = jnp.float32
BF16 = jnp.bfloat16
HIGHEST = lax.Precision.HIGHEST

NORM_EPS = 1e-6
BRANCH = 256
CONF_KERNEL = 31
SC_KERNEL = 3
DN_HEADS = 4
DN_HEAD_DIM = 64
DN_CONV = 4
DN_CHUNK = 64
S5_GROUPS = 16
S5_GROUP = 16
S5_STATE = 64
S5_Q = 8
S5_ROW = S5_Q * BRANCH
S5_NSTATE = S5_GROUPS * S5_STATE

COL_C, COL_B, COL_A, COL_QKV, COL_AB, COL_DZ = 0, 4, 6, 9, 12, 14
PROJ_COLS = 15 * BRANCH
PROJ_TILE = 768

VMEM_LIMIT_BYTES = 56 * 1024 * 1024


def _mm(a, b):
    return jnp.dot(a.astype(BF16), b.astype(BF16), preferred_element_type=F32)


def _mm_nt(a, b):
    return lax.dot_general(a.astype(BF16), b.astype(BF16), (((1,), (1,)), ((), ())),
                           preferred_element_type=F32)


def _mm_tn(a, b):
    return lax.dot_general(a.astype(BF16), b.astype(BF16), (((0,), (0,)), ((), ())),
                           preferred_element_type=F32)


def _mmh(a, b):
    return jnp.dot(a, b, precision=HIGHEST, preferred_element_type=F32)


def _mmh_nt(a, b):
    return lax.dot_general(a, b, (((1,), (1,)), ((), ())), precision=HIGHEST,
                           preferred_element_type=F32)


def _sigmoid(x):
    return 1.0 / (1.0 + jnp.exp(-x))


def _silu(x):
    return x * _sigmoid(x)


def _inproj_kernel(x_ref, g_ref, w_ref, proj_ref, u_ref):
    x = x_ref[...]
    h = x * lax.rsqrt(jnp.mean(x * x, axis=-1, keepdims=True) + NORM_EPS) * g_ref[...]
    hb = h.astype(BF16)
    for c in range(PROJ_COLS // PROJ_TILE):
        cols = slice(c * PROJ_TILE, (c + 1) * PROJ_TILE)
        proj_ref[:, cols] = jnp.dot(hb, w_ref[:, cols], preferred_element_type=F32)
    u_ref[...] = proj_ref[:, COL_B * BRANCH:(COL_B + 1) * BRANCH]


def _inproj(x2d, g, w, tm):
    t, d = x2d.shape
    return pl.pallas_call(
        _inproj_kernel,
        out_shape=(jax.ShapeDtypeStruct((t, PROJ_COLS), F32),
                   jax.ShapeDtypeStruct((t, BRANCH), F32)),
        grid=(t // tm,),
        in_specs=[pl.BlockSpec((tm, d), lambda i: (i, 0)),
                  pl.BlockSpec((1, d), lambda i: (0, 0)),
                  pl.BlockSpec((d, PROJ_COLS), lambda i: (0, 0))],
        out_specs=(pl.BlockSpec((tm, PROJ_COLS), lambda i: (i, 0)),
                   pl.BlockSpec((tm, BRANCH), lambda i: (i, 0))),
        compiler_params=pltpu.CompilerParams(
            dimension_semantics=("parallel",), vmem_limit_bytes=VMEM_LIMIT_BYTES),
        name="inproj",
    )(x2d, g, w)


def _s5_kernel(u_ref, w_ref, t_ref, v_ref, a_ref, y_ref, carry_scr, z_scr, xp_scr):
    n = S5_NSTATE
    rows = u_ref.shape[0]

    @pl.when(pl.program_id(1) == 0)
    def _():
        carry_scr[...] = jnp.zeros_like(carry_scr)

    ub = u_ref[...].astype(BF16)
    z_scr[...] = jnp.dot(ub, w_ref[...], preferred_element_type=F32)
    ar = a_ref[0:1, :]
    ai = a_ref[1:2, :]

    def body(c, carry):
        xr, xi = carry
        xp_scr[pl.ds(c, 1), 0:n] = xr
        xp_scr[pl.ds(c, 1), n:2 * n] = xi
        zr = z_scr[pl.ds(c, 1), 0:n]
        zi = z_scr[pl.ds(c, 1), n:2 * n]
        return ar * xr - ai * xi + zr, ar * xi + ai * xr + zi

    xr, xi = lax.fori_loop(0, rows, body, (carry_scr[0:1, 0:n], carry_scr[0:1, n:2 * n]),
                           unroll=4)
    carry_scr[0:1, 0:n] = xr
    carry_scr[0:1, n:2 * n] = xi
    y_ref[...] = (jnp.dot(ub, t_ref[...], preferred_element_type=F32)
                  + jnp.dot(xp_scr[...].astype(BF16), v_ref[...], preferred_element_type=F32))


def _s5(u_flat, w, t, v, a, rows):
    b, nrow, width = u_flat.shape
    const = lambda shape: pl.BlockSpec(shape, lambda bi, j: (0, 0),
                                       pipeline_mode=pl.Buffered(1))
    return pl.pallas_call(
        _s5_kernel,
        out_shape=jax.ShapeDtypeStruct((b, nrow, width), F32),
        grid=(b, nrow // rows),
        in_specs=[pl.BlockSpec((None, rows, width), lambda bi, j: (bi, j, 0)),
                  const((width, 2 * S5_NSTATE)),
                  const((width, width)),
                  const((2 * S5_NSTATE, width)),
                  const((8, S5_NSTATE))],
        out_specs=pl.BlockSpec((None, rows, width), lambda bi, j: (bi, j, 0)),
        scratch_shapes=[pltpu.VMEM((8, 2 * S5_NSTATE), F32),
                        pltpu.VMEM((rows, 2 * S5_NSTATE), F32),
                        pltpu.VMEM((rows, 2 * S5_NSTATE), F32)],
        compiler_params=pltpu.CompilerParams(
            dimension_semantics=("parallel", "arbitrary"), vmem_limit_bytes=VMEM_LIMIT_BYTES),
        name="s5",
    )(u_flat, w, t, v, a)


def _tile_heads(x):
    return jnp.concatenate([x] * DN_HEADS, axis=0)


def _dnet_kernel(qkv_ref, halo_ref, ab_ref, cw_ref, alog_ref, dtb_ref, ng_ref, o_ref,
                 s_scr, pad_scr, q_scr, k_scr, v_scr, beta_scr, g_scr, o_scr):
    tc = qkv_ref.shape[0]
    c64 = DN_CHUNK
    w = DN_HEADS * DN_HEAD_DIM
    first = pl.program_id(1) == 0

    @pl.when(first)
    def _():
        s_scr[...] = jnp.zeros_like(s_scr)

    pad_scr[0:8, :] = jnp.where(first, 0.0, halo_ref[...])
    pad_scr[8:8 + tc, :] = qkv_ref[...]
    acc = cw_ref[0:1, :] * pad_scr[pl.ds(8 - (DN_CONV - 1), tc), :]
    for kk in range(1, DN_CONV):
        acc = acc + cw_ref[kk:kk + 1, :] * pad_scr[pl.ds(8 - (DN_CONV - 1) + kk, tc), :]
    qkv = _silu(acc)

    row_h = lax.broadcasted_iota(jnp.int32, (w, w), 0) // DN_HEAD_DIM
    col_h = lax.broadcasted_iota(jnp.int32, (w, w), 1) // DN_HEAD_DIM
    bd = row_h == col_h
    ones_bd = jnp.where(bd, 1.0, 0.0).astype(F32)

    q = qkv[:, 0:w]
    k = qkv[:, w:2 * w]
    q_scr[...] = q * lax.rsqrt(_mmh(q * q, ones_bd) + NORM_EPS) * (DN_HEAD_DIM ** -0.5)
    k_scr[...] = k * lax.rsqrt(_mmh(k * k, ones_bd) + NORM_EPS)
    v_scr[...] = qkv[:, 2 * w:3 * w]
    beta_scr[...] = _sigmoid(ab_ref[:, w:2 * w])
    sp_in = ab_ref[:, 0:w] + dtb_ref[...]
    softplus = jnp.maximum(sp_in, 0.0) + jnp.log1p(jnp.exp(-jnp.abs(sp_in)))
    g_scr[...] = -jnp.exp(alog_ref[...]) * softplus

    ri = lax.broadcasted_iota(jnp.int32, (c64, w), 0)
    cj = lax.broadcasted_iota(jnp.int32, (c64, w), 1) % DN_HEAD_DIM
    causal = ri >= cj
    strict = ri > cj
    eye_all = jnp.where(ri == cj, 1.0, 0.0).astype(F32)
    pick0 = jnp.where(cj == 0, 1.0, 0.0).astype(F32)
    lt = jnp.where(lax.broadcasted_iota(jnp.int32, (c64, c64), 0)
                   >= lax.broadcasted_iota(jnp.int32, (c64, c64), 1), 1.0, 0.0).astype(F32)

    def blockdiag(x):
        return jnp.where(bd, _tile_heads(x), 0.0)

    def chunk(c, carry):
        rows = pl.ds(pl.multiple_of(c * c64, c64), c64)
        qc = q_scr[rows, :]
        kc = k_scr[rows, :]
        vc = v_scr[rows, :]
        bc = beta_scr[rows, :]
        gc = _mmh(lt, g_scr[rows, :])
        egc = jnp.exp(gc)
        gl = gc[c64 - 1:c64, :]
        kb = kc * bc
        gcol = _mmh_nt(pick0, blockdiag(gc))
        decay = jnp.exp(jnp.where(causal, gc - gcol, -jnp.inf))
        k_bd = blockdiag(kc)
        lmat = jnp.where(strict, _mm_nt(kb, k_bd) * decay, 0.0)
        attn = _mm_nt(qc, k_bd) * decay
        x = lmat
        p = eye_all - lmat
        for _ in range(5):
            x = _mmh(x, blockdiag(x))
            p = p + _mmh(p, blockdiag(x))
        u = _mmh(p, blockdiag(vc * bc))
        wmat = _mmh(p, blockdiag(kb * egc))
        s = s_scr[...]
        v_new = u - _mm(wmat, s)
        o_scr[rows, :] = _mm(qc * egc, s) + _mm(attn, blockdiag(v_new))
        kd = kc * jnp.exp(gl - gc)
        s_scr[...] = s * jnp.exp(gl) + jnp.where(bd, _mm_tn(kd, v_new), 0.0)
        return carry

    lax.fori_loop(0, tc // c64, chunk, 0)
    o = o_scr[...]
    ms = _mmh(o * o, ones_bd) * (1.0 / DN_HEAD_DIM)
    o_ref[...] = o * lax.rsqrt(ms + NORM_EPS) * ng_ref[...]


def _dnet(proj3, cw, alog, dtb, ng, tc):
    b, l, _ = proj3.shape
    w = BRANCH
    row = lambda shape: pl.BlockSpec(shape, lambda bi, i: (0, 0))
    return pl.pallas_call(
        _dnet_kernel,
        out_shape=jax.ShapeDtypeStruct((b, l, w), F32),
        grid=(b, l // tc),
        in_specs=[pl.BlockSpec((None, tc, 3 * w), lambda bi, i: (bi, i, COL_QKV // 3)),
                  pl.BlockSpec((None, 8, 3 * w),
                               lambda bi, i: (bi, jnp.maximum(i * (tc // 8) - 1, 0), COL_QKV // 3)),
                  pl.BlockSpec((None, tc, 2 * w), lambda bi, i: (bi, i, COL_AB // 2)),
                  row((8, 3 * w)), row((1, w)), row((1, w)), row((1, w))],
        out_specs=pl.BlockSpec((None, tc, w), lambda bi, i: (bi, i, 0)),
        scratch_shapes=[pltpu.VMEM((w, w), F32),
                        pltpu.VMEM((tc + 8, 3 * w), F32),
                        pltpu.VMEM((tc, w), F32), pltpu.VMEM((tc, w), F32),
                        pltpu.VMEM((tc, w), F32), pltpu.VMEM((tc, w), F32),
                        pltpu.VMEM((tc, w), F32), pltpu.VMEM((tc, w), F32)],
        compiler_params=pltpu.CompilerParams(
            dimension_semantics=("parallel", "arbitrary"), vmem_limit_bytes=VMEM_LIMIT_BYTES),
        name="dnet",
    )(proj3, proj3, proj3, cw, alog, dtb, ng)


def _mixout_kernel(c_ref, chalo_ref, b_ref, a_ref, ahalo_ref, dz_ref, ys_ref, od_ref, x_ref,
                   acw_ref, avec_ref, apw_ref, bvec_ref, bglu_ref, ccw_ref, wout_ref, fg_ref,
                   out_ref, apad_scr, cpad_scr, *, final, sub):
    tc = x_ref.shape[0]
    w = BRANCH
    first = pl.program_id(1) == 0
    ahalo = ahalo_ref[...]
    a_h = ahalo[:, 0:w] * _sigmoid(ahalo[:, w:2 * w])
    apad_scr[0:32, :] = jnp.where(first, 0.0, a_h)
    apad_scr[32:32 + tc, :] = a_ref[:, 0:w] * _sigmoid(a_ref[:, w:2 * w])
    chalo = chalo_ref[...]
    cpad_scr[0:8, :] = jnp.where(first, 0.0, chalo[:, w:2 * w] * chalo[:, 2 * w:3 * w])
    cpad_scr[8:8 + tc, :] = c_ref[:, w:2 * w] * c_ref[:, 2 * w:3 * w]

    conv_b, ln_g, ln_b, pw_b = (avec_ref[0:1, :], avec_ref[1:2, :], avec_ref[2:3, :],
                                avec_ref[3:4, :])
    d_skip, glu_b = bvec_ref[0:1, :], bvec_ref[1:2, :]

    for r in range(tc // sub):
        r0 = r * sub
        rows = slice(r0, r0 + sub)
        acc = conv_b + acw_ref[0:1, :] * apad_scr[pl.ds(r0 + 32 - (CONF_KERNEL - 1), sub), :]
        for kk in range(1, CONF_KERNEL):
            acc = acc + acw_ref[kk:kk + 1, :] * apad_scr[
                pl.ds(r0 + 32 - (CONF_KERNEL - 1) + kk, sub), :]
        mu = jnp.mean(acc, axis=-1, keepdims=True)
        xc = acc - mu
        ln = xc * lax.rsqrt(jnp.mean(xc * xc, axis=-1, keepdims=True) + NORM_EPS) * ln_g + ln_b
        ya = (_mm(_silu(ln), apw_ref[...]) + pw_b) * _silu(a_ref[rows, 2 * w:3 * w])
        ub = b_ref[rows, 0:w]
        yb = jax.nn.gelu(ys_ref[rows, :] + d_skip * ub)
        yb = yb * _sigmoid(_mm(yb, bglu_ref[...]) + glu_b) * _silu(b_ref[rows, w:2 * w])
        cacc = ccw_ref[0:1, :] * cpad_scr[pl.ds(r0 + 8 - (SC_KERNEL - 1), sub), :]
        for kk in range(1, SC_KERNEL):
            cacc = cacc + ccw_ref[kk:kk + 1, :] * cpad_scr[
                pl.ds(r0 + 8 - (SC_KERNEL - 1) + kk, sub), :]
        yc = c_ref[rows, 0:w] * cacc * _silu(c_ref[rows, 3 * w:4 * w])
        yd = od_ref[rows, :] * _silu(dz_ref[rows, :])
        y = (x_ref[rows, :]
             + _mm(ya, wout_ref[0:w, :]) + _mm(yb, wout_ref[w:2 * w, :])
             + _mm(yc, wout_ref[2 * w:3 * w, :]) + _mm(yd, wout_ref[3 * w:4 * w, :]))
        if final:
            y = y * lax.rsqrt(jnp.mean(y * y, axis=-1, keepdims=True) + NORM_EPS) * fg_ref[...]
        out_ref[rows, :] = y


def _mixout(proj3, ys, od, x3, acw, avec, apw, bvec, bglu, ccw, wout, fg, tc, final):
    b, l, d = x3.shape
    w = BRANCH
    blk = lambda width, col: pl.BlockSpec((None, tc, width), lambda bi, i: (bi, i, col))
    halo = lambda nrow, width, col: pl.BlockSpec(
        (None, nrow, width), lambda bi, i: (bi, jnp.maximum(i * (tc // nrow) - 1, 0), col))
    const = lambda shape: pl.BlockSpec(shape, lambda bi, i: (0, 0))
    return pl.pallas_call(
        functools.partial(_mixout_kernel, final=final, sub=128),
        out_shape=jax.ShapeDtypeStruct((b, l, d), F32),
        grid=(b, l // tc),
        in_specs=[blk(4 * w, COL_C // 4), halo(8, 4 * w, COL_C // 4),
                  blk(2 * w, COL_B // 2),
                  blk(3 * w, COL_A // 3), halo(32, 3 * w, COL_A // 3),
                  blk(w, COL_DZ), blk(w, 0), blk(w, 0), blk(d, 0),
                  const((32, w)), const((8, w)), const((w, w)), const((8, w)), const((w, w)),
                  const((8, w)), const((4 * w, d)), const((1, d))],
        out_specs=pl.BlockSpec((None, tc, d), lambda bi, i: (bi, i, 0)),
        scratch_shapes=[pltpu.VMEM((tc + 32, w), F32), pltpu.VMEM((tc + 8, w), F32)],
        compiler_params=pltpu.CompilerParams(
            dimension_semantics=("parallel", "parallel"), vmem_limit_bytes=VMEM_LIMIT_BYTES),
        name="mixout_final" if final else "mixout",
    )(proj3, proj3, proj3, proj3, proj3, proj3, ys, od, x3,
      acw, avec, apw, bvec, bglu, ccw, wout, fg)


def _pad_rows(a, n):
    return jnp.pad(a, ((0, 0), (0, n - a.shape[1]), (0, 0)))


def _s5_matrices(lam_re, lam_im, b_re, b_im, c_re, c_im, log_dt):
    q, g, p, h = S5_Q, S5_GROUPS, S5_STATE, S5_GROUP
    dep = lam_re.shape[0]
    lam = lax.complex(jnp.minimum(lam_re, -1e-4), lam_im)
    lamdt = lam * jnp.exp(log_dt)[..., None]
    steps = jnp.arange(q + 1, dtype=F32)
    pw = jnp.exp(lamdt[:, None] * steps[None, :, None, None])
    b = lax.complex(b_re, b_im)
    c = lax.complex(c_re, c_im)
    b_bar = ((pw[:, 1] - 1.0) / lam)[..., None] * b
    eye = jnp.eye(g, dtype=F32)

    wc = jnp.einsum("djgp,dgph->djghp", pw[:, q - 1::-1][:, :q], b_bar)
    wre = jnp.einsum("djghp,gk->djghkp", jnp.real(wc), eye).reshape(dep, q * g * h, g * p)
    wim = jnp.einsum("djghp,gk->djghkp", jnp.imag(wc), eye).reshape(dep, q * g * h, g * p)
    w = jnp.concatenate([wre, wim], axis=-1)

    mc = jnp.einsum("dghp,digp->dgpih", c, pw[:, 1:q + 1])
    vre = jnp.einsum("dgpih,gk->dgpikh", jnp.real(mc), eye).reshape(dep, g * p, q * g * h)
    vim = jnp.einsum("dgpih,gk->dgpikh", -jnp.imag(mc), eye).reshape(dep, g * p, q * g * h)
    v = jnp.concatenate([vre, vim], axis=1)

    kd = jnp.real(jnp.einsum("dghp,degp,dgpk->deghk", c, pw[:, :q], b_bar))
    ii = jnp.arange(q)
    lag = ii[None, :] - ii[:, None]
    tk = jnp.where((lag >= 0)[None, :, :, None, None, None],
                   kd[:, jnp.clip(lag, 0, q - 1)], 0.0)
    t = jnp.einsum("djighk,gm->djgkimh", tk, eye).reshape(dep, q * g * h, q * g * h)

    a = pw[:, q].reshape(dep, 1, g * p)
    a = jnp.concatenate([jnp.real(a), jnp.imag(a), jnp.zeros((dep, 6, g * p), F32)], axis=1)
    return w.astype(BF16), t.astype(BF16), v.astype(BF16), a


def _expand_w_in(w_in):
    w = BRANCH
    blocks = [w_in[..., i * w:(i + 1) * w] for i in range(12)]
    (a_val, a_gate, a_z, b_u, b_z, c_b, c_c, c_x, c_z, d_q, d_k, d_v) = blocks
    alpha = jnp.repeat(w_in[..., 12 * w:12 * w + DN_HEADS], DN_HEAD_DIM, axis=-1)
    beta = jnp.repeat(w_in[..., 12 * w + DN_HEADS:12 * w + 2 * DN_HEADS], DN_HEAD_DIM, axis=-1)
    d_z = w_in[..., 12 * w + 2 * DN_HEADS:]
    return jnp.concatenate([c_b, c_c, c_x, c_z, b_u, b_z, a_val, a_gate, a_z,
                            d_q, d_k, d_v, alpha, beta, d_z], axis=-1).astype(BF16)


def kernel(x, norm_g, w_in, a_conv_w, a_conv_b, a_ln_g, a_ln_b, a_pw_w, a_pw_b, s5_lambda_re, s5_lambda_im, s5_b_re, s5_b_im, s5_c_re, s5_c_im, s5_d, s5_log_dt, s5_glu_w, s5_glu_b, c_conv_w, d_conv_w, d_a_log, d_dt_bias, d_norm_g, w_out, final_g):
    bsz, seq, d = x.shape
    depth = w_in.shape[0]
    w = BRANCH
    tm = min(512, bsz * seq)
    tc = min(512, seq)
    s5_rows = min(256, seq // S5_Q)

    w_exp = _expand_w_in(w_in)
    s5_w, s5_t, s5_v, s5_a = _s5_matrices(s5_lambda_re, s5_lambda_im, s5_b_re, s5_b_im,
                                          s5_c_re, s5_c_im, s5_log_dt)
    acw = _pad_rows(a_conv_w, 32)
    zeros = jnp.zeros_like(a_conv_b)
    avec = jnp.stack([a_conv_b, a_ln_g, a_ln_b, a_pw_b, zeros, zeros, zeros, zeros], axis=1)
    bvec = jnp.stack([s5_d, s5_glu_b, zeros, zeros, zeros, zeros, zeros, zeros], axis=1)
    ccw = _pad_rows(c_conv_w, 8)
    dcw = _pad_rows(d_conv_w, 8)
    alog = jnp.repeat(d_a_log, DN_HEAD_DIM, axis=-1)[:, None, :]
    dtb = jnp.repeat(d_dt_bias, DN_HEAD_DIM, axis=-1)[:, None, :]
    ng = jnp.tile(d_norm_g, (1, DN_HEADS))[:, None, :]
    apw = a_pw_w.astype(BF16)
    bglu = s5_glu_w.astype(BF16)
    wout = w_out.astype(BF16)
    fg = final_g[None, :]

    for l in range(depth):
        proj, u = _inproj(x.reshape(bsz * seq, d), norm_g[l][None, :], w_exp[l], tm)
        proj3 = proj.reshape(bsz, seq, PROJ_COLS)
        ys = _s5(u.reshape(bsz, seq // S5_Q, S5_ROW), s5_w[l], s5_t[l], s5_v[l], s5_a[l],
                 s5_rows).reshape(bsz, seq, w)
        od = _dnet(proj3, dcw[l], alog[l], dtb[l], ng[l], tc)
        x = _mixout(proj3, ys, od, x, acw[l], avec[l], apw[l], bvec[l], bglu[l], ccw[l],
                    wout[l], fg, tc, l == depth - 1)
    return x
```

```python
import functools

import jax
import jax.numpy as jnp
from jax import lax
from jax.experimental import pallas as pl
from jax.experimental.pallas import tpu as pltpu

F32 = jnp.float32
BF16 = jnp.bfloat16
HIGHEST = lax.Precision.HIGHEST

NORM_EPS = 1e-6
BRANCH = 256
CONF_KERNEL = 31
SC_KERNEL = 3
DN_HEADS = 4
DN_HEAD_DIM = 64
DN_CONV = 4
DN_CHUNK = 64
DN_GROUP = 4
S5_GROUPS = 16
S5_GROUP = 16
S5_STATE = 64
S5_Q = 8
S5_ROW = S5_Q * BRANCH
S5_NSTATE = S5_GROUPS * S5_STATE

COL_C, COL_B, COL_A, COL_QKV, COL_AB, COL_DZ = 0, 4, 6, 9, 12, 14
PROJ_COLS = 15 * BRANCH
PROJ_TILE = 768

VMEM_LIMIT_BYTES = 56 * 1024 * 1024


def _mm(a, b):
    return jnp.dot(a.astype(BF16), b.astype(BF16), preferred_element_type=F32)


def _mm_tn(a, b):
    return lax.dot_general(a.astype(BF16), b.astype(BF16), (((0,), (0,)), ((), ())),
                           preferred_element_type=F32)


def _sigmoid(x):
    return 1.0 / (1.0 + jnp.exp(-x))


def _silu(x):
    return x * _sigmoid(x)


def _inproj_kernel(x_ref, g_ref, w_ref, proj_ref, u_ref):
    x = x_ref[...]
    h = x * lax.rsqrt(jnp.mean(x * x, axis=-1, keepdims=True) + NORM_EPS) * g_ref[...]
    hb = h.astype(BF16)
    for c in range(PROJ_COLS // PROJ_TILE):
        cols = slice(c * PROJ_TILE, (c + 1) * PROJ_TILE)
        proj_ref[:, cols] = jnp.dot(hb, w_ref[:, cols], preferred_element_type=F32)
    u_ref[...] = proj_ref[:, COL_B * BRANCH:(COL_B + 1) * BRANCH]


def _inproj(x2d, g, w, tm):
    t, d = x2d.shape
    return pl.pallas_call(
        _inproj_kernel,
        out_shape=(jax.ShapeDtypeStruct((t, PROJ_COLS), F32),
                   jax.ShapeDtypeStruct((t, BRANCH), F32)),
        grid=(t // tm,),
        in_specs=[pl.BlockSpec((tm, d), lambda i: (i, 0)),
                  pl.BlockSpec((1, d), lambda i: (0, 0)),
                  pl.BlockSpec((d, PROJ_COLS), lambda i: (0, 0))],
        out_specs=(pl.BlockSpec((tm, PROJ_COLS), lambda i: (i, 0)),
                   pl.BlockSpec((tm, BRANCH), lambda i: (i, 0))),
        compiler_params=pltpu.CompilerParams(
            dimension_semantics=("parallel",), vmem_limit_bytes=VMEM_LIMIT_BYTES),
        name="inproj",
    )(x2d, g, w)


def _s5_expand(tab_ref, out_scr, row_div, by_state):
    n, step = out_scr.shape[0], 256
    r = lax.broadcasted_iota(jnp.int32, (128, S5_ROW), 0)
    c = lax.broadcasted_iota(jnp.int32, (128, S5_ROW), 1)
    col = lax.broadcasted_iota(jnp.int32, (step, S5_ROW), 1)
    if by_state:
        sel = (r // S5_STATE == c // S5_NSTATE) & (r % S5_STATE == c % S5_STATE)
        col_g = (col // S5_STATE) % S5_GROUPS
    else:
        sel = (r // S5_GROUP == c // BRANCH) & (r % S5_GROUP == c % S5_GROUP)
        col_g = (col // S5_GROUP) % S5_GROUPS
    sel = jnp.where(sel, 1.0, 0.0).astype(BF16)
    for i in range(n // step):
        rows = slice(i * step, (i + 1) * step)
        row = lax.broadcasted_iota(jnp.int32, (step, S5_ROW), 0) + i * step
        dense = jnp.dot(tab_ref[rows, :], sel, preferred_element_type=F32)
        out_scr[rows, :] = jnp.where((row // row_div) % S5_GROUPS == col_g, dense, 0.0).astype(BF16)


def _s5_kernel(u_ref, wc_ref, tk_ref, mc_ref, a_ref, y_ref, w_scr, t_scr, v_scr,
               carry_scr, z_scr, xp_scr):
    n = S5_NSTATE
    rows = u_ref.shape[0]

    @pl.when((pl.program_id(0) == 0) & (pl.program_id(1) == 0))
    def _():
        _s5_expand(wc_ref, w_scr, S5_GROUP, True)
        _s5_expand(tk_ref, t_scr, S5_GROUP, False)
        _s5_expand(mc_ref, v_scr, S5_STATE, False)

    @pl.when(pl.program_id(1) == 0)
    def _():
        carry_scr[...] = jnp.zeros_like(carry_scr)

    ub = u_ref[...].astype(BF16)
    z_scr[...] = jnp.dot(ub, w_scr[...], preferred_element_type=F32)
    ar = a_ref[0:1, :]
    ai = a_ref[1:2, :]

    def body(c, carry):
        xr, xi = carry
        xp_scr[pl.ds(c, 1), 0:n] = xr
        xp_scr[pl.ds(c, 1), n:2 * n] = xi
        zr = z_scr[pl.ds(c, 1), 0:n]
        zi = z_scr[pl.ds(c, 1), n:2 * n]
        return ar * xr - ai * xi + zr, ar * xi + ai * xr + zi

    xr, xi = lax.fori_loop(0, rows, body, (carry_scr[0:1, 0:n], carry_scr[0:1, n:2 * n]),
                           unroll=4)
    carry_scr[0:1, 0:n] = xr
    carry_scr[0:1, n:2 * n] = xi
    y_ref[...] = (jnp.dot(ub, t_scr[...], preferred_element_type=F32)
                  + jnp.dot(xp_scr[...].astype(BF16), v_scr[...], preferred_element_type=F32))


def _s5(u_flat, wc, tk, mc, a, rows):
    b, nrow, width = u_flat.shape
    const = lambda shape: pl.BlockSpec(shape, lambda bi, j: (0, 0))
    dense = lambda: pltpu.VMEM((width, width), BF16)
    return pl.pallas_call(
        _s5_kernel,
        out_shape=jax.ShapeDtypeStruct((b, nrow, width), F32),
        grid=(b, nrow // rows),
        in_specs=[pl.BlockSpec((None, rows, width), lambda bi, j: (bi, j, 0)),
                  const((width, 128)), const((width, 128)), const((width, 128)),
                  const((8, S5_NSTATE))],
        out_specs=pl.BlockSpec((None, rows, width), lambda bi, j: (bi, j, 0)),
        scratch_shapes=[dense(), dense(), dense(),
                        pltpu.VMEM((8, 2 * S5_NSTATE), F32),
                        pltpu.VMEM((rows, 2 * S5_NSTATE), F32),
                        pltpu.VMEM((rows, 2 * S5_NSTATE), F32)],
        compiler_params=pltpu.CompilerParams(
            dimension_semantics=("arbitrary", "arbitrary"), vmem_limit_bytes=VMEM_LIMIT_BYTES),
        name="s5",
    )(u_flat, wc, tk, mc, a)


def _tile_heads(x):
    return jnp.concatenate([x] * DN_HEADS, axis=0)


def _dnet_kernel(qkv_ref, halo_ref, ab_ref, cw_ref, alog_ref, dtb_ref, ng_ref, o_ref,
                 s_scr, pad_scr, q_scr, k_scr, v_scr, beta_scr, g_scr, o_scr,
                 u_scr, wq_scr, attn_scr, kd_scr, egl_scr):
    nb, tc, _ = qkv_ref.shape
    c64 = DN_CHUNK
    w = DN_HEADS * DN_HEAD_DIM
    nrow = nb * tc
    first = pl.program_id(0) == 0

    @pl.when(first)
    def _():
        s_scr[...] = jnp.zeros_like(s_scr)

    pad_scr[:, 0:8, :] = jnp.where(first, 0.0, halo_ref[...])
    pad_scr[:, 8:8 + tc, :] = qkv_ref[...]
    acc = cw_ref[0:1, :] * pad_scr[:, pl.ds(8 - (DN_CONV - 1), tc), :]
    for kk in range(1, DN_CONV):
        acc = acc + cw_ref[kk:kk + 1, :] * pad_scr[:, pl.ds(8 - (DN_CONV - 1) + kk, tc), :]
    qkv = _silu(acc).reshape(nrow, 3 * w)

    row_h = lax.broadcasted_iota(jnp.int32, (w, w), 0) // DN_HEAD_DIM
    col_h = lax.broadcasted_iota(jnp.int32, (w, w), 1) // DN_HEAD_DIM
    bd = row_h == col_h
    bd_bf = jnp.where(bd, 1.0, 0.0).astype(BF16)

    def mmb(a, b_bf):
        return jnp.dot(a.astype(BF16), b_bf, preferred_element_type=F32)

    def mmb_nt(a, b_bf):
        return lax.dot_general(a.astype(BF16), b_bf, (((1,), (1,)), ((), ())),
                               preferred_element_type=F32)

    q = qkv[:, 0:w]
    k = qkv[:, w:2 * w]
    q_scr[...] = q * lax.rsqrt(mmb(q * q, bd_bf) + NORM_EPS) * (DN_HEAD_DIM ** -0.5)
    k_scr[...] = k * lax.rsqrt(mmb(k * k, bd_bf) + NORM_EPS)
    ab = ab_ref[...].reshape(nrow, 2 * w)
    beta = _sigmoid(ab[:, w:2 * w])
    beta_scr[...] = beta
    v_scr[...] = qkv[:, 2 * w:3 * w] * beta
    sp_in = ab[:, 0:w] + dtb_ref[...]
    softplus = jnp.maximum(sp_in, 0.0) + jnp.log1p(jnp.exp(-jnp.abs(sp_in)))
    g_scr[...] = -jnp.exp(alog_ref[...]) * softplus

    ri = lax.broadcasted_iota(jnp.int32, (c64, w), 0)
    cj = lax.broadcasted_iota(jnp.int32, (c64, w), 1) % DN_HEAD_DIM
    causal = ri >= cj
    strict = ri > cj
    diag = ri == cj
    eye_all = jnp.where(diag, 1.0, 0.0).astype(F32)
    lt_bf = jnp.where(lax.broadcasted_iota(jnp.int32, (c64, c64), 0)
                      >= lax.broadcasted_iota(jnp.int32, (c64, c64), 1), 1.0, 0.0).astype(BF16)

    def blockdiag(x):
        return _tile_heads(x.astype(BF16)) * bd_bf

    grp = range(DN_GROUP)

    def prepare(gi, carry):
        base = gi * (DN_GROUP * c64)
        rows = [pl.ds(pl.multiple_of(base + i * c64, c64), c64) for i in grp]
        qc = [q_scr[r, :] for r in rows]
        kc = [k_scr[r, :] for r in rows]
        bc = [beta_scr[r, :] for r in rows]
        g = [g_scr[r, :] for r in rows]
        g_hi = [x.astype(BF16) for x in g]
        g_lo = [(g[i] - g_hi[i].astype(F32)).astype(BF16) for i in grp]
        gc = [jnp.dot(lt_bf, g_hi[i], preferred_element_type=F32)
              + jnp.dot(lt_bf, g_lo[i], preferred_element_type=F32) for i in grp]
        egc = [jnp.exp(x) for x in gc]
        gl = [x[c64 - 1:c64, :] for x in gc]
        kb = [kc[i] * bc[i] for i in grp]
        grow = [jnp.sum(jnp.where(diag, x, 0.0), axis=0, keepdims=True) for x in gc]
        decay = [jnp.exp(jnp.where(causal, gc[i] - grow[i], -jnp.inf)) for i in grp]
        k_bd = [blockdiag(x) for x in kc]
        lmat = [jnp.where(strict, mmb_nt(kb[i], k_bd[i]) * decay[i], 0.0) for i in grp]
        for i in grp:
            attn_scr[rows[i], :] = mmb_nt(qc[i], k_bd[i]) * decay[i]
        x = lmat
        p = [eye_all - lmat[i] for i in grp]
        for _ in range(5):
            x = [mmb(x[i], blockdiag(x[i])) for i in grp]
            p = [p[i] + mmb(p[i], blockdiag(x[i])) for i in grp]
        for i in grp:
            ci = gi * DN_GROUP + i
            u_scr[rows[i], :] = mmb(p[i], blockdiag(v_scr[rows[i], :]))
            wq_scr[ci, 0:c64, :] = mmb(p[i], blockdiag(kb[i] * egc[i]))
            wq_scr[ci, c64:2 * c64, :] = qc[i] * egc[i]
            kd_scr[rows[i], :] = kc[i] * jnp.exp(gl[i] - gc[i])
            egl_scr[ci] = jnp.broadcast_to(jnp.exp(gl[i]), (8, w))
        return carry

    lax.fori_loop(0, nrow // (DN_GROUP * c64), prepare, 0)

    nck = tc // c64
    bs = range(nb)

    def recur(c, carry):
        rows = [pl.ds(pl.multiple_of(b * tc + c * c64, c64), c64) for b in bs]
        s = [s_scr[b] for b in bs]
        ws_qs = [_mm(wq_scr[b * nck + c], s[b]) for b in bs]
        v_new = [u_scr[rows[b], :] - ws_qs[b][0:c64, :] for b in bs]
        for b in bs:
            o_scr[rows[b], :] = (ws_qs[b][c64:2 * c64, :]
                                 + mmb(attn_scr[rows[b], :], blockdiag(v_new[b])))
        for b in bs:
            s_scr[b] = (s[b] * egl_scr[b * nck + c][0:1, :]
                        + jnp.where(bd, _mm_tn(kd_scr[rows[b], :], v_new[b]), 0.0))
        return carry

    lax.fori_loop(0, nck, recur, 0)

    o = o_scr[...]
    ms = mmb(o * o, bd_bf) * (1.0 / DN_HEAD_DIM)
    o_ref[...] = (o * lax.rsqrt(ms + NORM_EPS) * ng_ref[...]).reshape(nb, tc, w)


def _dnet(proj3, cw, alog, dtb, ng, tc):
    b, l, _ = proj3.shape
    w = BRANCH
    nchunk = b * tc // DN_CHUNK
    row = lambda shape: pl.BlockSpec(shape, lambda i: (0, 0))
    act = lambda: pltpu.VMEM((b * tc, w), F32)
    return pl.pallas_call(
        _dnet_kernel,
        out_shape=jax.ShapeDtypeStruct((b, l, w), F32),
        grid=(l // tc,),
        in_specs=[pl.BlockSpec((b, tc, 3 * w), lambda i: (0, i, COL_QKV // 3)),
                  pl.BlockSpec((b, 8, 3 * w),
                               lambda i: (0, jnp.maximum(i * (tc // 8) - 1, 0), COL_QKV // 3)),
                  pl.BlockSpec((b, tc, 2 * w), lambda i: (0, i, COL_AB // 2)),
                  row((8, 3 * w)), row((1, w)), row((1, w)), row((1, w))],
        out_specs=pl.BlockSpec((b, tc, w), lambda i: (0, i, 0)),
        scratch_shapes=[pltpu.VMEM((b, w, w), F32),
                        pltpu.VMEM((b, tc + 8, 3 * w), F32),
                        act(), act(), act(), act(), act(), act(), act(),
                        pltpu.VMEM((nchunk, 2 * DN_CHUNK, w), F32),
                        act(), act(),
                        pltpu.VMEM((nchunk, 8, w), F32)],
        compiler_params=pltpu.CompilerParams(
            dimension_semantics=("arbitrary",), vmem_limit_bytes=VMEM_LIMIT_BYTES),
        name="dnet",
    )(proj3, proj3, proj3, cw, alog, dtb, ng)


def _mixout_kernel(c_ref, chalo_ref, b_ref, a_ref, ahalo_ref, dz_ref, ys_ref, od_ref, x_ref,
                   acw_ref, avec_ref, apw_ref, bvec_ref, bglu_ref, ccw_ref, wout_ref, fg_ref,
                   out_ref, apad_scr, cpad_scr, *, final, sub):
    tc = x_ref.shape[0]
    w = BRANCH
    first = pl.program_id(1) == 0
    ahalo = ahalo_ref[...]
    a_h = ahalo[:, 0:w] * _sigmoid(ahalo[:, w:2 * w])
    apad_scr[0:32, :] = jnp.where(first, 0.0, a_h)
    apad_scr[32:32 + tc, :] = a_ref[:, 0:w] * _sigmoid(a_ref[:, w:2 * w])
    chalo = chalo_ref[...]
    cpad_scr[0:8, :] = jnp.where(first, 0.0, chalo[:, w:2 * w] * chalo[:, 2 * w:3 * w])
    cpad_scr[8:8 + tc, :] = c_ref[:, w:2 * w] * c_ref[:, 2 * w:3 * w]

    conv_b, ln_g, ln_b, pw_b = (avec_ref[0:1, :], avec_ref[1:2, :], avec_ref[2:3, :],
                                avec_ref[3:4, :])
    d_skip, glu_b = bvec_ref[0:1, :], bvec_ref[1:2, :]

    for r in range(tc // sub):
        r0 = r * sub
        rows = slice(r0, r0 + sub)
        acc = conv_b + acw_ref[0:1, :] * apad_scr[pl.ds(r0 + 32 - (CONF_KERNEL - 1), sub), :]
        for kk in range(1, CONF_KERNEL):
            acc = acc + acw_ref[kk:kk + 1, :] * apad_scr[
                pl.ds(r0 + 32 - (CONF_KERNEL - 1) + kk, sub), :]
        mu = jnp.mean(acc, axis=-1, keepdims=True)
        xc = acc - mu
        ln = xc * lax.rsqrt(jnp.mean(xc * xc, axis=-1, keepdims=True) + NORM_EPS) * ln_g + ln_b
        ya = (_mm(_silu(ln), apw_ref[...]) + pw_b) * _silu(a_ref[rows, 2 * w:3 * w])
        ub = b_ref[rows, 0:w]
        yb = jax.nn.gelu(ys_ref[rows, :] + d_skip * ub)
        yb = yb * _sigmoid(_mm(yb, bglu_ref[...]) + glu_b) * _silu(b_ref[rows, w:2 * w])
        cacc = ccw_ref[0:1, :] * cpad_scr[pl.ds(r0 + 8 - (SC_KERNEL - 1), sub), :]
        for kk in range(1, SC_KERNEL):
            cacc = cacc + ccw_ref[kk:kk + 1, :] * cpad_scr[
                pl.ds(r0 + 8 - (SC_KERNEL - 1) + kk, sub), :]
        yc = c_ref[rows, 0:w] * cacc * _silu(c_ref[rows, 3 * w:4 * w])
        yd = od_ref[rows, :] * _silu(dz_ref[rows, :])
        y = (x_ref[rows, :]
             + _mm(ya, wout_ref[0:w, :]) + _mm(yb, wout_ref[w:2 * w, :])
             + _mm(yc, wout_ref[2 * w:3 * w, :]) + _mm(yd, wout_ref[3 * w:4 * w, :]))
        if final:
            y = y * lax.rsqrt(jnp.mean(y * y, axis=-1, keepdims=True) + NORM_EPS) * fg_ref[...]
        out_ref[rows, :] = y


def _mixout(proj3, ys, od, x3, acw, avec, apw, bvec, bglu, ccw, wout, fg, tc, final):
    b, l, d = x3.shape
    w = BRANCH
    blk = lambda width, col: pl.BlockSpec((None, tc, width), lambda bi, i: (bi, i, col))
    halo = lambda nrow, width, col: pl.BlockSpec(
        (None, nrow, width), lambda bi, i: (bi, jnp.maximum(i * (tc // nrow) - 1, 0), col))
    const = lambda shape: pl.BlockSpec(shape, lambda bi, i: (0, 0))
    return pl.pallas_call(
        functools.partial(_mixout_kernel, final=final, sub=128),
        out_shape=jax.ShapeDtypeStruct((b, l, d), F32),
        grid=(b, l // tc),
        in_specs=[blk(4 * w, COL_C // 4), halo(8, 4 * w, COL_C // 4),
                  blk(2 * w, COL_B // 2),
                  blk(3 * w, COL_A // 3), halo(32, 3 * w, COL_A // 3),
                  blk(w, COL_DZ), blk(w, 0), blk(w, 0), blk(d, 0),
                  const((32, w)), const((8, w)), const((w, w)), const((8, w)), const((w, w)),
                  const((8, w)), const((4 * w, d)), const((1, d))],
        out_specs=pl.BlockSpec((None, tc, d), lambda bi, i: (bi, i, 0)),
        scratch_shapes=[pltpu.VMEM((tc + 32, w), F32), pltpu.VMEM((tc + 8, w), F32)],
        compiler_params=pltpu.CompilerParams(
            dimension_semantics=("parallel", "parallel"), vmem_limit_bytes=VMEM_LIMIT_BYTES),
        name="mixout_final" if final else "mixout",
    )(proj3, proj3, proj3, proj3, proj3, proj3, ys, od, x3,
      acw, avec, apw, bvec, bglu, ccw, wout, fg)


def _pad_rows(a, n):
    return jnp.pad(a, ((0, 0), (0, n - a.shape[1]), (0, 0)))


def _s5_tables(lam_re, lam_im, b_re, b_im, c_re, c_im, log_dt):
    q, g, p, h = S5_Q, S5_GROUPS, S5_STATE, S5_GROUP
    dep = lam_re.shape[0]
    lr = jnp.minimum(lam_re, -1e-4)
    li = lam_im
    dt = jnp.exp(log_dt)[..., None]
    lrdt, lidt = (lr * dt)[:, None], (li * dt)[:, None]

    def power(steps):
        s = steps[None, :, None, None]
        mag = jnp.exp(lrdt * s)
        return mag * jnp.cos(lidt * s), mag * jnp.sin(lidt * s)

    tok = jnp.arange(q, dtype=F32)
    p1r, p1i = power(jnp.ones((1,), F32))
    nr, ni = p1r[:, 0] - 1.0, p1i[:, 0]
    den = lr * lr + li * li
    fr, fi = (nr * lr + ni * li) / den, (ni * lr - nr * li) / den
    bbr = fr[..., None] * b_re - fi[..., None] * b_im
    bbi = fr[..., None] * b_im + fi[..., None] * b_re

    rr, ri = power(q - 1.0 - tok)
    rr, ri = rr[:, :, :, None, :], ri[:, :, :, None, :]
    br_t = jnp.transpose(bbr, (0, 1, 3, 2))[:, None]
    bi_t = jnp.transpose(bbi, (0, 1, 3, 2))[:, None]
    wc = jnp.concatenate([rr * br_t - ri * bi_t, rr * bi_t + ri * br_t], axis=-1)
    wc = wc.reshape(dep, q * g * h, 2 * p)

    qr, qi = power(tok + 1.0)
    qr = jnp.transpose(qr, (0, 2, 3, 1))[..., None]
    qi = jnp.transpose(qi, (0, 2, 3, 1))[..., None]
    cr_t = jnp.transpose(c_re, (0, 1, 3, 2))[:, :, :, None, :]
    ci_t = jnp.transpose(c_im, (0, 1, 3, 2))[:, :, :, None, :]
    mc = jnp.concatenate([cr_t * qr - ci_t * qi, -(cr_t * qi + ci_t * qr)], axis=1)
    mc = mc.reshape(dep, 2 * g * p, q * h)

    er, ei = power(tok)
    er, ei = er[:, :, :, None, :], ei[:, :, :, None, :]
    cpr = c_re[:, None] * er - c_im[:, None] * ei
    cpi = c_re[:, None] * ei + c_im[:, None] * er
    kd = (jnp.einsum("deghp,dgpk->deghk", cpr, bbr, precision=HIGHEST)
          - jnp.einsum("deghp,dgpk->deghk", cpi, bbi, precision=HIGHEST))
    ii = jnp.arange(q)
    lag_is = (ii[None, None, :] - ii[None, :, None] == ii[:, None, None]).astype(F32)
    tk = jnp.einsum("eji,deghk->djgkih", lag_is, kd, precision=HIGHEST)
    tk = tk.reshape(dep, q * g * h, q * h)

    ar, ai = power(jnp.full((1,), float(q), F32))
    a = jnp.concatenate([ar.reshape(dep, 1, g * p), ai.reshape(dep, 1, g * p),
                         jnp.zeros((dep, 6, g * p), F32)], axis=1)
    return wc.astype(BF16), tk.astype(BF16), mc.astype(BF16), a


def _expand_w_in(w_in):
    w = BRANCH
    blocks = [w_in[..., i * w:(i + 1) * w] for i in range(12)]
    (a_val, a_gate, a_z, b_u, b_z, c_b, c_c, c_x, c_z, d_q, d_k, d_v) = blocks
    alpha = jnp.repeat(w_in[..., 12 * w:12 * w + DN_HEADS], DN_HEAD_DIM, axis=-1)
    beta = jnp.repeat(w_in[..., 12 * w + DN_HEADS:12 * w + 2 * DN_HEADS], DN_HEAD_DIM, axis=-1)
    d_z = w_in[..., 12 * w + 2 * DN_HEADS:]
    return jnp.concatenate([c_b, c_c, c_x, c_z, b_u, b_z, a_val, a_gate, a_z,
                            d_q, d_k, d_v, alpha, beta, d_z], axis=-1).astype(BF16)


def kernel(x, norm_g, w_in, a_conv_w, a_conv_b, a_ln_g, a_ln_b, a_pw_w, a_pw_b, s5_lambda_re, s5_lambda_im, s5_b_re, s5_b_im, s5_c_re, s5_c_im, s5_d, s5_log_dt, s5_glu_w, s5_glu_b, c_conv_w, d_conv_w, d_a_log, d_dt_bias, d_norm_g, w_out, final_g):
    bsz, seq, d = x.shape
    depth = w_in.shape[0]
    w = BRANCH
    tm = min(512, bsz * seq)
    tc = min(512, seq)
    dn_tc = min(256, seq)
    s5_rows = min(256, seq // S5_Q)

    w_exp = _expand_w_in(w_in)
    s5_wc, s5_tk, s5_mc, s5_a = _s5_tables(s5_lambda_re, s5_lambda_im, s5_b_re, s5_b_im,
                                           s5_c_re, s5_c_im, s5_log_dt)
    acw = _pad_rows(a_conv_w, 32)
    zeros = jnp.zeros_like(a_conv_b)
    avec = jnp.stack([a_conv_b, a_ln_g, a_ln_b, a_pw_b, zeros, zeros, zeros, zeros], axis=1)
    bvec = jnp.stack([s5_d, s5_glu_b, zeros, zeros, zeros, zeros, zeros, zeros], axis=1)
    ccw = _pad_rows(c_conv_w, 8)
    dcw = _pad_rows(d_conv_w, 8)
    alog = jnp.repeat(d_a_log, DN_HEAD_DIM, axis=-1)[:, None, :]
    dtb = jnp.repeat(d_dt_bias, DN_HEAD_DIM, axis=-1)[:, None, :]
    ng = jnp.tile(d_norm_g, (1, DN_HEADS))[:, None, :]
    apw = a_pw_w.astype(BF16)
    bglu = s5_glu_w.astype(BF16)
    wout = w_out.astype(BF16)
    fg = final_g[None, :]

    for l in range(depth):
        proj, u = _inproj(x.reshape(bsz * seq, d), norm_g[l][None, :], w_exp[l], tm)
        proj3 = proj.reshape(bsz, seq, PROJ_COLS)
        ys = _s5(u.reshape(bsz, seq // S5_Q, S5_ROW), s5_wc[l], s5_tk[l], s5_mc[l], s5_a[l],
                 s5_rows).reshape(bsz, seq, w)
        od = _dnet(proj3, dcw[l], alog[l], dtb[l], ng[l], dn_tc)
        x = _mixout(proj3, ys, od, x, acw[l], avec[l], apw[l], bvec[l], bglu[l], ccw[l],
                    wout[l], fg, tc, l == depth - 1)
    return x
```

```python
import functools

import jax
import jax.numpy as jnp
from jax import lax
from jax.experimental import pallas as pl
from jax.experimental.pallas import tpu as pltpu

F32 = jnp.float32
BF16 = jnp.bfloat16
HIGHEST = lax.Precision.HIGHEST

NORM_EPS = 1e-6
BRANCH = 256
CONF_KERNEL = 31
SC_KERNEL = 3
DN_HEADS = 4
DN_HEAD_DIM = 64
DN_CONV = 4
DN_CHUNK = 64
DN_GROUP = 8
S5_GROUPS = 16
S5_GROUP = 16
S5_STATE = 64
S5_Q = 8
S5_ROW = S5_Q * BRANCH
S5_NSTATE = S5_GROUPS * S5_STATE

COL_A, COL_B, COL_C, COL_QKV, COL_AB = 0, 3, 6, 10, 13
PROJ_COLS = 15 * BRANCH
ACT_A, ACT_AZ, ACT_BU, ACT_BZ, ACT_CB, ACT_CX, ACT_DZ, ACT_COLS = 0, 1, 2, 3, 4, 5, 6, 7
DN_Q, DN_K, DN_V, DN_BETA, DN_G, DN_COLS = 0, 1, 2, 3, 4, 5

VMEM_LIMIT_BYTES = 56 * 1024 * 1024


def _mm(a, b):
    return jnp.dot(a.astype(BF16), b.astype(BF16), preferred_element_type=F32)


def _mm_tn(a, b):
    return lax.dot_general(a.astype(BF16), b.astype(BF16), (((0,), (0,)), ((), ())),
                           preferred_element_type=F32)


def _sigmoid(x):
    return 1.0 / (1.0 + jnp.exp(-x))


def _silu(x):
    return x * _sigmoid(x)


def _inproj_kernel(x_ref, g_ref, w_ref, dcw_ref, alog_ref, dtb_ref, act_ref, dn_ref, u_ref,
                   pad_scr, *, steps_per_seq, sub):
    tm = x_ref.shape[0]
    w = BRANCH
    first = pl.program_id(0) % steps_per_seq == 0

    @pl.when(first)
    def _():
        pad_scr[tm:tm + 8, :] = jnp.zeros((8, 3 * w), F32)

    pad_scr[0:8, :] = pad_scr[tm:tm + 8, :]

    row_h = lax.broadcasted_iota(jnp.int32, (w, w), 0) // DN_HEAD_DIM
    col_h = lax.broadcasted_iota(jnp.int32, (w, w), 1) // DN_HEAD_DIM
    ones_bd = jnp.where(row_h == col_h, 1.0, 0.0).astype(BF16)

    def project(hb, col, n):
        return jnp.dot(hb, w_ref[:, col * w:(col + n) * w], preferred_element_type=F32)

    def put(ref, rows, j, val):
        ref[rows, j * w:(j + 1) * w] = val

    def deltanet_inputs(rows, r0):
        acc = dcw_ref[0:1, :] * pad_scr[pl.ds(r0 + 8 - (DN_CONV - 1), sub), :]
        for kk in range(1, DN_CONV):
            acc = acc + dcw_ref[kk:kk + 1, :] * pad_scr[pl.ds(r0 + 8 - (DN_CONV - 1) + kk, sub), :]
        qkv = _silu(acc)
        q = qkv[:, 0:w]
        k = qkv[:, w:2 * w]
        ssq = jnp.dot((q * q).astype(BF16), ones_bd, preferred_element_type=F32)
        ssk = jnp.dot((k * k).astype(BF16), ones_bd, preferred_element_type=F32)
        put(dn_ref, rows, DN_Q, q * lax.rsqrt(ssq + NORM_EPS) * (DN_HEAD_DIM ** -0.5))
        put(dn_ref, rows, DN_K, k * lax.rsqrt(ssk + NORM_EPS))
        put(dn_ref, rows, DN_V, qkv[:, 2 * w:3 * w] * dn_ref[rows, DN_BETA * w:(DN_BETA + 1) * w])

    for r in range(tm // sub):
        r0 = r * sub
        rows = slice(r0, r0 + sub)
        x = x_ref[rows, :]
        h = x * lax.rsqrt(jnp.mean(x * x, axis=-1, keepdims=True) + NORM_EPS) * g_ref[...]
        hb = h.astype(BF16)
        pad_scr[r0 + 8:r0 + 8 + sub, :] = project(hb, COL_QKV, 3)
        p = project(hb, COL_A, 3)
        put(act_ref, rows, ACT_A, p[:, 0:w] * _sigmoid(p[:, w:2 * w]))
        put(act_ref, rows, ACT_AZ, _silu(p[:, 2 * w:3 * w]))
        p = project(hb, COL_B, 3)
        put(act_ref, rows, ACT_BU, p[:, 0:w])
        u_ref[rows, :] = p[:, 0:w]
        put(act_ref, rows, ACT_BZ, _silu(p[:, w:2 * w]))
        put(act_ref, rows, ACT_DZ, _silu(p[:, 2 * w:3 * w]))
        p = project(hb, COL_C, 4)
        put(act_ref, rows, ACT_CB, p[:, 0:w] * _silu(p[:, 3 * w:4 * w]))
        put(act_ref, rows, ACT_CX, p[:, w:2 * w] * p[:, 2 * w:3 * w])
        p = project(hb, COL_AB, 2)
        sp_in = p[:, 0:w] + dtb_ref[...]
        softplus = jnp.maximum(sp_in, 0.0) + jnp.log1p(jnp.exp(-jnp.abs(sp_in)))
        put(dn_ref, rows, DN_G, -jnp.exp(alog_ref[...]) * softplus)
        put(dn_ref, rows, DN_BETA, _sigmoid(p[:, w:2 * w]))
        if r > 0:
            deltanet_inputs(slice(r0 - sub, r0), r0 - sub)
    deltanet_inputs(slice(tm - sub, tm), tm - sub)


def _inproj(x2d, g, w, dcw, alog, dtb, tm, seq):
    t, d = x2d.shape
    const = lambda shape: pl.BlockSpec(shape, lambda i: (0, 0))
    out = lambda n: pl.BlockSpec((tm, n * BRANCH), lambda i: (i, 0))
    return pl.pallas_call(
        functools.partial(_inproj_kernel, steps_per_seq=seq // tm, sub=min(256, tm)),
        out_shape=(jax.ShapeDtypeStruct((t, ACT_COLS * BRANCH), F32),
                   jax.ShapeDtypeStruct((t, DN_COLS * BRANCH), F32),
                   jax.ShapeDtypeStruct((t, BRANCH), F32)),
        grid=(t // tm,),
        in_specs=[pl.BlockSpec((tm, d), lambda i: (i, 0)),
                  const((1, d)), const((d, PROJ_COLS)),
                  const((8, 3 * BRANCH)), const((1, BRANCH)), const((1, BRANCH))],
        out_specs=(out(ACT_COLS), out(DN_COLS), out(1)),
        scratch_shapes=[pltpu.VMEM((tm + 8, 3 * BRANCH), F32)],
        compiler_params=pltpu.CompilerParams(
            dimension_semantics=("arbitrary",), vmem_limit_bytes=VMEM_LIMIT_BYTES),
        name="inproj",
    )(x2d, g, w, dcw, alog, dtb)


def _s5_expand(tab_ref, out_scr, row_div, by_state):
    n, step = out_scr.shape[0], 256
    r = lax.broadcasted_iota(jnp.int32, (128, S5_ROW), 0)
    c = lax.broadcasted_iota(jnp.int32, (128, S5_ROW), 1)
    col = lax.broadcasted_iota(jnp.int32, (step, S5_ROW), 1)
    if by_state:
        sel = (r // S5_STATE == c // S5_NSTATE) & (r % S5_STATE == c % S5_STATE)
        col_g = (col // S5_STATE) % S5_GROUPS
    else:
        sel = (r // S5_GROUP == c // BRANCH) & (r % S5_GROUP == c % S5_GROUP)
        col_g = (col // S5_GROUP) % S5_GROUPS
    sel = jnp.where(sel, 1.0, 0.0).astype(BF16)
    for i in range(n // step):
        rows = slice(i * step, (i + 1) * step)
        row = lax.broadcasted_iota(jnp.int32, (step, S5_ROW), 0) + i * step
        dense = jnp.dot(tab_ref[rows, :], sel, preferred_element_type=F32)
        out_scr[rows, :] = jnp.where((row // row_div) % S5_GROUPS == col_g, dense, 0.0).astype(BF16)


def _s5_kernel(u_ref, wc_ref, tk_ref, mc_ref, a_ref, y_ref, w_scr, t_scr, v_scr,
               carry_scr, z_scr, xp_scr):
    n = S5_NSTATE
    rows = u_ref.shape[0]

    @pl.when((pl.program_id(0) == 0) & (pl.program_id(1) == 0))
    def _():
        _s5_expand(wc_ref, w_scr, S5_GROUP, True)
        _s5_expand(tk_ref, t_scr, S5_GROUP, False)
        _s5_expand(mc_ref, v_scr, S5_STATE, False)

    @pl.when(pl.program_id(1) == 0)
    def _():
        carry_scr[...] = jnp.zeros_like(carry_scr)

    ub = u_ref[...].astype(BF16)
    z_scr[...] = jnp.dot(ub, w_scr[...], preferred_element_type=F32)
    ar = a_ref[0:1, :]
    ai = a_ref[1:2, :]

    def body(c, carry):
        xr, xi = carry
        xp_scr[pl.ds(c, 1), 0:n] = xr
        xp_scr[pl.ds(c, 1), n:2 * n] = xi
        zr = z_scr[pl.ds(c, 1), 0:n]
        zi = z_scr[pl.ds(c, 1), n:2 * n]
        return ar * xr - ai * xi + zr, ar * xi + ai * xr + zi

    xr, xi = lax.fori_loop(0, rows, body, (carry_scr[0:1, 0:n], carry_scr[0:1, n:2 * n]),
                           unroll=4)
    carry_scr[0:1, 0:n] = xr
    carry_scr[0:1, n:2 * n] = xi
    y_ref[...] = (jnp.dot(ub, t_scr[...], preferred_element_type=F32)
                  + jnp.dot(xp_scr[...].astype(BF16), v_scr[...], preferred_element_type=F32))


def _s5(u_flat, wc, tk, mc, a, rows):
    b, nrow, width = u_flat.shape
    const = lambda shape: pl.BlockSpec(shape, lambda bi, j: (0, 0))
    dense = lambda: pltpu.VMEM((width, width), BF16)
    return pl.pallas_call(
        _s5_kernel,
        out_shape=jax.ShapeDtypeStruct((b, nrow, width), F32),
        grid=(b, nrow // rows),
        in_specs=[pl.BlockSpec((None, rows, width), lambda bi, j: (bi, j, 0)),
                  const((width, 128)), const((width, 128)), const((width, 128)),
                  const((8, S5_NSTATE))],
        out_specs=pl.BlockSpec((None, rows, width), lambda bi, j: (bi, j, 0)),
        scratch_shapes=[dense(), dense(), dense(),
                        pltpu.VMEM((8, 2 * S5_NSTATE), F32),
                        pltpu.VMEM((rows, 2 * S5_NSTATE), F32),
                        pltpu.VMEM((rows, 2 * S5_NSTATE), F32)],
        compiler_params=pltpu.CompilerParams(
            dimension_semantics=("arbitrary", "arbitrary"), vmem_limit_bytes=VMEM_LIMIT_BYTES),
        name="s5",
    )(u_flat, wc, tk, mc, a)


def _tile_heads(x):
    return jnp.concatenate([x] * DN_HEADS, axis=0)


def _dnet_kernel(dn_ref, ng_ref, o_ref, s_scr, o_scr, u_scr, wq_scr, attn_scr, kd_scr, egl_scr):
    nb, tc, _ = dn_ref.shape
    c64 = DN_CHUNK
    w = DN_HEADS * DN_HEAD_DIM
    nck = tc // c64
    gb = DN_GROUP // nck

    @pl.when(pl.program_id(0) == 0)
    def _():
        s_scr[...] = jnp.zeros_like(s_scr)

    row_h = lax.broadcasted_iota(jnp.int32, (w, w), 0) // DN_HEAD_DIM
    col_h = lax.broadcasted_iota(jnp.int32, (w, w), 1) // DN_HEAD_DIM
    bd = row_h == col_h
    bd_bf = jnp.where(bd, 1.0, 0.0).astype(BF16)

    def mmb(a, b_bf):
        return jnp.dot(a.astype(BF16), b_bf, preferred_element_type=F32)

    def mmb_nt(a, b_bf):
        return lax.dot_general(a.astype(BF16), b_bf, (((1,), (1,)), ((), ())),
                               preferred_element_type=F32)

    ri = lax.broadcasted_iota(jnp.int32, (c64, w), 0)
    cj = lax.broadcasted_iota(jnp.int32, (c64, w), 1) % DN_HEAD_DIM
    causal = ri >= cj
    strict = ri > cj
    diag = ri == cj
    eye_all = jnp.where(diag, 1.0, 0.0).astype(F32)
    lt_bf = jnp.where(lax.broadcasted_iota(jnp.int32, (c64, c64), 0)
                      >= lax.broadcasted_iota(jnp.int32, (c64, c64), 1), 1.0, 0.0).astype(BF16)

    def blockdiag(x):
        return _tile_heads(x.astype(BF16)) * bd_bf

    grp = range(DN_GROUP)

    def prepare(gi, carry):
        def col(i, j):
            return dn_ref[gi * gb + i // nck, (i % nck) * c64:(i % nck + 1) * c64, j * w:(j + 1) * w]

        rows = [pl.ds(pl.multiple_of((gi * gb + i // nck) * tc + (i % nck) * c64, c64), c64)
                for i in grp]
        qc = [col(i, DN_Q) for i in grp]
        kc = [col(i, DN_K) for i in grp]
        bc = [col(i, DN_BETA) for i in grp]
        g = [col(i, DN_G) for i in grp]
        g_hi = [x.astype(BF16) for x in g]
        g_lo = [(g[i] - g_hi[i].astype(F32)).astype(BF16) for i in grp]
        gc = [jnp.dot(lt_bf, g_hi[i], preferred_element_type=F32)
              + jnp.dot(lt_bf, g_lo[i], preferred_element_type=F32) for i in grp]
        egc = [jnp.exp(x) for x in gc]
        gl = [x[c64 - 1:c64, :] for x in gc]
        kb = [kc[i] * bc[i] for i in grp]
        grow = [jnp.sum(jnp.where(diag, x, 0.0), axis=0, keepdims=True) for x in gc]
        decay = [jnp.exp(jnp.where(causal, gc[i] - grow[i], -jnp.inf)) for i in grp]
        k_bd = [blockdiag(x) for x in kc]
        kq = [mmb_nt(jnp.concatenate([kb[i], qc[i]], axis=0), k_bd[i]) for i in grp]
        for i in grp:
            attn_scr[rows[i], :] = kq[i][c64:2 * c64, :] * decay[i]
        x = [jnp.where(strict, -kq[i][0:c64, :] * decay[i], 0.0) for i in grp]
        p = [eye_all + x[i] for i in grp]
        x = [mmb(x[i], blockdiag(x[i])) for i in grp]
        for _ in range(4):
            xp = [mmb(jnp.concatenate([x[i], p[i]], axis=0), blockdiag(x[i])) for i in grp]
            x = [xp[i][0:c64, :] for i in grp]
            p = [p[i] + xp[i][c64:2 * c64, :] for i in grp]
        p = [p[i] + mmb(p[i], blockdiag(x[i])) for i in grp]
        for i in grp:
            ci = gi * DN_GROUP + i
            u_scr[rows[i], :] = mmb(p[i], blockdiag(col(i, DN_V)))
            wq_scr[ci, 0:c64, :] = mmb(p[i], blockdiag(kb[i] * egc[i]))
            wq_scr[ci, c64:2 * c64, :] = qc[i] * egc[i]
            kd_scr[rows[i], :] = kc[i] * jnp.exp(gl[i] - gc[i])
            egl_scr[ci] = jnp.broadcast_to(jnp.exp(gl[i]), (8, w))
        return carry

    lax.fori_loop(0, nb // gb, prepare, 0)

    bs = range(nb)

    def recur(c, carry):
        rows = [pl.ds(pl.multiple_of(b * tc + c * c64, c64), c64) for b in bs]
        s = [s_scr[b] for b in bs]
        ws_qs = [_mm(wq_scr[b * nck + c], s[b]) for b in bs]
        v_new = [u_scr[rows[b], :] - ws_qs[b][0:c64, :] for b in bs]
        for b in bs:
            o_scr[rows[b], :] = (ws_qs[b][c64:2 * c64, :]
                                 + mmb(attn_scr[rows[b], :], blockdiag(v_new[b])))
        for b in bs:
            s_scr[b] = (s[b] * egl_scr[b * nck + c][0:1, :]
                        + jnp.where(bd, _mm_tn(kd_scr[rows[b], :], v_new[b]), 0.0))
        return carry

    lax.fori_loop(0, nck, recur, 0)

    o = o_scr[...]
    ms = mmb(o * o, bd_bf) * (1.0 / DN_HEAD_DIM)
    o_ref[...] = (o * lax.rsqrt(ms + NORM_EPS) * ng_ref[...]).reshape(nb, tc, w)


def _dnet(dn3, ng, tc):
    b, l, _ = dn3.shape
    w = BRANCH
    nchunk = b * tc // DN_CHUNK
    assert DN_GROUP % (tc // DN_CHUNK) == 0 and nchunk % DN_GROUP == 0
    act = lambda: pltpu.VMEM((b * tc, w), F32)
    return pl.pallas_call(
        _dnet_kernel,
        out_shape=jax.ShapeDtypeStruct((b, l, w), F32),
        grid=(l // tc,),
        in_specs=[pl.BlockSpec((b, tc, DN_COLS * w), lambda i: (0, i, 0)),
                  pl.BlockSpec((1, w), lambda i: (0, 0))],
        out_specs=pl.BlockSpec((b, tc, w), lambda i: (0, i, 0)),
        scratch_shapes=[pltpu.VMEM((b, w, w), F32),
                        act(), act(),
                        pltpu.VMEM((nchunk, 2 * DN_CHUNK, w), F32),
                        act(), act(),
                        pltpu.VMEM((nchunk, 8, w), F32)],
        compiler_params=pltpu.CompilerParams(
            dimension_semantics=("arbitrary",), vmem_limit_bytes=VMEM_LIMIT_BYTES),
        name="dnet",
    )(dn3, ng)


def _mixout_kernel(act_ref, ahalo_ref, chalo_ref, ys_ref, od_ref, x_ref,
                   acw_ref, avec_ref, apw_ref, bvec_ref, bglu_ref, ccw_ref, wout_ref, fg_ref,
                   out_ref, ash_scr, cpad_scr, *, final, sub):
    tc = x_ref.shape[0]
    w = BRANCH
    first = pl.program_id(1) == 0

    def act(j, rows=slice(None)):
        return act_ref[rows, j * w:(j + 1) * w]

    ash_scr[0, 0:32, :] = jnp.where(first, 0.0, ahalo_ref[...])
    ash_scr[0, 32:32 + tc, :] = act(ACT_A)
    for r in range(1, 8):
        ash_scr[r, 0:tc + 24, :] = ash_scr[0, pl.ds(r, tc + 24), :]
    cpad_scr[0:8, :] = jnp.where(first, 0.0, chalo_ref[...])
    cpad_scr[8:8 + tc, :] = act(ACT_CX)

    conv_b, ln_g, ln_b, pw_b = (avec_ref[0:1, :], avec_ref[1:2, :], avec_ref[2:3, :],
                                avec_ref[3:4, :])
    d_skip, glu_b = bvec_ref[0:1, :], bvec_ref[1:2, :]

    for r in range(tc // sub):
        r0 = r * sub
        rows = slice(r0, r0 + sub)
        acc = conv_b
        for kk in range(CONF_KERNEL):
            off = 32 - (CONF_KERNEL - 1) + kk
            acc = acc + acw_ref[kk:kk + 1, :] * ash_scr[off % 8, pl.ds(r0 + off - off % 8, sub), :]
        mu = jnp.mean(acc, axis=-1, keepdims=True)
        xc = acc - mu
        ln = xc * lax.rsqrt(jnp.mean(xc * xc, axis=-1, keepdims=True) + NORM_EPS) * ln_g + ln_b
        ya = (_mm(_silu(ln), apw_ref[...]) + pw_b) * act(ACT_AZ, rows)
        yb = jax.nn.gelu(ys_ref[rows, :] + d_skip * act(ACT_BU, rows))
        yb = yb * _sigmoid(_mm(yb, bglu_ref[...]) + glu_b) * act(ACT_BZ, rows)
        cacc = ccw_ref[0:1, :] * cpad_scr[pl.ds(r0 + 8 - (SC_KERNEL - 1), sub), :]
        for kk in range(1, SC_KERNEL):
            cacc = cacc + ccw_ref[kk:kk + 1, :] * cpad_scr[
                pl.ds(r0 + 8 - (SC_KERNEL - 1) + kk, sub), :]
        yc = act(ACT_CB, rows) * cacc
        yd = od_ref[rows, :] * act(ACT_DZ, rows)
        y = (x_ref[rows, :]
             + _mm(ya, wout_ref[0:w, :]) + _mm(yb, wout_ref[w:2 * w, :])
             + _mm(yc, wout_ref[2 * w:3 * w, :]) + _mm(yd, wout_ref[3 * w:4 * w, :]))
        if final:
            y = y * lax.rsqrt(jnp.mean(y * y, axis=-1, keepdims=True) + NORM_EPS) * fg_ref[...]
        out_ref[rows, :] = y


def _mixout(act3, ys, od, x3, acw, avec, apw, bvec, bglu, ccw, wout, fg, tc, final):
    b, l, d = x3.shape
    w = BRANCH
    blk = lambda width, col: pl.BlockSpec((None, tc, width), lambda bi, i: (bi, i, col))
    halo = lambda nrow, col: pl.BlockSpec(
        (None, nrow, w), lambda bi, i: (bi, jnp.maximum(i * (tc // nrow) - 1, 0), col))
    const = lambda shape: pl.BlockSpec(shape, lambda bi, i: (0, 0))
    return pl.pallas_call(
        functools.partial(_mixout_kernel, final=final, sub=128),
        out_shape=jax.ShapeDtypeStruct((b, l, d), F32),
        grid=(b, l // tc),
        in_specs=[blk(ACT_COLS * w, 0), halo(32, ACT_A), halo(8, ACT_CX),
                  blk(w, 0), blk(w, 0), blk(d, 0),
                  const((32, w)), const((8, w)), const((w, w)), const((8, w)), const((w, w)),
                  const((8, w)), const((4 * w, d)), const((1, d))],
        out_specs=pl.BlockSpec((None, tc, d), lambda bi, i: (bi, i, 0)),
        scratch_shapes=[pltpu.VMEM((8, tc + 32, w), F32), pltpu.VMEM((tc + 8, w), F32)],
        compiler_params=pltpu.CompilerParams(
            dimension_semantics=("parallel", "parallel"), vmem_limit_bytes=VMEM_LIMIT_BYTES),
        name="mixout_final" if final else "mixout",
    )(act3, act3, act3, ys, od, x3, acw, avec, apw, bvec, bglu, ccw, wout, fg)


def _pad_rows(a, n):
    return jnp.pad(a, ((0, 0), (0, n - a.shape[1]), (0, 0)))


def _s5_tables(lam_re, lam_im, b_re, b_im, c_re, c_im, log_dt):
    q, g, p, h = S5_Q, S5_GROUPS, S5_STATE, S5_GROUP
    dep = lam_re.shape[0]
    lr = jnp.minimum(lam_re, -1e-4)
    li = lam_im
    dt = jnp.exp(log_dt)[..., None]
    lrdt, lidt = (lr * dt)[:, None], (li * dt)[:, None]

    def power(steps):
        s = steps[None, :, None, None]
        mag = jnp.exp(lrdt * s)
        return mag * jnp.cos(lidt * s), mag * jnp.sin(lidt * s)

    tok = jnp.arange(q, dtype=F32)
    p1r, p1i = power(jnp.ones((1,), F32))
    nr, ni = p1r[:, 0] - 1.0, p1i[:, 0]
    den = lr * lr + li * li
    fr, fi = (nr * lr + ni * li) / den, (ni * lr - nr * li) / den
    bbr = fr[..., None] * b_re - fi[..., None] * b_im
    bbi = fr[..., None] * b_im + fi[..., None] * b_re

    rr, ri = power(q - 1.0 - tok)
    rr, ri = rr[:, :, :, None, :], ri[:, :, :, None, :]
    br_t = jnp.transpose(bbr, (0, 1, 3, 2))[:, None]
    bi_t = jnp.transpose(bbi, (0, 1, 3, 2))[:, None]
    wc = jnp.concatenate([rr * br_t - ri * bi_t, rr * bi_t + ri * br_t], axis=-1)
    wc = wc.reshape(dep, q * g * h, 2 * p)

    qr, qi = power(tok + 1.0)
    qr = jnp.transpose(qr, (0, 2, 3, 1))[..., None]
    qi = jnp.transpose(qi, (0, 2, 3, 1))[..., None]
    cr_t = jnp.transpose(c_re, (0, 1, 3, 2))[:, :, :, None, :]
    ci_t = jnp.transpose(c_im, (0, 1, 3, 2))[:, :, :, None, :]
    mc = jnp.concatenate([cr_t * qr - ci_t * qi, -(cr_t * qi + ci_t * qr)], axis=1)
    mc = mc.reshape(dep, 2 * g * p, q * h)

    er, ei = power(tok)
    er, ei = er[:, :, :, None, :], ei[:, :, :, None, :]
    cpr = c_re[:, None] * er - c_im[:, None] * ei
    cpi = c_re[:, None] * ei + c_im[:, None] * er
    kd = (jnp.einsum("deghp,dgpk->deghk", cpr, bbr, precision=HIGHEST)
          - jnp.einsum("deghp,dgpk->deghk", cpi, bbi, precision=HIGHEST))
    ii = jnp.arange(q)
    lag_is = (ii[None, None, :] - ii[None, :, None] == ii[:, None, None]).astype(F32)
    tk = jnp.einsum("eji,deghk->djgkih", lag_is, kd, precision=HIGHEST)
    tk = tk.reshape(dep, q * g * h, q * h)

    ar, ai = power(jnp.full((1,), float(q), F32))
    a = jnp.concatenate([ar.reshape(dep, 1, g * p), ai.reshape(dep, 1, g * p),
                         jnp.zeros((dep, 6, g * p), F32)], axis=1)
    return wc.astype(BF16), tk.astype(BF16), mc.astype(BF16), a


def _expand_w_in(w_in):
    w = BRANCH
    blocks = [w_in[..., i * w:(i + 1) * w] for i in range(12)]
    (a_val, a_gate, a_z, b_u, b_z, c_b, c_c, c_x, c_z, d_q, d_k, d_v) = blocks
    alpha = jnp.repeat(w_in[..., 12 * w:12 * w + DN_HEADS], DN_HEAD_DIM, axis=-1)
    beta = jnp.repeat(w_in[..., 12 * w + DN_HEADS:12 * w + 2 * DN_HEADS], DN_HEAD_DIM, axis=-1)
    d_z = w_in[..., 12 * w + 2 * DN_HEADS:]
    return jnp.concatenate([a_val, a_gate, a_z, b_u, b_z, d_z, c_b, c_c, c_x, c_z,
                            d_q, d_k, d_v, alpha, beta], axis=-1).astype(BF16)


def kernel(x, norm_g, w_in, a_conv_w, a_conv_b, a_ln_g, a_ln_b, a_pw_w, a_pw_b, s5_lambda_re, s5_lambda_im, s5_b_re, s5_b_im, s5_c_re, s5_c_im, s5_d, s5_log_dt, s5_glu_w, s5_glu_b, c_conv_w, d_conv_w, d_a_log, d_dt_bias, d_norm_g, w_out, final_g):
    bsz, seq, d = x.shape
    depth = w_in.shape[0]
    w = BRANCH
    tm = min(512, seq)
    tc = min(512, seq)
    dn_tc = min(256, seq)
    s5_rows = min(256, seq // S5_Q)

    w_exp = _expand_w_in(w_in)
    s5_wc, s5_tk, s5_mc, s5_a = _s5_tables(s5_lambda_re, s5_lambda_im, s5_b_re, s5_b_im,
                                           s5_c_re, s5_c_im, s5_log_dt)
    acw = _pad_rows(a_conv_w, 32)
    zeros = jnp.zeros_like(a_conv_b)
    avec = jnp.stack([a_conv_b, a_ln_g, a_ln_b, a_pw_b, zeros, zeros, zeros, zeros], axis=1)
    bvec = jnp.stack([s5_d, s5_glu_b, zeros, zeros, zeros, zeros, zeros, zeros], axis=1)
    ccw = _pad_rows(c_conv_w, 8)
    dcw = _pad_rows(d_conv_w, 8)
    alog = jnp.repeat(d_a_log, DN_HEAD_DIM, axis=-1)[:, None, :]
    dtb = jnp.repeat(d_dt_bias, DN_HEAD_DIM, axis=-1)[:, None, :]
    ng = jnp.tile(d_norm_g, (1, DN_HEADS))[:, None, :]
    apw = a_pw_w.astype(BF16)
    bglu = s5_glu_w.astype(BF16)
    wout = w_out.astype(BF16)
    fg = final_g[None, :]

    for l in range(depth):
        act, dn, u = _inproj(x.reshape(bsz * seq, d), norm_g[l][None, :], w_exp[l], dcw[l],
                             alog[l], dtb[l], tm, seq)
        ys = _s5(u.reshape(bsz, seq // S5_Q, S5_ROW), s5_wc[l], s5_tk[l], s5_mc[l], s5_a[l],
                 s5_rows).reshape(bsz, seq, w)
        od = _dnet(dn.reshape(bsz, seq, DN_COLS * w), ng[l], dn_tc)
        x = _mixout(act.reshape(bsz, seq, ACT_COLS * w), ys, od, x, acw[l], avec[l], apw[l],
                    bvec[l], bglu[l], ccw[l], wout[l], fg, tc, l == depth - 1)
    return x
```

```python
import functools

import jax
import jax.numpy as jnp
from jax import lax
from jax.experimental import pallas as pl
from jax.experimental.pallas import tpu as pltpu

F32 = jnp.float32
BF16 = jnp.bfloat16
HIGHEST = lax.Precision.HIGHEST

NORM_EPS = 1e-6
BRANCH = 256
CONF_KERNEL = 31
SC_KERNEL = 3
DN_HEADS = 4
DN_HEAD_DIM = 64
DN_CONV = 4
DN_CHUNK = 64
DN_GROUP = 8
S5_GROUPS = 16
S5_GROUP = 16
S5_STATE = 64
S5_Q = 8
S5_ROW = S5_Q * BRANCH
S5_NSTATE = S5_GROUPS * S5_STATE

COL_A, COL_B, COL_C, COL_QKV, COL_AB = 0, 3, 5, 9, 12
PROJ_COLS = 15 * BRANCH
HALF = BRANCH // 2
ACT_A, ACT_AZ, ACT_BZ, ACT_CB, ACT_CX, ACT_DZ, ACT_COLS = 0, 1, 2, 3, 4, 5, 6
DN_Q, DN_K, DN_V, DN_BETA, DN_G, DN_COLS = 0, 1, 2, 3, 4, 5

VMEM_LIMIT_BYTES = 56 * 1024 * 1024


def _mm(a, b):
    return jnp.dot(a.astype(BF16), b.astype(BF16), preferred_element_type=F32)


def _mm_tn(a, b):
    return lax.dot_general(a.astype(BF16), b.astype(BF16), (((0,), (0,)), ((), ())),
                           preferred_element_type=F32)


def _sigmoid(x):
    return 1.0 / (1.0 + jnp.exp(-x))


def _silu(x):
    return x * _sigmoid(x)


def _inproj_kernel(x_ref, g_ref, w_ref, dcw_ref, alog_ref, dtb_ref, act_ref, dn_ref, ulo_ref,
                   uhi_ref, pad_scr, *, steps_per_seq, sub):
    tm = x_ref.shape[0]
    w = BRANCH
    first = pl.program_id(0) % steps_per_seq == 0

    @pl.when(first)
    def _():
        pad_scr[tm:tm + 8, :] = jnp.zeros((8, 3 * w), F32)

    pad_scr[0:8, :] = pad_scr[tm:tm + 8, :]

    row_h = lax.broadcasted_iota(jnp.int32, (w, w), 0) // DN_HEAD_DIM
    col_h = lax.broadcasted_iota(jnp.int32, (w, w), 1) // DN_HEAD_DIM
    ones_bd = jnp.where(row_h == col_h, 1.0, 0.0).astype(BF16)

    def project(hb, col, n):
        return jnp.dot(hb, w_ref[:, col * w:(col + n) * w], preferred_element_type=F32)

    def put(ref, rows, j, val):
        ref[rows, j * w:(j + 1) * w] = val

    def deltanet_inputs(rows, r0):
        acc = dcw_ref[0:1, :] * pad_scr[pl.ds(r0 + 8 - (DN_CONV - 1), sub), :]
        for kk in range(1, DN_CONV):
            acc = acc + dcw_ref[kk:kk + 1, :] * pad_scr[pl.ds(r0 + 8 - (DN_CONV - 1) + kk, sub), :]
        qkv = _silu(acc)
        q = qkv[:, 0:w]
        k = qkv[:, w:2 * w]
        ssq = jnp.dot((q * q).astype(BF16), ones_bd, preferred_element_type=F32)
        ssk = jnp.dot((k * k).astype(BF16), ones_bd, preferred_element_type=F32)
        put(dn_ref, rows, DN_Q, q * lax.rsqrt(ssq + NORM_EPS) * (DN_HEAD_DIM ** -0.5))
        put(dn_ref, rows, DN_K, k * lax.rsqrt(ssk + NORM_EPS))
        put(dn_ref, rows, DN_V, qkv[:, 2 * w:3 * w] * dn_ref[rows, DN_BETA * w:(DN_BETA + 1) * w])

    for r in range(tm // sub):
        r0 = r * sub
        rows = slice(r0, r0 + sub)
        x = x_ref[rows, :]
        h = x * lax.rsqrt(jnp.mean(x * x, axis=-1, keepdims=True) + NORM_EPS) * g_ref[...]
        hb = h.astype(BF16)
        pad_scr[r0 + 8:r0 + 8 + sub, :] = project(hb, COL_QKV, 3)
        p = project(hb, COL_A, 3)
        put(act_ref, rows, ACT_A, p[:, 0:w] * _sigmoid(p[:, w:2 * w]))
        put(act_ref, rows, ACT_AZ, _silu(p[:, 2 * w:3 * w]))
        p = project(hb, COL_B, 2)
        ulo_ref[rows, :] = p[:, 0:HALF]
        uhi_ref[rows, :] = p[:, HALF:w]
        put(act_ref, rows, ACT_BZ, _silu(p[:, w:2 * w]))
        p = project(hb, COL_C, 4)
        put(act_ref, rows, ACT_CB, p[:, 0:w] * _silu(p[:, 3 * w:4 * w]))
        put(act_ref, rows, ACT_CX, p[:, w:2 * w] * p[:, 2 * w:3 * w])
        p = project(hb, COL_AB, 3)
        put(act_ref, rows, ACT_DZ, _silu(p[:, 2 * w:3 * w]))
        sp_in = p[:, 0:w] + dtb_ref[...]
        softplus = jnp.maximum(sp_in, 0.0) + jnp.log1p(jnp.exp(-jnp.abs(sp_in)))
        put(dn_ref, rows, DN_G, -jnp.exp(alog_ref[...]) * softplus)
        put(dn_ref, rows, DN_BETA, _sigmoid(p[:, w:2 * w]))
        if r > 0:
            deltanet_inputs(slice(r0 - sub, r0), r0 - sub)
    deltanet_inputs(slice(tm - sub, tm), tm - sub)


def _inproj(x2d, g, w, dcw, alog, dtb, tm, seq):
    t, d = x2d.shape
    const = lambda shape: pl.BlockSpec(shape, lambda i: (0, 0))
    out = lambda n: pl.BlockSpec((tm, n * BRANCH), lambda i: (i, 0))
    return pl.pallas_call(
        functools.partial(_inproj_kernel, steps_per_seq=seq // tm, sub=min(256, tm)),
        out_shape=(jax.ShapeDtypeStruct((t, ACT_COLS * BRANCH), F32),
                   jax.ShapeDtypeStruct((t, DN_COLS * BRANCH), F32),
                   jax.ShapeDtypeStruct((t, HALF), F32),
                   jax.ShapeDtypeStruct((t, HALF), F32)),
        grid=(t // tm,),
        in_specs=[pl.BlockSpec((tm, d), lambda i: (i, 0)),
                  const((1, d)), const((d, PROJ_COLS)),
                  const((8, 3 * BRANCH)), const((1, BRANCH)), const((1, BRANCH))],
        out_specs=(out(ACT_COLS), out(DN_COLS), pl.BlockSpec((tm, HALF), lambda i: (i, 0)),
                   pl.BlockSpec((tm, HALF), lambda i: (i, 0))),
        scratch_shapes=[pltpu.VMEM((tm + 8, 3 * BRANCH), F32)],
        compiler_params=pltpu.CompilerParams(
            dimension_semantics=("arbitrary",), vmem_limit_bytes=VMEM_LIMIT_BYTES),
        name="inproj",
    )(x2d, g, w, dcw, alog, dtb)


def _s5_expand(tab_ref, out_scr, row_div, by_state):
    n, step = out_scr.shape[0], 256
    r = lax.broadcasted_iota(jnp.int32, (128, S5_ROW), 0)
    c = lax.broadcasted_iota(jnp.int32, (128, S5_ROW), 1)
    col = lax.broadcasted_iota(jnp.int32, (step, S5_ROW), 1)
    if by_state:
        sel = (r // S5_STATE == c // S5_NSTATE) & (r % S5_STATE == c % S5_STATE)
        col_g = (col // S5_STATE) % S5_GROUPS
    else:
        sel = (r // S5_GROUP == c // BRANCH) & (r % S5_GROUP == c % S5_GROUP)
        col_g = (col // S5_GROUP) % S5_GROUPS
    sel = jnp.where(sel, 1.0, 0.0).astype(BF16)
    for i in range(n // step):
        rows = slice(i * step, (i + 1) * step)
        row = lax.broadcasted_iota(jnp.int32, (step, S5_ROW), 0) + i * step
        dense = jnp.dot(tab_ref[rows, :], sel, preferred_element_type=F32)
        out_scr[rows, :] = jnp.where((row // row_div) % S5_GROUPS == col_g, dense, 0.0).astype(BF16)


def _s5_kernel(ulo_ref, uhi_ref, wc_ref, tk_ref, mc_ref, a_ref, ylo_ref, yhi_ref,
               w_scr, t_scr, v_scr, carry_scr, z_scr, xp_scr):
    n = S5_NSTATE
    rows = ulo_ref.shape[0] // S5_Q

    @pl.when((pl.program_id(0) == 0) & (pl.program_id(1) == 0))
    def _():
        _s5_expand(wc_ref, w_scr, S5_GROUP, True)
        _s5_expand(tk_ref, t_scr, S5_GROUP, False)
        _s5_expand(mc_ref, v_scr, S5_STATE, False)

    @pl.when(pl.program_id(1) == 0)
    def _():
        carry_scr[...] = jnp.zeros_like(carry_scr)

    ub = jnp.concatenate(
        [half[pl.ds(j, rows, stride=S5_Q), :] for j in range(S5_Q) for half in (ulo_ref, uhi_ref)],
        axis=1).astype(BF16)
    z_scr[...] = jnp.dot(ub, w_scr[...], preferred_element_type=F32)
    ar = a_ref[0:1, :]
    ai = a_ref[1:2, :]

    def body(c, carry):
        xr, xi = carry
        xp_scr[pl.ds(c, 1), 0:n] = xr
        xp_scr[pl.ds(c, 1), n:2 * n] = xi
        zr = z_scr[pl.ds(c, 1), 0:n]
        zi = z_scr[pl.ds(c, 1), n:2 * n]
        return ar * xr - ai * xi + zr, ar * xi + ai * xr + zi

    xr, xi = lax.fori_loop(0, rows, body, (carry_scr[0:1, 0:n], carry_scr[0:1, n:2 * n]),
                           unroll=4)
    carry_scr[0:1, 0:n] = xr
    carry_scr[0:1, n:2 * n] = xi
    y = (jnp.dot(ub, t_scr[...], preferred_element_type=F32)
         + jnp.dot(xp_scr[...].astype(BF16), v_scr[...], preferred_element_type=F32))
    for i in range(S5_Q):
        ylo_ref[pl.ds(i, rows, stride=S5_Q), :] = y[:, i * BRANCH:i * BRANCH + HALF]
        yhi_ref[pl.ds(i, rows, stride=S5_Q), :] = y[:, i * BRANCH + HALF:(i + 1) * BRANCH]


def _s5(ulo, uhi, wc, tk, mc, a, rows):
    b, l, _ = ulo.shape
    width = S5_ROW
    const = lambda shape: pl.BlockSpec(shape, lambda bi, j: (0, 0))
    half = lambda: pl.BlockSpec((None, rows * S5_Q, HALF), lambda bi, j: (bi, j, 0))
    dense = lambda: pltpu.VMEM((width, width), BF16)
    return pl.pallas_call(
        _s5_kernel,
        out_shape=(jax.ShapeDtypeStruct((b, l, HALF), F32), jax.ShapeDtypeStruct((b, l, HALF), F32)),
        grid=(b, l // (rows * S5_Q)),
        in_specs=[half(), half(),
                  const((width, 128)), const((width, 128)), const((width, 128)),
                  const((8, S5_NSTATE))],
        out_specs=(half(), half()),
        scratch_shapes=[dense(), dense(), dense(),
                        pltpu.VMEM((8, 2 * S5_NSTATE), F32),
                        pltpu.VMEM((rows, 2 * S5_NSTATE), F32),
                        pltpu.VMEM((rows, 2 * S5_NSTATE), F32)],
        compiler_params=pltpu.CompilerParams(
            dimension_semantics=("arbitrary", "arbitrary"), vmem_limit_bytes=VMEM_LIMIT_BYTES),
        name="s5",
    )(ulo, uhi, wc, tk, mc, a)


def _tile_heads(x):
    return jnp.concatenate([x] * DN_HEADS, axis=0)


def _dnet_kernel(dn_ref, ng_ref, o_ref, s_scr, o_scr, u_scr, wq_scr, attn_scr, kd_scr, egl_scr):
    nb, tc, _ = dn_ref.shape
    c64 = DN_CHUNK
    w = DN_HEADS * DN_HEAD_DIM
    nck = tc // c64
    gb = DN_GROUP // nck

    @pl.when(pl.program_id(0) == 0)
    def _():
        s_scr[...] = jnp.zeros_like(s_scr)

    row_h = lax.broadcasted_iota(jnp.int32, (w, w), 0) // DN_HEAD_DIM
    col_h = lax.broadcasted_iota(jnp.int32, (w, w), 1) // DN_HEAD_DIM
    bd = row_h == col_h
    bd_bf = jnp.where(bd, 1.0, 0.0).astype(BF16)

    def mmb(a, b_bf):
        return jnp.dot(a.astype(BF16), b_bf, preferred_element_type=F32)

    def mmb_nt(a, b_bf):
        return lax.dot_general(a.astype(BF16), b_bf, (((1,), (1,)), ((), ())),
                               preferred_element_type=F32)

    ri = lax.broadcasted_iota(jnp.int32, (c64, w), 0)
    cj = lax.broadcasted_iota(jnp.int32, (c64, w), 1) % DN_HEAD_DIM
    causal = ri >= cj
    strict = ri > cj
    diag = ri == cj
    eye_all = jnp.where(diag, 1.0, 0.0).astype(F32)
    lt_bf = jnp.where(lax.broadcasted_iota(jnp.int32, (c64, c64), 0)
                      >= lax.broadcasted_iota(jnp.int32, (c64, c64), 1), 1.0, 0.0).astype(BF16)

    def blockdiag(x):
        return _tile_heads(x.astype(BF16)) * bd_bf

    grp = range(DN_GROUP)

    def prepare(gi, carry):
        def col(i, j):
            return dn_ref[gi * gb + i // nck, (i % nck) * c64:(i % nck + 1) * c64, j * w:(j + 1) * w]

        rows = [pl.ds(pl.multiple_of((gi * gb + i // nck) * tc + (i % nck) * c64, c64), c64)
                for i in grp]
        qc = [col(i, DN_Q) for i in grp]
        kc = [col(i, DN_K) for i in grp]
        bc = [col(i, DN_BETA) for i in grp]
        g = [col(i, DN_G) for i in grp]
        g_hi = [x.astype(BF16) for x in g]
        g_lo = [(g[i] - g_hi[i].astype(F32)).astype(BF16) for i in grp]
        gc = [jnp.dot(lt_bf, g_hi[i], preferred_element_type=F32)
              + jnp.dot(lt_bf, g_lo[i], preferred_element_type=F32) for i in grp]
        egc = [jnp.exp(x) for x in gc]
        gl = [x[c64 - 1:c64, :] for x in gc]
        kb = [kc[i] * bc[i] for i in grp]
        grow = [jnp.sum(jnp.where(diag, x, 0.0), axis=0, keepdims=True) for x in gc]
        decay = [jnp.exp(jnp.where(causal, gc[i] - grow[i], -jnp.inf)) for i in grp]
        k_bd = [blockdiag(x) for x in kc]
        kq = [mmb_nt(jnp.concatenate([kb[i], qc[i]], axis=0), k_bd[i]) for i in grp]
        for i in grp:
            attn_scr[rows[i], :] = kq[i][c64:2 * c64, :] * decay[i]
        x = [jnp.where(strict, -kq[i][0:c64, :] * decay[i], 0.0) for i in grp]
        p = [eye_all + x[i] for i in grp]
        x = [mmb(x[i], blockdiag(x[i])) for i in grp]
        for _ in range(4):
            xp = [mmb(jnp.concatenate([x[i], p[i]], axis=0), blockdiag(x[i])) for i in grp]
            x = [xp[i][0:c64, :] for i in grp]
            p = [p[i] + xp[i][c64:2 * c64, :] for i in grp]
        p = [p[i] + mmb(p[i], blockdiag(x[i])) for i in grp]
        for i in grp:
            ci = gi * DN_GROUP + i
            u_scr[rows[i], :] = mmb(p[i], blockdiag(col(i, DN_V)))
            wq_scr[ci, 0:c64, :] = mmb(p[i], blockdiag(kb[i] * egc[i]))
            wq_scr[ci, c64:2 * c64, :] = qc[i] * egc[i]
            kd_scr[rows[i], :] = kc[i] * jnp.exp(gl[i] - gc[i])
            egl_scr[ci] = jnp.broadcast_to(jnp.exp(gl[i]), (8, w))
        return carry

    lax.fori_loop(0, nb // gb, prepare, 0)

    bs = range(nb)

    def recur(c, carry):
        rows = [pl.ds(pl.multiple_of(b * tc + c * c64, c64), c64) for b in bs]
        s = [s_scr[b] for b in bs]
        ws_qs = [_mm(wq_scr[b * nck + c], s[b]) for b in bs]
        v_new = [u_scr[rows[b], :] - ws_qs[b][0:c64, :] for b in bs]
        for b in bs:
            o_scr[rows[b], :] = (ws_qs[b][c64:2 * c64, :]
                                 + mmb(attn_scr[rows[b], :], blockdiag(v_new[b])))
        for b in bs:
            s_scr[b] = (s[b] * egl_scr[b * nck + c][0:1, :]
                        + jnp.where(bd, _mm_tn(kd_scr[rows[b], :], v_new[b]), 0.0))
        return carry

    lax.fori_loop(0, nck, recur, 0)

    o = o_scr[...]
    ms = mmb(o * o, bd_bf) * (1.0 / DN_HEAD_DIM)
    o_ref[...] = (o * lax.rsqrt(ms + NORM_EPS) * ng_ref[...]).reshape(nb, tc, w)


def _dnet(dn3, ng, tc):
    b, l, _ = dn3.shape
    w = BRANCH
    nchunk = b * tc // DN_CHUNK
    assert DN_GROUP % (tc // DN_CHUNK) == 0 and nchunk % DN_GROUP == 0
    act = lambda: pltpu.VMEM((b * tc, w), F32)
    return pl.pallas_call(
        _dnet_kernel,
        out_shape=jax.ShapeDtypeStruct((b, l, w), F32),
        grid=(l // tc,),
        in_specs=[pl.BlockSpec((b, tc, DN_COLS * w), lambda i: (0, i, 0)),
                  pl.BlockSpec((1, w), lambda i: (0, 0))],
        out_specs=pl.BlockSpec((b, tc, w), lambda i: (0, i, 0)),
        scratch_shapes=[pltpu.VMEM((b, w, w), F32),
                        act(), act(),
                        pltpu.VMEM((nchunk, 2 * DN_CHUNK, w), F32),
                        act(), act(),
                        pltpu.VMEM((nchunk, 8, w), F32)],
        compiler_params=pltpu.CompilerParams(
            dimension_semantics=("arbitrary",), vmem_limit_bytes=VMEM_LIMIT_BYTES),
        name="dnet",
    )(dn3, ng)


def _mixout_kernel(act_ref, ahalo_ref, chalo_ref, ulo_ref, uhi_ref, ylo_ref, yhi_ref, od_ref, x_ref,
                   acw_ref, avec_ref, apw_ref, bvec_ref, bglu_ref, ccw_ref, wout_ref, fg_ref,
                   out_ref, ash_scr, cpad_scr, *, final, sub):
    tc = x_ref.shape[0]
    w = BRANCH
    first = pl.program_id(1) == 0

    def act(j, rows=slice(None)):
        return act_ref[rows, j * w:(j + 1) * w]

    ash_scr[0, 0:32, :] = jnp.where(first, 0.0, ahalo_ref[...])
    ash_scr[0, 32:32 + tc, :] = act(ACT_A)
    for r in range(1, 8):
        ash_scr[r, 0:tc + 24, :] = ash_scr[0, pl.ds(r, tc + 24), :]
    cpad_scr[0:8, :] = jnp.where(first, 0.0, chalo_ref[...])
    cpad_scr[8:8 + tc, :] = act(ACT_CX)

    conv_b, ln_g, ln_b, pw_b = (avec_ref[0:1, :], avec_ref[1:2, :], avec_ref[2:3, :],
                                avec_ref[3:4, :])
    d_skip, glu_b = bvec_ref[0:1, :], bvec_ref[1:2, :]

    for r in range(tc // sub):
        r0 = r * sub
        rows = slice(r0, r0 + sub)
        acc = conv_b
        for kk in range(CONF_KERNEL):
            off = 32 - (CONF_KERNEL - 1) + kk
            acc = acc + acw_ref[kk:kk + 1, :] * ash_scr[off % 8, pl.ds(r0 + off - off % 8, sub), :]
        mu = jnp.mean(acc, axis=-1, keepdims=True)
        xc = acc - mu
        ln = xc * lax.rsqrt(jnp.mean(xc * xc, axis=-1, keepdims=True) + NORM_EPS) * ln_g + ln_b
        ya = (_mm(_silu(ln), apw_ref[...]) + pw_b) * act(ACT_AZ, rows)
        ub = jnp.concatenate([ulo_ref[rows, :], uhi_ref[rows, :]], axis=1)
        ys = jnp.concatenate([ylo_ref[rows, :], yhi_ref[rows, :]], axis=1)
        yb = jax.nn.gelu(ys + d_skip * ub)
        yb = yb * _sigmoid(_mm(yb, bglu_ref[...]) + glu_b) * act(ACT_BZ, rows)
        cacc = ccw_ref[0:1, :] * cpad_scr[pl.ds(r0 + 8 - (SC_KERNEL - 1), sub), :]
        for kk in range(1, SC_KERNEL):
            cacc = cacc + ccw_ref[kk:kk + 1, :] * cpad_scr[
                pl.ds(r0 + 8 - (SC_KERNEL - 1) + kk, sub), :]
        yc = act(ACT_CB, rows) * cacc
        yd = od_ref[rows, :] * act(ACT_DZ, rows)
        y = (x_ref[rows, :]
             + _mm(ya, wout_ref[0:w, :]) + _mm(yb, wout_ref[w:2 * w, :])
             + _mm(yc, wout_ref[2 * w:3 * w, :]) + _mm(yd, wout_ref[3 * w:4 * w, :]))
        if final:
            y = y * lax.rsqrt(jnp.mean(y * y, axis=-1, keepdims=True) + NORM_EPS) * fg_ref[...]
        out_ref[rows, :] = y


def _mixout(act3, ulo, uhi, ylo, yhi, od, x3, acw, avec, apw, bvec, bglu, ccw, wout, fg, tc, final):
    b, l, d = x3.shape
    w = BRANCH
    blk = lambda width, col: pl.BlockSpec((None, tc, width), lambda bi, i: (bi, i, col))
    halo = lambda nrow, col: pl.BlockSpec(
        (None, nrow, w), lambda bi, i: (bi, jnp.maximum(i * (tc // nrow) - 1, 0), col))
    const = lambda shape: pl.BlockSpec(shape, lambda bi, i: (0, 0))
    return pl.pallas_call(
        functools.partial(_mixout_kernel, final=final, sub=128),
        out_shape=jax.ShapeDtypeStruct((b, l, d), F32),
        grid=(b, l // tc),
        in_specs=[blk(ACT_COLS * w, 0), halo(32, ACT_A), halo(8, ACT_CX),
                  blk(HALF, 0), blk(HALF, 0), blk(HALF, 0), blk(HALF, 0), blk(w, 0), blk(d, 0),
                  const((32, w)), const((8, w)), const((w, w)), const((8, w)), const((w, w)),
                  const((8, w)), const((4 * w, d)), const((1, d))],
        out_specs=pl.BlockSpec((None, tc, d), lambda bi, i: (bi, i, 0)),
        scratch_shapes=[pltpu.VMEM((8, tc + 32, w), F32), pltpu.VMEM((tc + 8, w), F32)],
        compiler_params=pltpu.CompilerParams(
            dimension_semantics=("parallel", "parallel"), vmem_limit_bytes=VMEM_LIMIT_BYTES),
        name="mixout_final" if final else "mixout",
    )(act3, act3, act3, ulo, uhi, ylo, yhi, od, x3, acw, avec, apw, bvec, bglu, ccw, wout, fg)


def _pad_rows(a, n):
    return jnp.pad(a, ((0, 0), (0, n - a.shape[1]), (0, 0)))


def _s5_tables(lam_re, lam_im, b_re, b_im, c_re, c_im, log_dt):
    q, g, p, h = S5_Q, S5_GROUPS, S5_STATE, S5_GROUP
    dep = lam_re.shape[0]
    lr = jnp.minimum(lam_re, -1e-4)
    li = lam_im
    dt = jnp.exp(log_dt)[..., None]
    lrdt, lidt = (lr * dt)[:, None], (li * dt)[:, None]

    def power(steps):
        s = steps[None, :, None, None]
        mag = jnp.exp(lrdt * s)
        return mag * jnp.cos(lidt * s), mag * jnp.sin(lidt * s)

    tok = jnp.arange(q, dtype=F32)
    p1r, p1i = power(jnp.ones((1,), F32))
    nr, ni = p1r[:, 0] - 1.0, p1i[:, 0]
    den = lr * lr + li * li
    fr, fi = (nr * lr + ni * li) / den, (ni * lr - nr * li) / den
    bbr = fr[..., None] * b_re - fi[..., None] * b_im
    bbi = fr[..., None] * b_im + fi[..., None] * b_re

    rr, ri = power(q - 1.0 - tok)
    rr, ri = rr[:, :, :, None, :], ri[:, :, :, None, :]
    br_t = jnp.transpose(bbr, (0, 1, 3, 2))[:, None]
    bi_t = jnp.transpose(bbi, (0, 1, 3, 2))[:, None]
    wc = jnp.concatenate([rr * br_t - ri * bi_t, rr * bi_t + ri * br_t], axis=-1)
    wc = wc.reshape(dep, q * g * h, 2 * p)

    qr, qi = power(tok + 1.0)
    qr = jnp.transpose(qr, (0, 2, 3, 1))[..., None]
    qi = jnp.transpose(qi, (0, 2, 3, 1))[..., None]
    cr_t = jnp.transpose(c_re, (0, 1, 3, 2))[:, :, :, None, :]
    ci_t = jnp.transpose(c_im, (0, 1, 3, 2))[:, :, :, None, :]
    mc = jnp.concatenate([cr_t * qr - ci_t * qi, -(cr_t * qi + ci_t * qr)], axis=1)
    mc = mc.reshape(dep, 2 * g * p, q * h)

    er, ei = power(tok)
    er, ei = er[:, :, :, None, :], ei[:, :, :, None, :]
    cpr = c_re[:, None] * er - c_im[:, None] * ei
    cpi = c_re[:, None] * ei + c_im[:, None] * er
    kd = (jnp.einsum("deghp,dgpk->deghk", cpr, bbr, precision=HIGHEST)
          - jnp.einsum("deghp,dgpk->deghk", cpi, bbi, precision=HIGHEST))
    ii = jnp.arange(q)
    lag_is = (ii[None, None, :] - ii[None, :, None] == ii[:, None, None]).astype(F32)
    tk = jnp.einsum("eji,deghk->djgkih", lag_is, kd, precision=HIGHEST)
    tk = tk.reshape(dep, q * g * h, q * h)

    ar, ai = power(jnp.full((1,), float(q), F32))
    a = jnp.concatenate([ar.reshape(dep, 1, g * p), ai.reshape(dep, 1, g * p),
                         jnp.zeros((dep, 6, g * p), F32)], axis=1)
    return wc.astype(BF16), tk.astype(BF16), mc.astype(BF16), a


def _expand_w_in(w_in):
    n = 12 * BRANCH
    heads = jnp.repeat(w_in[..., n:n + 2 * DN_HEADS], DN_HEAD_DIM, axis=-1)
    return jnp.concatenate([w_in[..., :n], heads, w_in[..., n + 2 * DN_HEADS:]],
                           axis=-1).astype(BF16)


def kernel(x, norm_g, w_in, a_conv_w, a_conv_b, a_ln_g, a_ln_b, a_pw_w, a_pw_b, s5_lambda_re, s5_lambda_im, s5_b_re, s5_b_im, s5_c_re, s5_c_im, s5_d, s5_log_dt, s5_glu_w, s5_glu_b, c_conv_w, d_conv_w, d_a_log, d_dt_bias, d_norm_g, w_out, final_g):
    bsz, seq, d = x.shape
    depth = w_in.shape[0]
    w = BRANCH
    tm = min(512, seq)
    tc = min(512, seq)
    dn_tc = min(256, seq)
    s5_rows = min(256, seq // S5_Q)

    w_exp = _expand_w_in(w_in)
    s5_wc, s5_tk, s5_mc, s5_a = _s5_tables(s5_lambda_re, s5_lambda_im, s5_b_re, s5_b_im,
                                           s5_c_re, s5_c_im, s5_log_dt)
    acw = _pad_rows(a_conv_w, 32)
    zeros = jnp.zeros_like(a_conv_b)
    avec = jnp.stack([a_conv_b, a_ln_g, a_ln_b, a_pw_b, zeros, zeros, zeros, zeros], axis=1)
    bvec = jnp.stack([s5_d, s5_glu_b, zeros, zeros, zeros, zeros, zeros, zeros], axis=1)
    ccw = _pad_rows(c_conv_w, 8)
    dcw = _pad_rows(d_conv_w, 8)
    alog = jnp.repeat(d_a_log, DN_HEAD_DIM, axis=-1)[:, None, :]
    dtb = jnp.repeat(d_dt_bias, DN_HEAD_DIM, axis=-1)[:, None, :]
    ng = jnp.tile(d_norm_g, (1, DN_HEADS))[:, None, :]
    apw = a_pw_w.astype(BF16)
    bglu = s5_glu_w.astype(BF16)
    wout = w_out.astype(BF16)
    fg = final_g[None, :]

    def seq3(a):
        return a.reshape(bsz, seq, a.shape[-1])

    for l in range(depth):
        act, dn, ulo, uhi = _inproj(x.reshape(bsz * seq, d), norm_g[l][None, :], w_exp[l], dcw[l],
                                    alog[l], dtb[l], tm, seq)
        ulo, uhi = seq3(ulo), seq3(uhi)
        ylo, yhi = _s5(ulo, uhi, s5_wc[l], s5_tk[l], s5_mc[l], s5_a[l], s5_rows)
        od = _dnet(seq3(dn), ng[l], dn_tc)
        x = _mixout(seq3(act), ulo, uhi, ylo, yhi, od, x, acw[l], avec[l], apw[l],
                    bvec[l], bglu[l], ccw[l], wout[l], fg, tc, l == depth - 1)
    return x
```

```python
import functools

import jax
import jax.numpy as jnp
from jax import lax
from jax.experimental import pallas as pl
from jax.experimental.pallas import tpu as pltpu

F32 = jnp.float32
BF16 = jnp.bfloat16
HIGHEST = lax.Precision.HIGHEST

NORM_EPS = 1e-6
BRANCH = 256
CONF_KERNEL = 31
SC_KERNEL = 3
DN_HEADS = 4
DN_HEAD_DIM = 64
DN_CONV = 4
DN_CHUNK = 64
DN_GROUP = 8
S5_GROUPS = 16
S5_GROUP = 16
S5_STATE = 64
S5_Q = 8
S5_ROW = S5_Q * BRANCH
S5_NSTATE = S5_GROUPS * S5_STATE

COL_A, COL_B, COL_C, COL_QKV, MAIN_BLOCKS = 0, 3, 5, 9, 12
IN_COLS = MAIN_BLOCKS * BRANCH + 2 * DN_HEADS + BRANCH
TAIL_COLS = 3 * BRANCH
HALF = BRANCH // 2
ACT_A, ACT_AZ, ACT_BZ, ACT_CB, ACT_CX, ACT_DZ, ACT_COLS = 0, 1, 2, 3, 4, 5, 6
DN_Q, DN_K, DN_V, DN_BETA, DN_G, DN_COLS = 0, 1, 2, 3, 4, 5

VMEM_LIMIT_BYTES = 56 * 1024 * 1024


def _mm(a, b):
    return jnp.dot(a.astype(BF16), b.astype(BF16), preferred_element_type=F32)


def _mm_tn(a, b):
    return lax.dot_general(a.astype(BF16), b.astype(BF16), (((0,), (0,)), ((), ())),
                           preferred_element_type=F32)


def _layer_spec(layer, shape):
    zeros = (0,) * len(shape)
    return pl.BlockSpec((None,) + tuple(shape), lambda *_: (layer,) + zeros)


def _sigmoid(x):
    return 1.0 / (1.0 + jnp.exp(-x))


def _silu(x):
    return x * _sigmoid(x)


def _inproj_kernel(x_ref, g_ref, w_ref, wt_ref, dcw_ref, alog_ref, dtb_ref, act_ref, dn_ref, ulo_ref,
                   uhi_ref, pad_scr, *, steps_per_seq, sub):
    tm = x_ref.shape[0]
    w = BRANCH
    first = pl.program_id(0) % steps_per_seq == 0

    @pl.when(first)
    def _():
        pad_scr[tm:tm + 8, :] = jnp.zeros((8, 3 * w), F32)

    pad_scr[0:8, :] = pad_scr[tm:tm + 8, :]

    row_h = lax.broadcasted_iota(jnp.int32, (w, w), 0) // DN_HEAD_DIM
    col_h = lax.broadcasted_iota(jnp.int32, (w, w), 1) // DN_HEAD_DIM
    ones_bd = jnp.where(row_h == col_h, 1.0, 0.0).astype(BF16)

    def project(hb, col, n):
        return jnp.dot(hb, w_ref[:, col * w:(col + n) * w], preferred_element_type=F32)

    def put(ref, rows, j, val):
        ref[rows, j * w:(j + 1) * w] = val

    def deltanet_inputs(rows, r0):
        acc = dcw_ref[0:1, :] * pad_scr[pl.ds(r0 + 8 - (DN_CONV - 1), sub), :]
        for kk in range(1, DN_CONV):
            acc = acc + dcw_ref[kk:kk + 1, :] * pad_scr[pl.ds(r0 + 8 - (DN_CONV - 1) + kk, sub), :]
        qkv = _silu(acc)
        q = qkv[:, 0:w]
        k = qkv[:, w:2 * w]
        ssq = jnp.dot((q * q).astype(BF16), ones_bd, preferred_element_type=F32)
        ssk = jnp.dot((k * k).astype(BF16), ones_bd, preferred_element_type=F32)
        put(dn_ref, rows, DN_Q, q * lax.rsqrt(ssq + NORM_EPS) * (DN_HEAD_DIM ** -0.5))
        put(dn_ref, rows, DN_K, k * lax.rsqrt(ssk + NORM_EPS))
        put(dn_ref, rows, DN_V, qkv[:, 2 * w:3 * w] * dn_ref[rows, DN_BETA * w:(DN_BETA + 1) * w])

    for r in range(tm // sub):
        r0 = r * sub
        rows = slice(r0, r0 + sub)
        x = x_ref[rows, :]
        h = x * lax.rsqrt(jnp.mean(x * x, axis=-1, keepdims=True) + NORM_EPS) * g_ref[...]
        hb = h.astype(BF16)
        pad_scr[r0 + 8:r0 + 8 + sub, :] = project(hb, COL_QKV, 3)
        p = project(hb, COL_A, 3)
        put(act_ref, rows, ACT_A, p[:, 0:w] * _sigmoid(p[:, w:2 * w]))
        put(act_ref, rows, ACT_AZ, _silu(p[:, 2 * w:3 * w]))
        p = project(hb, COL_B, 2)
        ulo_ref[rows, :] = p[:, 0:HALF]
        uhi_ref[rows, :] = p[:, HALF:w]
        put(act_ref, rows, ACT_BZ, _silu(p[:, w:2 * w]))
        p = project(hb, COL_C, 4)
        put(act_ref, rows, ACT_CB, p[:, 0:w] * _silu(p[:, 3 * w:4 * w]))
        put(act_ref, rows, ACT_CX, p[:, w:2 * w] * p[:, 2 * w:3 * w])
        p = jnp.dot(hb, wt_ref[...], preferred_element_type=F32)
        put(act_ref, rows, ACT_DZ, _silu(p[:, 2 * w:3 * w]))
        sp_in = p[:, 0:w] + dtb_ref[...]
        softplus = jnp.maximum(sp_in, 0.0) + jnp.log1p(jnp.exp(-jnp.abs(sp_in)))
        put(dn_ref, rows, DN_G, -jnp.exp(alog_ref[...]) * softplus)
        put(dn_ref, rows, DN_BETA, _sigmoid(p[:, w:2 * w]))
        if r > 0:
            deltanet_inputs(slice(r0 - sub, r0), r0 - sub)
    deltanet_inputs(slice(tm - sub, tm), tm - sub)


def _inproj(layer, x2d, g, w, wt, dcw, alog, dtb, tm, seq):
    t, d = x2d.shape
    const = functools.partial(_layer_spec, layer)
    out = lambda n: pl.BlockSpec((tm, n * BRANCH), lambda i: (i, 0))
    return pl.pallas_call(
        functools.partial(_inproj_kernel, steps_per_seq=seq // tm, sub=min(256, tm)),
        out_shape=(jax.ShapeDtypeStruct((t, ACT_COLS * BRANCH), F32),
                   jax.ShapeDtypeStruct((t, DN_COLS * BRANCH), F32),
                   jax.ShapeDtypeStruct((t, HALF), F32),
                   jax.ShapeDtypeStruct((t, HALF), F32)),
        grid=(t // tm,),
        in_specs=[pl.BlockSpec((tm, d), lambda i: (i, 0)),
                  const((1, d)), const((d, IN_COLS)), const((d, TAIL_COLS)),
                  const((8, 3 * BRANCH)), const((1, BRANCH)), const((1, BRANCH))],
        out_specs=(out(ACT_COLS), out(DN_COLS), pl.BlockSpec((tm, HALF), lambda i: (i, 0)),
                   pl.BlockSpec((tm, HALF), lambda i: (i, 0))),
        scratch_shapes=[pltpu.VMEM((tm + 8, 3 * BRANCH), F32)],
        compiler_params=pltpu.CompilerParams(
            dimension_semantics=("arbitrary",), vmem_limit_bytes=VMEM_LIMIT_BYTES),
        name="inproj",
    )(x2d, g, w, wt, dcw, alog, dtb)


S5_TILES = S5_GROUPS // 2


def _block_transpose(pieces, block):
    lane = lax.broadcasted_iota(jnp.int32, pieces[0].shape, 1)
    n = len(pieces)
    s = n // 2
    while s:
        width = s * block
        lower = lane % (2 * width) < width
        out = list(pieces)
        for j in range(n):
            if j & s == 0:
                a, b = pieces[j], pieces[j + s]
                out[j] = jnp.where(lower, a, pltpu.roll(b, width, 1))
                out[j + s] = jnp.where(lower, pltpu.roll(a, 128 - width, 1), b)
        pieces = out
        s //= 2
    return pieces


def _s5_kernel(ulo_ref, uhi_ref, wd_ref, td_ref, vdt_ref, a_ref, ylo_ref, yhi_ref,
               carry_scr, z_scr, xp_scr):
    rows = ulo_ref.shape[0] // S5_Q
    n = S5_NSTATE
    tiles_per_half = S5_TILES // 2
    lanes = 2 * S5_GROUP

    @pl.when(pl.program_id(1) == 0)
    def _():
        carry_scr[...] = jnp.zeros_like(carry_scr)

    ub = []
    for half in (ulo_ref, uhi_ref):
        tok = [half[pl.ds(j, rows, stride=S5_Q), :] for j in range(S5_Q)]
        first = _block_transpose(tok[0:4], lanes)
        second = _block_transpose(tok[4:8], lanes)
        ub += [jnp.concatenate([first[k], second[k]], axis=1).astype(BF16)
               for k in range(tiles_per_half)]
    for k in range(S5_TILES):
        z = jnp.dot(ub[k], wd_ref[k], preferred_element_type=F32)
        z_scr[:, k * 128:(k + 1) * 128] = z[:, 0:128]
        z_scr[:, n + k * 128:n + (k + 1) * 128] = z[:, 128:256]

    ar = a_ref[0:1, :]
    ai = a_ref[1:2, :]

    def body(c, carry):
        xr, xi = carry
        xp_scr[pl.ds(c, 1), 0:n] = xr
        xp_scr[pl.ds(c, 1), n:2 * n] = xi
        zr = z_scr[pl.ds(c, 1), 0:n]
        zi = z_scr[pl.ds(c, 1), n:2 * n]
        return ar * xr - ai * xi + zr, ar * xi + ai * xr + zi

    xr, xi = lax.fori_loop(0, rows, body, (carry_scr[0:1, 0:n], carry_scr[0:1, n:2 * n]),
                           unroll=4)
    carry_scr[0:1, 0:n] = xr
    carry_scr[0:1, n:2 * n] = xi

    ys = []
    for k in range(S5_TILES):
        xp = jnp.concatenate([xp_scr[:, k * 128:(k + 1) * 128],
                              xp_scr[:, n + k * 128:n + (k + 1) * 128]], axis=1).astype(BF16)
        ys.append(jnp.dot(ub[k], td_ref[k], preferred_element_type=F32)
                  + lax.dot_general(xp, vdt_ref[k], (((1,), (1,)), ((), ())),
                                    preferred_element_type=F32))
    for h, out_ref in enumerate((ylo_ref, yhi_ref)):
        tiles = ys[h * tiles_per_half:(h + 1) * tiles_per_half]
        for m in range(2):
            tok = _block_transpose([y[:, m * 128:(m + 1) * 128] for y in tiles], lanes)
            for b in range(4):
                out_ref[pl.ds(4 * m + b, rows, stride=S5_Q), :] = tok[b]


def _s5(layer, ulo, uhi, wd, td, vd, a, rows):
    b, l, _ = ulo.shape
    const = functools.partial(_layer_spec, layer)
    half = lambda: pl.BlockSpec((None, rows * S5_Q, HALF), lambda bi, j: (bi, j, 0))
    tiles = (S5_TILES, BRANCH, BRANCH)
    return pl.pallas_call(
        _s5_kernel,
        out_shape=(jax.ShapeDtypeStruct((b, l, HALF), F32), jax.ShapeDtypeStruct((b, l, HALF), F32)),
        grid=(b, l // (rows * S5_Q)),
        in_specs=[half(), half(), const(tiles), const(tiles), const(tiles), const((8, S5_NSTATE))],
        out_specs=(half(), half()),
        scratch_shapes=[pltpu.VMEM((8, 2 * S5_NSTATE), F32),
                        pltpu.VMEM((rows, 2 * S5_NSTATE), F32),
                        pltpu.VMEM((rows, 2 * S5_NSTATE), F32)],
        compiler_params=pltpu.CompilerParams(
            dimension_semantics=("parallel", "arbitrary"), vmem_limit_bytes=VMEM_LIMIT_BYTES),
        name="s5",
    )(ulo, uhi, wd, td, vd, a)


def _tile_heads(x):
    return jnp.concatenate([x] * DN_HEADS, axis=0)


def _dnet_kernel(dn_ref, ng_ref, o_ref, s_scr, o_scr, u_scr, wq_scr, attn_scr, kd_scr, egl_scr):
    nb, tc, _ = dn_ref.shape
    c64 = DN_CHUNK
    w = DN_HEADS * DN_HEAD_DIM
    nck = tc // c64
    gb = DN_GROUP // nck

    @pl.when(pl.program_id(0) == 0)
    def _():
        s_scr[...] = jnp.zeros_like(s_scr)

    row_h = lax.broadcasted_iota(jnp.int32, (w, w), 0) // DN_HEAD_DIM
    col_h = lax.broadcasted_iota(jnp.int32, (w, w), 1) // DN_HEAD_DIM
    bd = row_h == col_h
    bd_bf = jnp.where(bd, 1.0, 0.0).astype(BF16)

    def mmb(a, b_bf):
        return jnp.dot(a.astype(BF16), b_bf, preferred_element_type=F32)

    def mmb_nt(a, b_bf):
        return lax.dot_general(a.astype(BF16), b_bf, (((1,), (1,)), ((), ())),
                               preferred_element_type=F32)

    ri = lax.broadcasted_iota(jnp.int32, (c64, w), 0)
    cj = lax.broadcasted_iota(jnp.int32, (c64, w), 1) % DN_HEAD_DIM
    causal = ri >= cj
    strict = ri > cj
    diag = ri == cj
    eye_all = jnp.where(diag, 1.0, 0.0).astype(F32)
    lt_bf = jnp.where(lax.broadcasted_iota(jnp.int32, (c64, c64), 0)
                      >= lax.broadcasted_iota(jnp.int32, (c64, c64), 1), 1.0, 0.0).astype(BF16)

    def blockdiag(x):
        return _tile_heads(x.astype(BF16)) * bd_bf

    grp = range(DN_GROUP)

    def prepare(gi, carry):
        def col(i, j):
            return dn_ref[gi * gb + i // nck, (i % nck) * c64:(i % nck + 1) * c64, j * w:(j + 1) * w]

        rows = [pl.ds(pl.multiple_of((gi * gb + i // nck) * tc + (i % nck) * c64, c64), c64)
                for i in grp]
        qc = [col(i, DN_Q) for i in grp]
        kc = [col(i, DN_K) for i in grp]
        bc = [col(i, DN_BETA) for i in grp]
        g = [col(i, DN_G) for i in grp]
        g_hi = [x.astype(BF16) for x in g]
        g_lo = [(g[i] - g_hi[i].astype(F32)).astype(BF16) for i in grp]
        gc = [jnp.dot(lt_bf, g_hi[i], preferred_element_type=F32)
              + jnp.dot(lt_bf, g_lo[i], preferred_element_type=F32) for i in grp]
        egc = [jnp.exp(x) for x in gc]
        gl = [x[c64 - 1:c64, :] for x in gc]
        kb = [kc[i] * bc[i] for i in grp]
        grow = [jnp.sum(jnp.where(diag, x, 0.0), axis=0, keepdims=True) for x in gc]
        decay = [jnp.exp(jnp.where(causal, gc[i] - grow[i], -jnp.inf)) for i in grp]
        k_bd = [blockdiag(x) for x in kc]
        kq = [mmb_nt(jnp.concatenate([kb[i], qc[i]], axis=0), k_bd[i]) for i in grp]
        for i in grp:
            attn_scr[rows[i], :] = kq[i][c64:2 * c64, :] * decay[i]
        x = [jnp.where(strict, -kq[i][0:c64, :] * decay[i], 0.0) for i in grp]
        p = [eye_all + x[i] for i in grp]
        x = [mmb(x[i], blockdiag(x[i])) for i in grp]
        for _ in range(4):
            xp = [mmb(jnp.concatenate([x[i], p[i]], axis=0), blockdiag(x[i])) for i in grp]
            x = [xp[i][0:c64, :] for i in grp]
            p = [p[i] + xp[i][c64:2 * c64, :] for i in grp]
        p = [p[i] + mmb(p[i], blockdiag(x[i])) for i in grp]
        for i in grp:
            ci = gi * DN_GROUP + i
            u_scr[rows[i], :] = mmb(p[i], blockdiag(col(i, DN_V)))
            wq_scr[ci, 0:c64, :] = mmb(p[i], blockdiag(kb[i] * egc[i]))
            wq_scr[ci, c64:2 * c64, :] = qc[i] * egc[i]
            kd_scr[rows[i], :] = kc[i] * jnp.exp(gl[i] - gc[i])
            egl_scr[ci] = jnp.broadcast_to(jnp.exp(gl[i]), (8, w))
        return carry

    lax.fori_loop(0, nb // gb, prepare, 0)

    bs = range(nb)

    def recur(c, carry):
        rows = [pl.ds(pl.multiple_of(b * tc + c * c64, c64), c64) for b in bs]
        s = [s_scr[b] for b in bs]
        ws_qs = [_mm(wq_scr[b * nck + c], s[b]) for b in bs]
        v_new = [u_scr[rows[b], :] - ws_qs[b][0:c64, :] for b in bs]
        for b in bs:
            o_scr[rows[b], :] = (ws_qs[b][c64:2 * c64, :]
                                 + mmb(attn_scr[rows[b], :], blockdiag(v_new[b])))
        for b in bs:
            s_scr[b] = (s[b] * egl_scr[b * nck + c][0:1, :]
                        + jnp.where(bd, _mm_tn(kd_scr[rows[b], :], v_new[b]), 0.0))
        return carry

    lax.fori_loop(0, nck, recur, 0)

    o = o_scr[...]
    ms = mmb(o * o, bd_bf) * (1.0 / DN_HEAD_DIM)
    o_ref[...] = (o * lax.rsqrt(ms + NORM_EPS) * ng_ref[...]).reshape(nb, tc, w)


def _dnet(layer, dn3, ng, tc):
    b, l, _ = dn3.shape
    w = BRANCH
    nchunk = b * tc // DN_CHUNK
    assert DN_GROUP % (tc // DN_CHUNK) == 0 and nchunk % DN_GROUP == 0
    act = lambda: pltpu.VMEM((b * tc, w), F32)
    return pl.pallas_call(
        _dnet_kernel,
        out_shape=jax.ShapeDtypeStruct((b, l, w), F32),
        grid=(l // tc,),
        in_specs=[pl.BlockSpec((b, tc, DN_COLS * w), lambda i: (0, i, 0)),
                  _layer_spec(layer, (1, w))],
        out_specs=pl.BlockSpec((b, tc, w), lambda i: (0, i, 0)),
        scratch_shapes=[pltpu.VMEM((b, w, w), F32),
                        act(), act(),
                        pltpu.VMEM((nchunk, 2 * DN_CHUNK, w), F32),
                        act(), act(),
                        pltpu.VMEM((nchunk, 8, w), F32)],
        compiler_params=pltpu.CompilerParams(
            dimension_semantics=("arbitrary",), vmem_limit_bytes=VMEM_LIMIT_BYTES),
        name="dnet",
    )(dn3, ng)


def _mixout_kernel(act_ref, ahalo_ref, chalo_ref, ulo_ref, uhi_ref, ylo_ref, yhi_ref, od_ref, x_ref,
                   acw_ref, avec_ref, apw_ref, bvec_ref, bglu_ref, ccw_ref, wout_ref, fg_ref,
                   out_ref, ash_scr, cpad_scr, *, final, sub):
    tc = x_ref.shape[0]
    w = BRANCH
    first = pl.program_id(1) == 0

    def act(j, rows=slice(None)):
        return act_ref[rows, j * w:(j + 1) * w]

    ash_scr[0, 0:32, :] = jnp.where(first, 0.0, ahalo_ref[...])
    ash_scr[0, 32:32 + tc, :] = act(ACT_A)
    for r in range(1, 8):
        ash_scr[r, 0:tc + 24, :] = ash_scr[0, pl.ds(r, tc + 24), :]
    cpad_scr[0:8, :] = jnp.where(first, 0.0, chalo_ref[...])
    cpad_scr[8:8 + tc, :] = act(ACT_CX)

    conv_b, ln_g, ln_b, pw_b = (avec_ref[0:1, :], avec_ref[1:2, :], avec_ref[2:3, :],
                                avec_ref[3:4, :])
    d_skip, glu_b = bvec_ref[0:1, :], bvec_ref[1:2, :]

    for r in range(tc // sub):
        r0 = r * sub
        rows = slice(r0, r0 + sub)
        acc = conv_b
        for kk in range(CONF_KERNEL):
            off = 32 - (CONF_KERNEL - 1) + kk
            acc = acc + acw_ref[kk:kk + 1, :] * ash_scr[off % 8, pl.ds(r0 + off - off % 8, sub), :]
        mu = jnp.mean(acc, axis=-1, keepdims=True)
        xc = acc - mu
        ln = xc * lax.rsqrt(jnp.mean(xc * xc, axis=-1, keepdims=True) + NORM_EPS) * ln_g + ln_b
        ya = (_mm(_silu(ln), apw_ref[...]) + pw_b) * act(ACT_AZ, rows)
        ub = jnp.concatenate([ulo_ref[rows, :], uhi_ref[rows, :]], axis=1)
        ys = jnp.concatenate([ylo_ref[rows, :], yhi_ref[rows, :]], axis=1)
        yb = jax.nn.gelu(ys + d_skip * ub)
        yb = yb * _sigmoid(_mm(yb, bglu_ref[...]) + glu_b) * act(ACT_BZ, rows)
        cacc = ccw_ref[0:1, :] * cpad_scr[pl.ds(r0 + 8 - (SC_KERNEL - 1), sub), :]
        for kk in range(1, SC_KERNEL):
            cacc = cacc + ccw_ref[kk:kk + 1, :] * cpad_scr[
                pl.ds(r0 + 8 - (SC_KERNEL - 1) + kk, sub), :]
        yc = act(ACT_CB, rows) * cacc
        yd = od_ref[rows, :] * act(ACT_DZ, rows)
        y = (x_ref[rows, :]
             + _mm(ya, wout_ref[0:w, :]) + _mm(yb, wout_ref[w:2 * w, :])
             + _mm(yc, wout_ref[2 * w:3 * w, :]) + _mm(yd, wout_ref[3 * w:4 * w, :]))
        if final:
            y = y * lax.rsqrt(jnp.mean(y * y, axis=-1, keepdims=True) + NORM_EPS) * fg_ref[...]
        out_ref[rows, :] = y


def _mixout(layer, act3, ulo, uhi, ylo, yhi, od, x3, acw, avec, apw, bvec, bglu, ccw, wout, fg, tc,
            final):
    b, l, d = x3.shape
    w = BRANCH
    blk = lambda width, col: pl.BlockSpec((None, tc, width), lambda bi, i: (bi, i, col))
    halo = lambda nrow, col: pl.BlockSpec(
        (None, nrow, w), lambda bi, i: (bi, jnp.maximum(i * (tc // nrow) - 1, 0), col))
    const = functools.partial(_layer_spec, layer)
    return pl.pallas_call(
        functools.partial(_mixout_kernel, final=final, sub=128),
        out_shape=jax.ShapeDtypeStruct((b, l, d), F32),
        grid=(b, l // tc),
        in_specs=[blk(ACT_COLS * w, 0), halo(32, ACT_A), halo(8, ACT_CX),
                  blk(HALF, 0), blk(HALF, 0), blk(HALF, 0), blk(HALF, 0), blk(w, 0), blk(d, 0),
                  const((32, w)), const((8, w)), const((w, w)), const((8, w)), const((w, w)),
                  const((8, w)), const((4 * w, d)), pl.BlockSpec((1, d), lambda bi, i: (0, 0))],
        out_specs=pl.BlockSpec((None, tc, d), lambda bi, i: (bi, i, 0)),
        scratch_shapes=[pltpu.VMEM((8, tc + 32, w), F32), pltpu.VMEM((tc + 8, w), F32)],
        compiler_params=pltpu.CompilerParams(
            dimension_semantics=("parallel", "parallel"), vmem_limit_bytes=VMEM_LIMIT_BYTES),
        name="mixout_final" if final else "mixout",
    )(act3, act3, act3, ulo, uhi, ylo, yhi, od, x3, acw, avec, apw, bvec, bglu, ccw, wout, fg)


def _pad_rows(a, n):
    return jnp.pad(a, ((0, 0), (0, n - a.shape[1]), (0, 0)))


def _s5_tiles(tab, state_cols):
    dep = tab.shape[0]
    nt, rb = S5_TILES, 2 * S5_GROUP
    t = tab.reshape(dep, S5_Q, nt, rb, 128)
    t = jnp.transpose(t, (0, 2, 1, 3, 4)).reshape(dep, nt, S5_Q * rb, 128)
    per = S5_STATE if state_cols else S5_GROUP
    src = jnp.arange(128)
    dst = jnp.arange(2 * 128)
    spread = ((src[:, None] // per == dst[None, :] // (2 * per))
              & (src[:, None] % per == dst[None, :] % per)).astype(F32)
    row_group = (jnp.arange(S5_Q * rb) % rb) // S5_GROUP
    col_group = (dst % (2 * per)) // per
    mask = (row_group[:, None] == col_group[None, :]).astype(F32)
    return (jnp.einsum("dkrc,cn->dkrn", t, spread, precision=HIGHEST) * mask).astype(BF16)


def _s5_tables(lam_re, lam_im, b_re, b_im, c_re, c_im, log_dt):
    q, g, p, h = S5_Q, S5_GROUPS, S5_STATE, S5_GROUP
    dep = lam_re.shape[0]
    lr = jnp.minimum(lam_re, -1e-4)
    li = lam_im
    dt = jnp.exp(log_dt)[..., None]
    lrdt, lidt = (lr * dt)[:, None], (li * dt)[:, None]

    def power(steps):
        s = steps[None, :, None, None]
        mag = jnp.exp(lrdt * s)
        return (mag * jnp.cos(lidt * s))[:, :, :, None, :], (mag * jnp.sin(lidt * s))[:, :, :, None, :]

    tok = jnp.arange(q, dtype=F32)
    p1r, p1i = power(jnp.ones((1,), F32))
    nr, ni = p1r[:, 0, :, 0] - 1.0, p1i[:, 0, :, 0]
    den = lr * lr + li * li
    fr, fi = (nr * lr + ni * li) / den, (ni * lr - nr * li) / den
    bbr = jnp.transpose(fr[..., None] * b_re - fi[..., None] * b_im, (0, 1, 3, 2))[:, None]
    bbi = jnp.transpose(fr[..., None] * b_im + fi[..., None] * b_re, (0, 1, 3, 2))[:, None]
    cr, ci = c_re[:, None], c_im[:, None]

    rr, ri = power(q - 1.0 - tok)
    wc = jnp.concatenate([rr * bbr - ri * bbi, rr * bbi + ri * bbr], axis=-1)
    qr, qi = power(tok + 1.0)
    mc = jnp.concatenate([cr * qr - ci * qi, -(cr * qi + ci * qr)], axis=-1)
    er, ei = power(tok)
    cpr = jnp.transpose(cr * er - ci * ei, (0, 2, 1, 3, 4)).reshape(dep, g, q * h, p)
    cpi = jnp.transpose(cr * ei + ci * er, (0, 2, 1, 3, 4)).reshape(dep, g, q * h, p)
    kd = (jnp.einsum("dgcp,dgyp->dgcy", bbr[:, 0], cpr, precision=HIGHEST)
          - jnp.einsum("dgcp,dgyp->dgcy", bbi[:, 0], cpi, precision=HIGHEST))
    lane = jnp.arange(q * h)
    tk = jnp.stack([jnp.where(lane >= j * h, jnp.roll(kd, j * h, axis=-1), 0.0) for j in range(q)],
                   axis=1)

    flat = lambda t: t.reshape(dep, q * g * h, t.shape[-1])
    ar, ai = power(jnp.full((1,), float(q), F32))
    a = jnp.concatenate([ar.reshape(dep, 1, g * p), ai.reshape(dep, 1, g * p),
                         jnp.zeros((dep, 6, g * p), F32)], axis=1)
    return _s5_tiles(flat(wc), True), _s5_tiles(flat(tk), False), _s5_tiles(flat(mc), True), a


def _w_in_tail(w_in):
    n = MAIN_BLOCKS * BRANCH
    heads = jnp.repeat(w_in[..., n:n + 2 * DN_HEADS], DN_HEAD_DIM, axis=-1)
    return jnp.concatenate([heads, w_in[..., n + 2 * DN_HEADS:]], axis=-1).astype(BF16)


def kernel(x, norm_g, w_in, a_conv_w, a_conv_b, a_ln_g, a_ln_b, a_pw_w, a_pw_b, s5_lambda_re, s5_lambda_im, s5_b_re, s5_b_im, s5_c_re, s5_c_im, s5_d, s5_log_dt, s5_glu_w, s5_glu_b, c_conv_w, d_conv_w, d_a_log, d_dt_bias, d_norm_g, w_out, final_g):
    bsz, seq, d = x.shape
    depth = w_in.shape[0]
    w = BRANCH
    tm = min(512, seq)
    tc = min(512, seq)
    dn_tc = min(256, seq)
    s5_rows = min(256, seq // S5_Q)

    w_bf = w_in.astype(BF16)
    w_tail = _w_in_tail(w_in)
    s5_wd, s5_td, s5_vd, s5_a = _s5_tables(s5_lambda_re, s5_lambda_im, s5_b_re, s5_b_im,
                                           s5_c_re, s5_c_im, s5_log_dt)
    acw = _pad_rows(a_conv_w, 32)
    zeros = jnp.zeros_like(a_conv_b)
    avec = jnp.stack([a_conv_b, a_ln_g, a_ln_b, a_pw_b, zeros, zeros, zeros, zeros], axis=1)
    bvec = jnp.stack([s5_d, s5_glu_b, zeros, zeros, zeros, zeros, zeros, zeros], axis=1)
    ccw = _pad_rows(c_conv_w, 8)
    dcw = _pad_rows(d_conv_w, 8)
    alog = jnp.repeat(d_a_log, DN_HEAD_DIM, axis=-1)[:, None, :]
    dtb = jnp.repeat(d_dt_bias, DN_HEAD_DIM, axis=-1)[:, None, :]
    ng = jnp.tile(d_norm_g, (1, DN_HEADS))[:, None, :]
    apw = a_pw_w.astype(BF16)
    bglu = s5_glu_w.astype(BF16)
    wout = w_out.astype(BF16)
    fg = final_g[None, :]

    def seq3(a):
        return a.reshape(bsz, seq, a.shape[-1])

    ng3 = norm_g[:, None, :]
    for l in range(depth):
        act, dn, ulo, uhi = _inproj(l, x.reshape(bsz * seq, d), ng3, w_bf, w_tail, dcw, alog, dtb,
                                    tm, seq)
        ulo, uhi = seq3(ulo), seq3(uhi)
        ylo, yhi = _s5(l, ulo, uhi, s5_wd, s5_td, s5_vd, s5_a, s5_rows)
        od = _dnet(l, seq3(dn), ng, dn_tc)
        x = _mixout(l, seq3(act), ulo, uhi, ylo, yhi, od, x, acw, avec, apw, bvec, bglu, ccw, wout,
                    fg, tc, l == depth - 1)
    return x
```

```python
import functools

import jax
import jax.numpy as jnp
from jax import lax
from jax.experimental import pallas as pl
from jax.experimental.pallas import tpu as pltpu

F32 = jnp.float32
BF16 = jnp.bfloat16
HIGHEST = lax.Precision.HIGHEST

NORM_EPS = 1e-6
BRANCH = 256
CONF_KERNEL = 31
SC_KERNEL = 3
DN_HEADS = 4
DN_HEAD_DIM = 64
DN_CONV = 4
DN_CHUNK = 64
DN_GROUP = 8
S5_GROUPS = 16
S5_GROUP = 16
S5_STATE = 64
S5_Q = 8
S5_ROW = S5_Q * BRANCH
S5_NSTATE = S5_GROUPS * S5_STATE

COL_A, COL_B, COL_C, COL_QKV, MAIN_BLOCKS = 0, 3, 5, 9, 12
IN_COLS = MAIN_BLOCKS * BRANCH + 2 * DN_HEADS + BRANCH
TAIL_COLS = BRANCH + 128
HALF = BRANCH // 2
ACT_A, ACT_AZ, ACT_BZ, ACT_CB, ACT_CX, ACT_DZ, ACT_COLS = 0, 1, 2, 3, 4, 5, 6
DN_Q, DN_K, DN_V, DN_BETA, DN_G, DN_COLS = 0, 1, 2, 3, 4, 5

VMEM_LIMIT_BYTES = 56 * 1024 * 1024


def _mm(a, b):
    return jnp.dot(a.astype(BF16), b.astype(BF16), preferred_element_type=F32)


def _mm_tn(a, b):
    return lax.dot_general(a.astype(BF16), b.astype(BF16), (((0,), (0,)), ((), ())),
                           preferred_element_type=F32)


def _layer_spec(layer, shape):
    zeros = (0,) * len(shape)
    return pl.BlockSpec((None,) + tuple(shape), lambda *_: (layer,) + zeros)


def _sigmoid(x):
    return 1.0 / (1.0 + jnp.exp(-x))


def _silu(x):
    return x * _sigmoid(x)


def _inproj_kernel(x_ref, g_ref, w_ref, wt_ref, dcw_ref, alog_ref, dtb_ref, act_ref, dn_ref, ulo_ref,
                   uhi_ref, pad_scr, *, steps_per_seq, sub):
    tm = x_ref.shape[0]
    w = BRANCH
    first = pl.program_id(0) % steps_per_seq == 0

    @pl.when(first)
    def _():
        pad_scr[tm:tm + 8, :] = jnp.zeros((8, 3 * w), F32)

    pad_scr[0:8, :] = pad_scr[tm:tm + 8, :]

    row_h = lax.broadcasted_iota(jnp.int32, (w, w), 0) // DN_HEAD_DIM
    col_h = lax.broadcasted_iota(jnp.int32, (w, w), 1) // DN_HEAD_DIM
    ones_bd = jnp.where(row_h == col_h, 1.0, 0.0).astype(BF16)

    def project(hb, col, n):
        return jnp.dot(hb, w_ref[:, col * w:(col + n) * w], preferred_element_type=F32)

    def put(ref, rows, j, val):
        ref[rows, j * w:(j + 1) * w] = val

    def deltanet_inputs(rows, r0):
        acc = dcw_ref[0:1, :] * pad_scr[pl.ds(r0 + 8 - (DN_CONV - 1), sub), :]
        for kk in range(1, DN_CONV):
            acc = acc + dcw_ref[kk:kk + 1, :] * pad_scr[pl.ds(r0 + 8 - (DN_CONV - 1) + kk, sub), :]
        qkv = _silu(acc)
        q = qkv[:, 0:w]
        k = qkv[:, w:2 * w]
        ssq = jnp.dot((q * q).astype(BF16), ones_bd, preferred_element_type=F32)
        ssk = jnp.dot((k * k).astype(BF16), ones_bd, preferred_element_type=F32)
        put(dn_ref, rows, DN_Q, q * lax.rsqrt(ssq + NORM_EPS) * (DN_HEAD_DIM ** -0.5))
        put(dn_ref, rows, DN_K, k * lax.rsqrt(ssk + NORM_EPS))
        put(dn_ref, rows, DN_V, qkv[:, 2 * w:3 * w] * dn_ref[rows, DN_BETA * w:(DN_BETA + 1) * w])

    for r in range(tm // sub):
        r0 = r * sub
        rows = slice(r0, r0 + sub)
        x = x_ref[rows, :]
        h = x * lax.rsqrt(jnp.mean(x * x, axis=-1, keepdims=True) + NORM_EPS) * g_ref[...]
        hb = h.astype(BF16)
        pad_scr[r0 + 8:r0 + 8 + sub, :] = project(hb, COL_QKV, 3)
        p = project(hb, COL_A, 3)
        put(act_ref, rows, ACT_A, p[:, 0:w] * _sigmoid(p[:, w:2 * w]))
        put(act_ref, rows, ACT_AZ, _silu(p[:, 2 * w:3 * w]))
        p = project(hb, COL_B, 2)
        ulo_ref[rows, :] = p[:, 0:HALF]
        uhi_ref[rows, :] = p[:, HALF:w]
        put(act_ref, rows, ACT_BZ, _silu(p[:, w:2 * w]))
        p = project(hb, COL_C, 4)
        put(act_ref, rows, ACT_CB, p[:, 0:w] * _silu(p[:, 3 * w:4 * w]))
        put(act_ref, rows, ACT_CX, p[:, w:2 * w] * p[:, 2 * w:3 * w])
        p = jnp.dot(hb, wt_ref[...], preferred_element_type=F32)
        put(act_ref, rows, ACT_DZ, _silu(p[:, 0:w]))
        p = jnp.concatenate(
            [jnp.broadcast_to(p[:, w + j:w + j + 1], (sub, DN_HEAD_DIM)) for j in range(2 * DN_HEADS)],
            axis=1)
        sp_in = p[:, 0:w] + dtb_ref[...]
        softplus = jnp.maximum(sp_in, 0.0) + jnp.log1p(jnp.exp(-jnp.abs(sp_in)))
        put(dn_ref, rows, DN_G, -jnp.exp(alog_ref[...]) * softplus)
        put(dn_ref, rows, DN_BETA, _sigmoid(p[:, w:2 * w]))
        if r > 0:
            deltanet_inputs(slice(r0 - sub, r0), r0 - sub)
    deltanet_inputs(slice(tm - sub, tm), tm - sub)


def _inproj(layer, x2d, g, w, wt, dcw, alog, dtb, tm, seq):
    t, d = x2d.shape
    const = functools.partial(_layer_spec, layer)
    out = lambda n: pl.BlockSpec((tm, n * BRANCH), lambda i: (i, 0))
    return pl.pallas_call(
        functools.partial(_inproj_kernel, steps_per_seq=seq // tm, sub=min(256, tm)),
        out_shape=(jax.ShapeDtypeStruct((t, ACT_COLS * BRANCH), F32),
                   jax.ShapeDtypeStruct((t, DN_COLS * BRANCH), F32),
                   jax.ShapeDtypeStruct((t, HALF), F32),
                   jax.ShapeDtypeStruct((t, HALF), F32)),
        grid=(t // tm,),
        in_specs=[pl.BlockSpec((tm, d), lambda i: (i, 0)),
                  const((1, d)), const((d, IN_COLS)), const((d, TAIL_COLS)),
                  const((8, 3 * BRANCH)), const((1, BRANCH)), const((1, BRANCH))],
        out_specs=(out(ACT_COLS), out(DN_COLS), pl.BlockSpec((tm, HALF), lambda i: (i, 0)),
                   pl.BlockSpec((tm, HALF), lambda i: (i, 0))),
        scratch_shapes=[pltpu.VMEM((tm + 8, 3 * BRANCH), F32)],
        compiler_params=pltpu.CompilerParams(
            dimension_semantics=("arbitrary",), vmem_limit_bytes=VMEM_LIMIT_BYTES),
        name="inproj",
    )(x2d, g, w, wt, dcw, alog, dtb)


S5_TILES = S5_GROUPS // 2


def _block_transpose(pieces, block):
    lane = lax.broadcasted_iota(jnp.int32, pieces[0].shape, 1)
    n = len(pieces)
    s = n // 2
    while s:
        width = s * block
        lower = lane % (2 * width) < width
        out = list(pieces)
        for j in range(n):
            if j & s == 0:
                a, b = pieces[j], pieces[j + s]
                out[j] = jnp.where(lower, a, pltpu.roll(b, width, 1))
                out[j + s] = jnp.where(lower, pltpu.roll(a, 128 - width, 1), b)
        pieces = out
        s //= 2
    return pieces


def _s5_kernel(ulo_ref, uhi_ref, wd_ref, td_ref, vdt_ref, a_ref, ylo_ref, yhi_ref,
               carry_scr, z_scr, xp_scr):
    rows = ulo_ref.shape[0] // S5_Q
    n = S5_NSTATE
    tiles_per_half = S5_TILES // 2
    lanes = 2 * S5_GROUP

    @pl.when(pl.program_id(1) == 0)
    def _():
        carry_scr[...] = jnp.zeros_like(carry_scr)

    ub = []
    for half in (ulo_ref, uhi_ref):
        tok = [half[pl.ds(j, rows, stride=S5_Q), :] for j in range(S5_Q)]
        first = _block_transpose(tok[0:4], lanes)
        second = _block_transpose(tok[4:8], lanes)
        ub += [jnp.concatenate([first[k], second[k]], axis=1).astype(BF16)
               for k in range(tiles_per_half)]
    for k in range(S5_TILES):
        z = jnp.dot(ub[k], wd_ref[k], preferred_element_type=F32)
        z_scr[:, k * 128:(k + 1) * 128] = z[:, 0:128]
        z_scr[:, n + k * 128:n + (k + 1) * 128] = z[:, 128:256]

    ar = a_ref[0:1, :]
    ai = a_ref[1:2, :]

    def body(c, carry):
        xr, xi = carry
        xp_scr[pl.ds(c, 1), 0:n] = xr
        xp_scr[pl.ds(c, 1), n:2 * n] = xi
        zr = z_scr[pl.ds(c, 1), 0:n]
        zi = z_scr[pl.ds(c, 1), n:2 * n]
        return ar * xr - ai * xi + zr, ar * xi + ai * xr + zi

    xr, xi = lax.fori_loop(0, rows, body, (carry_scr[0:1, 0:n], carry_scr[0:1, n:2 * n]),
                           unroll=4)
    carry_scr[0:1, 0:n] = xr
    carry_scr[0:1, n:2 * n] = xi

    ys = []
    for k in range(S5_TILES):
        xp = jnp.concatenate([xp_scr[:, k * 128:(k + 1) * 128],
                              xp_scr[:, n + k * 128:n + (k + 1) * 128]], axis=1).astype(BF16)
        ys.append(jnp.dot(ub[k], td_ref[k], preferred_element_type=F32)
                  + lax.dot_general(xp, vdt_ref[k], (((1,), (1,)), ((), ())),
                                    preferred_element_type=F32))
    for h, out_ref in enumerate((ylo_ref, yhi_ref)):
        tiles = ys[h * tiles_per_half:(h + 1) * tiles_per_half]
        for m in range(2):
            tok = _block_transpose([y[:, m * 128:(m + 1) * 128] for y in tiles], lanes)
            for b in range(4):
                out_ref[pl.ds(4 * m + b, rows, stride=S5_Q), :] = tok[b]


def _s5(layer, ulo, uhi, wd, td, vd, a, rows):
    b, l, _ = ulo.shape
    const = functools.partial(_layer_spec, layer)
    half = lambda: pl.BlockSpec((None, rows * S5_Q, HALF), lambda bi, j: (bi, j, 0))
    tiles = (S5_TILES, BRANCH, BRANCH)
    return pl.pallas_call(
        _s5_kernel,
        out_shape=(jax.ShapeDtypeStruct((b, l, HALF), F32), jax.ShapeDtypeStruct((b, l, HALF), F32)),
        grid=(b, l // (rows * S5_Q)),
        in_specs=[half(), half(), const(tiles), const(tiles), const(tiles), const((8, S5_NSTATE))],
        out_specs=(half(), half()),
        scratch_shapes=[pltpu.VMEM((8, 2 * S5_NSTATE), F32),
                        pltpu.VMEM((rows, 2 * S5_NSTATE), F32),
                        pltpu.VMEM((rows, 2 * S5_NSTATE), F32)],
        compiler_params=pltpu.CompilerParams(
            dimension_semantics=("parallel", "arbitrary"), vmem_limit_bytes=VMEM_LIMIT_BYTES),
        name="s5",
    )(ulo, uhi, wd, td, vd, a)


def _tile_heads(x):
    return jnp.concatenate([x] * DN_HEADS, axis=0)


def _dnet_kernel(dn_ref, ng_ref, o_ref, s_scr, o_scr, u_scr, wq_scr, attn_scr, kd_scr, egl_scr):
    nb, tc, _ = dn_ref.shape
    c64 = DN_CHUNK
    w = DN_HEADS * DN_HEAD_DIM
    nck = tc // c64
    gb = DN_GROUP // nck

    @pl.when(pl.program_id(0) == 0)
    def _():
        s_scr[...] = jnp.zeros_like(s_scr)

    row_h = lax.broadcasted_iota(jnp.int32, (w, w), 0) // DN_HEAD_DIM
    col_h = lax.broadcasted_iota(jnp.int32, (w, w), 1) // DN_HEAD_DIM
    bd = row_h == col_h
    bd_bf = jnp.where(bd, 1.0, 0.0).astype(BF16)

    def mmb(a, b_bf):
        return jnp.dot(a.astype(BF16), b_bf, preferred_element_type=F32)

    def mmb_nt(a, b_bf):
        return lax.dot_general(a.astype(BF16), b_bf, (((1,), (1,)), ((), ())),
                               preferred_element_type=F32)

    ri = lax.broadcasted_iota(jnp.int32, (c64, w), 0)
    cj = lax.broadcasted_iota(jnp.int32, (c64, w), 1) % DN_HEAD_DIM
    causal = ri >= cj
    strict = ri > cj
    diag = ri == cj
    eye_all = jnp.where(diag, 1.0, 0.0).astype(F32)
    lt_bf = jnp.where(lax.broadcasted_iota(jnp.int32, (c64, c64), 0)
                      >= lax.broadcasted_iota(jnp.int32, (c64, c64), 1), 1.0, 0.0).astype(BF16)

    def blockdiag(x):
        return _tile_heads(x.astype(BF16)) * bd_bf

    grp = range(DN_GROUP)

    def prepare(gi, carry):
        def col(i, j):
            return dn_ref[gi * gb + i // nck, (i % nck) * c64:(i % nck + 1) * c64, j * w:(j + 1) * w]

        rows = [pl.ds(pl.multiple_of((gi * gb + i // nck) * tc + (i % nck) * c64, c64), c64)
                for i in grp]
        qc = [col(i, DN_Q) for i in grp]
        kc = [col(i, DN_K) for i in grp]
        bc = [col(i, DN_BETA) for i in grp]
        g = [col(i, DN_G) for i in grp]
        g_hi = [x.astype(BF16) for x in g]
        g_lo = [(g[i] - g_hi[i].astype(F32)).astype(BF16) for i in grp]
        gc = [jnp.dot(lt_bf, g_hi[i], preferred_element_type=F32)
              + jnp.dot(lt_bf, g_lo[i], preferred_element_type=F32) for i in grp]
        egc = [jnp.exp(x) for x in gc]
        gl = [x[c64 - 1:c64, :] for x in gc]
        kb = [kc[i] * bc[i] for i in grp]
        grow = [jnp.sum(jnp.where(diag, x, 0.0), axis=0, keepdims=True) for x in gc]
        decay = [jnp.exp(jnp.where(causal, gc[i] - grow[i], -jnp.inf)) for i in grp]
        k_bd = [blockdiag(x) for x in kc]
        kq = [mmb_nt(jnp.concatenate([kb[i], qc[i]], axis=0), k_bd[i]) for i in grp]
        for i in grp:
            attn_scr[rows[i], :] = kq[i][c64:2 * c64, :] * decay[i]
        x = [jnp.where(strict, -kq[i][0:c64, :] * decay[i], 0.0) for i in grp]
        p = [eye_all + x[i] for i in grp]
        x = [mmb(x[i], blockdiag(x[i])) for i in grp]
        for _ in range(4):
            xp = [mmb(jnp.concatenate([x[i], p[i]], axis=0), blockdiag(x[i])) for i in grp]
            x = [xp[i][0:c64, :] for i in grp]
            p = [p[i] + xp[i][c64:2 * c64, :] for i in grp]
        p = [p[i] + mmb(p[i], blockdiag(x[i])) for i in grp]
        for i in grp:
            ci = gi * DN_GROUP + i
            u_scr[rows[i], :] = mmb(p[i], blockdiag(col(i, DN_V)))
            wq_scr[ci, 0:c64, :] = mmb(p[i], blockdiag(kb[i] * egc[i]))
            wq_scr[ci, c64:2 * c64, :] = qc[i] * egc[i]
            kd_scr[rows[i], :] = kc[i] * jnp.exp(gl[i] - gc[i])
            egl_scr[ci] = jnp.broadcast_to(jnp.exp(gl[i]), (8, w))
        return carry

    lax.fori_loop(0, nb // gb, prepare, 0)

    bs = range(nb)

    def recur(c, carry):
        rows = [pl.ds(pl.multiple_of(b * tc + c * c64, c64), c64) for b in bs]
        s = [s_scr[b] for b in bs]
        ws_qs = [_mm(wq_scr[b * nck + c], s[b]) for b in bs]
        v_new = [u_scr[rows[b], :] - ws_qs[b][0:c64, :] for b in bs]
        for b in bs:
            o_scr[rows[b], :] = (ws_qs[b][c64:2 * c64, :]
                                 + mmb(attn_scr[rows[b], :], blockdiag(v_new[b])))
        for b in bs:
            s_scr[b] = (s[b] * egl_scr[b * nck + c][0:1, :]
                        + jnp.where(bd, _mm_tn(kd_scr[rows[b], :], v_new[b]), 0.0))
        return carry

    lax.fori_loop(0, nck, recur, 0)

    o = o_scr[...]
    ms = mmb(o * o, bd_bf) * (1.0 / DN_HEAD_DIM)
    o_ref[...] = (o * lax.rsqrt(ms + NORM_EPS) * ng_ref[...]).reshape(nb, tc, w)


def _dnet(layer, dn3, ng, tc):
    b, l, _ = dn3.shape
    w = BRANCH
    nchunk = b * tc // DN_CHUNK
    assert DN_GROUP % (tc // DN_CHUNK) == 0 and nchunk % DN_GROUP == 0
    act = lambda: pltpu.VMEM((b * tc, w), F32)
    return pl.pallas_call(
        _dnet_kernel,
        out_shape=jax.ShapeDtypeStruct((b, l, w), F32),
        grid=(l // tc,),
        in_specs=[pl.BlockSpec((b, tc, DN_COLS * w), lambda i: (0, i, 0)),
                  _layer_spec(layer, (1, w))],
        out_specs=pl.BlockSpec((b, tc, w), lambda i: (0, i, 0)),
        scratch_shapes=[pltpu.VMEM((b, w, w), F32),
                        act(), act(),
                        pltpu.VMEM((nchunk, 2 * DN_CHUNK, w), F32),
                        act(), act(),
                        pltpu.VMEM((nchunk, 8, w), F32)],
        compiler_params=pltpu.CompilerParams(
            dimension_semantics=("arbitrary",), vmem_limit_bytes=VMEM_LIMIT_BYTES),
        name="dnet",
    )(dn3, ng)


def _mixout_kernel(act_ref, ahalo_ref, chalo_ref, ulo_ref, uhi_ref, ylo_ref, yhi_ref, od_ref, x_ref,
                   acw_ref, avec_ref, apw_ref, bvec_ref, bglu_ref, ccw_ref, wout_ref, fg_ref,
                   out_ref, ash_scr, cpad_scr, *, final, sub):
    tc = x_ref.shape[0]
    w = BRANCH
    first = pl.program_id(1) == 0

    def act(j, rows=slice(None)):
        return act_ref[rows, j * w:(j + 1) * w]

    ash_scr[0, 0:32, :] = jnp.where(first, 0.0, ahalo_ref[...])
    ash_scr[0, 32:32 + tc, :] = act(ACT_A)
    for r in range(1, 8):
        ash_scr[r, 0:tc + 24, :] = ash_scr[0, pl.ds(r, tc + 24), :]
    cpad_scr[0:8, :] = jnp.where(first, 0.0, chalo_ref[...])
    cpad_scr[8:8 + tc, :] = act(ACT_CX)

    conv_b, ln_g, ln_b, pw_b = (avec_ref[0:1, :], avec_ref[1:2, :], avec_ref[2:3, :],
                                avec_ref[3:4, :])
    d_skip, glu_b = bvec_ref[0:1, :], bvec_ref[1:2, :]

    for r in range(tc // sub):
        r0 = r * sub
        rows = slice(r0, r0 + sub)
        acc = conv_b
        for kk in range(CONF_KERNEL):
            off = 32 - (CONF_KERNEL - 1) + kk
            acc = acc + acw_ref[kk:kk + 1, :] * ash_scr[off % 8, pl.ds(r0 + off - off % 8, sub), :]
        mu = jnp.mean(acc, axis=-1, keepdims=True)
        xc = acc - mu
        ln = xc * lax.rsqrt(jnp.mean(xc * xc, axis=-1, keepdims=True) + NORM_EPS) * ln_g + ln_b
        ya = (_mm(_silu(ln), apw_ref[...]) + pw_b) * act(ACT_AZ, rows)
        ub = jnp.concatenate([ulo_ref[rows, :], uhi_ref[rows, :]], axis=1)
        ys = jnp.concatenate([ylo_ref[rows, :], yhi_ref[rows, :]], axis=1)
        yb = jax.nn.gelu(ys + d_skip * ub)
        yb = yb * _sigmoid(_mm(yb, bglu_ref[...]) + glu_b) * act(ACT_BZ, rows)
        cacc = ccw_ref[0:1, :] * cpad_scr[pl.ds(r0 + 8 - (SC_KERNEL - 1), sub), :]
        for kk in range(1, SC_KERNEL):
            cacc = cacc + ccw_ref[kk:kk + 1, :] * cpad_scr[
                pl.ds(r0 + 8 - (SC_KERNEL - 1) + kk, sub), :]
        yc = act(ACT_CB, rows) * cacc
        yd = od_ref[rows, :] * act(ACT_DZ, rows)
        y = (x_ref[rows, :]
             + _mm(ya, wout_ref[0:w, :]) + _mm(yb, wout_ref[w:2 * w, :])
             + _mm(yc, wout_ref[2 * w:3 * w, :]) + _mm(yd, wout_ref[3 * w:4 * w, :]))
        if final:
            y = y * lax.rsqrt(jnp.mean(y * y, axis=-1, keepdims=True) + NORM_EPS) * fg_ref[...]
        out_ref[rows, :] = y


def _mixout(layer, act3, ulo, uhi, ylo, yhi, od, x3, acw, avec, apw, bvec, bglu, ccw, wout, fg, tc,
            final):
    b, l, d = x3.shape
    w = BRANCH
    blk = lambda width, col: pl.BlockSpec((None, tc, width), lambda bi, i: (bi, i, col))
    halo = lambda nrow, col: pl.BlockSpec(
        (None, nrow, w), lambda bi, i: (bi, jnp.maximum(i * (tc // nrow) - 1, 0), col))
    const = functools.partial(_layer_spec, layer)
    return pl.pallas_call(
        functools.partial(_mixout_kernel, final=final, sub=256),
        out_shape=jax.ShapeDtypeStruct((b, l, d), F32),
        grid=(b, l // tc),
        in_specs=[blk(ACT_COLS * w, 0), halo(32, ACT_A), halo(8, ACT_CX),
                  blk(HALF, 0), blk(HALF, 0), blk(HALF, 0), blk(HALF, 0), blk(w, 0), blk(d, 0),
                  const((32, w)), const((8, w)), const((w, w)), const((8, w)), const((w, w)),
                  const((8, w)), const((4 * w, d)), pl.BlockSpec((1, d), lambda bi, i: (0, 0))],
        out_specs=pl.BlockSpec((None, tc, d), lambda bi, i: (bi, i, 0)),
        scratch_shapes=[pltpu.VMEM((8, tc + 32, w), F32), pltpu.VMEM((tc + 8, w), F32)],
        compiler_params=pltpu.CompilerParams(
            dimension_semantics=("parallel", "parallel"), vmem_limit_bytes=VMEM_LIMIT_BYTES),
        name="mixout_final" if final else "mixout",
    )(act3, act3, act3, ulo, uhi, ylo, yhi, od, x3, acw, avec, apw, bvec, bglu, ccw, wout, fg)


def _pad_rows(a, n):
    return jnp.pad(a, ((0, 0), (0, n - a.shape[1]), (0, 0)))


def _s5_tiles(tab, state_cols):
    dep = tab.shape[0]
    nt, rb = S5_TILES, 2 * S5_GROUP
    t = tab.reshape(dep, S5_Q, nt, rb, 128)
    t = jnp.transpose(t, (0, 2, 1, 3, 4)).reshape(dep, nt, S5_Q * rb, 128)
    per = S5_STATE if state_cols else S5_GROUP
    src = jnp.arange(128)
    dst = jnp.arange(2 * 128)
    spread = ((src[:, None] // per == dst[None, :] // (2 * per))
              & (src[:, None] % per == dst[None, :] % per)).astype(F32)
    row_group = (jnp.arange(S5_Q * rb) % rb) // S5_GROUP
    col_group = (dst % (2 * per)) // per
    mask = (row_group[:, None] == col_group[None, :]).astype(F32)
    return (jnp.einsum("dkrc,cn->dkrn", t, spread, precision=HIGHEST) * mask).astype(BF16)


def _s5_tables(lam_re, lam_im, b_re, b_im, c_re, c_im, log_dt):
    q, g, p, h = S5_Q, S5_GROUPS, S5_STATE, S5_GROUP
    dep = lam_re.shape[0]
    lr = jnp.minimum(lam_re, -1e-4)
    li = lam_im
    dt = jnp.exp(log_dt)[..., None]
    lrdt, lidt = (lr * dt)[:, None], (li * dt)[:, None]

    def power(steps):
        s = steps[None, :, None, None]
        mag = jnp.exp(lrdt * s)
        return (mag * jnp.cos(lidt * s))[:, :, :, None, :], (mag * jnp.sin(lidt * s))[:, :, :, None, :]

    tok = jnp.arange(q, dtype=F32)
    p1r, p1i = power(jnp.ones((1,), F32))
    nr, ni = p1r[:, 0, :, 0] - 1.0, p1i[:, 0, :, 0]
    den = lr * lr + li * li
    fr, fi = (nr * lr + ni * li) / den, (ni * lr - nr * li) / den
    bbr = jnp.transpose(fr[..., None] * b_re - fi[..., None] * b_im, (0, 1, 3, 2))[:, None]
    bbi = jnp.transpose(fr[..., None] * b_im + fi[..., None] * b_re, (0, 1, 3, 2))[:, None]
    cr, ci = c_re[:, None], c_im[:, None]

    rr, ri = power(q - 1.0 - tok)
    wc = jnp.concatenate([rr * bbr - ri * bbi, rr * bbi + ri * bbr], axis=-1)
    qr, qi = power(tok + 1.0)
    mc = jnp.concatenate([cr * qr - ci * qi, -(cr * qi + ci * qr)], axis=-1)
    er, ei = power(tok)
    cpr = jnp.transpose(cr * er - ci * ei, (0, 2, 1, 3, 4)).reshape(dep, g, q * h, p)
    cpi = jnp.transpose(cr * ei + ci * er, (0, 2, 1, 3, 4)).reshape(dep, g, q * h, p)
    kd = (jnp.einsum("dgcp,dgyp->dgcy", bbr[:, 0], cpr, precision=HIGHEST)
          - jnp.einsum("dgcp,dgyp->dgcy", bbi[:, 0], cpi, precision=HIGHEST))
    lane = jnp.arange(q * h)
    tk = jnp.stack([jnp.where(lane >= j * h, jnp.roll(kd, j * h, axis=-1), 0.0) for j in range(q)],
                   axis=1)

    flat = lambda t: t.reshape(dep, q * g * h, t.shape[-1])
    ar, ai = power(jnp.full((1,), float(q), F32))
    a = jnp.concatenate([ar.reshape(dep, 1, g * p), ai.reshape(dep, 1, g * p),
                         jnp.zeros((dep, 6, g * p), F32)], axis=1)
    return _s5_tiles(flat(wc), True), _s5_tiles(flat(tk), False), _s5_tiles(flat(mc), True), a


def _w_in_tail(w_in):
    n = MAIN_BLOCKS * BRANCH
    pad = jnp.zeros(w_in.shape[:-1] + (128 - 2 * DN_HEADS,), w_in.dtype)
    return jnp.concatenate([w_in[..., n + 2 * DN_HEADS:], w_in[..., n:n + 2 * DN_HEADS], pad],
                           axis=-1).astype(BF16)


def kernel(x, norm_g, w_in, a_conv_w, a_conv_b, a_ln_g, a_ln_b, a_pw_w, a_pw_b, s5_lambda_re, s5_lambda_im, s5_b_re, s5_b_im, s5_c_re, s5_c_im, s5_d, s5_log_dt, s5_glu_w, s5_glu_b, c_conv_w, d_conv_w, d_a_log, d_dt_bias, d_norm_g, w_out, final_g):
    bsz, seq, d = x.shape
    depth = w_in.shape[0]
    w = BRANCH
    tm = min(512, seq)
    tc = min(512, seq)
    dn_tc = min(256, seq)
    s5_rows = min(512, seq // S5_Q)

    w_bf = w_in.astype(BF16)
    w_tail = _w_in_tail(w_in)
    s5_wd, s5_td, s5_vd, s5_a = _s5_tables(s5_lambda_re, s5_lambda_im, s5_b_re, s5_b_im,
                                           s5_c_re, s5_c_im, s5_log_dt)
    acw = _pad_rows(a_conv_w, 32)
    zeros = jnp.zeros_like(a_conv_b)
    avec = jnp.stack([a_conv_b, a_ln_g, a_ln_b, a_pw_b, zeros, zeros, zeros, zeros], axis=1)
    bvec = jnp.stack([s5_d, s5_glu_b, zeros, zeros, zeros, zeros, zeros, zeros], axis=1)
    ccw = _pad_rows(c_conv_w, 8)
    dcw = _pad_rows(d_conv_w, 8)
    alog = jnp.repeat(d_a_log, DN_HEAD_DIM, axis=-1)[:, None, :]
    dtb = jnp.repeat(d_dt_bias, DN_HEAD_DIM, axis=-1)[:, None, :]
    ng = jnp.tile(d_norm_g, (1, DN_HEADS))[:, None, :]
    apw = a_pw_w.astype(BF16)
    bglu = s5_glu_w.astype(BF16)
    wout = w_out.astype(BF16)
    fg = final_g[None, :]

    def seq3(a):
        return a.reshape(bsz, seq, a.shape[-1])

    ng3 = norm_g[:, None, :]
    for l in range(depth):
        act, dn, ulo, uhi = _inproj(l, x.reshape(bsz * seq, d), ng3, w_bf, w_tail, dcw, alog, dtb,
                                    tm, seq)
        ulo, uhi = seq3(ulo), seq3(uhi)
        ylo, yhi = _s5(l, ulo, uhi, s5_wd, s5_td, s5_vd, s5_a, s5_rows)
        od = _dnet(l, seq3(dn), ng, dn_tc)
        x = _mixout(l, seq3(act), ulo, uhi, ylo, yhi, od, x, acw, avec, apw, bvec, bglu, ccw, wout,
                    fg, tc, l == depth - 1)
    return x
```

```python
import functools

import jax
import jax.numpy as jnp
from jax import lax
from jax.experimental import pallas as pl
from jax.experimental.pallas import tpu as pltpu

F32 = jnp.float32
BF16 = jnp.bfloat16
HIGHEST = lax.Precision.HIGHEST

NORM_EPS = 1e-6
BRANCH = 256
CONF_KERNEL = 31
SC_KERNEL = 3
DN_HEADS = 4
DN_HEAD_DIM = 64
DN_CONV = 4
DN_CHUNK = 64
DN_GROUP = 8
S5_GROUPS = 16
S5_GROUP = 16
S5_STATE = 64
S5_Q = 8
S5_ROW = S5_Q * BRANCH
S5_NSTATE = S5_GROUPS * S5_STATE

COL_A, COL_B, COL_C, COL_QKV, MAIN_BLOCKS = 0, 3, 5, 9, 12
IN_COLS = MAIN_BLOCKS * BRANCH + 2 * DN_HEADS + BRANCH
TAIL_COLS = BRANCH + 128
HALF = BRANCH // 2
ACT_A, ACT_AZ, ACT_BZ, ACT_CB, ACT_CX, ACT_DZ, ACT_COLS = 0, 1, 2, 3, 4, 5, 6
DN_Q, DN_K, DN_V, DN_BETA, DN_G, DN_COLS = 0, 1, 2, 3, 4, 5

VMEM_LIMIT_BYTES = 56 * 1024 * 1024


def _mm(a, b):
    return jnp.dot(a.astype(BF16), b.astype(BF16), preferred_element_type=F32)


def _mm_tn(a, b):
    return lax.dot_general(a.astype(BF16), b.astype(BF16), (((0,), (0,)), ((), ())),
                           preferred_element_type=F32)


def _layer_spec(layer, shape):
    zeros = (0,) * len(shape)
    return pl.BlockSpec((None,) + tuple(shape), lambda *_: (layer,) + zeros)


def _sigmoid(x):
    return 1.0 / (1.0 + jnp.exp(-x))


def _silu(x):
    return x * _sigmoid(x)


def _inproj_kernel(x_ref, g_ref, w_ref, wt_ref, dcw_ref, alog_ref, dtb_ref, act_ref, dn_ref, ulo_ref,
                   uhi_ref, pad_scr, *, steps_per_seq, sub):
    tm = x_ref.shape[0]
    w = BRANCH
    first = pl.program_id(0) % steps_per_seq == 0

    @pl.when(first)
    def _():
        pad_scr[tm:tm + 8, :] = jnp.zeros((8, 3 * w), F32)

    pad_scr[0:8, :] = pad_scr[tm:tm + 8, :]

    row_h = lax.broadcasted_iota(jnp.int32, (w, w), 0) // DN_HEAD_DIM
    col_h = lax.broadcasted_iota(jnp.int32, (w, w), 1) // DN_HEAD_DIM
    ones_bd = jnp.where(row_h == col_h, 1.0, 0.0).astype(BF16)

    def project(hb, col, n):
        return jnp.dot(hb, w_ref[:, col * w:(col + n) * w], preferred_element_type=F32)

    def put(ref, rows, j, val):
        ref[rows, j * w:(j + 1) * w] = val

    def deltanet_inputs(rows, r0):
        xp = pad_scr[r0:r0 + sub + 8, :]
        acc = dcw_ref[DN_CONV - 1:DN_CONV, :] * xp[8:, :]
        for kk in range(DN_CONV - 1):
            acc = acc + dcw_ref[kk:kk + 1, :] * pltpu.roll(xp, DN_CONV - 1 - kk, 0)[8:, :]
        qkv = _silu(acc)
        q = qkv[:, 0:w]
        k = qkv[:, w:2 * w]
        ssq = jnp.dot((q * q).astype(BF16), ones_bd, preferred_element_type=F32)
        ssk = jnp.dot((k * k).astype(BF16), ones_bd, preferred_element_type=F32)
        put(dn_ref, rows, DN_Q, q * lax.rsqrt(ssq + NORM_EPS) * (DN_HEAD_DIM ** -0.5))
        put(dn_ref, rows, DN_K, k * lax.rsqrt(ssk + NORM_EPS))
        put(dn_ref, rows, DN_V, qkv[:, 2 * w:3 * w] * dn_ref[rows, DN_BETA * w:(DN_BETA + 1) * w])

    for r in range(tm // sub):
        r0 = r * sub
        rows = slice(r0, r0 + sub)
        x = x_ref[rows, :]
        h = x * lax.rsqrt(jnp.mean(x * x, axis=-1, keepdims=True) + NORM_EPS) * g_ref[...]
        hb = h.astype(BF16)
        pad_scr[r0 + 8:r0 + 8 + sub, :] = project(hb, COL_QKV, 3)
        p = project(hb, COL_A, 3)
        put(act_ref, rows, ACT_A, p[:, 0:w] * _sigmoid(p[:, w:2 * w]))
        put(act_ref, rows, ACT_AZ, _silu(p[:, 2 * w:3 * w]))
        p = project(hb, COL_B, 2)
        ulo_ref[rows, :] = p[:, 0:HALF]
        uhi_ref[rows, :] = p[:, HALF:w]
        put(act_ref, rows, ACT_BZ, _silu(p[:, w:2 * w]))
        p = project(hb, COL_C, 4)
        put(act_ref, rows, ACT_CB, p[:, 0:w] * _silu(p[:, 3 * w:4 * w]))
        put(act_ref, rows, ACT_CX, p[:, w:2 * w] * p[:, 2 * w:3 * w])
        p = jnp.dot(hb, wt_ref[...], preferred_element_type=F32)
        put(act_ref, rows, ACT_DZ, _silu(p[:, 0:w]))
        p = jnp.concatenate(
            [jnp.broadcast_to(p[:, w + j:w + j + 1], (sub, DN_HEAD_DIM)) for j in range(2 * DN_HEADS)],
            axis=1)
        sp_in = p[:, 0:w] + dtb_ref[...]
        softplus = jnp.maximum(sp_in, 0.0) + jnp.log1p(jnp.exp(-jnp.abs(sp_in)))
        put(dn_ref, rows, DN_G, -jnp.exp(alog_ref[...]) * softplus)
        put(dn_ref, rows, DN_BETA, _sigmoid(p[:, w:2 * w]))
        if r > 0:
            deltanet_inputs(slice(r0 - sub, r0), r0 - sub)
    deltanet_inputs(slice(tm - sub, tm), tm - sub)


def _inproj(layer, x2d, g, w, wt, dcw, alog, dtb, tm, seq):
    t, d = x2d.shape
    const = functools.partial(_layer_spec, layer)
    out = lambda n: pl.BlockSpec((tm, n * BRANCH), lambda i: (i, 0))
    return pl.pallas_call(
        functools.partial(_inproj_kernel, steps_per_seq=seq // tm, sub=min(256, tm)),
        out_shape=(jax.ShapeDtypeStruct((t, ACT_COLS * BRANCH), F32),
                   jax.ShapeDtypeStruct((t, DN_COLS * BRANCH), F32),
                   jax.ShapeDtypeStruct((t, HALF), F32),
                   jax.ShapeDtypeStruct((t, HALF), F32)),
        grid=(t // tm,),
        in_specs=[pl.BlockSpec((tm, d), lambda i: (i, 0)),
                  const((1, d)), const((d, IN_COLS)), const((d, TAIL_COLS)),
                  const((8, 3 * BRANCH)), const((1, BRANCH)), const((1, BRANCH))],
        out_specs=(out(ACT_COLS), out(DN_COLS), pl.BlockSpec((tm, HALF), lambda i: (i, 0)),
                   pl.BlockSpec((tm, HALF), lambda i: (i, 0))),
        scratch_shapes=[pltpu.VMEM((tm + 8, 3 * BRANCH), F32)],
        compiler_params=pltpu.CompilerParams(
            dimension_semantics=("arbitrary",), vmem_limit_bytes=VMEM_LIMIT_BYTES),
        name="inproj",
    )(x2d, g, w, wt, dcw, alog, dtb)


S5_TILES = S5_GROUPS // 2


def _block_transpose(pieces, block):
    lane = lax.broadcasted_iota(jnp.int32, pieces[0].shape, 1)
    n = len(pieces)
    s = n // 2
    while s:
        width = s * block
        lower = lane % (2 * width) < width
        out = list(pieces)
        for j in range(n):
            if j & s == 0:
                a, b = pieces[j], pieces[j + s]
                out[j] = jnp.where(lower, a, pltpu.roll(b, width, 1))
                out[j + s] = jnp.where(lower, pltpu.roll(a, 128 - width, 1), b)
        pieces = out
        s //= 2
    return pieces


def _s5_kernel(ulo_ref, uhi_ref, wd_ref, td_ref, vdt_ref, a_ref, ylo_ref, yhi_ref,
               carry_scr, z_scr, xp_scr):
    rows = ulo_ref.shape[0] // S5_Q
    n = S5_NSTATE
    tiles_per_half = S5_TILES // 2
    lanes = 2 * S5_GROUP

    @pl.when(pl.program_id(1) == 0)
    def _():
        carry_scr[...] = jnp.zeros_like(carry_scr)

    ub = []
    for half in (ulo_ref, uhi_ref):
        tok = [half[pl.ds(j, rows, stride=S5_Q), :] for j in range(S5_Q)]
        first = _block_transpose(tok[0:4], lanes)
        second = _block_transpose(tok[4:8], lanes)
        ub += [jnp.concatenate([first[k], second[k]], axis=1).astype(BF16)
               for k in range(tiles_per_half)]
    for k in range(S5_TILES):
        z = jnp.dot(ub[k], wd_ref[k], preferred_element_type=F32)
        z_scr[:, k * 128:(k + 1) * 128] = z[:, 0:128]
        z_scr[:, n + k * 128:n + (k + 1) * 128] = z[:, 128:256]

    ar = a_ref[0:1, :]
    ai = a_ref[1:2, :]

    def body(c, carry):
        xr, xi = carry
        xp_scr[pl.ds(c, 1), 0:n] = xr
        xp_scr[pl.ds(c, 1), n:2 * n] = xi
        zr = z_scr[pl.ds(c, 1), 0:n]
        zi = z_scr[pl.ds(c, 1), n:2 * n]
        return ar * xr - ai * xi + zr, ar * xi + ai * xr + zi

    xr, xi = lax.fori_loop(0, rows, body, (carry_scr[0:1, 0:n], carry_scr[0:1, n:2 * n]),
                           unroll=4)
    carry_scr[0:1, 0:n] = xr
    carry_scr[0:1, n:2 * n] = xi

    ys = []
    for k in range(S5_TILES):
        xp = jnp.concatenate([xp_scr[:, k * 128:(k + 1) * 128],
                              xp_scr[:, n + k * 128:n + (k + 1) * 128]], axis=1).astype(BF16)
        ys.append(jnp.dot(ub[k], td_ref[k], preferred_element_type=F32)
                  + lax.dot_general(xp, vdt_ref[k], (((1,), (1,)), ((), ())),
                                    preferred_element_type=F32))
    for h, out_ref in enumerate((ylo_ref, yhi_ref)):
        tiles = ys[h * tiles_per_half:(h + 1) * tiles_per_half]
        for m in range(2):
            tok = _block_transpose([y[:, m * 128:(m + 1) * 128] for y in tiles], lanes)
            for b in range(4):
                out_ref[pl.ds(4 * m + b, rows, stride=S5_Q), :] = tok[b]


def _s5(layer, ulo, uhi, wd, td, vd, a, rows):
    b, l, _ = ulo.shape
    const = functools.partial(_layer_spec, layer)
    half = lambda: pl.BlockSpec((None, rows * S5_Q, HALF), lambda bi, j: (bi, j, 0))
    tiles = (S5_TILES, BRANCH, BRANCH)
    return pl.pallas_call(
        _s5_kernel,
        out_shape=(jax.ShapeDtypeStruct((b, l, HALF), F32), jax.ShapeDtypeStruct((b, l, HALF), F32)),
        grid=(b, l // (rows * S5_Q)),
        in_specs=[half(), half(), const(tiles), const(tiles), const(tiles), const((8, S5_NSTATE))],
        out_specs=(half(), half()),
        scratch_shapes=[pltpu.VMEM((8, 2 * S5_NSTATE), F32),
                        pltpu.VMEM((rows, 2 * S5_NSTATE), F32),
                        pltpu.VMEM((rows, 2 * S5_NSTATE), F32)],
        compiler_params=pltpu.CompilerParams(
            dimension_semantics=("parallel", "arbitrary"), vmem_limit_bytes=VMEM_LIMIT_BYTES),
        name="s5",
    )(ulo, uhi, wd, td, vd, a)


def _tile_heads(x):
    return jnp.concatenate([x] * DN_HEADS, axis=0)


def _dnet_kernel(dn_ref, ng_ref, o_ref, s_scr, o_scr, u_scr, wq_scr, attn_scr, kd_scr, egl_scr):
    nb, tc, _ = dn_ref.shape
    c64 = DN_CHUNK
    w = DN_HEADS * DN_HEAD_DIM
    nck = tc // c64
    gb = DN_GROUP // nck

    @pl.when(pl.program_id(0) == 0)
    def _():
        s_scr[...] = jnp.zeros_like(s_scr)

    row_h = lax.broadcasted_iota(jnp.int32, (w, w), 0) // DN_HEAD_DIM
    col_h = lax.broadcasted_iota(jnp.int32, (w, w), 1) // DN_HEAD_DIM
    bd = row_h == col_h
    bd_bf = jnp.where(bd, 1.0, 0.0).astype(BF16)

    def mmb(a, b_bf):
        return jnp.dot(a.astype(BF16), b_bf, preferred_element_type=F32)

    def mmb_nt(a, b_bf):
        return lax.dot_general(a.astype(BF16), b_bf, (((1,), (1,)), ((), ())),
                               preferred_element_type=F32)

    ri = lax.broadcasted_iota(jnp.int32, (c64, w), 0)
    cj = lax.broadcasted_iota(jnp.int32, (c64, w), 1) % DN_HEAD_DIM
    causal = ri >= cj
    strict = ri > cj
    diag = ri == cj
    eye_all = jnp.where(diag, 1.0, 0.0).astype(F32)
    lt_bf = jnp.where(lax.broadcasted_iota(jnp.int32, (c64, c64), 0)
                      >= lax.broadcasted_iota(jnp.int32, (c64, c64), 1), 1.0, 0.0).astype(BF16)

    def blockdiag(x):
        return _tile_heads(x.astype(BF16)) * bd_bf

    grp = range(DN_GROUP)

    def prepare(gi, carry):
        def col(i, j):
            return dn_ref[gi * gb + i // nck, (i % nck) * c64:(i % nck + 1) * c64, j * w:(j + 1) * w]

        rows = [pl.ds(pl.multiple_of((gi * gb + i // nck) * tc + (i % nck) * c64, c64), c64)
                for i in grp]
        qc = [col(i, DN_Q) for i in grp]
        kc = [col(i, DN_K) for i in grp]
        bc = [col(i, DN_BETA) for i in grp]
        g = [col(i, DN_G) for i in grp]
        g_hi = [x.astype(BF16) for x in g]
        g_lo = [(g[i] - g_hi[i].astype(F32)).astype(BF16) for i in grp]
        gc = [jnp.dot(lt_bf, g_hi[i], preferred_element_type=F32)
              + jnp.dot(lt_bf, g_lo[i], preferred_element_type=F32) for i in grp]
        egc = [jnp.exp(x) for x in gc]
        gl = [x[c64 - 1:c64, :] for x in gc]
        kb = [kc[i] * bc[i] for i in grp]
        grow = [jnp.sum(jnp.where(diag, x, 0.0), axis=0, keepdims=True) for x in gc]
        decay = [jnp.exp(jnp.where(causal, gc[i] - grow[i], -jnp.inf)) for i in grp]
        k_bd = [blockdiag(x) for x in kc]
        kq = [mmb_nt(jnp.concatenate([kb[i], qc[i]], axis=0), k_bd[i]) for i in grp]
        for i in grp:
            attn_scr[rows[i], :] = kq[i][c64:2 * c64, :] * decay[i]
        x = [jnp.where(strict, -kq[i][0:c64, :] * decay[i], 0.0) for i in grp]
        p = [eye_all + x[i] for i in grp]
        x = [mmb(x[i], blockdiag(x[i])) for i in grp]
        for _ in range(4):
            xp = [mmb(jnp.concatenate([x[i], p[i]], axis=0), blockdiag(x[i])) for i in grp]
            x = [xp[i][0:c64, :] for i in grp]
            p = [p[i] + xp[i][c64:2 * c64, :] for i in grp]
        p = [p[i] + mmb(p[i], blockdiag(x[i])) for i in grp]
        for i in grp:
            ci = gi * DN_GROUP + i
            u_scr[rows[i], :] = mmb(p[i], blockdiag(col(i, DN_V)))
            wq_scr[ci, 0:c64, :] = mmb(p[i], blockdiag(kb[i] * egc[i]))
            wq_scr[ci, c64:2 * c64, :] = qc[i] * egc[i]
            kd_scr[rows[i], :] = kc[i] * jnp.exp(gl[i] - gc[i])
            egl_scr[ci] = jnp.broadcast_to(jnp.exp(gl[i]), (8, w))
        return carry

    lax.fori_loop(0, nb // gb, prepare, 0)

    bs = range(nb)

    def recur(c, carry):
        rows = [pl.ds(pl.multiple_of(b * tc + c * c64, c64), c64) for b in bs]
        s = [s_scr[b] for b in bs]
        ws_qs = [_mm(wq_scr[b * nck + c], s[b]) for b in bs]
        v_new = [u_scr[rows[b], :] - ws_qs[b][0:c64, :] for b in bs]
        for b in bs:
            o_scr[rows[b], :] = (ws_qs[b][c64:2 * c64, :]
                                 + mmb(attn_scr[rows[b], :], blockdiag(v_new[b])))
        for b in bs:
            s_scr[b] = (s[b] * egl_scr[b * nck + c][0:1, :]
                        + jnp.where(bd, _mm_tn(kd_scr[rows[b], :], v_new[b]), 0.0))
        return carry

    lax.fori_loop(0, nck, recur, 0)

    o = o_scr[...]
    ms = mmb(o * o, bd_bf) * (1.0 / DN_HEAD_DIM)
    o_ref[...] = (o * lax.rsqrt(ms + NORM_EPS) * ng_ref[...]).reshape(nb, tc, w)


def _dnet(layer, dn3, ng, tc):
    b, l, _ = dn3.shape
    w = BRANCH
    nchunk = b * tc // DN_CHUNK
    assert DN_GROUP % (tc // DN_CHUNK) == 0 and nchunk % DN_GROUP == 0
    act = lambda: pltpu.VMEM((b * tc, w), F32)
    return pl.pallas_call(
        _dnet_kernel,
        out_shape=jax.ShapeDtypeStruct((b, l, w), F32),
        grid=(l // tc,),
        in_specs=[pl.BlockSpec((b, tc, DN_COLS * w), lambda i: (0, i, 0)),
                  _layer_spec(layer, (1, w))],
        out_specs=pl.BlockSpec((b, tc, w), lambda i: (0, i, 0)),
        scratch_shapes=[pltpu.VMEM((b, w, w), F32),
                        act(), act(),
                        pltpu.VMEM((nchunk, 2 * DN_CHUNK, w), F32),
                        act(), act(),
                        pltpu.VMEM((nchunk, 8, w), F32)],
        compiler_params=pltpu.CompilerParams(
            dimension_semantics=("arbitrary",), vmem_limit_bytes=VMEM_LIMIT_BYTES),
        name="dnet",
    )(dn3, ng)


def _mixout_kernel(act_ref, ahalo_ref, chalo_ref, ulo_ref, uhi_ref, ylo_ref, yhi_ref, od_ref, x_ref,
                   acw_ref, avec_ref, apw_ref, bvec_ref, bglu_ref, ccw_ref, wout_ref, fg_ref,
                   out_ref, ash_scr, cpad_scr, *, final, sub):
    tc = x_ref.shape[0]
    w = BRANCH
    first = pl.program_id(1) == 0

    def act(j, rows=slice(None)):
        return act_ref[rows, j * w:(j + 1) * w]

    ash_scr[0, 0:32, :] = jnp.where(first, 0.0, ahalo_ref[...])
    ash_scr[0, 32:32 + tc, :] = act(ACT_A)
    apad = ash_scr[0]
    for r in range(1, 8):
        ash_scr[r] = pltpu.roll(apad, tc + 32 - r, 0)
    cpad_scr[0:8, :] = jnp.where(first, 0.0, chalo_ref[...])
    cpad_scr[8:8 + tc, :] = act(ACT_CX)

    conv_b, ln_g, ln_b, pw_b = (avec_ref[0:1, :], avec_ref[1:2, :], avec_ref[2:3, :],
                                avec_ref[3:4, :])
    d_skip, glu_b = bvec_ref[0:1, :], bvec_ref[1:2, :]

    for r in range(tc // sub):
        r0 = r * sub
        rows = slice(r0, r0 + sub)
        acc = conv_b
        for kk in range(CONF_KERNEL):
            off = 32 - (CONF_KERNEL - 1) + kk
            acc = acc + acw_ref[kk:kk + 1, :] * ash_scr[off % 8, pl.ds(r0 + off - off % 8, sub), :]
        mu = jnp.mean(acc, axis=-1, keepdims=True)
        xc = acc - mu
        ln = xc * lax.rsqrt(jnp.mean(xc * xc, axis=-1, keepdims=True) + NORM_EPS) * ln_g + ln_b
        ya = (_mm(_silu(ln), apw_ref[...]) + pw_b) * act(ACT_AZ, rows)
        ub = jnp.concatenate([ulo_ref[rows, :], uhi_ref[rows, :]], axis=1)
        ys = jnp.concatenate([ylo_ref[rows, :], yhi_ref[rows, :]], axis=1)
        yb = jax.nn.gelu(ys + d_skip * ub)
        yb = yb * _sigmoid(_mm(yb, bglu_ref[...]) + glu_b) * act(ACT_BZ, rows)
        cp = cpad_scr[r0:r0 + sub + 8, :]
        cacc = ccw_ref[SC_KERNEL - 1:SC_KERNEL, :] * cp[8:, :]
        for kk in range(SC_KERNEL - 1):
            cacc = cacc + ccw_ref[kk:kk + 1, :] * pltpu.roll(cp, SC_KERNEL - 1 - kk, 0)[8:, :]
        yc = act(ACT_CB, rows) * cacc
        yd = od_ref[rows, :] * act(ACT_DZ, rows)
        y = (x_ref[rows, :]
             + _mm(ya, wout_ref[0:w, :]) + _mm(yb, wout_ref[w:2 * w, :])
             + _mm(yc, wout_ref[2 * w:3 * w, :]) + _mm(yd, wout_ref[3 * w:4 * w, :]))
        if final:
            y = y * lax.rsqrt(jnp.mean(y * y, axis=-1, keepdims=True) + NORM_EPS) * fg_ref[...]
        out_ref[rows, :] = y


def _mixout(layer, act3, ulo, uhi, ylo, yhi, od, x3, acw, avec, apw, bvec, bglu, ccw, wout, fg, tc,
            final):
    b, l, d = x3.shape
    w = BRANCH
    blk = lambda width, col: pl.BlockSpec((None, tc, width), lambda bi, i: (bi, i, col))
    halo = lambda nrow, col: pl.BlockSpec(
        (None, nrow, w), lambda bi, i: (bi, jnp.maximum(i * (tc // nrow) - 1, 0), col))
    const = functools.partial(_layer_spec, layer)
    return pl.pallas_call(
        functools.partial(_mixout_kernel, final=final, sub=256),
        out_shape=jax.ShapeDtypeStruct((b, l, d), F32),
        grid=(b, l // tc),
        in_specs=[blk(ACT_COLS * w, 0), halo(32, ACT_A), halo(8, ACT_CX),
                  blk(HALF, 0), blk(HALF, 0), blk(HALF, 0), blk(HALF, 0), blk(w, 0), blk(d, 0),
                  const((32, w)), const((8, w)), const((w, w)), const((8, w)), const((w, w)),
                  const((8, w)), const((4 * w, d)), pl.BlockSpec((1, d), lambda bi, i: (0, 0))],
        out_specs=pl.BlockSpec((None, tc, d), lambda bi, i: (bi, i, 0)),
        scratch_shapes=[pltpu.VMEM((8, tc + 32, w), F32), pltpu.VMEM((tc + 8, w), F32)],
        compiler_params=pltpu.CompilerParams(
            dimension_semantics=("parallel", "parallel"), vmem_limit_bytes=VMEM_LIMIT_BYTES),
        name="mixout_final" if final else "mixout",
    )(act3, act3, act3, ulo, uhi, ylo, yhi, od, x3, acw, avec, apw, bvec, bglu, ccw, wout, fg)


def _pad_rows(a, n):
    return jnp.pad(a, ((0, 0), (0, n - a.shape[1]), (0, 0)))


def _s5_tiles(tab, state_cols):
    dep = tab.shape[0]
    nt, rb = S5_TILES, 2 * S5_GROUP
    t = tab.reshape(dep, S5_Q, nt, rb, 128)
    t = jnp.transpose(t, (0, 2, 1, 3, 4)).reshape(dep, nt, S5_Q * rb, 128)
    per = S5_STATE if state_cols else S5_GROUP
    src = jnp.arange(128)
    dst = jnp.arange(2 * 128)
    spread = ((src[:, None] // per == dst[None, :] // (2 * per))
              & (src[:, None] % per == dst[None, :] % per)).astype(F32)
    row_group = (jnp.arange(S5_Q * rb) % rb) // S5_GROUP
    col_group = (dst % (2 * per)) // per
    mask = (row_group[:, None] == col_group[None, :]).astype(F32)
    return (jnp.einsum("dkrc,cn->dkrn", t, spread, precision=HIGHEST) * mask).astype(BF16)


def _s5_tables(lam_re, lam_im, b_re, b_im, c_re, c_im, log_dt):
    q, g, p, h = S5_Q, S5_GROUPS, S5_STATE, S5_GROUP
    dep = lam_re.shape[0]
    lr = jnp.minimum(lam_re, -1e-4)
    li = lam_im
    dt = jnp.exp(log_dt)[..., None]
    lrdt, lidt = (lr * dt)[:, None], (li * dt)[:, None]

    def power(steps):
        s = steps[None, :, None, None]
        mag = jnp.exp(lrdt * s)
        return (mag * jnp.cos(lidt * s))[:, :, :, None, :], (mag * jnp.sin(lidt * s))[:, :, :, None, :]

    tok = jnp.arange(q, dtype=F32)
    p1r, p1i = power(jnp.ones((1,), F32))
    nr, ni = p1r[:, 0, :, 0] - 1.0, p1i[:, 0, :, 0]
    den = lr * lr + li * li
    fr, fi = (nr * lr + ni * li) / den, (ni * lr - nr * li) / den
    bbr = jnp.transpose(fr[..., None] * b_re - fi[..., None] * b_im, (0, 1, 3, 2))[:, None]
    bbi = jnp.transpose(fr[..., None] * b_im + fi[..., None] * b_re, (0, 1, 3, 2))[:, None]
    cr, ci = c_re[:, None], c_im[:, None]

    rr, ri = power(q - 1.0 - tok)
    wc = jnp.concatenate([rr * bbr - ri * bbi, rr * bbi + ri * bbr], axis=-1)
    qr, qi = power(tok + 1.0)
    mc = jnp.concatenate([cr * qr - ci * qi, -(cr * qi + ci * qr)], axis=-1)
    er, ei = power(tok)
    cpr = jnp.transpose(cr * er - ci * ei, (0, 2, 1, 3, 4)).reshape(dep, g, q * h, p)
    cpi = jnp.transpose(cr * ei + ci * er, (0, 2, 1, 3, 4)).reshape(dep, g, q * h, p)
    kd = (jnp.einsum("dgcp,dgyp->dgcy", bbr[:, 0], cpr, precision=HIGHEST)
          - jnp.einsum("dgcp,dgyp->dgcy", bbi[:, 0], cpi, precision=HIGHEST))
    lane = jnp.arange(q * h)
    tk = jnp.stack([jnp.where(lane >= j * h, jnp.roll(kd, j * h, axis=-1), 0.0) for j in range(q)],
                   axis=1)

    flat = lambda t: t.reshape(dep, q * g * h, t.shape[-1])
    ar, ai = power(jnp.full((1,), float(q), F32))
    a = jnp.concatenate([ar.reshape(dep, 1, g * p), ai.reshape(dep, 1, g * p),
                         jnp.zeros((dep, 6, g * p), F32)], axis=1)
    return _s5_tiles(flat(wc), True), _s5_tiles(flat(tk), False), _s5_tiles(flat(mc), True), a


def _w_in_tail(w_in):
    n = MAIN_BLOCKS * BRANCH
    pad = jnp.zeros(w_in.shape[:-1] + (128 - 2 * DN_HEADS,), w_in.dtype)
    return jnp.concatenate([w_in[..., n + 2 * DN_HEADS:], w_in[..., n:n + 2 * DN_HEADS], pad],
                           axis=-1).astype(BF16)


def kernel(x, norm_g, w_in, a_conv_w, a_conv_b, a_ln_g, a_ln_b, a_pw_w, a_pw_b, s5_lambda_re, s5_lambda_im, s5_b_re, s5_b_im, s5_c_re, s5_c_im, s5_d, s5_log_dt, s5_glu_w, s5_glu_b, c_conv_w, d_conv_w, d_a_log, d_dt_bias, d_norm_g, w_out, final_g):
    bsz, seq, d = x.shape
    depth = w_in.shape[0]
    w = BRANCH
    tm = min(512, seq)
    tc = min(512, seq)
    dn_tc = min(256, seq)
    s5_rows = min(512, seq // S5_Q)

    w_bf = w_in.astype(BF16)
    w_tail = _w_in_tail(w_in)
    s5_wd, s5_td, s5_vd, s5_a = _s5_tables(s5_lambda_re, s5_lambda_im, s5_b_re, s5_b_im,
                                           s5_c_re, s5_c_im, s5_log_dt)
    acw = _pad_rows(a_conv_w, 32)
    zeros = jnp.zeros_like(a_conv_b)
    avec = jnp.stack([a_conv_b, a_ln_g, a_ln_b, a_pw_b, zeros, zeros, zeros, zeros], axis=1)
    bvec = jnp.stack([s5_d, s5_glu_b, zeros, zeros, zeros, zeros, zeros, zeros], axis=1)
    ccw = _pad_rows(c_conv_w, 8)
    dcw = _pad_rows(d_conv_w, 8)
    alog = jnp.repeat(d_a_log, DN_HEAD_DIM, axis=-1)[:, None, :]
    dtb = jnp.repeat(d_dt_bias, DN_HEAD_DIM, axis=-1)[:, None, :]
    ng = jnp.tile(d_norm_g, (1, DN_HEADS))[:, None, :]
    apw = a_pw_w.astype(BF16)
    bglu = s5_glu_w.astype(BF16)
    wout = w_out.astype(BF16)
    fg = final_g[None, :]

    def seq3(a):
        return a.reshape(bsz, seq, a.shape[-1])

    ng3 = norm_g[:, None, :]
    for l in range(depth):
        act, dn, ulo, uhi = _inproj(l, x.reshape(bsz * seq, d), ng3, w_bf, w_tail, dcw, alog, dtb,
                                    tm, seq)
        ulo, uhi = seq3(ulo), seq3(uhi)
        ylo, yhi = _s5(l, ulo, uhi, s5_wd, s5_td, s5_vd, s5_a, s5_rows)
        od = _dnet(l, seq3(dn), ng, dn_tc)
        x = _mixout(l, seq3(act), ulo, uhi, ylo, yhi, od, x, acw, avec, apw, bvec, bglu, ccw, wout,
                    fg, tc, l == depth - 1)
    return x
```

```python
import functools

import jax
import jax.numpy as jnp
from jax import lax
from jax.experimental import pallas as pl
from jax.experimental.pallas import tpu as pltpu

F32 = jnp.float32
BF16 = jnp.bfloat16
HIGHEST = lax.Precision.HIGHEST

NORM_EPS = 1e-6
BRANCH = 256
CONF_KERNEL = 31
SC_KERNEL = 3
DN_HEADS = 4
DN_HEAD_DIM = 64
DN_CONV = 4
DN_CHUNK = 64
DN_GROUP = 16
S5_GROUPS = 16
S5_GROUP = 16
S5_STATE = 64
S5_Q = 8
S5_ROW = S5_Q * BRANCH
S5_NSTATE = S5_GROUPS * S5_STATE

COL_A, COL_B, COL_C, COL_QKV, MAIN_BLOCKS = 0, 3, 5, 9, 12
IN_COLS = MAIN_BLOCKS * BRANCH + 2 * DN_HEADS + BRANCH
TAIL_COLS = BRANCH + 128
HALF = BRANCH // 2
ACT_A, ACT_AZ, ACT_BZ, ACT_CB, ACT_CX, ACT_DZ, ACT_COLS = 0, 1, 2, 3, 4, 5, 6
DN_Q, DN_K, DN_V, DN_BETA, DN_G, DN_COLS = 0, 1, 2, 3, 4, 5

VMEM_LIMIT_BYTES = 56 * 1024 * 1024


def _mm(a, b):
    return jnp.dot(a.astype(BF16), b.astype(BF16), preferred_element_type=F32)


def _mm_tn(a, b):
    return lax.dot_general(a.astype(BF16), b.astype(BF16), (((0,), (0,)), ((), ())),
                           preferred_element_type=F32)


def _layer_spec(layer, shape):
    zeros = (0,) * len(shape)
    return pl.BlockSpec((None,) + tuple(shape), lambda *_: (layer,) + zeros)


def _sigmoid(x):
    return 1.0 / (1.0 + jnp.exp(-x))


def _silu(x):
    return x * _sigmoid(x)


def _inproj_kernel(x_ref, g_ref, wf_ref, wt_ref, dcw_ref, alog_ref, dtb_ref, act_ref, dn_ref, ulo_ref,
                   uhi_ref, pad_scr, w_ref, *, steps_per_seq, sub):
    tm = x_ref.shape[0]
    w = BRANCH
    first = pl.program_id(0) % steps_per_seq == 0

    @pl.when(pl.program_id(0) == 0)
    def _():
        for c in range(MAIN_BLOCKS):
            w_ref[:, c * w:(c + 1) * w] = wf_ref[:, c * w:(c + 1) * w].astype(BF16)

    @pl.when(first)
    def _():
        pad_scr[tm:tm + 8, :] = jnp.zeros((8, 3 * w), F32)

    pad_scr[0:8, :] = pad_scr[tm:tm + 8, :]

    row_h = lax.broadcasted_iota(jnp.int32, (w, w), 0) // DN_HEAD_DIM
    col_h = lax.broadcasted_iota(jnp.int32, (w, w), 1) // DN_HEAD_DIM
    ones_bd = jnp.where(row_h == col_h, 1.0, 0.0).astype(BF16)

    def project(hb, col, n):
        return jnp.dot(hb, w_ref[:, col * w:(col + n) * w], preferred_element_type=F32)

    def put(ref, rows, j, val):
        ref[rows, j * w:(j + 1) * w] = val

    def deltanet_inputs(rows, r0):
        xp = pad_scr[r0:r0 + sub + 8, :]
        acc = dcw_ref[DN_CONV - 1:DN_CONV, :] * xp[8:, :]
        for kk in range(DN_CONV - 1):
            acc = acc + dcw_ref[kk:kk + 1, :] * pltpu.roll(xp, DN_CONV - 1 - kk, 0)[8:, :]
        qkv = _silu(acc)
        q = qkv[:, 0:w]
        k = qkv[:, w:2 * w]
        ssq = jnp.dot((q * q).astype(BF16), ones_bd, preferred_element_type=F32)
        ssk = jnp.dot((k * k).astype(BF16), ones_bd, preferred_element_type=F32)
        put(dn_ref, rows, DN_Q, q * lax.rsqrt(ssq + NORM_EPS) * (DN_HEAD_DIM ** -0.5))
        put(dn_ref, rows, DN_K, k * lax.rsqrt(ssk + NORM_EPS))
        put(dn_ref, rows, DN_V, qkv[:, 2 * w:3 * w] * dn_ref[rows, DN_BETA * w:(DN_BETA + 1) * w])

    for r in range(tm // sub):
        r0 = r * sub
        rows = slice(r0, r0 + sub)
        x = x_ref[rows, :]
        h = x * lax.rsqrt(jnp.mean(x * x, axis=-1, keepdims=True) + NORM_EPS) * g_ref[...]
        hb = h.astype(BF16)
        pad_scr[r0 + 8:r0 + 8 + sub, :] = project(hb, COL_QKV, 3)
        p = project(hb, COL_A, 3)
        put(act_ref, rows, ACT_A, p[:, 0:w] * _sigmoid(p[:, w:2 * w]))
        put(act_ref, rows, ACT_AZ, _silu(p[:, 2 * w:3 * w]))
        p = project(hb, COL_B, 2)
        ulo_ref[rows, :] = p[:, 0:HALF]
        uhi_ref[rows, :] = p[:, HALF:w]
        put(act_ref, rows, ACT_BZ, _silu(p[:, w:2 * w]))
        p = project(hb, COL_C, 4)
        put(act_ref, rows, ACT_CB, p[:, 0:w] * _silu(p[:, 3 * w:4 * w]))
        put(act_ref, rows, ACT_CX, p[:, w:2 * w] * p[:, 2 * w:3 * w])
        p = jnp.dot(hb, wt_ref[...], preferred_element_type=F32)
        put(act_ref, rows, ACT_DZ, _silu(p[:, 0:w]))
        p = jnp.concatenate(
            [jnp.broadcast_to(p[:, w + j:w + j + 1], (sub, DN_HEAD_DIM)) for j in range(2 * DN_HEADS)],
            axis=1)
        sp_in = p[:, 0:w] + dtb_ref[...]
        softplus = jnp.maximum(sp_in, 0.0) + jnp.log1p(jnp.exp(-jnp.abs(sp_in)))
        put(dn_ref, rows, DN_G, -jnp.exp(alog_ref[...]) * softplus)
        put(dn_ref, rows, DN_BETA, _sigmoid(p[:, w:2 * w]))
        if r > 0:
            deltanet_inputs(slice(r0 - sub, r0), r0 - sub)
    deltanet_inputs(slice(tm - sub, tm), tm - sub)


def _inproj(layer, x2d, g, w, wt, dcw, alog, dtb, tm, seq):
    t, d = x2d.shape
    const = functools.partial(_layer_spec, layer)
    out = lambda n: pl.BlockSpec((tm, n * BRANCH), lambda i: (i, 0))
    return pl.pallas_call(
        functools.partial(_inproj_kernel, steps_per_seq=seq // tm, sub=min(256, tm)),
        out_shape=(jax.ShapeDtypeStruct((t, ACT_COLS * BRANCH), F32),
                   jax.ShapeDtypeStruct((t, DN_COLS * BRANCH), F32),
                   jax.ShapeDtypeStruct((t, HALF), F32),
                   jax.ShapeDtypeStruct((t, HALF), F32)),
        grid=(t // tm,),
        in_specs=[pl.BlockSpec((tm, d), lambda i: (i, 0)),
                  const((1, d)),
                  pl.BlockSpec((None, d, IN_COLS), lambda i: (layer, 0, 0),
                               pipeline_mode=pl.Buffered(1)),
                  const((d, TAIL_COLS)),
                  const((8, 3 * BRANCH)), const((1, BRANCH)), const((1, BRANCH))],
        out_specs=(out(ACT_COLS), out(DN_COLS), pl.BlockSpec((tm, HALF), lambda i: (i, 0)),
                   pl.BlockSpec((tm, HALF), lambda i: (i, 0))),
        scratch_shapes=[pltpu.VMEM((tm + 8, 3 * BRANCH), F32),
                        pltpu.VMEM((d, MAIN_BLOCKS * BRANCH), BF16)],
        compiler_params=pltpu.CompilerParams(
            dimension_semantics=("arbitrary",), vmem_limit_bytes=VMEM_LIMIT_BYTES),
        name="inproj",
    )(x2d, g, w, wt, dcw, alog, dtb)


S5_TILES = S5_GROUPS // 2


def _block_transpose(pieces, block):
    lane = lax.broadcasted_iota(jnp.int32, pieces[0].shape, 1)
    n = len(pieces)
    s = n // 2
    while s:
        width = s * block
        lower = lane % (2 * width) < width
        out = list(pieces)
        for j in range(n):
            if j & s == 0:
                a, b = pieces[j], pieces[j + s]
                out[j] = jnp.where(lower, a, pltpu.roll(b, width, 1))
                out[j + s] = jnp.where(lower, pltpu.roll(a, 128 - width, 1), b)
        pieces = out
        s //= 2
    return pieces


def _s5_kernel(ulo_ref, uhi_ref, wd_ref, td_ref, vdt_ref, a_ref, ylo_ref, yhi_ref,
               carry_scr, z_scr, xp_scr):
    rows = ulo_ref.shape[0] // S5_Q
    n = S5_NSTATE
    tiles_per_half = S5_TILES // 2
    lanes = 2 * S5_GROUP

    @pl.when(pl.program_id(1) == 0)
    def _():
        carry_scr[...] = jnp.zeros_like(carry_scr)

    ub = []
    for half in (ulo_ref, uhi_ref):
        tok = [half[pl.ds(j, rows, stride=S5_Q), :] for j in range(S5_Q)]
        first = _block_transpose(tok[0:4], lanes)
        second = _block_transpose(tok[4:8], lanes)
        ub += [jnp.concatenate([first[k], second[k]], axis=1).astype(BF16)
               for k in range(tiles_per_half)]
    for k in range(S5_TILES):
        z = jnp.dot(ub[k], wd_ref[k], preferred_element_type=F32)
        z_scr[:, k * 128:(k + 1) * 128] = z[:, 0:128]
        z_scr[:, n + k * 128:n + (k + 1) * 128] = z[:, 128:256]

    ar = a_ref[0:1, :]
    ai = a_ref[1:2, :]

    def body(c, carry):
        xr, xi = carry
        xp_scr[pl.ds(c, 1), 0:n] = xr
        xp_scr[pl.ds(c, 1), n:2 * n] = xi
        zr = z_scr[pl.ds(c, 1), 0:n]
        zi = z_scr[pl.ds(c, 1), n:2 * n]
        return ar * xr - ai * xi + zr, ar * xi + ai * xr + zi

    xr, xi = lax.fori_loop(0, rows, body, (carry_scr[0:1, 0:n], carry_scr[0:1, n:2 * n]),
                           unroll=4)
    carry_scr[0:1, 0:n] = xr
    carry_scr[0:1, n:2 * n] = xi

    ys = []
    for k in range(S5_TILES):
        xp = jnp.concatenate([xp_scr[:, k * 128:(k + 1) * 128],
                              xp_scr[:, n + k * 128:n + (k + 1) * 128]], axis=1).astype(BF16)
        ys.append(jnp.dot(ub[k], td_ref[k], preferred_element_type=F32)
                  + lax.dot_general(xp, vdt_ref[k], (((1,), (1,)), ((), ())),
                                    preferred_element_type=F32))
    for h, out_ref in enumerate((ylo_ref, yhi_ref)):
        tiles = ys[h * tiles_per_half:(h + 1) * tiles_per_half]
        for m in range(2):
            tok = _block_transpose([y[:, m * 128:(m + 1) * 128] for y in tiles], lanes)
            for b in range(4):
                out_ref[pl.ds(4 * m + b, rows, stride=S5_Q), :] = tok[b]


def _s5(layer, ulo, uhi, wd, td, vd, a, rows):
    b, l, _ = ulo.shape
    const = functools.partial(_layer_spec, layer)
    half = lambda: pl.BlockSpec((None, rows * S5_Q, HALF), lambda bi, j: (bi, j, 0))
    tiles = (S5_TILES, BRANCH, BRANCH)
    return pl.pallas_call(
        _s5_kernel,
        out_shape=(jax.ShapeDtypeStruct((b, l, HALF), F32), jax.ShapeDtypeStruct((b, l, HALF), F32)),
        grid=(b, l // (rows * S5_Q)),
        in_specs=[half(), half(), const(tiles), const(tiles), const(tiles), const((8, S5_NSTATE))],
        out_specs=(half(), half()),
        scratch_shapes=[pltpu.VMEM((8, 2 * S5_NSTATE), F32),
                        pltpu.VMEM((rows, 2 * S5_NSTATE), F32),
                        pltpu.VMEM((rows, 2 * S5_NSTATE), F32)],
        compiler_params=pltpu.CompilerParams(
            dimension_semantics=("parallel", "arbitrary"), vmem_limit_bytes=VMEM_LIMIT_BYTES),
        name="s5",
    )(ulo, uhi, wd, td, vd, a)


def _tile_heads(x):
    return jnp.concatenate([x] * DN_HEADS, axis=0)


def _dnet_kernel(dn_ref, ng_ref, o_ref, s_scr, o_scr, u_scr, wq_scr, attn_scr, kd_scr, egl_scr):
    nb, tc, _ = dn_ref.shape
    c64 = DN_CHUNK
    w = DN_HEADS * DN_HEAD_DIM
    nck = tc // c64
    gb = DN_GROUP // nck

    @pl.when(pl.program_id(0) == 0)
    def _():
        s_scr[...] = jnp.zeros_like(s_scr)

    row_h = lax.broadcasted_iota(jnp.int32, (w, w), 0) // DN_HEAD_DIM
    col_h = lax.broadcasted_iota(jnp.int32, (w, w), 1) // DN_HEAD_DIM
    bd = row_h == col_h
    bd_bf = jnp.where(bd, 1.0, 0.0).astype(BF16)

    def mmb(a, b_bf):
        return jnp.dot(a.astype(BF16), b_bf, preferred_element_type=F32)

    def mmb_nt(a, b_bf):
        return lax.dot_general(a.astype(BF16), b_bf, (((1,), (1,)), ((), ())),
                               preferred_element_type=F32)

    ri = lax.broadcasted_iota(jnp.int32, (c64, w), 0)
    cj = lax.broadcasted_iota(jnp.int32, (c64, w), 1) % DN_HEAD_DIM
    causal = ri >= cj
    strict = ri > cj
    diag = ri == cj
    eye_all = jnp.where(diag, 1.0, 0.0).astype(F32)
    lt_bf = jnp.where(lax.broadcasted_iota(jnp.int32, (c64, c64), 0)
                      >= lax.broadcasted_iota(jnp.int32, (c64, c64), 1), 1.0, 0.0).astype(BF16)

    def blockdiag(x):
        return _tile_heads(x.astype(BF16)) * bd_bf

    grp = range(DN_GROUP)

    def prepare(gi, carry):
        def col(i, j):
            return dn_ref[gi * gb + i // nck, (i % nck) * c64:(i % nck + 1) * c64, j * w:(j + 1) * w]

        rows = [pl.ds(pl.multiple_of((gi * gb + i // nck) * tc + (i % nck) * c64, c64), c64)
                for i in grp]
        qc = [col(i, DN_Q) for i in grp]
        kc = [col(i, DN_K) for i in grp]
        bc = [col(i, DN_BETA) for i in grp]
        g = [col(i, DN_G) for i in grp]
        g_hi = [x.astype(BF16) for x in g]
        g_lo = [(g[i] - g_hi[i].astype(F32)).astype(BF16) for i in grp]
        gc = [jnp.dot(lt_bf, g_hi[i], preferred_element_type=F32)
              + jnp.dot(lt_bf, g_lo[i], preferred_element_type=F32) for i in grp]
        egc = [jnp.exp(x) for x in gc]
        gl = [x[c64 - 1:c64, :] for x in gc]
        kb = [kc[i] * bc[i] for i in grp]
        grow = [jnp.sum(jnp.where(diag, x, 0.0), axis=0, keepdims=True) for x in gc]
        decay = [jnp.exp(jnp.where(causal, gc[i] - grow[i], -jnp.inf)) for i in grp]
        k_bd = [blockdiag(x) for x in kc]
        kq = [mmb_nt(jnp.concatenate([kb[i], qc[i]], axis=0), k_bd[i]) for i in grp]
        for i in grp:
            attn_scr[rows[i], :] = kq[i][c64:2 * c64, :] * decay[i]
        x = [jnp.where(strict, -kq[i][0:c64, :] * decay[i], 0.0) for i in grp]
        p = [eye_all + x[i] for i in grp]
        x = [mmb(x[i], blockdiag(x[i])) for i in grp]
        for _ in range(4):
            xp = [mmb(jnp.concatenate([x[i], p[i]], axis=0), blockdiag(x[i])) for i in grp]
            x = [xp[i][0:c64, :] for i in grp]
            p = [p[i] + xp[i][c64:2 * c64, :] for i in grp]
        p = [p[i] + mmb(p[i], blockdiag(x[i])) for i in grp]
        for i in grp:
            ci = gi * DN_GROUP + i
            u_scr[rows[i], :] = mmb(p[i], blockdiag(col(i, DN_V)))
            wq_scr[ci, 0:c64, :] = mmb(p[i], blockdiag(kb[i] * egc[i]))
            wq_scr[ci, c64:2 * c64, :] = qc[i] * egc[i]
            kd_scr[rows[i], :] = kc[i] * jnp.exp(gl[i] - gc[i])
            egl_scr[ci] = jnp.broadcast_to(jnp.exp(gl[i]), (8, w))
        return carry

    lax.fori_loop(0, nb // gb, prepare, 0)

    bs = range(nb)

    def recur(c, carry):
        rows = [pl.ds(pl.multiple_of(b * tc + c * c64, c64), c64) for b in bs]
        s = [s_scr[b] for b in bs]
        ws_qs = [_mm(wq_scr[b * nck + c], s[b]) for b in bs]
        v_new = [u_scr[rows[b], :] - ws_qs[b][0:c64, :] for b in bs]
        for b in bs:
            o_scr[rows[b], :] = (ws_qs[b][c64:2 * c64, :]
                                 + mmb(attn_scr[rows[b], :], blockdiag(v_new[b])))
        for b in bs:
            s_scr[b] = (s[b] * egl_scr[b * nck + c][0:1, :]
                        + jnp.where(bd, _mm_tn(kd_scr[rows[b], :], v_new[b]), 0.0))
        return carry

    lax.fori_loop(0, nck, recur, 0)

    o = o_scr[...]
    ms = mmb(o * o, bd_bf) * (1.0 / DN_HEAD_DIM)
    o_ref[...] = (o * lax.rsqrt(ms + NORM_EPS) * ng_ref[...]).reshape(nb, tc, w)


def _dnet(layer, dn3, ng, tc):
    b, l, _ = dn3.shape
    w = BRANCH
    nchunk = b * tc // DN_CHUNK
    assert DN_GROUP % (tc // DN_CHUNK) == 0 and nchunk % DN_GROUP == 0
    act = lambda: pltpu.VMEM((b * tc, w), F32)
    return pl.pallas_call(
        _dnet_kernel,
        out_shape=jax.ShapeDtypeStruct((b, l, w), F32),
        grid=(l // tc,),
        in_specs=[pl.BlockSpec((b, tc, DN_COLS * w), lambda i: (0, i, 0)),
                  _layer_spec(layer, (1, w))],
        out_specs=pl.BlockSpec((b, tc, w), lambda i: (0, i, 0)),
        scratch_shapes=[pltpu.VMEM((b, w, w), F32),
                        act(), act(),
                        pltpu.VMEM((nchunk, 2 * DN_CHUNK, w), F32),
                        act(), act(),
                        pltpu.VMEM((nchunk, 8, w), F32)],
        compiler_params=pltpu.CompilerParams(
            dimension_semantics=("arbitrary",), vmem_limit_bytes=VMEM_LIMIT_BYTES),
        name="dnet",
    )(dn3, ng)


def _mixout_kernel(act_ref, ahalo_ref, chalo_ref, ulo_ref, uhi_ref, ylo_ref, yhi_ref, od_ref, x_ref,
                   acw_ref, avec_ref, apw_ref, bvec_ref, bglu_ref, ccw_ref, wout_ref, fg_ref,
                   out_ref, ash_scr, cpad_scr, *, final, sub):
    tc = x_ref.shape[0]
    w = BRANCH
    first = pl.program_id(1) == 0

    def act(j, rows=slice(None)):
        return act_ref[rows, j * w:(j + 1) * w]

    ash_scr[0, 0:32, :] = jnp.where(first, 0.0, ahalo_ref[...])
    ash_scr[0, 32:32 + tc, :] = act(ACT_A)
    apad = ash_scr[0]
    for r in range(1, 8):
        ash_scr[r] = pltpu.roll(apad, tc + 32 - r, 0)
    cpad_scr[0:8, :] = jnp.where(first, 0.0, chalo_ref[...])
    cpad_scr[8:8 + tc, :] = act(ACT_CX)

    conv_b, ln_g, ln_b, pw_b = (avec_ref[0:1, :], avec_ref[1:2, :], avec_ref[2:3, :],
                                avec_ref[3:4, :])
    d_skip, glu_b = bvec_ref[0:1, :], bvec_ref[1:2, :]

    for r in range(tc // sub):
        r0 = r * sub
        rows = slice(r0, r0 + sub)
        acc = conv_b
        for kk in range(CONF_KERNEL):
            off = 32 - (CONF_KERNEL - 1) + kk
            acc = acc + acw_ref[kk:kk + 1, :] * ash_scr[off % 8, pl.ds(r0 + off - off % 8, sub), :]
        mu = jnp.mean(acc, axis=-1, keepdims=True)
        xc = acc - mu
        ln = xc * lax.rsqrt(jnp.mean(xc * xc, axis=-1, keepdims=True) + NORM_EPS) * ln_g + ln_b
        ya = (_mm(_silu(ln), apw_ref[...]) + pw_b) * act(ACT_AZ, rows)
        ub = jnp.concatenate([ulo_ref[rows, :], uhi_ref[rows, :]], axis=1)
        ys = jnp.concatenate([ylo_ref[rows, :], yhi_ref[rows, :]], axis=1)
        yb = jax.nn.gelu(ys + d_skip * ub)
        yb = yb * _sigmoid(_mm(yb, bglu_ref[...]) + glu_b) * act(ACT_BZ, rows)
        cp = cpad_scr[r0:r0 + sub + 8, :]
        cacc = ccw_ref[SC_KERNEL - 1:SC_KERNEL, :] * cp[8:, :]
        for kk in range(SC_KERNEL - 1):
            cacc = cacc + ccw_ref[kk:kk + 1, :] * pltpu.roll(cp, SC_KERNEL - 1 - kk, 0)[8:, :]
        yc = act(ACT_CB, rows) * cacc
        yd = od_ref[rows, :] * act(ACT_DZ, rows)
        y = (x_ref[rows, :]
             + _mm(ya, wout_ref[0:w, :]) + _mm(yb, wout_ref[w:2 * w, :])
             + _mm(yc, wout_ref[2 * w:3 * w, :]) + _mm(yd, wout_ref[3 * w:4 * w, :]))
        if final:
            y = y * lax.rsqrt(jnp.mean(y * y, axis=-1, keepdims=True) + NORM_EPS) * fg_ref[...]
        out_ref[rows, :] = y


def _mixout(layer, act3, ulo, uhi, ylo, yhi, od, x3, acw, avec, apw, bvec, bglu, ccw, wout, fg, tc,
            final):
    b, l, d = x3.shape
    w = BRANCH
    blk = lambda width, col: pl.BlockSpec((None, tc, width), lambda bi, i: (bi, i, col))
    halo = lambda nrow, col: pl.BlockSpec(
        (None, nrow, w), lambda bi, i: (bi, jnp.maximum(i * (tc // nrow) - 1, 0), col))
    const = functools.partial(_layer_spec, layer)
    return pl.pallas_call(
        functools.partial(_mixout_kernel, final=final, sub=256),
        out_shape=jax.ShapeDtypeStruct((b, l, d), F32),
        grid=(b, l // tc),
        in_specs=[blk(ACT_COLS * w, 0), halo(32, ACT_A), halo(8, ACT_CX),
                  blk(HALF, 0), blk(HALF, 0), blk(HALF, 0), blk(HALF, 0), blk(w, 0), blk(d, 0),
                  const((32, w)), const((8, w)), const((w, w)), const((8, w)), const((w, w)),
                  const((8, w)), const((4 * w, d)), pl.BlockSpec((1, d), lambda bi, i: (0, 0))],
        out_specs=pl.BlockSpec((None, tc, d), lambda bi, i: (bi, i, 0)),
        scratch_shapes=[pltpu.VMEM((8, tc + 32, w), F32), pltpu.VMEM((tc + 8, w), F32)],
        compiler_params=pltpu.CompilerParams(
            dimension_semantics=("parallel", "parallel"), vmem_limit_bytes=VMEM_LIMIT_BYTES),
        name="mixout_final" if final else "mixout",
    )(act3, act3, act3, ulo, uhi, ylo, yhi, od, x3, acw, avec, apw, bvec, bglu, ccw, wout, fg)


def _pad_rows(a, n):
    return jnp.pad(a, ((0, 0), (0, n - a.shape[1]), (0, 0)))


def _s5_tiles(tab, state_cols):
    dep = tab.shape[0]
    nt, rb = S5_TILES, 2 * S5_GROUP
    t = tab.reshape(dep, S5_Q, nt, rb, 128)
    t = jnp.transpose(t, (0, 2, 1, 3, 4)).reshape(dep, nt, S5_Q * rb, 128)
    per = S5_STATE if state_cols else S5_GROUP
    src = jnp.arange(128)
    dst = jnp.arange(2 * 128)
    spread = ((src[:, None] // per == dst[None, :] // (2 * per))
              & (src[:, None] % per == dst[None, :] % per)).astype(F32)
    row_group = (jnp.arange(S5_Q * rb) % rb) // S5_GROUP
    col_group = (dst % (2 * per)) // per
    mask = (row_group[:, None] == col_group[None, :]).astype(F32)
    return (jnp.einsum("dkrc,cn->dkrn", t, spread, precision=HIGHEST) * mask).astype(BF16)


def _s5_tables(lam_re, lam_im, b_re, b_im, c_re, c_im, log_dt):
    q, g, p, h = S5_Q, S5_GROUPS, S5_STATE, S5_GROUP
    dep = lam_re.shape[0]
    lr = jnp.minimum(lam_re, -1e-4)
    li = lam_im
    dt = jnp.exp(log_dt)[..., None]
    lrdt, lidt = (lr * dt)[:, None], (li * dt)[:, None]

    def power(steps):
        s = steps[None, :, None, None]
        mag = jnp.exp(lrdt * s)
        return (mag * jnp.cos(lidt * s))[:, :, :, None, :], (mag * jnp.sin(lidt * s))[:, :, :, None, :]

    tok = jnp.arange(q, dtype=F32)
    p1r, p1i = power(jnp.ones((1,), F32))
    nr, ni = p1r[:, 0, :, 0] - 1.0, p1i[:, 0, :, 0]
    den = lr * lr + li * li
    fr, fi = (nr * lr + ni * li) / den, (ni * lr - nr * li) / den
    bbr = jnp.transpose(fr[..., None] * b_re - fi[..., None] * b_im, (0, 1, 3, 2))[:, None]
    bbi = jnp.transpose(fr[..., None] * b_im + fi[..., None] * b_re, (0, 1, 3, 2))[:, None]
    cr, ci = c_re[:, None], c_im[:, None]

    rr, ri = power(q - 1.0 - tok)
    wc = jnp.concatenate([rr * bbr - ri * bbi, rr * bbi + ri * bbr], axis=-1)
    qr, qi = power(tok + 1.0)
    mc = jnp.concatenate([cr * qr - ci * qi, -(cr * qi + ci * qr)], axis=-1)
    er, ei = power(tok)
    cpr = jnp.transpose(cr * er - ci * ei, (0, 2, 1, 3, 4)).reshape(dep, g, q * h, p)
    cpi = jnp.transpose(cr * ei + ci * er, (0, 2, 1, 3, 4)).reshape(dep, g, q * h, p)
    kd = (jnp.einsum("dgcp,dgyp->dgcy", bbr[:, 0], cpr, precision=HIGHEST)
          - jnp.einsum("dgcp,dgyp->dgcy", bbi[:, 0], cpi, precision=HIGHEST))
    lane = jnp.arange(q * h)
    tk = jnp.stack([jnp.where(lane >= j * h, jnp.roll(kd, j * h, axis=-1), 0.0) for j in range(q)],
                   axis=1)

    flat = lambda t: t.reshape(dep, q * g * h, t.shape[-1])
    ar, ai = power(jnp.full((1,), float(q), F32))
    a = jnp.concatenate([ar.reshape(dep, 1, g * p), ai.reshape(dep, 1, g * p),
                         jnp.zeros((dep, 6, g * p), F32)], axis=1)
    return _s5_tiles(flat(wc), True), _s5_tiles(flat(tk), False), _s5_tiles(flat(mc), True), a


def _w_in_tail(w_in):
    n = MAIN_BLOCKS * BRANCH
    pad = jnp.zeros(w_in.shape[:-1] + (128 - 2 * DN_HEADS,), w_in.dtype)
    return jnp.concatenate([w_in[..., n + 2 * DN_HEADS:], w_in[..., n:n + 2 * DN_HEADS], pad],
                           axis=-1).astype(BF16)


def kernel(x, norm_g, w_in, a_conv_w, a_conv_b, a_ln_g, a_ln_b, a_pw_w, a_pw_b, s5_lambda_re, s5_lambda_im, s5_b_re, s5_b_im, s5_c_re, s5_c_im, s5_d, s5_log_dt, s5_glu_w, s5_glu_b, c_conv_w, d_conv_w, d_a_log, d_dt_bias, d_norm_g, w_out, final_g):
    bsz, seq, d = x.shape
    depth = w_in.shape[0]
    w = BRANCH
    tm = min(512, seq)
    tc = min(512, seq)
    dn_tc = min(256, seq)
    s5_rows = min(512, seq // S5_Q)

    w_tail = _w_in_tail(w_in)
    s5_wd, s5_td, s5_vd, s5_a = _s5_tables(s5_lambda_re, s5_lambda_im, s5_b_re, s5_b_im,
                                           s5_c_re, s5_c_im, s5_log_dt)
    acw = _pad_rows(a_conv_w, 32)
    zeros = jnp.zeros_like(a_conv_b)
    avec = jnp.stack([a_conv_b, a_ln_g, a_ln_b, a_pw_b, zeros, zeros, zeros, zeros], axis=1)
    bvec = jnp.stack([s5_d, s5_glu_b, zeros, zeros, zeros, zeros, zeros, zeros], axis=1)
    ccw = _pad_rows(c_conv_w, 8)
    dcw = _pad_rows(d_conv_w, 8)
    alog = jnp.repeat(d_a_log, DN_HEAD_DIM, axis=-1)[:, None, :]
    dtb = jnp.repeat(d_dt_bias, DN_HEAD_DIM, axis=-1)[:, None, :]
    ng = jnp.tile(d_norm_g, (1, DN_HEADS))[:, None, :]
    apw = a_pw_w.astype(BF16)
    bglu = s5_glu_w.astype(BF16)
    wout = w_out.astype(BF16)
    fg = final_g[None, :]

    def seq3(a):
        return a.reshape(bsz, seq, a.shape[-1])

    ng3 = norm_g[:, None, :]
    for l in range(depth):
        act, dn, ulo, uhi = _inproj(l, x.reshape(bsz * seq, d), ng3, w_in, w_tail, dcw, alog, dtb,
                                    tm, seq)
        ulo, uhi = seq3(ulo), seq3(uhi)
        ylo, yhi = _s5(l, ulo, uhi, s5_wd, s5_td, s5_vd, s5_a, s5_rows)
        od = _dnet(l, seq3(dn), ng, dn_tc)
        x = _mixout(l, seq3(act), ulo, uhi, ylo, yhi, od, x, acw, avec, apw, bvec, bglu, ccw, wout,
                    fg, tc, l == depth - 1)
    return x
```

```python
import functools

import jax
import jax.numpy as jnp
from jax import lax
from jax.experimental import pallas as pl
from jax.experimental.pallas import tpu as pltpu

F32 = jnp.float32
BF16 = jnp.bfloat16
HIGHEST = lax.Precision.HIGHEST

NORM_EPS = 1e-6
BRANCH = 256
CONF_KERNEL = 31
SC_KERNEL = 3
DN_HEADS = 4
DN_HEAD_DIM = 64
DN_CONV = 4
DN_CHUNK = 64
DN_GROUP = 16
S5_GROUPS = 16
S5_GROUP = 16
S5_STATE = 64
S5_Q = 8
S5_ROW = S5_Q * BRANCH
S5_NSTATE = S5_GROUPS * S5_STATE

COL_A, COL_B, COL_C, COL_QKV, MAIN_BLOCKS = 0, 3, 5, 9, 12
IN_COLS = MAIN_BLOCKS * BRANCH + 2 * DN_HEADS + BRANCH
TAIL_COLS = BRANCH + 128
HALF = BRANCH // 2
ACT_A, ACT_AZ, ACT_BZ, ACT_CB, ACT_CX, ACT_DZ, ACT_COLS = 0, 1, 2, 3, 4, 5, 6
DN_Q, DN_K, DN_V, DN_BETA, DN_G, DN_COLS = 0, 1, 2, 3, 4, 5

VMEM_LIMIT_BYTES = 56 * 1024 * 1024


def _mm(a, b):
    return jnp.dot(a.astype(BF16), b.astype(BF16), preferred_element_type=F32)


def _mm_tn(a, b):
    return lax.dot_general(a.astype(BF16), b.astype(BF16), (((0,), (0,)), ((), ())),
                           preferred_element_type=F32)


def _layer_spec(layer, shape):
    zeros = (0,) * len(shape)
    return pl.BlockSpec((None,) + tuple(shape), lambda *_: (layer,) + zeros)


def _sigmoid(x):
    return 1.0 / (1.0 + jnp.exp(-x))


def _silu(x):
    return x * _sigmoid(x)


def _inproj_kernel(x_ref, g_ref, w_ref, wt_ref, dcw_ref, alog_ref, dtb_ref, act_ref, dn_ref, ulo_ref,
                   uhi_ref, pad_scr, *, steps_per_seq, sub):
    tm = x_ref.shape[0]
    w = BRANCH
    first = pl.program_id(0) % steps_per_seq == 0

    @pl.when(first)
    def _():
        pad_scr[tm:tm + 8, :] = jnp.zeros((8, 3 * w), F32)

    pad_scr[0:8, :] = pad_scr[tm:tm + 8, :]

    row_h = lax.broadcasted_iota(jnp.int32, (w, w), 0) // DN_HEAD_DIM
    col_h = lax.broadcasted_iota(jnp.int32, (w, w), 1) // DN_HEAD_DIM
    ones_bd = jnp.where(row_h == col_h, 1.0, 0.0).astype(BF16)

    def project(hb, col, n):
        return jnp.dot(hb, w_ref[:, col * w:(col + n) * w], preferred_element_type=F32)

    def put(ref, rows, j, val):
        ref[rows, j * w:(j + 1) * w] = val

    def deltanet_inputs(rows, r0):
        xp = pad_scr[r0:r0 + sub + 8, :]
        acc = dcw_ref[DN_CONV - 1:DN_CONV, :] * xp[8:, :]
        for kk in range(DN_CONV - 1):
            acc = acc + dcw_ref[kk:kk + 1, :] * pltpu.roll(xp, DN_CONV - 1 - kk, 0)[8:, :]
        qkv = _silu(acc)
        q = qkv[:, 0:w]
        k = qkv[:, w:2 * w]
        ssq = jnp.dot((q * q).astype(BF16), ones_bd, preferred_element_type=F32)
        ssk = jnp.dot((k * k).astype(BF16), ones_bd, preferred_element_type=F32)
        put(dn_ref, rows, DN_Q, q * lax.rsqrt(ssq + NORM_EPS) * (DN_HEAD_DIM ** -0.5))
        put(dn_ref, rows, DN_K, k * lax.rsqrt(ssk + NORM_EPS))
        put(dn_ref, rows, DN_V, qkv[:, 2 * w:3 * w] * dn_ref[rows, DN_BETA * w:(DN_BETA + 1) * w])

    for r in range(tm // sub):
        r0 = r * sub
        rows = slice(r0, r0 + sub)
        x = x_ref[rows, :]
        h = x * lax.rsqrt(jnp.mean(x * x, axis=-1, keepdims=True) + NORM_EPS) * g_ref[...]
        hb = h.astype(BF16)
        pad_scr[r0 + 8:r0 + 8 + sub, :] = project(hb, COL_QKV, 3)
        p = project(hb, COL_A, 3)
        put(act_ref, rows, ACT_A, p[:, 0:w] * _sigmoid(p[:, w:2 * w]))
        put(act_ref, rows, ACT_AZ, _silu(p[:, 2 * w:3 * w]))
        p = project(hb, COL_B, 2)
        ulo_ref[rows, :] = p[:, 0:HALF]
        uhi_ref[rows, :] = p[:, HALF:w]
        put(act_ref, rows, ACT_BZ, _silu(p[:, w:2 * w]))
        p = project(hb, COL_C, 4)
        put(act_ref, rows, ACT_CB, p[:, 0:w] * _silu(p[:, 3 * w:4 * w]))
        put(act_ref, rows, ACT_CX, p[:, w:2 * w] * p[:, 2 * w:3 * w])
        p = jnp.dot(hb, wt_ref[...], preferred_element_type=F32)
        put(act_ref, rows, ACT_DZ, _silu(p[:, 0:w]))
        p = jnp.concatenate(
            [jnp.broadcast_to(p[:, w + j:w + j + 1], (sub, DN_HEAD_DIM)) for j in range(2 * DN_HEADS)],
            axis=1)
        sp_in = p[:, 0:w] + dtb_ref[...]
        softplus = jnp.maximum(sp_in, 0.0) + jnp.log1p(jnp.exp(-jnp.abs(sp_in)))
        put(dn_ref, rows, DN_G, -jnp.exp(alog_ref[...]) * softplus)
        put(dn_ref, rows, DN_BETA, _sigmoid(p[:, w:2 * w]))
        if r > 0:
            deltanet_inputs(slice(r0 - sub, r0), r0 - sub)
    deltanet_inputs(slice(tm - sub, tm), tm - sub)


def _inproj(layer, x2d, g, w, wt, dcw, alog, dtb, tm, seq):
    t, d = x2d.shape
    const = functools.partial(_layer_spec, layer)
    out = lambda n: pl.BlockSpec((tm, n * BRANCH), lambda i: (i, 0))
    return pl.pallas_call(
        functools.partial(_inproj_kernel, steps_per_seq=seq // tm, sub=min(256, tm)),
        out_shape=(jax.ShapeDtypeStruct((t, ACT_COLS * BRANCH), F32),
                   jax.ShapeDtypeStruct((t, DN_COLS * BRANCH), F32),
                   jax.ShapeDtypeStruct((t, HALF), F32),
                   jax.ShapeDtypeStruct((t, HALF), F32)),
        grid=(t // tm,),
        in_specs=[pl.BlockSpec((tm, d), lambda i: (i, 0)),
                  const((1, d)), const((d, IN_COLS)), const((d, TAIL_COLS)),
                  const((8, 3 * BRANCH)), const((1, BRANCH)), const((1, BRANCH))],
        out_specs=(out(ACT_COLS), out(DN_COLS), pl.BlockSpec((tm, HALF), lambda i: (i, 0)),
                   pl.BlockSpec((tm, HALF), lambda i: (i, 0))),
        scratch_shapes=[pltpu.VMEM((tm + 8, 3 * BRANCH), F32)],
        compiler_params=pltpu.CompilerParams(
            dimension_semantics=("arbitrary",), vmem_limit_bytes=VMEM_LIMIT_BYTES),
        name="inproj",
    )(x2d, g, w, wt, dcw, alog, dtb)


S5_TILES = S5_GROUPS // 2


def _block_transpose(pieces, block):
    lane = lax.broadcasted_iota(jnp.int32, pieces[0].shape, 1)
    n = len(pieces)
    s = n // 2
    while s:
        width = s * block
        lower = lane % (2 * width) < width
        out = list(pieces)
        for j in range(n):
            if j & s == 0:
                a, b = pieces[j], pieces[j + s]
                out[j] = jnp.where(lower, a, pltpu.roll(b, width, 1))
                out[j + s] = jnp.where(lower, pltpu.roll(a, 128 - width, 1), b)
        pieces = out
        s //= 2
    return pieces


def _s5_kernel(ulo_ref, uhi_ref, wd_ref, td_ref, vdt_ref, a_ref, ylo_ref, yhi_ref,
               carry_scr, z_scr, xp_scr):
    rows = ulo_ref.shape[0] // S5_Q
    n = S5_NSTATE
    tiles_per_half = S5_TILES // 2
    lanes = 2 * S5_GROUP

    @pl.when(pl.program_id(1) == 0)
    def _():
        carry_scr[...] = jnp.zeros_like(carry_scr)

    ub = []
    for half in (ulo_ref, uhi_ref):
        tok = [half[pl.ds(j, rows, stride=S5_Q), :] for j in range(S5_Q)]
        first = _block_transpose(tok[0:4], lanes)
        second = _block_transpose(tok[4:8], lanes)
        ub += [jnp.concatenate([first[k], second[k]], axis=1).astype(BF16)
               for k in range(tiles_per_half)]
    for k in range(S5_TILES):
        z = jnp.dot(ub[k], wd_ref[k], preferred_element_type=F32)
        z_scr[:, k * 128:(k + 1) * 128] = z[:, 0:128]
        z_scr[:, n + k * 128:n + (k + 1) * 128] = z[:, 128:256]

    ar = a_ref[0:1, :]
    ai = a_ref[1:2, :]

    def body(c, carry):
        xr, xi = carry
        xp_scr[pl.ds(c, 1), 0:n] = xr
        xp_scr[pl.ds(c, 1), n:2 * n] = xi
        zr = z_scr[pl.ds(c, 1), 0:n]
        zi = z_scr[pl.ds(c, 1), n:2 * n]
        return ar * xr - ai * xi + zr, ar * xi + ai * xr + zi

    xr, xi = lax.fori_loop(0, rows, body, (carry_scr[0:1, 0:n], carry_scr[0:1, n:2 * n]),
                           unroll=4)
    carry_scr[0:1, 0:n] = xr
    carry_scr[0:1, n:2 * n] = xi

    ys = []
    for k in range(S5_TILES):
        xp = jnp.concatenate([xp_scr[:, k * 128:(k + 1) * 128],
                              xp_scr[:, n + k * 128:n + (k + 1) * 128]], axis=1).astype(BF16)
        ys.append(jnp.dot(ub[k], td_ref[k], preferred_element_type=F32)
                  + lax.dot_general(xp, vdt_ref[k], (((1,), (1,)), ((), ())),
                                    preferred_element_type=F32))
    for h, out_ref in enumerate((ylo_ref, yhi_ref)):
        tiles = ys[h * tiles_per_half:(h + 1) * tiles_per_half]
        for m in range(2):
            tok = _block_transpose([y[:, m * 128:(m + 1) * 128] for y in tiles], lanes)
            for b in range(4):
                out_ref[pl.ds(4 * m + b, rows, stride=S5_Q), :] = tok[b]


def _s5(layer, ulo, uhi, wd, td, vd, a, rows):
    b, l, _ = ulo.shape
    const = functools.partial(_layer_spec, layer)
    half = lambda: pl.BlockSpec((None, rows * S5_Q, HALF), lambda bi, j: (bi, j, 0))
    tiles = (S5_TILES, BRANCH, BRANCH)
    return pl.pallas_call(
        _s5_kernel,
        out_shape=(jax.ShapeDtypeStruct((b, l, HALF), F32), jax.ShapeDtypeStruct((b, l, HALF), F32)),
        grid=(b, l // (rows * S5_Q)),
        in_specs=[half(), half(), const(tiles), const(tiles), const(tiles), const((8, S5_NSTATE))],
        out_specs=(half(), half()),
        scratch_shapes=[pltpu.VMEM((8, 2 * S5_NSTATE), F32),
                        pltpu.VMEM((rows, 2 * S5_NSTATE), F32),
                        pltpu.VMEM((rows, 2 * S5_NSTATE), F32)],
        compiler_params=pltpu.CompilerParams(
            dimension_semantics=("parallel", "arbitrary"), vmem_limit_bytes=VMEM_LIMIT_BYTES),
        name="s5",
    )(ulo, uhi, wd, td, vd, a)


def _tile_heads(x):
    return jnp.concatenate([x] * DN_HEADS, axis=0)


def _dnet_kernel(dn_ref, ng_ref, o_ref, s_scr, o_scr, u_scr, wq_scr, attn_scr, kd_scr, egl_scr):
    nb, tc, _ = dn_ref.shape
    c64 = DN_CHUNK
    w = DN_HEADS * DN_HEAD_DIM
    nck = tc // c64
    gb = DN_GROUP // nck

    @pl.when(pl.program_id(0) == 0)
    def _():
        s_scr[...] = jnp.zeros_like(s_scr)

    row_h = lax.broadcasted_iota(jnp.int32, (w, w), 0) // DN_HEAD_DIM
    col_h = lax.broadcasted_iota(jnp.int32, (w, w), 1) // DN_HEAD_DIM
    bd = row_h == col_h
    bd_bf = jnp.where(bd, 1.0, 0.0).astype(BF16)

    def mmb(a, b_bf):
        return jnp.dot(a.astype(BF16), b_bf, preferred_element_type=F32)

    def mmb_nt(a, b_bf):
        return lax.dot_general(a.astype(BF16), b_bf, (((1,), (1,)), ((), ())),
                               preferred_element_type=F32)

    ri = lax.broadcasted_iota(jnp.int32, (c64, w), 0)
    cj = lax.broadcasted_iota(jnp.int32, (c64, w), 1) % DN_HEAD_DIM
    causal = ri >= cj
    strict = ri > cj
    diag = ri == cj
    eye_all = jnp.where(diag, 1.0, 0.0).astype(F32)
    lt_bf = jnp.where(lax.broadcasted_iota(jnp.int32, (c64, c64), 0)
                      >= lax.broadcasted_iota(jnp.int32, (c64, c64), 1), 1.0, 0.0).astype(BF16)

    def blockdiag(x):
        return _tile_heads(x.astype(BF16)) * bd_bf

    grp = range(DN_GROUP)

    def prepare(gi, carry):
        def col(i, j):
            return dn_ref[gi * gb + i // nck, (i % nck) * c64:(i % nck + 1) * c64, j * w:(j + 1) * w]

        rows = [pl.ds(pl.multiple_of((gi * gb + i // nck) * tc + (i % nck) * c64, c64), c64)
                for i in grp]
        qc = [col(i, DN_Q) for i in grp]
        kc = [col(i, DN_K) for i in grp]
        bc = [col(i, DN_BETA) for i in grp]
        g = [col(i, DN_G) for i in grp]
        g_hi = [x.astype(BF16) for x in g]
        g_lo = [(g[i] - g_hi[i].astype(F32)).astype(BF16) for i in grp]
        gc = [jnp.dot(lt_bf, g_hi[i], preferred_element_type=F32)
              + jnp.dot(lt_bf, g_lo[i], preferred_element_type=F32) for i in grp]
        egc = [jnp.exp(x) for x in gc]
        gl = [x[c64 - 1:c64, :] for x in gc]
        kb = [kc[i] * bc[i] for i in grp]
        grow = [jnp.sum(jnp.where(diag, x, 0.0), axis=0, keepdims=True) for x in gc]
        decay = [jnp.exp(jnp.where(causal, gc[i] - grow[i], -jnp.inf)) for i in grp]
        k_bd = [blockdiag(x) for x in kc]
        kq = [mmb_nt(jnp.concatenate([kb[i], qc[i]], axis=0), k_bd[i]) for i in grp]
        for i in grp:
            attn_scr[rows[i], :] = kq[i][c64:2 * c64, :] * decay[i]
        x = [jnp.where(strict, -kq[i][0:c64, :] * decay[i], 0.0) for i in grp]
        p = [eye_all + x[i] for i in grp]
        x = [mmb(x[i], blockdiag(x[i])) for i in grp]
        for _ in range(4):
            xp = [mmb(jnp.concatenate([x[i], p[i]], axis=0), blockdiag(x[i])) for i in grp]
            x = [xp[i][0:c64, :] for i in grp]
            p = [p[i] + xp[i][c64:2 * c64, :] for i in grp]
        p = [p[i] + mmb(p[i], blockdiag(x[i])) for i in grp]
        for i in grp:
            ci = gi * DN_GROUP + i
            u_scr[rows[i], :] = mmb(p[i], blockdiag(col(i, DN_V)))
            wq_scr[ci, 0:c64, :] = mmb(p[i], blockdiag(kb[i] * egc[i]))
            wq_scr[ci, c64:2 * c64, :] = qc[i] * egc[i]
            kd_scr[rows[i], :] = kc[i] * jnp.exp(gl[i] - gc[i])
            egl_scr[ci] = jnp.broadcast_to(jnp.exp(gl[i]), (8, w))
        return carry

    lax.fori_loop(0, nb // gb, prepare, 0)

    bs = range(nb)

    def recur(c, carry):
        rows = [pl.ds(pl.multiple_of(b * tc + c * c64, c64), c64) for b in bs]
        s = [s_scr[b] for b in bs]
        ws_qs = [_mm(wq_scr[b * nck + c], s[b]) for b in bs]
        v_new = [u_scr[rows[b], :] - ws_qs[b][0:c64, :] for b in bs]
        for b in bs:
            o_scr[rows[b], :] = (ws_qs[b][c64:2 * c64, :]
                                 + mmb(attn_scr[rows[b], :], blockdiag(v_new[b])))
        for b in bs:
            s_scr[b] = (s[b] * egl_scr[b * nck + c][0:1, :]
                        + jnp.where(bd, _mm_tn(kd_scr[rows[b], :], v_new[b]), 0.0))
        return carry

    lax.fori_loop(0, nck, recur, 0)

    o = o_scr[...]
    ms = mmb(o * o, bd_bf) * (1.0 / DN_HEAD_DIM)
    o_ref[...] = (o * lax.rsqrt(ms + NORM_EPS) * ng_ref[...]).reshape(nb, tc, w)


def _dnet(layer, dn3, ng, tc):
    b, l, _ = dn3.shape
    w = BRANCH
    nchunk = b * tc // DN_CHUNK
    assert DN_GROUP % (tc // DN_CHUNK) == 0 and nchunk % DN_GROUP == 0
    act = lambda: pltpu.VMEM((b * tc, w), F32)
    return pl.pallas_call(
        _dnet_kernel,
        out_shape=jax.ShapeDtypeStruct((b, l, w), F32),
        grid=(l // tc,),
        in_specs=[pl.BlockSpec((b, tc, DN_COLS * w), lambda i: (0, i, 0)),
                  _layer_spec(layer, (1, w))],
        out_specs=pl.BlockSpec((b, tc, w), lambda i: (0, i, 0)),
        scratch_shapes=[pltpu.VMEM((b, w, w), F32),
                        act(), act(),
                        pltpu.VMEM((nchunk, 2 * DN_CHUNK, w), F32),
                        act(), act(),
                        pltpu.VMEM((nchunk, 8, w), F32)],
        compiler_params=pltpu.CompilerParams(
            dimension_semantics=("arbitrary",), vmem_limit_bytes=VMEM_LIMIT_BYTES),
        name="dnet",
    )(dn3, ng)


def _mixout_kernel(act_ref, ahalo_ref, chalo_ref, ulo_ref, uhi_ref, ylo_ref, yhi_ref, od_ref, x_ref,
                   acw_ref, avec_ref, apw_ref, bvec_ref, bglu_ref, ccw_ref, wout_ref, fg_ref,
                   out_ref, ash_scr, cpad_scr, *, final, sub):
    tc = x_ref.shape[0]
    w = BRANCH
    first = pl.program_id(1) == 0

    def act(j, rows=slice(None)):
        return act_ref[rows, j * w:(j + 1) * w]

    ash_scr[0, 0:32, :] = jnp.where(first, 0.0, ahalo_ref[...])
    ash_scr[0, 32:32 + tc, :] = act(ACT_A)
    apad = ash_scr[0]
    for r in range(1, 8):
        ash_scr[r] = pltpu.roll(apad, tc + 32 - r, 0)
    cpad_scr[0:8, :] = jnp.where(first, 0.0, chalo_ref[...])
    cpad_scr[8:8 + tc, :] = act(ACT_CX)

    conv_b, ln_g, ln_b, pw_b = (avec_ref[0:1, :], avec_ref[1:2, :], avec_ref[2:3, :],
                                avec_ref[3:4, :])
    d_skip, glu_b = bvec_ref[0:1, :], bvec_ref[1:2, :]

    for r in range(tc // sub):
        r0 = r * sub
        rows = slice(r0, r0 + sub)
        acc = conv_b
        for kk in range(CONF_KERNEL):
            off = 32 - (CONF_KERNEL - 1) + kk
            acc = acc + acw_ref[kk:kk + 1, :] * ash_scr[off % 8, pl.ds(r0 + off - off % 8, sub), :]
        mu = jnp.mean(acc, axis=-1, keepdims=True)
        xc = acc - mu
        ln = xc * lax.rsqrt(jnp.mean(xc * xc, axis=-1, keepdims=True) + NORM_EPS) * ln_g + ln_b
        ya = (_mm(_silu(ln), apw_ref[...]) + pw_b) * act(ACT_AZ, rows)
        ub = jnp.concatenate([ulo_ref[rows, :], uhi_ref[rows, :]], axis=1)
        ys = jnp.concatenate([ylo_ref[rows, :], yhi_ref[rows, :]], axis=1)
        yb = jax.nn.gelu(ys + d_skip * ub)
        yb = yb * _sigmoid(_mm(yb, bglu_ref[...]) + glu_b) * act(ACT_BZ, rows)
        cp = cpad_scr[r0:r0 + sub + 8, :]
        cacc = ccw_ref[SC_KERNEL - 1:SC_KERNEL, :] * cp[8:, :]
        for kk in range(SC_KERNEL - 1):
            cacc = cacc + ccw_ref[kk:kk + 1, :] * pltpu.roll(cp, SC_KERNEL - 1 - kk, 0)[8:, :]
        yc = act(ACT_CB, rows) * cacc
        yd = od_ref[rows, :] * act(ACT_DZ, rows)
        mixed = jnp.concatenate([ya, yb, yc, yd], axis=1)
        y = x_ref[rows, :] + _mm(mixed, wout_ref[...])
        if final:
            y = y * lax.rsqrt(jnp.mean(y * y, axis=-1, keepdims=True) + NORM_EPS) * fg_ref[...]
        out_ref[rows, :] = y


def _mixout(layer, act3, ulo, uhi, ylo, yhi, od, x3, acw, avec, apw, bvec, bglu, ccw, wout, fg, tc,
            final):
    b, l, d = x3.shape
    w = BRANCH
    blk = lambda width, col: pl.BlockSpec((None, tc, width), lambda bi, i: (bi, i, col))
    halo = lambda nrow, col: pl.BlockSpec(
        (None, nrow, w), lambda bi, i: (bi, jnp.maximum(i * (tc // nrow) - 1, 0), col))
    const = functools.partial(_layer_spec, layer)
    return pl.pallas_call(
        functools.partial(_mixout_kernel, final=final, sub=256),
        out_shape=jax.ShapeDtypeStruct((b, l, d), F32),
        grid=(b, l // tc),
        in_specs=[blk(ACT_COLS * w, 0), halo(32, ACT_A), halo(8, ACT_CX),
                  blk(HALF, 0), blk(HALF, 0), blk(HALF, 0), blk(HALF, 0), blk(w, 0), blk(d, 0),
                  const((32, w)), const((8, w)), const((w, w)), const((8, w)), const((w, w)),
                  const((8, w)), const((4 * w, d)), pl.BlockSpec((1, d), lambda bi, i: (0, 0))],
        out_specs=pl.BlockSpec((None, tc, d), lambda bi, i: (bi, i, 0)),
        scratch_shapes=[pltpu.VMEM((8, tc + 32, w), F32), pltpu.VMEM((tc + 8, w), F32)],
        compiler_params=pltpu.CompilerParams(
            dimension_semantics=("parallel", "parallel"), vmem_limit_bytes=VMEM_LIMIT_BYTES),
        name="mixout_final" if final else "mixout",
    )(act3, act3, act3, ulo, uhi, ylo, yhi, od, x3, acw, avec, apw, bvec, bglu, ccw, wout, fg)


def _pad_rows(a, n):
    return jnp.pad(a, ((0, 0), (0, n - a.shape[1]), (0, 0)))


def _s5_tiles(tab, state_cols):
    dep = tab.shape[0]
    nt, rb = S5_TILES, 2 * S5_GROUP
    t = tab.reshape(dep, S5_Q, nt, rb, 128)
    t = jnp.transpose(t, (0, 2, 1, 3, 4)).reshape(dep, nt, S5_Q * rb, 128)
    per = S5_STATE if state_cols else S5_GROUP
    src = jnp.arange(128)
    dst = jnp.arange(2 * 128)
    spread = ((src[:, None] // per == dst[None, :] // (2 * per))
              & (src[:, None] % per == dst[None, :] % per)).astype(F32)
    row_group = (jnp.arange(S5_Q * rb) % rb) // S5_GROUP
    col_group = (dst % (2 * per)) // per
    mask = (row_group[:, None] == col_group[None, :]).astype(F32)
    return (jnp.einsum("dkrc,cn->dkrn", t, spread, precision=HIGHEST) * mask).astype(BF16)


def _s5_tables(lam_re, lam_im, b_re, b_im, c_re, c_im, log_dt):
    q, g, p, h = S5_Q, S5_GROUPS, S5_STATE, S5_GROUP
    dep = lam_re.shape[0]
    lr = jnp.minimum(lam_re, -1e-4)
    li = lam_im
    dt = jnp.exp(log_dt)[..., None]
    lrdt, lidt = (lr * dt)[:, None], (li * dt)[:, None]

    def power(steps):
        s = steps[None, :, None, None]
        mag = jnp.exp(lrdt * s)
        return (mag * jnp.cos(lidt * s))[:, :, :, None, :], (mag * jnp.sin(lidt * s))[:, :, :, None, :]

    tok = jnp.arange(q, dtype=F32)
    p1r, p1i = power(jnp.ones((1,), F32))
    nr, ni = p1r[:, 0, :, 0] - 1.0, p1i[:, 0, :, 0]
    den = lr * lr + li * li
    fr, fi = (nr * lr + ni * li) / den, (ni * lr - nr * li) / den
    bbr = jnp.transpose(fr[..., None] * b_re - fi[..., None] * b_im, (0, 1, 3, 2))[:, None]
    bbi = jnp.transpose(fr[..., None] * b_im + fi[..., None] * b_re, (0, 1, 3, 2))[:, None]
    cr, ci = c_re[:, None], c_im[:, None]

    rr, ri = power(q - 1.0 - tok)
    wc = jnp.concatenate([rr * bbr - ri * bbi, rr * bbi + ri * bbr], axis=-1)
    qr, qi = power(tok + 1.0)
    mc = jnp.concatenate([cr * qr - ci * qi, -(cr * qi + ci * qr)], axis=-1)
    er, ei = power(tok)
    cpr = jnp.transpose(cr * er - ci * ei, (0, 2, 1, 3, 4)).reshape(dep, g, q * h, p)
    cpi = jnp.transpose(cr * ei + ci * er, (0, 2, 1, 3, 4)).reshape(dep, g, q * h, p)
    kd = (jnp.einsum("dgcp,dgyp->dgcy", bbr[:, 0], cpr, precision=HIGHEST)
          - jnp.einsum("dgcp,dgyp->dgcy", bbi[:, 0], cpi, precision=HIGHEST))
    lane = jnp.arange(q * h)
    tk = jnp.stack([jnp.where(lane >= j * h, jnp.roll(kd, j * h, axis=-1), 0.0) for j in range(q)],
                   axis=1)

    flat = lambda t: t.reshape(dep, q * g * h, t.shape[-1])
    ar, ai = power(jnp.full((1,), float(q), F32))
    a = jnp.concatenate([ar.reshape(dep, 1, g * p), ai.reshape(dep, 1, g * p),
                         jnp.zeros((dep, 6, g * p), F32)], axis=1)
    return _s5_tiles(flat(wc), True), _s5_tiles(flat(tk), False), _s5_tiles(flat(mc), True), a


def _w_in_tail(w_in):
    n = MAIN_BLOCKS * BRANCH
    pad = jnp.zeros(w_in.shape[:-1] + (128 - 2 * DN_HEADS,), w_in.dtype)
    return jnp.concatenate([w_in[..., n + 2 * DN_HEADS:], w_in[..., n:n + 2 * DN_HEADS], pad],
                           axis=-1).astype(BF16)


def kernel(x, norm_g, w_in, a_conv_w, a_conv_b, a_ln_g, a_ln_b, a_pw_w, a_pw_b, s5_lambda_re, s5_lambda_im, s5_b_re, s5_b_im, s5_c_re, s5_c_im, s5_d, s5_log_dt, s5_glu_w, s5_glu_b, c_conv_w, d_conv_w, d_a_log, d_dt_bias, d_norm_g, w_out, final_g):
    bsz, seq, d = x.shape
    depth = w_in.shape[0]
    w = BRANCH
    tm = min(512, seq)
    tc = min(512, seq)
    dn_tc = min(256, seq)
    s5_rows = min(512, seq // S5_Q)

    w_bf = w_in.astype(BF16)
    w_tail = _w_in_tail(w_in)
    s5_wd, s5_td, s5_vd, s5_a = _s5_tables(s5_lambda_re, s5_lambda_im, s5_b_re, s5_b_im,
                                           s5_c_re, s5_c_im, s5_log_dt)
    acw = _pad_rows(a_conv_w, 32)
    zeros = jnp.zeros_like(a_conv_b)
    avec = jnp.stack([a_conv_b, a_ln_g, a_ln_b, a_pw_b, zeros, zeros, zeros, zeros], axis=1)
    bvec = jnp.stack([s5_d, s5_glu_b, zeros, zeros, zeros, zeros, zeros, zeros], axis=1)
    ccw = _pad_rows(c_conv_w, 8)
    dcw = _pad_rows(d_conv_w, 8)
    alog = jnp.repeat(d_a_log, DN_HEAD_DIM, axis=-1)[:, None, :]
    dtb = jnp.repeat(d_dt_bias, DN_HEAD_DIM, axis=-1)[:, None, :]
    ng = jnp.tile(d_norm_g, (1, DN_HEADS))[:, None, :]
    apw = a_pw_w.astype(BF16)
    bglu = s5_glu_w.astype(BF16)
    wout = w_out.astype(BF16)
    fg = final_g[None, :]

    def seq3(a):
        return a.reshape(bsz, seq, a.shape[-1])

    ng3 = norm_g[:, None, :]
    for l in range(depth):
        act, dn, ulo, uhi = _inproj(l, x.reshape(bsz * seq, d), ng3, w_bf, w_tail, dcw, alog, dtb,
                                    tm, seq)
        ulo, uhi = seq3(ulo), seq3(uhi)
        ylo, yhi = _s5(l, ulo, uhi, s5_wd, s5_td, s5_vd, s5_a, s5_rows)
        od = _dnet(l, seq3(dn), ng, dn_tc)
        x = _mixout(l, seq3(act), ulo, uhi, ylo, yhi, od, x, acw, avec, apw, bvec, bglu, ccw, wout,
                    fg, tc, l == depth - 1)
    return x
```

```python
import functools

import jax
import jax.numpy as jnp
from jax import lax
from jax.experimental import pallas as pl
from jax.experimental.pallas import tpu as pltpu

F32 = jnp.float32
BF16 = jnp.bfloat16
HIGHEST = lax.Precision.HIGHEST

NORM_EPS = 1e-6
BRANCH = 256
CONF_KERNEL = 31
SC_KERNEL = 3
DN_HEADS = 4
DN_HEAD_DIM = 64
DN_CONV = 4
DN_CHUNK = 64
DN_GROUP = 16
S5_GROUPS = 16
S5_GROUP = 16
S5_STATE = 64
S5_Q = 8
S5_ROW = S5_Q * BRANCH
S5_NSTATE = S5_GROUPS * S5_STATE

COL_A, COL_B, COL_C, COL_QKV, MAIN_BLOCKS = 0, 3, 5, 9, 12
IN_COLS = MAIN_BLOCKS * BRANCH + 2 * DN_HEADS + BRANCH
TAIL_COLS = BRANCH + 128
HALF = BRANCH // 2
ACT_A, ACT_AZ, ACT_BZ, ACT_CB, ACT_CX, ACT_DZ, ACT_COLS = 0, 1, 2, 3, 4, 5, 6
DN_Q, DN_K, DN_V, DN_BETA, DN_G, DN_COLS = 0, 1, 2, 3, 4, 5

VMEM_LIMIT_BYTES = 56 * 1024 * 1024


def _mm(a, b):
    return jnp.dot(a.astype(BF16), b.astype(BF16), preferred_element_type=F32)


def _mm_tn(a, b):
    return lax.dot_general(a.astype(BF16), b.astype(BF16), (((0,), (0,)), ((), ())),
                           preferred_element_type=F32)


def _layer_spec(layer, shape):
    zeros = (0,) * len(shape)
    return pl.BlockSpec((None,) + tuple(shape), lambda *_: (layer,) + zeros)


def _sigmoid(x):
    return 1.0 / (1.0 + jnp.exp(-x))


def _silu(x):
    return x * _sigmoid(x)


def _inproj_kernel(x_ref, g_ref, w_ref, wt_ref, dcw_ref, alog_ref, dtb_ref, act_ref, dn_ref, ulo_ref,
                   uhi_ref, pad_scr, *, steps_per_seq, sub):
    tm = x_ref.shape[0]
    w = BRANCH
    first = pl.program_id(0) % steps_per_seq == 0

    @pl.when(first)
    def _():
        pad_scr[tm:tm + 8, :] = jnp.zeros((8, 3 * w), F32)

    pad_scr[0:8, :] = pad_scr[tm:tm + 8, :]

    row_h = lax.broadcasted_iota(jnp.int32, (w, w), 0) // DN_HEAD_DIM
    col_h = lax.broadcasted_iota(jnp.int32, (w, w), 1) // DN_HEAD_DIM
    ones_bd = jnp.where(row_h == col_h, 1.0, 0.0).astype(BF16)

    def project(hb, col, n):
        return jnp.dot(hb, w_ref[:, col * w:(col + n) * w], preferred_element_type=F32)

    def put(ref, rows, j, val):
        ref[rows, j * w:(j + 1) * w] = val

    def deltanet_inputs(rows, r0):
        xp = pad_scr[r0:r0 + sub + 8, :]
        acc = dcw_ref[DN_CONV - 1:DN_CONV, :] * xp[8:, :]
        for kk in range(DN_CONV - 1):
            acc = acc + dcw_ref[kk:kk + 1, :] * pltpu.roll(xp, DN_CONV - 1 - kk, 0)[8:, :]
        qkv = _silu(acc)
        q = qkv[:, 0:w]
        k = qkv[:, w:2 * w]
        ssq = jnp.dot((q * q).astype(BF16), ones_bd, preferred_element_type=F32)
        ssk = jnp.dot((k * k).astype(BF16), ones_bd, preferred_element_type=F32)
        put(dn_ref, rows, DN_Q, q * lax.rsqrt(ssq + NORM_EPS) * (DN_HEAD_DIM ** -0.5))
        put(dn_ref, rows, DN_K, k * lax.rsqrt(ssk + NORM_EPS))
        put(dn_ref, rows, DN_V, qkv[:, 2 * w:3 * w] * dn_ref[rows, DN_BETA * w:(DN_BETA + 1) * w])

    for r in range(tm // sub):
        r0 = r * sub
        rows = slice(r0, r0 + sub)
        x = x_ref[rows, :]
        h = x * lax.rsqrt(jnp.mean(x * x, axis=-1, keepdims=True) + NORM_EPS) * g_ref[...]
        hb = h.astype(BF16)
        pad_scr[r0 + 8:r0 + 8 + sub, :] = project(hb, COL_QKV, 3)
        p = project(hb, COL_A, 3)
        put(act_ref, rows, ACT_A, p[:, 0:w] * _sigmoid(p[:, w:2 * w]))
        put(act_ref, rows, ACT_AZ, _silu(p[:, 2 * w:3 * w]))
        p = project(hb, COL_B, 2)
        ulo_ref[rows, :] = p[:, 0:HALF]
        uhi_ref[rows, :] = p[:, HALF:w]
        put(act_ref, rows, ACT_BZ, _silu(p[:, w:2 * w]))
        p = project(hb, COL_C, 4)
        put(act_ref, rows, ACT_CB, p[:, 0:w] * _silu(p[:, 3 * w:4 * w]))
        put(act_ref, rows, ACT_CX, p[:, w:2 * w] * p[:, 2 * w:3 * w])
        p = jnp.dot(hb, wt_ref[...], preferred_element_type=F32)
        put(act_ref, rows, ACT_DZ, _silu(p[:, 0:w]))
        p = jnp.concatenate(
            [jnp.broadcast_to(p[:, w + j:w + j + 1], (sub, DN_HEAD_DIM)) for j in range(2 * DN_HEADS)],
            axis=1)
        sp_in = p[:, 0:w] + dtb_ref[...]
        softplus = jnp.maximum(sp_in, 0.0) + jnp.log1p(jnp.exp(-jnp.abs(sp_in)))
        put(dn_ref, rows, DN_G, -jnp.exp(alog_ref[...]) * softplus)
        put(dn_ref, rows, DN_BETA, _sigmoid(p[:, w:2 * w]))
        if r > 0:
            deltanet_inputs(slice(r0 - sub, r0), r0 - sub)
    deltanet_inputs(slice(tm - sub, tm), tm - sub)


def _inproj(layer, x2d, g, w, wt, dcw, alog, dtb, tm, seq):
    t, d = x2d.shape
    const = functools.partial(_layer_spec, layer)
    out = lambda n: pl.BlockSpec((tm, n * BRANCH), lambda i: (i, 0))
    return pl.pallas_call(
        functools.partial(_inproj_kernel, steps_per_seq=seq // tm, sub=min(256, tm)),
        out_shape=(jax.ShapeDtypeStruct((t, ACT_COLS * BRANCH), F32),
                   jax.ShapeDtypeStruct((t, DN_COLS * BRANCH), F32),
                   jax.ShapeDtypeStruct((t, HALF), F32),
                   jax.ShapeDtypeStruct((t, HALF), F32)),
        grid=(t // tm,),
        in_specs=[pl.BlockSpec((tm, d), lambda i: (i, 0)),
                  const((1, d)), const((d, IN_COLS)), const((d, TAIL_COLS)),
                  const((8, 3 * BRANCH)), const((1, BRANCH)), const((1, BRANCH))],
        out_specs=(out(ACT_COLS), out(DN_COLS), pl.BlockSpec((tm, HALF), lambda i: (i, 0)),
                   pl.BlockSpec((tm, HALF), lambda i: (i, 0))),
        scratch_shapes=[pltpu.VMEM((tm + 8, 3 * BRANCH), F32)],
        compiler_params=pltpu.CompilerParams(
            dimension_semantics=("arbitrary",), vmem_limit_bytes=VMEM_LIMIT_BYTES),
        name="inproj",
    )(x2d, g, w, wt, dcw, alog, dtb)


S5_TILES = S5_GROUPS // 2


def _block_transpose(pieces, block):
    lane = lax.broadcasted_iota(jnp.int32, pieces[0].shape, 1)
    n = len(pieces)
    s = n // 2
    while s:
        width = s * block
        lower = lane % (2 * width) < width
        out = list(pieces)
        for j in range(n):
            if j & s == 0:
                a, b = pieces[j], pieces[j + s]
                out[j] = jnp.where(lower, a, pltpu.roll(b, width, 1))
                out[j + s] = jnp.where(lower, pltpu.roll(a, 128 - width, 1), b)
        pieces = out
        s //= 2
    return pieces


def _s5_kernel(ulo_ref, uhi_ref, wd_ref, td_ref, vdt_ref, a_ref, ylo_ref, yhi_ref,
               carry_scr, z_scr, xp_scr):
    rows = ulo_ref.shape[0] // S5_Q
    n = S5_NSTATE
    tiles_per_half = S5_TILES // 2
    lanes = 2 * S5_GROUP

    @pl.when(pl.program_id(1) == 0)
    def _():
        carry_scr[...] = jnp.zeros_like(carry_scr)

    ub = []
    for half in (ulo_ref, uhi_ref):
        tok = [half[pl.ds(j, rows, stride=S5_Q), :] for j in range(S5_Q)]
        first = _block_transpose(tok[0:4], lanes)
        second = _block_transpose(tok[4:8], lanes)
        ub += [jnp.concatenate([first[k], second[k]], axis=1).astype(BF16)
               for k in range(tiles_per_half)]
    for k in range(S5_TILES):
        z = jnp.dot(ub[k], wd_ref[k], preferred_element_type=F32)
        z_scr[:, k * 128:(k + 1) * 128] = z[:, 0:128]
        z_scr[:, n + k * 128:n + (k + 1) * 128] = z[:, 128:256]

    ar = a_ref[0:1, :]
    ai = a_ref[1:2, :]

    def body(c, carry):
        xr, xi = carry
        xp_scr[pl.ds(c, 1), 0:n] = xr
        xp_scr[pl.ds(c, 1), n:2 * n] = xi
        zr = z_scr[pl.ds(c, 1), 0:n]
        zi = z_scr[pl.ds(c, 1), n:2 * n]
        return ar * xr - ai * xi + zr, ar * xi + ai * xr + zi

    xr, xi = lax.fori_loop(0, rows, body, (carry_scr[0:1, 0:n], carry_scr[0:1, n:2 * n]),
                           unroll=4)
    carry_scr[0:1, 0:n] = xr
    carry_scr[0:1, n:2 * n] = xi

    ys = []
    for k in range(S5_TILES):
        xp = jnp.concatenate([xp_scr[:, k * 128:(k + 1) * 128],
                              xp_scr[:, n + k * 128:n + (k + 1) * 128]], axis=1).astype(BF16)
        ys.append(jnp.dot(ub[k], td_ref[k], preferred_element_type=F32)
                  + lax.dot_general(xp, vdt_ref[k], (((1,), (1,)), ((), ())),
                                    preferred_element_type=F32))
    for h, out_ref in enumerate((ylo_ref, yhi_ref)):
        tiles = ys[h * tiles_per_half:(h + 1) * tiles_per_half]
        for m in range(2):
            tok = _block_transpose([y[:, m * 128:(m + 1) * 128] for y in tiles], lanes)
            for b in range(4):
                out_ref[pl.ds(4 * m + b, rows, stride=S5_Q), :] = tok[b]


def _s5(layer, ulo, uhi, wd, td, vd, a, rows):
    b, l, _ = ulo.shape
    const = functools.partial(_layer_spec, layer)
    half = lambda: pl.BlockSpec((None, rows * S5_Q, HALF), lambda bi, j: (bi, j, 0))
    tiles = (S5_TILES, BRANCH, BRANCH)
    return pl.pallas_call(
        _s5_kernel,
        out_shape=(jax.ShapeDtypeStruct((b, l, HALF), F32), jax.ShapeDtypeStruct((b, l, HALF), F32)),
        grid=(b, l // (rows * S5_Q)),
        in_specs=[half(), half(), const(tiles), const(tiles), const(tiles), const((8, S5_NSTATE))],
        out_specs=(half(), half()),
        scratch_shapes=[pltpu.VMEM((8, 2 * S5_NSTATE), F32),
                        pltpu.VMEM((rows, 2 * S5_NSTATE), F32),
                        pltpu.VMEM((rows, 2 * S5_NSTATE), F32)],
        compiler_params=pltpu.CompilerParams(
            dimension_semantics=("parallel", "arbitrary"), vmem_limit_bytes=VMEM_LIMIT_BYTES),
        name="s5",
    )(ulo, uhi, wd, td, vd, a)


def _tile_heads(x):
    return jnp.concatenate([x] * DN_HEADS, axis=0)


def _dnet_kernel(dn_ref, ng_ref, o_ref, s_scr, o_scr, u_scr, wq_scr, attn_scr, kd_scr, egl_scr):
    nb, tc, _ = dn_ref.shape
    c64 = DN_CHUNK
    w = DN_HEADS * DN_HEAD_DIM
    nck = tc // c64
    gb = DN_GROUP // nck

    @pl.when(pl.program_id(0) == 0)
    def _():
        s_scr[...] = jnp.zeros_like(s_scr)

    row_h = lax.broadcasted_iota(jnp.int32, (w, w), 0) // DN_HEAD_DIM
    col_h = lax.broadcasted_iota(jnp.int32, (w, w), 1) // DN_HEAD_DIM
    bd = row_h == col_h
    bd_bf = jnp.where(bd, 1.0, 0.0).astype(BF16)

    def mmb(a, b_bf):
        return jnp.dot(a.astype(BF16), b_bf, preferred_element_type=F32)

    def mmb_nt(a, b_bf):
        return lax.dot_general(a.astype(BF16), b_bf, (((1,), (1,)), ((), ())),
                               preferred_element_type=F32)

    ri = lax.broadcasted_iota(jnp.int32, (c64, w), 0)
    cj = lax.broadcasted_iota(jnp.int32, (c64, w), 1) % DN_HEAD_DIM
    causal = ri >= cj
    strict = ri > cj
    diag = ri == cj
    eye_all = jnp.where(diag, 1.0, 0.0).astype(F32)
    lt_bf = jnp.where(lax.broadcasted_iota(jnp.int32, (c64, c64), 0)
                      >= lax.broadcasted_iota(jnp.int32, (c64, c64), 1), 1.0, 0.0).astype(BF16)

    def blockdiag(x):
        return _tile_heads(x.astype(BF16)) * bd_bf

    grp = range(DN_GROUP)

    def prepare(gi, carry):
        def col(i, j):
            return dn_ref[gi * gb + i // nck, (i % nck) * c64:(i % nck + 1) * c64, j * w:(j + 1) * w]

        rows = [pl.ds(pl.multiple_of((gi * gb + i // nck) * tc + (i % nck) * c64, c64), c64)
                for i in grp]
        qc = [col(i, DN_Q) for i in grp]
        kc = [col(i, DN_K) for i in grp]
        bc = [col(i, DN_BETA) for i in grp]
        g = [col(i, DN_G) for i in grp]
        g_hi = [x.astype(BF16) for x in g]
        g_lo = [(g[i] - g_hi[i].astype(F32)).astype(BF16) for i in grp]
        gc = [jnp.dot(lt_bf, g_hi[i], preferred_element_type=F32)
              + jnp.dot(lt_bf, g_lo[i], preferred_element_type=F32) for i in grp]
        egc = [jnp.exp(x) for x in gc]
        gl = [x[c64 - 1:c64, :] for x in gc]
        kb = [kc[i] * bc[i] for i in grp]
        grow = [jnp.sum(jnp.where(diag, x, 0.0), axis=0, keepdims=True) for x in gc]
        decay = [jnp.exp(jnp.where(causal, gc[i] - grow[i], -jnp.inf)) for i in grp]
        k_bd = [blockdiag(x) for x in kc]
        kq = [mmb_nt(jnp.concatenate([kb[i], qc[i]], axis=0), k_bd[i]) for i in grp]
        for i in grp:
            attn_scr[rows[i], :] = kq[i][c64:2 * c64, :] * decay[i]
        x = [jnp.where(strict, -kq[i][0:c64, :] * decay[i], 0.0) for i in grp]
        p = [eye_all + x[i] for i in grp]
        x = [mmb(x[i], blockdiag(x[i])) for i in grp]
        for _ in range(4):
            xp = [mmb(jnp.concatenate([x[i], p[i]], axis=0), blockdiag(x[i])) for i in grp]
            x = [xp[i][0:c64, :] for i in grp]
            p = [p[i] + xp[i][c64:2 * c64, :] for i in grp]
        p = [p[i] + mmb(p[i], blockdiag(x[i])) for i in grp]
        for i in grp:
            ci = gi * DN_GROUP + i
            u_scr[rows[i], :] = mmb(p[i], blockdiag(col(i, DN_V)))
            wq_scr[ci, 0:c64, :] = mmb(p[i], blockdiag(kb[i] * egc[i]))
            wq_scr[ci, c64:2 * c64, :] = qc[i] * egc[i]
            kd_scr[rows[i], :] = kc[i] * jnp.exp(gl[i] - gc[i])
            egl_scr[ci] = jnp.broadcast_to(jnp.exp(gl[i]), (8, w))
        return carry

    lax.fori_loop(0, nb // gb, prepare, 0)

    bs = range(nb)

    def recur(c, carry):
        rows = [pl.ds(pl.multiple_of(b * tc + c * c64, c64), c64) for b in bs]
        s = [s_scr[b] for b in bs]
        ws_qs = [_mm(wq_scr[b * nck + c], s[b]) for b in bs]
        v_new = [u_scr[rows[b], :] - ws_qs[b][0:c64, :] for b in bs]
        for b in bs:
            o_scr[rows[b], :] = (ws_qs[b][c64:2 * c64, :]
                                 + mmb(attn_scr[rows[b], :], blockdiag(v_new[b])))
        for b in bs:
            s_scr[b] = (s[b] * egl_scr[b * nck + c][0:1, :]
                        + jnp.where(bd, _mm_tn(kd_scr[rows[b], :], v_new[b]), 0.0))
        return carry

    lax.fori_loop(0, nck, recur, 0)

    o = o_scr[...]
    ms = mmb(o * o, bd_bf) * (1.0 / DN_HEAD_DIM)
    o_ref[...] = (o * lax.rsqrt(ms + NORM_EPS) * ng_ref[...]).reshape(nb, tc, w)


def _dnet(layer, dn3, ng, tc):
    b, l, _ = dn3.shape
    w = BRANCH
    nchunk = b * tc // DN_CHUNK
    assert DN_GROUP % (tc // DN_CHUNK) == 0 and nchunk % DN_GROUP == 0
    act = lambda: pltpu.VMEM((b * tc, w), F32)
    return pl.pallas_call(
        _dnet_kernel,
        out_shape=jax.ShapeDtypeStruct((b, l, w), F32),
        grid=(l // tc,),
        in_specs=[pl.BlockSpec((b, tc, DN_COLS * w), lambda i: (0, i, 0)),
                  _layer_spec(layer, (1, w))],
        out_specs=pl.BlockSpec((b, tc, w), lambda i: (0, i, 0)),
        scratch_shapes=[pltpu.VMEM((b, w, w), F32),
                        act(), act(),
                        pltpu.VMEM((nchunk, 2 * DN_CHUNK, w), F32),
                        act(), act(),
                        pltpu.VMEM((nchunk, 8, w), F32)],
        compiler_params=pltpu.CompilerParams(
            dimension_semantics=("arbitrary",), vmem_limit_bytes=VMEM_LIMIT_BYTES),
        name="dnet",
    )(dn3, ng)


def _mixout_kernel(act_ref, ahalo_ref, chalo_ref, ulo_ref, uhi_ref, ylo_ref, yhi_ref, od_ref, x_ref,
                   acw_ref, avec_ref, apw_ref, bvec_ref, bglu_ref, ccw_ref, wout_ref, fg_ref,
                   out_ref, ash_scr, cpad_scr, *, final, sub):
    tc = x_ref.shape[0]
    w = BRANCH
    first = pl.program_id(1) == 0

    def act(j, rows=slice(None)):
        return act_ref[rows, j * w:(j + 1) * w]

    ash_scr[0, 0:32, :] = jnp.where(first, 0.0, ahalo_ref[...])
    ash_scr[0, 32:32 + tc, :] = act(ACT_A)
    apad = ash_scr[0]
    for r in range(1, 8):
        ash_scr[r] = pltpu.roll(apad, tc + 32 - r, 0)
    cpad_scr[0:8, :] = jnp.where(first, 0.0, chalo_ref[...])
    cpad_scr[8:8 + tc, :] = act(ACT_CX)

    conv_b, ln_g, ln_b, pw_b = (avec_ref[0:1, :], avec_ref[1:2, :], avec_ref[2:3, :],
                                avec_ref[3:4, :])
    d_skip, glu_b = bvec_ref[0:1, :], bvec_ref[1:2, :]

    for r in range(tc // sub):
        r0 = r * sub
        rows = slice(r0, r0 + sub)
        acc = conv_b
        for kk in range(CONF_KERNEL):
            off = 32 - (CONF_KERNEL - 1) + kk
            acc = acc + acw_ref[kk:kk + 1, :] * ash_scr[off % 8, pl.ds(r0 + off - off % 8, sub), :]
        mu = jnp.mean(acc, axis=-1, keepdims=True)
        xc = acc - mu
        ln = xc * lax.rsqrt(jnp.mean(xc * xc, axis=-1, keepdims=True) + NORM_EPS) * ln_g + ln_b
        ya = (_mm(_silu(ln), apw_ref[...]) + pw_b) * act(ACT_AZ, rows)
        ub = jnp.concatenate([ulo_ref[rows, :], uhi_ref[rows, :]], axis=1)
        ys = jnp.concatenate([ylo_ref[rows, :], yhi_ref[rows, :]], axis=1)
        yb = jax.nn.gelu(ys + d_skip * ub)
        yb = yb * _sigmoid(_mm(yb, bglu_ref[...]) + glu_b) * act(ACT_BZ, rows)
        cp = cpad_scr[r0:r0 + sub + 8, :]
        cacc = ccw_ref[SC_KERNEL - 1:SC_KERNEL, :] * cp[8:, :]
        for kk in range(SC_KERNEL - 1):
            cacc = cacc + ccw_ref[kk:kk + 1, :] * pltpu.roll(cp, SC_KERNEL - 1 - kk, 0)[8:, :]
        yc = act(ACT_CB, rows) * cacc
        yd = od_ref[rows, :] * act(ACT_DZ, rows)
        mixed = jnp.concatenate([ya, yb, yc, yd], axis=1)
        y = x_ref[rows, :] + _mm(mixed, wout_ref[...])
        if final:
            y = y * lax.rsqrt(jnp.mean(y * y, axis=-1, keepdims=True) + NORM_EPS) * fg_ref[...]
        out_ref[rows, :] = y


def _mixout(layer, act3, ulo, uhi, ylo, yhi, od, x3, acw, avec, apw, bvec, bglu, ccw, wout, fg, tc,
            final):
    b, l, d = x3.shape
    w = BRANCH
    blk = lambda width, col: pl.BlockSpec((None, tc, width), lambda bi, i: (bi, i, col))
    halo = lambda nrow, col: pl.BlockSpec(
        (None, nrow, w), lambda bi, i: (bi, jnp.maximum(i * (tc // nrow) - 1, 0), col))
    const = functools.partial(_layer_spec, layer)
    return pl.pallas_call(
        functools.partial(_mixout_kernel, final=final, sub=256),
        out_shape=jax.ShapeDtypeStruct((b, l, d), F32),
        grid=(b, l // tc),
        in_specs=[blk(ACT_COLS * w, 0), halo(32, ACT_A), halo(8, ACT_CX),
                  blk(HALF, 0), blk(HALF, 0), blk(HALF, 0), blk(HALF, 0), blk(w, 0), blk(d, 0),
                  const((32, w)), const((8, w)), const((w, w)), const((8, w)), const((w, w)),
                  const((8, w)), const((4 * w, d)), pl.BlockSpec((1, d), lambda bi, i: (0, 0))],
        out_specs=pl.BlockSpec((None, tc, d), lambda bi, i: (bi, i, 0)),
        scratch_shapes=[pltpu.VMEM((8, tc + 32, w), F32), pltpu.VMEM((tc + 8, w), F32)],
        compiler_params=pltpu.CompilerParams(
            dimension_semantics=("parallel", "parallel"), vmem_limit_bytes=VMEM_LIMIT_BYTES),
        name="mixout_final" if final else "mixout",
    )(act3, act3, act3, ulo, uhi, ylo, yhi, od, x3, acw, avec, apw, bvec, bglu, ccw, wout, fg)


def _pad_rows(a, n):
    return jnp.pad(a, ((0, 0), (0, n - a.shape[1]), (0, 0)))


def _s5_tiles(tab, state_cols):
    dep = tab.shape[0]
    nt, rb = S5_TILES, 2 * S5_GROUP
    t = tab.reshape(dep, S5_Q, nt, rb, 128)
    t = jnp.transpose(t, (0, 2, 1, 3, 4)).reshape(dep, nt, S5_Q * rb, 128)
    per = S5_STATE if state_cols else S5_GROUP
    src = jnp.arange(128)
    dst = jnp.arange(2 * 128)
    spread = ((src[:, None] // per == dst[None, :] // (2 * per))
              & (src[:, None] % per == dst[None, :] % per)).astype(F32)
    row_group = (jnp.arange(S5_Q * rb) % rb) // S5_GROUP
    col_group = (dst % (2 * per)) // per
    mask = (row_group[:, None] == col_group[None, :]).astype(F32)
    return (jnp.einsum("dkrc,cn->dkrn", t, spread, precision=HIGHEST) * mask).astype(BF16)


def _s5_tables(lam_re, lam_im, b_re, b_im, c_re, c_im, log_dt):
    q, g, p, h = S5_Q, S5_GROUPS, S5_STATE, S5_GROUP
    dep = lam_re.shape[0]
    lr = jnp.minimum(lam_re, -1e-4)
    li = lam_im
    dt = jnp.exp(log_dt)[..., None]
    lrdt, lidt = (lr * dt)[:, None], (li * dt)[:, None]

    def power(steps):
        s = steps[None, :, None, None]
        mag = jnp.exp(lrdt * s)
        return (mag * jnp.cos(lidt * s))[:, :, :, None, :], (mag * jnp.sin(lidt * s))[:, :, :, None, :]

    tok = jnp.arange(q, dtype=F32)
    p1r, p1i = power(jnp.ones((1,), F32))
    nr, ni = p1r[:, 0, :, 0] - 1.0, p1i[:, 0, :, 0]
    den = lr * lr + li * li
    fr, fi = (nr * lr + ni * li) / den, (ni * lr - nr * li) / den
    bbr = jnp.transpose(fr[..., None] * b_re - fi[..., None] * b_im, (0, 1, 3, 2))[:, None]
    bbi = jnp.transpose(fr[..., None] * b_im + fi[..., None] * b_re, (0, 1, 3, 2))[:, None]
    cr, ci = c_re[:, None], c_im[:, None]

    rr, ri = power(q - 1.0 - tok)
    wc = jnp.concatenate([rr * bbr - ri * bbi, rr * bbi + ri * bbr], axis=-1)
    qr, qi = power(tok + 1.0)
    mc = jnp.concatenate([cr * qr - ci * qi, -(cr * qi + ci * qr)], axis=-1)
    er, ei = power(tok)
    cpr = jnp.transpose(cr * er - ci * ei, (0, 2, 1, 3, 4)).reshape(dep, g, q * h, p)
    cpi = jnp.transpose(cr * ei + ci * er, (0, 2, 1, 3, 4)).reshape(dep, g, q * h, p)
    kd = (jnp.einsum("dgcp,dgyp->dgcy", bbr[:, 0], cpr, precision=HIGHEST)
          - jnp.einsum("dgcp,dgyp->dgcy", bbi[:, 0], cpi, precision=HIGHEST))
    lane = jnp.arange(q * h)
    tk = jnp.stack([jnp.where(lane >= j * h, jnp.roll(kd, j * h, axis=-1), 0.0) for j in range(q)],
                   axis=1)

    flat = lambda t: t.reshape(dep, q * g * h, t.shape[-1])
    ar, ai = power(jnp.full((1,), float(q), F32))
    a = jnp.concatenate([ar.reshape(dep, 1, g * p), ai.reshape(dep, 1, g * p),
                         jnp.zeros((dep, 6, g * p), F32)], axis=1)
    return _s5_tiles(flat(wc), True), _s5_tiles(flat(tk), False), _s5_tiles(flat(mc), True), a


def _w_in_tail(w_in):
    n = MAIN_BLOCKS * BRANCH
    pad = jnp.zeros(w_in.shape[:-1] + (128 - 2 * DN_HEADS,), w_in.dtype)
    return jnp.concatenate([w_in[..., n + 2 * DN_HEADS:], w_in[..., n:n + 2 * DN_HEADS], pad],
                           axis=-1).astype(BF16)


def kernel(x, norm_g, w_in, a_conv_w, a_conv_b, a_ln_g, a_ln_b, a_pw_w, a_pw_b, s5_lambda_re, s5_lambda_im, s5_b_re, s5_b_im, s5_c_re, s5_c_im, s5_d, s5_log_dt, s5_glu_w, s5_glu_b, c_conv_w, d_conv_w, d_a_log, d_dt_bias, d_norm_g, w_out, final_g):
    bsz, seq, d = x.shape
    depth = w_in.shape[0]
    w = BRANCH
    tm = min(512, seq)
    tc = min(1024, seq)
    dn_tc = min(256, seq)
    s5_rows = min(512, seq // S5_Q)

    w_bf = w_in.astype(BF16)
    w_tail = _w_in_tail(w_in)
    s5_wd, s5_td, s5_vd, s5_a = _s5_tables(s5_lambda_re, s5_lambda_im, s5_b_re, s5_b_im,
                                           s5_c_re, s5_c_im, s5_log_dt)
    acw = _pad_rows(a_conv_w, 32)
    zeros = jnp.zeros_like(a_conv_b)
    avec = jnp.stack([a_conv_b, a_ln_g, a_ln_b, a_pw_b, zeros, zeros, zeros, zeros], axis=1)
    bvec = jnp.stack([s5_d, s5_glu_b, zeros, zeros, zeros, zeros, zeros, zeros], axis=1)
    ccw = _pad_rows(c_conv_w, 8)
    dcw = _pad_rows(d_conv_w, 8)
    alog = jnp.repeat(d_a_log, DN_HEAD_DIM, axis=-1)[:, None, :]
    dtb = jnp.repeat(d_dt_bias, DN_HEAD_DIM, axis=-1)[:, None, :]
    ng = jnp.tile(d_norm_g, (1, DN_HEADS))[:, None, :]
    apw = a_pw_w.astype(BF16)
    bglu = s5_glu_w.astype(BF16)
    wout = w_out.astype(BF16)
    fg = final_g[None, :]

    def seq3(a):
        return a.reshape(bsz, seq, a.shape[-1])

    ng3 = norm_g[:, None, :]
    for l in range(depth):
        act, dn, ulo, uhi = _inproj(l, x.reshape(bsz * seq, d), ng3, w_bf, w_tail, dcw, alog, dtb,
                                    tm, seq)
        ulo, uhi = seq3(ulo), seq3(uhi)
        ylo, yhi = _s5(l, ulo, uhi, s5_wd, s5_td, s5_vd, s5_a, s5_rows)
        od = _dnet(l, seq3(dn), ng, dn_tc)
        x = _mixout(l, seq3(act), ulo, uhi, ylo, yhi, od, x, acw, avec, apw, bvec, bglu, ccw, wout,
                    fg, tc, l == depth - 1)
    return x
```

```python
import functools

import jax
import jax.numpy as jnp
from jax import lax
from jax.experimental import pallas as pl
from jax.experimental.pallas import tpu as pltpu

F32 = jnp.float32
BF16 = jnp.bfloat16
HIGHEST = lax.Precision.HIGHEST

NORM_EPS = 1e-6
BRANCH = 256
CONF_KERNEL = 31
SC_KERNEL = 3
DN_HEADS = 4
DN_HEAD_DIM = 64
DN_CONV = 4
DN_CHUNK = 64
DN_GROUP = 16
S5_GROUPS = 16
S5_GROUP = 16
S5_STATE = 64
S5_Q = 8
S5_ROW = S5_Q * BRANCH
S5_NSTATE = S5_GROUPS * S5_STATE

COL_A, COL_B, COL_C, COL_QKV, MAIN_BLOCKS = 0, 3, 5, 9, 12
IN_COLS = MAIN_BLOCKS * BRANCH + 2 * DN_HEADS + BRANCH
TAIL_COLS = BRANCH + 128
HALF = BRANCH // 2
ACT_A, ACT_AZ, ACT_BZ, ACT_CB, ACT_CX, ACT_DZ, ACT_COLS = 0, 1, 2, 3, 4, 5, 6
DN_Q, DN_K, DN_V, DN_BETA, DN_G, DN_COLS = 0, 1, 2, 3, 4, 5

VMEM_LIMIT_BYTES = 56 * 1024 * 1024


def _mm(a, b):
    return jnp.dot(a.astype(BF16), b.astype(BF16), preferred_element_type=F32)


def _mm_tn(a, b):
    return lax.dot_general(a.astype(BF16), b.astype(BF16), (((0,), (0,)), ((), ())),
                           preferred_element_type=F32)


def _layer_spec(layer, shape):
    zeros = (0,) * len(shape)
    return pl.BlockSpec((None,) + tuple(shape), lambda *_: (layer,) + zeros)


def _sigmoid(x):
    return 1.0 / (1.0 + jnp.exp(-x))


def _silu(x):
    return x * _sigmoid(x)


def _inproj_kernel(x_ref, g_ref, w_ref, wt_ref, dcw_ref, alog_ref, dtb_ref, act_ref, dn_ref, ulo_ref,
                   uhi_ref, pad_scr, *, steps_per_seq, sub):
    tm = x_ref.shape[0]
    w = BRANCH
    first = pl.program_id(0) % steps_per_seq == 0

    @pl.when(first)
    def _():
        pad_scr[tm:tm + 8, :] = jnp.zeros((8, 3 * w), F32)

    pad_scr[0:8, :] = pad_scr[tm:tm + 8, :]

    row_h = lax.broadcasted_iota(jnp.int32, (w, w), 0) // DN_HEAD_DIM
    col_h = lax.broadcasted_iota(jnp.int32, (w, w), 1) // DN_HEAD_DIM
    ones_bd = jnp.where(row_h == col_h, 1.0, 0.0).astype(BF16)

    def project(hb, col, n):
        return lax.dot_general(hb, w_ref[col * w:(col + n) * w, :], (((1,), (1,)), ((), ())),
                               preferred_element_type=F32)

    def put(ref, rows, j, val):
        ref[rows, j * w:(j + 1) * w] = val

    def deltanet_inputs(rows, r0):
        xp = pad_scr[r0:r0 + sub + 8, :]
        acc = dcw_ref[DN_CONV - 1:DN_CONV, :] * xp[8:, :]
        for kk in range(DN_CONV - 1):
            acc = acc + dcw_ref[kk:kk + 1, :] * pltpu.roll(xp, DN_CONV - 1 - kk, 0)[8:, :]
        qkv = _silu(acc)
        q = qkv[:, 0:w]
        k = qkv[:, w:2 * w]
        ssq = jnp.dot((q * q).astype(BF16), ones_bd, preferred_element_type=F32)
        ssk = jnp.dot((k * k).astype(BF16), ones_bd, preferred_element_type=F32)
        put(dn_ref, rows, DN_Q, q * lax.rsqrt(ssq + NORM_EPS) * (DN_HEAD_DIM ** -0.5))
        put(dn_ref, rows, DN_K, k * lax.rsqrt(ssk + NORM_EPS))
        put(dn_ref, rows, DN_V, qkv[:, 2 * w:3 * w] * dn_ref[rows, DN_BETA * w:(DN_BETA + 1) * w])

    for r in range(tm // sub):
        r0 = r * sub
        rows = slice(r0, r0 + sub)
        x = x_ref[rows, :]
        h = x * lax.rsqrt(jnp.mean(x * x, axis=-1, keepdims=True) + NORM_EPS) * g_ref[...]
        hb = h.astype(BF16)
        pad_scr[r0 + 8:r0 + 8 + sub, :] = project(hb, COL_QKV, 3)
        p = project(hb, COL_A, 3)
        put(act_ref, rows, ACT_A, p[:, 0:w] * _sigmoid(p[:, w:2 * w]))
        put(act_ref, rows, ACT_AZ, _silu(p[:, 2 * w:3 * w]))
        p = project(hb, COL_B, 2)
        ulo_ref[rows, :] = p[:, 0:HALF]
        uhi_ref[rows, :] = p[:, HALF:w]
        put(act_ref, rows, ACT_BZ, _silu(p[:, w:2 * w]))
        p = project(hb, COL_C, 4)
        put(act_ref, rows, ACT_CB, p[:, 0:w] * _silu(p[:, 3 * w:4 * w]))
        put(act_ref, rows, ACT_CX, p[:, w:2 * w] * p[:, 2 * w:3 * w])
        p = lax.dot_general(hb, wt_ref[...], (((1,), (1,)), ((), ())),
                            preferred_element_type=F32)
        put(act_ref, rows, ACT_DZ, _silu(p[:, 0:w]))
        p = jnp.concatenate(
            [jnp.broadcast_to(p[:, w + j:w + j + 1], (sub, DN_HEAD_DIM)) for j in range(2 * DN_HEADS)],
            axis=1)
        sp_in = p[:, 0:w] + dtb_ref[...]
        softplus = jnp.maximum(sp_in, 0.0) + jnp.log1p(jnp.exp(-jnp.abs(sp_in)))
        put(dn_ref, rows, DN_G, -jnp.exp(alog_ref[...]) * softplus)
        put(dn_ref, rows, DN_BETA, _sigmoid(p[:, w:2 * w]))
        if r > 0:
            deltanet_inputs(slice(r0 - sub, r0), r0 - sub)
    deltanet_inputs(slice(tm - sub, tm), tm - sub)


def _inproj(layer, x2d, g, w, wt, dcw, alog, dtb, tm, seq):
    t, d = x2d.shape
    const = functools.partial(_layer_spec, layer)
    out = lambda n: pl.BlockSpec((tm, n * BRANCH), lambda i: (i, 0))
    return pl.pallas_call(
        functools.partial(_inproj_kernel, steps_per_seq=seq // tm, sub=min(256, tm)),
        out_shape=(jax.ShapeDtypeStruct((t, ACT_COLS * BRANCH), F32),
                   jax.ShapeDtypeStruct((t, DN_COLS * BRANCH), F32),
                   jax.ShapeDtypeStruct((t, HALF), F32),
                   jax.ShapeDtypeStruct((t, HALF), F32)),
        grid=(t // tm,),
        in_specs=[pl.BlockSpec((tm, d), lambda i: (i, 0)),
                  const((1, d)), const((IN_COLS, d)), const((TAIL_COLS, d)),
                  const((8, 3 * BRANCH)), const((1, BRANCH)), const((1, BRANCH))],
        out_specs=(out(ACT_COLS), out(DN_COLS), pl.BlockSpec((tm, HALF), lambda i: (i, 0)),
                   pl.BlockSpec((tm, HALF), lambda i: (i, 0))),
        scratch_shapes=[pltpu.VMEM((tm + 8, 3 * BRANCH), F32)],
        compiler_params=pltpu.CompilerParams(
            dimension_semantics=("arbitrary",), vmem_limit_bytes=VMEM_LIMIT_BYTES),
        name="inproj",
    )(x2d, g, w, wt, dcw, alog, dtb)


S5_TILES = S5_GROUPS // 2


def _block_transpose(pieces, block):
    lane = lax.broadcasted_iota(jnp.int32, pieces[0].shape, 1)
    n = len(pieces)
    s = n // 2
    while s:
        width = s * block
        lower = lane % (2 * width) < width
        out = list(pieces)
        for j in range(n):
            if j & s == 0:
                a, b = pieces[j], pieces[j + s]
                out[j] = jnp.where(lower, a, pltpu.roll(b, width, 1))
                out[j + s] = jnp.where(lower, pltpu.roll(a, 128 - width, 1), b)
        pieces = out
        s //= 2
    return pieces


def _s5_kernel(ulo_ref, uhi_ref, wd_ref, td_ref, vdt_ref, a_ref, ylo_ref, yhi_ref,
               carry_scr, z_scr, xp_scr):
    rows = ulo_ref.shape[0] // S5_Q
    n = S5_NSTATE
    tiles_per_half = S5_TILES // 2
    lanes = 2 * S5_GROUP

    @pl.when(pl.program_id(1) == 0)
    def _():
        carry_scr[...] = jnp.zeros_like(carry_scr)

    ub = []
    for half in (ulo_ref, uhi_ref):
        tok = [half[pl.ds(j, rows, stride=S5_Q), :] for j in range(S5_Q)]
        first = _block_transpose(tok[0:4], lanes)
        second = _block_transpose(tok[4:8], lanes)
        ub += [jnp.concatenate([first[k], second[k]], axis=1).astype(BF16)
               for k in range(tiles_per_half)]
    for k in range(S5_TILES):
        z = jnp.dot(ub[k], wd_ref[k], preferred_element_type=F32)
        z_scr[:, k * 128:(k + 1) * 128] = z[:, 0:128]
        z_scr[:, n + k * 128:n + (k + 1) * 128] = z[:, 128:256]

    ar = a_ref[0:1, :]
    ai = a_ref[1:2, :]

    def body(c, carry):
        xr, xi = carry
        xp_scr[pl.ds(c, 1), 0:n] = xr
        xp_scr[pl.ds(c, 1), n:2 * n] = xi
        zr = z_scr[pl.ds(c, 1), 0:n]
        zi = z_scr[pl.ds(c, 1), n:2 * n]
        return ar * xr - ai * xi + zr, ar * xi + ai * xr + zi

    xr, xi = lax.fori_loop(0, rows, body, (carry_scr[0:1, 0:n], carry_scr[0:1, n:2 * n]),
                           unroll=4)
    carry_scr[0:1, 0:n] = xr
    carry_scr[0:1, n:2 * n] = xi

    ys = []
    for k in range(S5_TILES):
        xp = jnp.concatenate([xp_scr[:, k * 128:(k + 1) * 128],
                              xp_scr[:, n + k * 128:n + (k + 1) * 128]], axis=1).astype(BF16)
        ys.append(jnp.dot(ub[k], td_ref[k], preferred_element_type=F32)
                  + lax.dot_general(xp, vdt_ref[k], (((1,), (1,)), ((), ())),
                                    preferred_element_type=F32))
    for h, out_ref in enumerate((ylo_ref, yhi_ref)):
        tiles = ys[h * tiles_per_half:(h + 1) * tiles_per_half]
        for m in range(2):
            tok = _block_transpose([y[:, m * 128:(m + 1) * 128] for y in tiles], lanes)
            for b in range(4):
                out_ref[pl.ds(4 * m + b, rows, stride=S5_Q), :] = tok[b]


def _s5(layer, ulo, uhi, wd, td, vd, a, rows):
    b, l, _ = ulo.shape
    const = functools.partial(_layer_spec, layer)
    half = lambda: pl.BlockSpec((None, rows * S5_Q, HALF), lambda bi, j: (bi, j, 0))
    tiles = (S5_TILES, BRANCH, BRANCH)
    return pl.pallas_call(
        _s5_kernel,
        out_shape=(jax.ShapeDtypeStruct((b, l, HALF), F32), jax.ShapeDtypeStruct((b, l, HALF), F32)),
        grid=(b, l // (rows * S5_Q)),
        in_specs=[half(), half(), const(tiles), const(tiles), const(tiles), const((8, S5_NSTATE))],
        out_specs=(half(), half()),
        scratch_shapes=[pltpu.VMEM((8, 2 * S5_NSTATE), F32),
                        pltpu.VMEM((rows, 2 * S5_NSTATE), F32),
                        pltpu.VMEM((rows, 2 * S5_NSTATE), F32)],
        compiler_params=pltpu.CompilerParams(
            dimension_semantics=("parallel", "arbitrary"), vmem_limit_bytes=VMEM_LIMIT_BYTES),
        name="s5",
    )(ulo, uhi, wd, td, vd, a)


def _tile_heads(x):
    return jnp.concatenate([x] * DN_HEADS, axis=0)


def _dnet_kernel(dn_ref, ng_ref, o_ref, s_scr, o_scr, u_scr, wq_scr, attn_scr, kd_scr, egl_scr):
    nb, tc, _ = dn_ref.shape
    c64 = DN_CHUNK
    w = DN_HEADS * DN_HEAD_DIM
    nck = tc // c64
    gb = DN_GROUP // nck

    @pl.when(pl.program_id(0) == 0)
    def _():
        s_scr[...] = jnp.zeros_like(s_scr)

    row_h = lax.broadcasted_iota(jnp.int32, (w, w), 0) // DN_HEAD_DIM
    col_h = lax.broadcasted_iota(jnp.int32, (w, w), 1) // DN_HEAD_DIM
    bd = row_h == col_h
    bd_bf = jnp.where(bd, 1.0, 0.0).astype(BF16)

    def mmb(a, b_bf):
        return jnp.dot(a.astype(BF16), b_bf, preferred_element_type=F32)

    def mmb_nt(a, b_bf):
        return lax.dot_general(a.astype(BF16), b_bf, (((1,), (1,)), ((), ())),
                               preferred_element_type=F32)

    ri = lax.broadcasted_iota(jnp.int32, (c64, w), 0)
    cj = lax.broadcasted_iota(jnp.int32, (c64, w), 1) % DN_HEAD_DIM
    causal = ri >= cj
    strict = ri > cj
    diag = ri == cj
    eye_all = jnp.where(diag, 1.0, 0.0).astype(F32)
    lt_bf = jnp.where(lax.broadcasted_iota(jnp.int32, (c64, c64), 0)
                      >= lax.broadcasted_iota(jnp.int32, (c64, c64), 1), 1.0, 0.0).astype(BF16)

    def blockdiag(x):
        return _tile_heads(x.astype(BF16)) * bd_bf

    grp = range(DN_GROUP)

    def prepare(gi, carry):
        def col(i, j):
            return dn_ref[gi * gb + i // nck, (i % nck) * c64:(i % nck + 1) * c64, j * w:(j + 1) * w]

        rows = [pl.ds(pl.multiple_of((gi * gb + i // nck) * tc + (i % nck) * c64, c64), c64)
                for i in grp]
        qc = [col(i, DN_Q) for i in grp]
        kc = [col(i, DN_K) for i in grp]
        bc = [col(i, DN_BETA) for i in grp]
        g = [col(i, DN_G) for i in grp]
        g_hi = [x.astype(BF16) for x in g]
        g_lo = [(g[i] - g_hi[i].astype(F32)).astype(BF16) for i in grp]
        gc = [jnp.dot(lt_bf, g_hi[i], preferred_element_type=F32)
              + jnp.dot(lt_bf, g_lo[i], preferred_element_type=F32) for i in grp]
        egc = [jnp.exp(x) for x in gc]
        gl = [x[c64 - 1:c64, :] for x in gc]
        kb = [kc[i] * bc[i] for i in grp]
        grow = [jnp.sum(jnp.where(diag, x, 0.0), axis=0, keepdims=True) for x in gc]
        decay = [jnp.exp(jnp.where(causal, gc[i] - grow[i], -jnp.inf)) for i in grp]
        k_bd = [blockdiag(x) for x in kc]
        kq = [mmb_nt(jnp.concatenate([kb[i], qc[i]], axis=0), k_bd[i]) for i in grp]
        for i in grp:
            attn_scr[rows[i], :] = kq[i][c64:2 * c64, :] * decay[i]
        x = [jnp.where(strict, -kq[i][0:c64, :] * decay[i], 0.0) for i in grp]
        p = [eye_all + x[i] for i in grp]
        x = [mmb(x[i], blockdiag(x[i])) for i in grp]
        for _ in range(4):
            xp = [mmb(jnp.concatenate([x[i], p[i]], axis=0), blockdiag(x[i])) for i in grp]
            x = [xp[i][0:c64, :] for i in grp]
            p = [p[i] + xp[i][c64:2 * c64, :] for i in grp]
        p = [p[i] + mmb(p[i], blockdiag(x[i])) for i in grp]
        for i in grp:
            ci = gi * DN_GROUP + i
            u_scr[rows[i], :] = mmb(p[i], blockdiag(col(i, DN_V)))
            wq_scr[ci, 0:c64, :] = mmb(p[i], blockdiag(kb[i] * egc[i]))
            wq_scr[ci, c64:2 * c64, :] = qc[i] * egc[i]
            kd_scr[rows[i], :] = kc[i] * jnp.exp(gl[i] - gc[i])
            egl_scr[ci] = jnp.broadcast_to(jnp.exp(gl[i]), (8, w))
        return carry

    lax.fori_loop(0, nb // gb, prepare, 0)

    bs = range(nb)

    def recur(c, carry):
        rows = [pl.ds(pl.multiple_of(b * tc + c * c64, c64), c64) for b in bs]
        s = [s_scr[b] for b in bs]
        ws_qs = [_mm(wq_scr[b * nck + c], s[b]) for b in bs]
        v_new = [u_scr[rows[b], :] - ws_qs[b][0:c64, :] for b in bs]
        for b in bs:
            o_scr[rows[b], :] = (ws_qs[b][c64:2 * c64, :]
                                 + mmb(attn_scr[rows[b], :], blockdiag(v_new[b])))
        for b in bs:
            s_scr[b] = (s[b] * egl_scr[b * nck + c][0:1, :]
                        + jnp.where(bd, _mm_tn(kd_scr[rows[b], :], v_new[b]), 0.0))
        return carry

    lax.fori_loop(0, nck, recur, 0)

    o = o_scr[...]
    ms = mmb(o * o, bd_bf) * (1.0 / DN_HEAD_DIM)
    o_ref[...] = (o * lax.rsqrt(ms + NORM_EPS) * ng_ref[...]).reshape(nb, tc, w)


def _dnet(layer, dn3, ng, tc):
    b, l, _ = dn3.shape
    w = BRANCH
    nchunk = b * tc // DN_CHUNK
    assert DN_GROUP % (tc // DN_CHUNK) == 0 and nchunk % DN_GROUP == 0
    act = lambda: pltpu.VMEM((b * tc, w), F32)
    return pl.pallas_call(
        _dnet_kernel,
        out_shape=jax.ShapeDtypeStruct((b, l, w), F32),
        grid=(l // tc,),
        in_specs=[pl.BlockSpec((b, tc, DN_COLS * w), lambda i: (0, i, 0)),
                  _layer_spec(layer, (1, w))],
        out_specs=pl.BlockSpec((b, tc, w), lambda i: (0, i, 0)),
        scratch_shapes=[pltpu.VMEM((b, w, w), F32),
                        act(), act(),
                        pltpu.VMEM((nchunk, 2 * DN_CHUNK, w), F32),
                        act(), act(),
                        pltpu.VMEM((nchunk, 8, w), F32)],
        compiler_params=pltpu.CompilerParams(
            dimension_semantics=("arbitrary",), vmem_limit_bytes=VMEM_LIMIT_BYTES),
        name="dnet",
    )(dn3, ng)


def _mixout_kernel(act_ref, ahalo_ref, chalo_ref, ulo_ref, uhi_ref, ylo_ref, yhi_ref, od_ref, x_ref,
                   acw_ref, avec_ref, apw_ref, bvec_ref, bglu_ref, ccw_ref, wout_ref, fg_ref,
                   out_ref, ash_scr, cpad_scr, *, final, sub):
    tc = x_ref.shape[0]
    w = BRANCH
    first = pl.program_id(1) == 0

    def act(j, rows=slice(None)):
        return act_ref[rows, j * w:(j + 1) * w]

    ash_scr[0, 0:32, :] = jnp.where(first, 0.0, ahalo_ref[...])
    ash_scr[0, 32:32 + tc, :] = act(ACT_A)
    apad = ash_scr[0]
    for r in range(1, 8):
        ash_scr[r] = pltpu.roll(apad, tc + 32 - r, 0)
    cpad_scr[0:8, :] = jnp.where(first, 0.0, chalo_ref[...])
    cpad_scr[8:8 + tc, :] = act(ACT_CX)

    conv_b, ln_g, ln_b, pw_b = (avec_ref[0:1, :], avec_ref[1:2, :], avec_ref[2:3, :],
                                avec_ref[3:4, :])
    d_skip, glu_b = bvec_ref[0:1, :], bvec_ref[1:2, :]

    for r in range(tc // sub):
        r0 = r * sub
        rows = slice(r0, r0 + sub)
        acc = conv_b
        for kk in range(CONF_KERNEL):
            off = 32 - (CONF_KERNEL - 1) + kk
            acc = acc + acw_ref[kk:kk + 1, :] * ash_scr[off % 8, pl.ds(r0 + off - off % 8, sub), :]
        mu = jnp.mean(acc, axis=-1, keepdims=True)
        xc = acc - mu
        ln = xc * lax.rsqrt(jnp.mean(xc * xc, axis=-1, keepdims=True) + NORM_EPS) * ln_g + ln_b
        ya = (_mm(_silu(ln), apw_ref[...]) + pw_b) * act(ACT_AZ, rows)
        ub = jnp.concatenate([ulo_ref[rows, :], uhi_ref[rows, :]], axis=1)
        ys = jnp.concatenate([ylo_ref[rows, :], yhi_ref[rows, :]], axis=1)
        yb = jax.nn.gelu(ys + d_skip * ub)
        yb = yb * _sigmoid(_mm(yb, bglu_ref[...]) + glu_b) * act(ACT_BZ, rows)
        cp = cpad_scr[r0:r0 + sub + 8, :]
        cacc = ccw_ref[SC_KERNEL - 1:SC_KERNEL, :] * cp[8:, :]
        for kk in range(SC_KERNEL - 1):
            cacc = cacc + ccw_ref[kk:kk + 1, :] * pltpu.roll(cp, SC_KERNEL - 1 - kk, 0)[8:, :]
        yc = act(ACT_CB, rows) * cacc
        yd = od_ref[rows, :] * act(ACT_DZ, rows)
        mixed = jnp.concatenate([ya, yb, yc, yd], axis=1)
        y = x_ref[rows, :] + _mm(mixed, wout_ref[...])
        if final:
            y = y * lax.rsqrt(jnp.mean(y * y, axis=-1, keepdims=True) + NORM_EPS) * fg_ref[...]
        out_ref[rows, :] = y


def _mixout(layer, act3, ulo, uhi, ylo, yhi, od, x3, acw, avec, apw, bvec, bglu, ccw, wout, fg, tc,
            final):
    b, l, d = x3.shape
    w = BRANCH
    blk = lambda width, col: pl.BlockSpec((None, tc, width), lambda bi, i: (bi, i, col))
    halo = lambda nrow, col: pl.BlockSpec(
        (None, nrow, w), lambda bi, i: (bi, jnp.maximum(i * (tc // nrow) - 1, 0), col))
    const = functools.partial(_layer_spec, layer)
    return pl.pallas_call(
        functools.partial(_mixout_kernel, final=final, sub=256),
        out_shape=jax.ShapeDtypeStruct((b, l, d), F32),
        grid=(b, l // tc),
        in_specs=[blk(ACT_COLS * w, 0), halo(32, ACT_A), halo(8, ACT_CX),
                  blk(HALF, 0), blk(HALF, 0), blk(HALF, 0), blk(HALF, 0), blk(w, 0), blk(d, 0),
                  const((32, w)), const((8, w)), const((w, w)), const((8, w)), const((w, w)),
                  const((8, w)), const((4 * w, d)), pl.BlockSpec((1, d), lambda bi, i: (0, 0))],
        out_specs=pl.BlockSpec((None, tc, d), lambda bi, i: (bi, i, 0)),
        scratch_shapes=[pltpu.VMEM((8, tc + 32, w), F32), pltpu.VMEM((tc + 8, w), F32)],
        compiler_params=pltpu.CompilerParams(
            dimension_semantics=("parallel", "parallel"), vmem_limit_bytes=VMEM_LIMIT_BYTES),
        name="mixout_final" if final else "mixout",
    )(act3, act3, act3, ulo, uhi, ylo, yhi, od, x3, acw, avec, apw, bvec, bglu, ccw, wout, fg)


def _pad_rows(a, n):
    return jnp.pad(a, ((0, 0), (0, n - a.shape[1]), (0, 0)))


def _s5_tiles(tab, state_cols):
    dep = tab.shape[0]
    nt, rb = S5_TILES, 2 * S5_GROUP
    t = tab.reshape(dep, S5_Q, nt, rb, 128)
    t = jnp.transpose(t, (0, 2, 1, 3, 4)).reshape(dep, nt, S5_Q * rb, 128)
    per = S5_STATE if state_cols else S5_GROUP
    src = jnp.arange(128)
    dst = jnp.arange(2 * 128)
    spread = ((src[:, None] // per == dst[None, :] // (2 * per))
              & (src[:, None] % per == dst[None, :] % per)).astype(F32)
    row_group = (jnp.arange(S5_Q * rb) % rb) // S5_GROUP
    col_group = (dst % (2 * per)) // per
    mask = (row_group[:, None] == col_group[None, :]).astype(F32)
    return (jnp.einsum("dkrc,cn->dkrn", t, spread, precision=HIGHEST) * mask).astype(BF16)


def _s5_tables(lam_re, lam_im, b_re, b_im, c_re, c_im, log_dt):
    q, g, p, h = S5_Q, S5_GROUPS, S5_STATE, S5_GROUP
    dep = lam_re.shape[0]
    lr = jnp.minimum(lam_re, -1e-4)
    li = lam_im
    dt = jnp.exp(log_dt)[..., None]
    lrdt, lidt = (lr * dt)[:, None], (li * dt)[:, None]

    def power(steps):
        s = steps[None, :, None, None]
        mag = jnp.exp(lrdt * s)
        return (mag * jnp.cos(lidt * s))[:, :, :, None, :], (mag * jnp.sin(lidt * s))[:, :, :, None, :]

    tok = jnp.arange(q, dtype=F32)
    p1r, p1i = power(jnp.ones((1,), F32))
    nr, ni = p1r[:, 0, :, 0] - 1.0, p1i[:, 0, :, 0]
    den = lr * lr + li * li
    fr, fi = (nr * lr + ni * li) / den, (ni * lr - nr * li) / den
    bbr = jnp.transpose(fr[..., None] * b_re - fi[..., None] * b_im, (0, 1, 3, 2))[:, None]
    bbi = jnp.transpose(fr[..., None] * b_im + fi[..., None] * b_re, (0, 1, 3, 2))[:, None]
    cr, ci = c_re[:, None], c_im[:, None]

    rr, ri = power(q - 1.0 - tok)
    wc = jnp.concatenate([rr * bbr - ri * bbi, rr * bbi + ri * bbr], axis=-1)
    qr, qi = power(tok + 1.0)
    mc = jnp.concatenate([cr * qr - ci * qi, -(cr * qi + ci * qr)], axis=-1)
    er, ei = power(tok)
    cpr = jnp.transpose(cr * er - ci * ei, (0, 2, 1, 3, 4)).reshape(dep, g, q * h, p)
    cpi = jnp.transpose(cr * ei + ci * er, (0, 2, 1, 3, 4)).reshape(dep, g, q * h, p)
    kd = (jnp.einsum("dgcp,dgyp->dgcy", bbr[:, 0], cpr, precision=HIGHEST)
          - jnp.einsum("dgcp,dgyp->dgcy", bbi[:, 0], cpi, precision=HIGHEST))
    lane = jnp.arange(q * h)
    tk = jnp.stack([jnp.where(lane >= j * h, jnp.roll(kd, j * h, axis=-1), 0.0) for j in range(q)],
                   axis=1)

    flat = lambda t: t.reshape(dep, q * g * h, t.shape[-1])
    ar, ai = power(jnp.full((1,), float(q), F32))
    a = jnp.concatenate([ar.reshape(dep, 1, g * p), ai.reshape(dep, 1, g * p),
                         jnp.zeros((dep, 6, g * p), F32)], axis=1)
    return _s5_tiles(flat(wc), True), _s5_tiles(flat(tk), False), _s5_tiles(flat(mc), True), a


def _w_in_tail(w_t):
    n = MAIN_BLOCKS * BRANCH
    pad = jnp.zeros(w_t.shape[:1] + (128 - 2 * DN_HEADS, w_t.shape[2]), w_t.dtype)
    return jnp.concatenate([w_t[:, n + 2 * DN_HEADS:], w_t[:, n:n + 2 * DN_HEADS], pad],
                           axis=1).astype(BF16)


def kernel(x, norm_g, w_in, a_conv_w, a_conv_b, a_ln_g, a_ln_b, a_pw_w, a_pw_b, s5_lambda_re, s5_lambda_im, s5_b_re, s5_b_im, s5_c_re, s5_c_im, s5_d, s5_log_dt, s5_glu_w, s5_glu_b, c_conv_w, d_conv_w, d_a_log, d_dt_bias, d_norm_g, w_out, final_g):
    bsz, seq, d = x.shape
    depth = w_in.shape[0]
    w = BRANCH
    tm = min(512, seq)
    tc = min(1024, seq)
    dn_tc = min(256, seq)
    s5_rows = min(512, seq // S5_Q)

    w_t = jnp.swapaxes(w_in, 1, 2)
    w_bf = w_t.astype(BF16)
    w_tail = _w_in_tail(w_t)
    s5_wd, s5_td, s5_vd, s5_a = _s5_tables(s5_lambda_re, s5_lambda_im, s5_b_re, s5_b_im,
                                           s5_c_re, s5_c_im, s5_log_dt)
    acw = _pad_rows(a_conv_w, 32)
    zeros = jnp.zeros_like(a_conv_b)
    avec = jnp.stack([a_conv_b, a_ln_g, a_ln_b, a_pw_b, zeros, zeros, zeros, zeros], axis=1)
    bvec = jnp.stack([s5_d, s5_glu_b, zeros, zeros, zeros, zeros, zeros, zeros], axis=1)
    ccw = _pad_rows(c_conv_w, 8)
    dcw = _pad_rows(d_conv_w, 8)
    alog = jnp.repeat(d_a_log, DN_HEAD_DIM, axis=-1)[:, None, :]
    dtb = jnp.repeat(d_dt_bias, DN_HEAD_DIM, axis=-1)[:, None, :]
    ng = jnp.tile(d_norm_g, (1, DN_HEADS))[:, None, :]
    apw = a_pw_w.astype(BF16)
    bglu = s5_glu_w.astype(BF16)
    wout = w_out.astype(BF16)
    fg = final_g[None, :]

    def seq3(a):
        return a.reshape(bsz, seq, a.shape[-1])

    ng3 = norm_g[:, None, :]
    for l in range(depth):
        act, dn, ulo, uhi = _inproj(l, x.reshape(bsz * seq, d), ng3, w_bf, w_tail, dcw, alog, dtb,
                                    tm, seq)
        ulo, uhi = seq3(ulo), seq3(uhi)
        ylo, yhi = _s5(l, ulo, uhi, s5_wd, s5_td, s5_vd, s5_a, s5_rows)
        od = _dnet(l, seq3(dn), ng, dn_tc)
        x = _mixout(l, seq3(act), ulo, uhi, ylo, yhi, od, x, acw, avec, apw, bvec, bglu, ccw, wout,
                    fg, tc, l == depth - 1)
    return x
```

```python
import functools

import jax
import jax.numpy as jnp
from jax import lax
from jax.experimental import pallas as pl
from jax.experimental.pallas import tpu as pltpu

F32 = jnp.float32
BF16 = jnp.bfloat16
HIGHEST = lax.Precision.HIGHEST

NORM_EPS = 1e-6
BRANCH = 256
CONF_KERNEL = 31
SC_KERNEL = 3
DN_HEADS = 4
DN_HEAD_DIM = 64
DN_CONV = 4
DN_CHUNK = 64
DN_GROUP = 16
S5_GROUPS = 16
S5_GROUP = 16
S5_STATE = 64
S5_Q = 8
S5_ROW = S5_Q * BRANCH
S5_NSTATE = S5_GROUPS * S5_STATE

COL_A, COL_B, COL_C, COL_QKV, MAIN_BLOCKS = 0, 3, 5, 9, 12
IN_COLS = MAIN_BLOCKS * BRANCH + 2 * DN_HEADS + BRANCH
TAIL_COLS = BRANCH + 128
HALF = BRANCH // 2
ACT_A, ACT_AZ, ACT_BZ, ACT_CB, ACT_CX, ACT_DZ, ACT_COLS = 0, 1, 2, 3, 4, 5, 6
DN_Q, DN_K, DN_V, DN_BETA, DN_G, DN_COLS = 0, 1, 2, 3, 4, 5

VMEM_LIMIT_BYTES = 56 * 1024 * 1024


def _mm(a, b):
    return jnp.dot(a.astype(BF16), b.astype(BF16), preferred_element_type=F32)


def _layer_spec(layer, shape):
    zeros = (0,) * len(shape)
    return pl.BlockSpec((None,) + tuple(shape), lambda *_: (layer,) + zeros)


def _sigmoid(x):
    return 1.0 / (1.0 + jnp.exp(-x))


def _silu(x):
    return x * _sigmoid(x)


def _inproj_kernel(x_ref, g_ref, w_ref, wt_ref, dcw_ref, alog_ref, dtb_ref, act_ref, dn_ref, ulo_ref,
                   uhi_ref, pad_scr, *, steps_per_seq, sub):
    tm = x_ref.shape[0]
    w = BRANCH
    first = pl.program_id(0) % steps_per_seq == 0

    @pl.when(first)
    def _():
        pad_scr[tm:tm + 8, :] = jnp.zeros((8, 3 * w), F32)

    pad_scr[0:8, :] = pad_scr[tm:tm + 8, :]

    row_h = lax.broadcasted_iota(jnp.int32, (w, w), 0) // DN_HEAD_DIM
    col_h = lax.broadcasted_iota(jnp.int32, (w, w), 1) // DN_HEAD_DIM
    ones_bd = jnp.where(row_h == col_h, 1.0, 0.0).astype(BF16)

    def project(hb, col, n):
        return jnp.dot(hb, w_ref[:, col * w:(col + n) * w], preferred_element_type=F32)

    def put(ref, rows, j, val):
        ref[rows, j * w:(j + 1) * w] = val

    def deltanet_inputs(rows, r0):
        xp = pad_scr[r0:r0 + sub + 8, :]
        acc = dcw_ref[DN_CONV - 1:DN_CONV, :] * xp[8:, :]
        for kk in range(DN_CONV - 1):
            acc = acc + dcw_ref[kk:kk + 1, :] * pltpu.roll(xp, DN_CONV - 1 - kk, 0)[8:, :]
        qkv = _silu(acc)
        q = qkv[:, 0:w]
        k = qkv[:, w:2 * w]
        ssq = jnp.dot((q * q).astype(BF16), ones_bd, preferred_element_type=F32)
        ssk = jnp.dot((k * k).astype(BF16), ones_bd, preferred_element_type=F32)
        put(dn_ref, rows, DN_Q, q * lax.rsqrt(ssq + NORM_EPS) * (DN_HEAD_DIM ** -0.5))
        put(dn_ref, rows, DN_K, k * lax.rsqrt(ssk + NORM_EPS))
        put(dn_ref, rows, DN_V, qkv[:, 2 * w:3 * w] * dn_ref[rows, DN_BETA * w:(DN_BETA + 1) * w])

    for r in range(tm // sub):
        r0 = r * sub
        rows = slice(r0, r0 + sub)
        x = x_ref[rows, :]
        h = x * lax.rsqrt(jnp.mean(x * x, axis=-1, keepdims=True) + NORM_EPS) * g_ref[...]
        hb = h.astype(BF16)
        pad_scr[r0 + 8:r0 + 8 + sub, :] = project(hb, COL_QKV, 3)
        p = project(hb, COL_A, 3)
        put(act_ref, rows, ACT_A, p[:, 0:w] * _sigmoid(p[:, w:2 * w]))
        put(act_ref, rows, ACT_AZ, _silu(p[:, 2 * w:3 * w]))
        p = project(hb, COL_B, 2)
        ulo_ref[rows, :] = p[:, 0:HALF]
        uhi_ref[rows, :] = p[:, HALF:w]
        put(act_ref, rows, ACT_BZ, _silu(p[:, w:2 * w]))
        p = project(hb, COL_C, 4)
        put(act_ref, rows, ACT_CB, p[:, 0:w] * _silu(p[:, 3 * w:4 * w]))
        put(act_ref, rows, ACT_CX, p[:, w:2 * w] * p[:, 2 * w:3 * w])
        p = jnp.dot(hb, wt_ref[...], preferred_element_type=F32)
        put(act_ref, rows, ACT_DZ, _silu(p[:, 0:w]))
        p = jnp.concatenate(
            [jnp.broadcast_to(p[:, w + j:w + j + 1], (sub, DN_HEAD_DIM)) for j in range(2 * DN_HEADS)],
            axis=1)
        sp_in = p[:, 0:w] + dtb_ref[...]
        softplus = jnp.maximum(sp_in, 0.0) + jnp.log1p(jnp.exp(-jnp.abs(sp_in)))
        put(dn_ref, rows, DN_G, -jnp.exp(alog_ref[...]) * softplus)
        put(dn_ref, rows, DN_BETA, _sigmoid(p[:, w:2 * w]))
        if r > 0:
            deltanet_inputs(slice(r0 - sub, r0), r0 - sub)
    deltanet_inputs(slice(tm - sub, tm), tm - sub)


def _inproj(layer, x2d, g, w, wt, dcw, alog, dtb, tm, seq):
    t, d = x2d.shape
    const = functools.partial(_layer_spec, layer)
    out = lambda n: pl.BlockSpec((tm, n * BRANCH), lambda i: (i, 0))
    return pl.pallas_call(
        functools.partial(_inproj_kernel, steps_per_seq=seq // tm, sub=min(256, tm)),
        out_shape=(jax.ShapeDtypeStruct((t, ACT_COLS * BRANCH), F32),
                   jax.ShapeDtypeStruct((t, DN_COLS * BRANCH), F32),
                   jax.ShapeDtypeStruct((t, HALF), F32),
                   jax.ShapeDtypeStruct((t, HALF), F32)),
        grid=(t // tm,),
        in_specs=[pl.BlockSpec((tm, d), lambda i: (i, 0)),
                  const((1, d)), const((d, IN_COLS)), const((d, TAIL_COLS)),
                  const((8, 3 * BRANCH)), const((1, BRANCH)), const((1, BRANCH))],
        out_specs=(out(ACT_COLS), out(DN_COLS), pl.BlockSpec((tm, HALF), lambda i: (i, 0)),
                   pl.BlockSpec((tm, HALF), lambda i: (i, 0))),
        scratch_shapes=[pltpu.VMEM((tm + 8, 3 * BRANCH), F32)],
        compiler_params=pltpu.CompilerParams(
            dimension_semantics=("arbitrary",), vmem_limit_bytes=VMEM_LIMIT_BYTES),
        name="inproj",
    )(x2d, g, w, wt, dcw, alog, dtb)


S5_TILES = S5_GROUPS // 2


def _block_transpose(pieces, block):
    lane = lax.broadcasted_iota(jnp.int32, pieces[0].shape, 1)
    n = len(pieces)
    s = n // 2
    while s:
        width = s * block
        lower = lane % (2 * width) < width
        out = list(pieces)
        for j in range(n):
            if j & s == 0:
                a, b = pieces[j], pieces[j + s]
                out[j] = jnp.where(lower, a, pltpu.roll(b, width, 1))
                out[j + s] = jnp.where(lower, pltpu.roll(a, 128 - width, 1), b)
        pieces = out
        s //= 2
    return pieces


def _s5_kernel(ulo_ref, uhi_ref, wd_ref, td_ref, vdt_ref, a_ref, ylo_ref, yhi_ref,
               carry_scr, z_scr, xp_scr):
    rows = ulo_ref.shape[0] // S5_Q
    n = S5_NSTATE
    tiles_per_half = S5_TILES // 2
    lanes = 2 * S5_GROUP

    @pl.when(pl.program_id(1) == 0)
    def _():
        carry_scr[...] = jnp.zeros_like(carry_scr)

    ub = []
    for half in (ulo_ref, uhi_ref):
        tok = [half[pl.ds(j, rows, stride=S5_Q), :] for j in range(S5_Q)]
        first = _block_transpose(tok[0:4], lanes)
        second = _block_transpose(tok[4:8], lanes)
        ub += [jnp.concatenate([first[k], second[k]], axis=1).astype(BF16)
               for k in range(tiles_per_half)]
    for k in range(S5_TILES):
        z = jnp.dot(ub[k], wd_ref[k], preferred_element_type=F32)
        z_scr[:, k * 128:(k + 1) * 128] = z[:, 0:128]
        z_scr[:, n + k * 128:n + (k + 1) * 128] = z[:, 128:256]

    ar = a_ref[0:1, :]
    ai = a_ref[1:2, :]

    def body(c, carry):
        xr, xi = carry
        xp_scr[pl.ds(c, 1), 0:n] = xr
        xp_scr[pl.ds(c, 1), n:2 * n] = xi
        zr = z_scr[pl.ds(c, 1), 0:n]
        zi = z_scr[pl.ds(c, 1), n:2 * n]
        return ar * xr - ai * xi + zr, ar * xi + ai * xr + zi

    xr, xi = lax.fori_loop(0, rows, body, (carry_scr[0:1, 0:n], carry_scr[0:1, n:2 * n]),
                           unroll=4)
    carry_scr[0:1, 0:n] = xr
    carry_scr[0:1, n:2 * n] = xi

    ys = []
    for k in range(S5_TILES):
        xp = jnp.concatenate([xp_scr[:, k * 128:(k + 1) * 128],
                              xp_scr[:, n + k * 128:n + (k + 1) * 128]], axis=1).astype(BF16)
        ys.append(jnp.dot(ub[k], td_ref[k], preferred_element_type=F32)
                  + lax.dot_general(xp, vdt_ref[k], (((1,), (1,)), ((), ())),
                                    preferred_element_type=F32))
    for h, out_ref in enumerate((ylo_ref, yhi_ref)):
        tiles = ys[h * tiles_per_half:(h + 1) * tiles_per_half]
        for m in range(2):
            tok = _block_transpose([y[:, m * 128:(m + 1) * 128] for y in tiles], lanes)
            for b in range(4):
                out_ref[pl.ds(4 * m + b, rows, stride=S5_Q), :] = tok[b]


def _s5(layer, ulo, uhi, wd, td, vd, a, rows):
    b, l, _ = ulo.shape
    const = functools.partial(_layer_spec, layer)
    half = lambda: pl.BlockSpec((None, rows * S5_Q, HALF), lambda bi, j: (bi, j, 0))
    tiles = (S5_TILES, BRANCH, BRANCH)
    return pl.pallas_call(
        _s5_kernel,
        out_shape=(jax.ShapeDtypeStruct((b, l, HALF), F32), jax.ShapeDtypeStruct((b, l, HALF), F32)),
        grid=(b, l // (rows * S5_Q)),
        in_specs=[half(), half(), const(tiles), const(tiles), const(tiles), const((8, S5_NSTATE))],
        out_specs=(half(), half()),
        scratch_shapes=[pltpu.VMEM((8, 2 * S5_NSTATE), F32),
                        pltpu.VMEM((rows, 2 * S5_NSTATE), F32),
                        pltpu.VMEM((rows, 2 * S5_NSTATE), F32)],
        compiler_params=pltpu.CompilerParams(
            dimension_semantics=("parallel", "arbitrary"), vmem_limit_bytes=VMEM_LIMIT_BYTES),
        name="s5",
    )(ulo, uhi, wd, td, vd, a)


def _tile_heads(x):
    return jnp.concatenate([x] * DN_HEADS, axis=0)


def _dnet_kernel(dn_ref, ng_ref, o_ref, s_scr, o_scr, u_scr, wq_scr, al_scr, egl_scr):
    nb, tc, _ = dn_ref.shape
    c64 = DN_CHUNK
    w = DN_HEADS * DN_HEAD_DIM
    nck = tc // c64
    gb = DN_GROUP // nck

    @pl.when(pl.program_id(0) == 0)
    def _():
        s_scr[...] = jnp.zeros_like(s_scr)

    row_h = lax.broadcasted_iota(jnp.int32, (w, w), 0) // DN_HEAD_DIM
    col_h = lax.broadcasted_iota(jnp.int32, (w, w), 1) // DN_HEAD_DIM
    bd = row_h == col_h
    bd_bf = jnp.where(bd, 1.0, 0.0).astype(BF16)

    def mmb(a, b_bf):
        return jnp.dot(a.astype(BF16), b_bf, preferred_element_type=F32)

    def mmb_nt(a, b_bf):
        return lax.dot_general(a.astype(BF16), b_bf, (((1,), (1,)), ((), ())),
                               preferred_element_type=F32)

    ri = lax.broadcasted_iota(jnp.int32, (c64, w), 0)
    cj = lax.broadcasted_iota(jnp.int32, (c64, w), 1) % DN_HEAD_DIM
    causal = ri >= cj
    strict = ri > cj
    diag = ri == cj
    eye_all = jnp.where(diag, 1.0, 0.0).astype(F32)
    lt_bf = jnp.where(lax.broadcasted_iota(jnp.int32, (c64, c64), 0)
                      >= lax.broadcasted_iota(jnp.int32, (c64, c64), 1), 1.0, 0.0).astype(BF16)

    def blockdiag(x):
        return _tile_heads(x.astype(BF16)) * bd_bf

    grp = range(DN_GROUP)

    def prepare(gi, carry):
        def col(i, j):
            return dn_ref[gi * gb + i // nck, (i % nck) * c64:(i % nck + 1) * c64, j * w:(j + 1) * w]

        rows = [pl.ds(pl.multiple_of((gi * gb + i // nck) * tc + (i % nck) * c64, c64), c64)
                for i in grp]
        qc = [col(i, DN_Q) for i in grp]
        kc = [col(i, DN_K) for i in grp]
        bc = [col(i, DN_BETA) for i in grp]
        g = [col(i, DN_G) for i in grp]
        g_hi = [x.astype(BF16) for x in g]
        g_lo = [(g[i] - g_hi[i].astype(F32)).astype(BF16) for i in grp]
        gc = [jnp.dot(lt_bf, g_hi[i], preferred_element_type=F32)
              + jnp.dot(lt_bf, g_lo[i], preferred_element_type=F32) for i in grp]
        egc = [jnp.exp(x) for x in gc]
        gl = [x[c64 - 1:c64, :] for x in gc]
        kb = [kc[i] * bc[i] for i in grp]
        grow = [jnp.sum(jnp.where(diag, x, 0.0), axis=0, keepdims=True) for x in gc]
        decay = [jnp.exp(jnp.where(causal, gc[i] - grow[i], -jnp.inf)) for i in grp]
        k_bd = [blockdiag(x) for x in kc]
        kq = [mmb_nt(jnp.concatenate([kb[i], qc[i]], axis=0), k_bd[i]) for i in grp]
        for i in grp:
            al_scr[gi * DN_GROUP + i, 0:c64, :] = kq[i][c64:2 * c64, :] * decay[i]
        x = [jnp.where(strict, -kq[i][0:c64, :] * decay[i], 0.0) for i in grp]
        p = [eye_all + x[i] for i in grp]
        x = [mmb(x[i], blockdiag(x[i])) for i in grp]
        for _ in range(4):
            xp = [mmb(jnp.concatenate([x[i], p[i]], axis=0), blockdiag(x[i])) for i in grp]
            x = [xp[i][0:c64, :] for i in grp]
            p = [p[i] + xp[i][c64:2 * c64, :] for i in grp]
        p = [p[i] + mmb(p[i], blockdiag(x[i])) for i in grp]
        for i in grp:
            ci = gi * DN_GROUP + i
            u_scr[rows[i], :] = mmb(p[i], blockdiag(col(i, DN_V)))
            wq_scr[ci, 0:c64, :] = mmb(p[i], blockdiag(kb[i] * egc[i]))
            wq_scr[ci, c64:2 * c64, :] = qc[i] * egc[i]
            kd_t = (kc[i] * jnp.exp(gl[i] - gc[i])).T
            al_scr[ci, c64:2 * c64, :] = jnp.concatenate(
                [kd_t[h * c64:(h + 1) * c64, :] for h in range(DN_HEADS)], axis=1)
            egl_scr[ci] = jnp.broadcast_to(jnp.exp(gl[i]), (8, w))
        return carry

    lax.fori_loop(0, nb // gb, prepare, 0)

    bs = range(nb)

    def recur(c, carry):
        rows = [pl.ds(pl.multiple_of(b * tc + c * c64, c64), c64) for b in bs]
        s = [s_scr[b] for b in bs]
        ws_qs = [mmb(wq_scr[b * nck + c], blockdiag(s[b])) for b in bs]
        v_new = [u_scr[rows[b], :] - ws_qs[b][0:c64, :] for b in bs]
        od = [mmb(al_scr[b * nck + c], blockdiag(v_new[b])) for b in bs]
        for b in bs:
            o_scr[rows[b], :] = ws_qs[b][c64:2 * c64, :] + od[b][0:c64, :]
        for b in bs:
            s_scr[b] = s[b] * egl_scr[b * nck + c][0:1, :] + od[b][c64:2 * c64, :]
        return carry

    lax.fori_loop(0, nck, recur, 0)

    o = o_scr[...]
    ms = mmb(o * o, bd_bf) * (1.0 / DN_HEAD_DIM)
    o_ref[...] = (o * lax.rsqrt(ms + NORM_EPS) * ng_ref[...]).reshape(nb, tc, w)


def _dnet(layer, dn3, ng, tc):
    b, l, _ = dn3.shape
    w = BRANCH
    nchunk = b * tc // DN_CHUNK
    assert DN_GROUP % (tc // DN_CHUNK) == 0 and nchunk % DN_GROUP == 0
    act = lambda: pltpu.VMEM((b * tc, w), F32)
    return pl.pallas_call(
        _dnet_kernel,
        out_shape=jax.ShapeDtypeStruct((b, l, w), F32),
        grid=(l // tc,),
        in_specs=[pl.BlockSpec((b, tc, DN_COLS * w), lambda i: (0, i, 0)),
                  _layer_spec(layer, (1, w))],
        out_specs=pl.BlockSpec((b, tc, w), lambda i: (0, i, 0)),
        scratch_shapes=[pltpu.VMEM((b, DN_CHUNK, w), F32),
                        act(), act(),
                        pltpu.VMEM((nchunk, 2 * DN_CHUNK, w), F32),
                        pltpu.VMEM((nchunk, 2 * DN_CHUNK, w), F32),
                        pltpu.VMEM((nchunk, 8, w), F32)],
        compiler_params=pltpu.CompilerParams(
            dimension_semantics=("arbitrary",), vmem_limit_bytes=VMEM_LIMIT_BYTES),
        name="dnet",
    )(dn3, ng)


def _mixout_kernel(act_ref, ahalo_ref, chalo_ref, ulo_ref, uhi_ref, ylo_ref, yhi_ref, od_ref, x_ref,
                   acw_ref, avec_ref, apw_ref, bvec_ref, bglu_ref, ccw_ref, wout_ref, fg_ref,
                   out_ref, ash_scr, cpad_scr, *, final, sub):
    tc = x_ref.shape[0]
    w = BRANCH
    first = pl.program_id(1) == 0

    def act(j, rows=slice(None)):
        return act_ref[rows, j * w:(j + 1) * w]

    ash_scr[0, 0:32, :] = jnp.where(first, 0.0, ahalo_ref[...])
    ash_scr[0, 32:32 + tc, :] = act(ACT_A)
    apad = ash_scr[0]
    for r in range(1, 8):
        ash_scr[r] = pltpu.roll(apad, tc + 32 - r, 0)
    cpad_scr[0:8, :] = jnp.where(first, 0.0, chalo_ref[...])
    cpad_scr[8:8 + tc, :] = act(ACT_CX)

    conv_b, ln_g, ln_b, pw_b = (avec_ref[0:1, :], avec_ref[1:2, :], avec_ref[2:3, :],
                                avec_ref[3:4, :])
    d_skip, glu_b = bvec_ref[0:1, :], bvec_ref[1:2, :]

    for r in range(tc // sub):
        r0 = r * sub
        rows = slice(r0, r0 + sub)
        acc = conv_b
        for kk in range(CONF_KERNEL):
            off = 32 - (CONF_KERNEL - 1) + kk
            acc = acc + acw_ref[kk:kk + 1, :] * ash_scr[off % 8, pl.ds(r0 + off - off % 8, sub), :]
        mu = jnp.mean(acc, axis=-1, keepdims=True)
        xc = acc - mu
        ln = xc * lax.rsqrt(jnp.mean(xc * xc, axis=-1, keepdims=True) + NORM_EPS) * ln_g + ln_b
        ya = (_mm(_silu(ln), apw_ref[...]) + pw_b) * act(ACT_AZ, rows)
        ub = jnp.concatenate([ulo_ref[rows, :], uhi_ref[rows, :]], axis=1)
        ys = jnp.concatenate([ylo_ref[rows, :], yhi_ref[rows, :]], axis=1)
        yb = jax.nn.gelu(ys + d_skip * ub)
        yb = yb * _sigmoid(_mm(yb, bglu_ref[...]) + glu_b) * act(ACT_BZ, rows)
        cp = cpad_scr[r0:r0 + sub + 8, :]
        cacc = ccw_ref[SC_KERNEL - 1:SC_KERNEL, :] * cp[8:, :]
        for kk in range(SC_KERNEL - 1):
            cacc = cacc + ccw_ref[kk:kk + 1, :] * pltpu.roll(cp, SC_KERNEL - 1 - kk, 0)[8:, :]
        yc = act(ACT_CB, rows) * cacc
        yd = od_ref[rows, :] * act(ACT_DZ, rows)
        mixed = jnp.concatenate([ya, yb, yc, yd], axis=1)
        y = x_ref[rows, :] + _mm(mixed, wout_ref[...])
        if final:
            y = y * lax.rsqrt(jnp.mean(y * y, axis=-1, keepdims=True) + NORM_EPS) * fg_ref[...]
        out_ref[rows, :] = y


def _mixout(layer, act3, ulo, uhi, ylo, yhi, od, x3, acw, avec, apw, bvec, bglu, ccw, wout, fg, tc,
            final):
    b, l, d = x3.shape
    w = BRANCH
    blk = lambda width, col: pl.BlockSpec((None, tc, width), lambda bi, i: (bi, i, col))
    halo = lambda nrow, col: pl.BlockSpec(
        (None, nrow, w), lambda bi, i: (bi, jnp.maximum(i * (tc // nrow) - 1, 0), col))
    const = functools.partial(_layer_spec, layer)
    return pl.pallas_call(
        functools.partial(_mixout_kernel, final=final, sub=256),
        out_shape=jax.ShapeDtypeStruct((b, l, d), F32),
        grid=(b, l // tc),
        in_specs=[blk(ACT_COLS * w, 0), halo(32, ACT_A), halo(8, ACT_CX),
                  blk(HALF, 0), blk(HALF, 0), blk(HALF, 0), blk(HALF, 0), blk(w, 0), blk(d, 0),
                  const((32, w)), const((8, w)), const((w, w)), const((8, w)), const((w, w)),
                  const((8, w)), const((4 * w, d)), pl.BlockSpec((1, d), lambda bi, i: (0, 0))],
        out_specs=pl.BlockSpec((None, tc, d), lambda bi, i: (bi, i, 0)),
        scratch_shapes=[pltpu.VMEM((8, tc + 32, w), F32), pltpu.VMEM((tc + 8, w), F32)],
        compiler_params=pltpu.CompilerParams(
            dimension_semantics=("parallel", "parallel"), vmem_limit_bytes=VMEM_LIMIT_BYTES),
        name="mixout_final" if final else "mixout",
    )(act3, act3, act3, ulo, uhi, ylo, yhi, od, x3, acw, avec, apw, bvec, bglu, ccw, wout, fg)


def _pad_rows(a, n):
    return jnp.pad(a, ((0, 0), (0, n - a.shape[1]), (0, 0)))


def _s5_tiles(tab, state_cols):
    dep = tab.shape[0]
    nt, rb = S5_TILES, 2 * S5_GROUP
    t = tab.reshape(dep, S5_Q, nt, rb, 128)
    t = jnp.transpose(t, (0, 2, 1, 3, 4)).reshape(dep, nt, S5_Q * rb, 128)
    per = S5_STATE if state_cols else S5_GROUP
    src = jnp.arange(128)
    dst = jnp.arange(2 * 128)
    spread = ((src[:, None] // per == dst[None, :] // (2 * per))
              & (src[:, None] % per == dst[None, :] % per)).astype(F32)
    row_group = (jnp.arange(S5_Q * rb) % rb) // S5_GROUP
    col_group = (dst % (2 * per)) // per
    mask = (row_group[:, None] == col_group[None, :]).astype(F32)
    return (jnp.einsum("dkrc,cn->dkrn", t, spread, precision=HIGHEST) * mask).astype(BF16)


def _s5_tables(lam_re, lam_im, b_re, b_im, c_re, c_im, log_dt):
    q, g, p, h = S5_Q, S5_GROUPS, S5_STATE, S5_GROUP
    dep = lam_re.shape[0]
    lr = jnp.minimum(lam_re, -1e-4)
    li = lam_im
    dt = jnp.exp(log_dt)[..., None]
    lrdt, lidt = (lr * dt)[:, None], (li * dt)[:, None]

    def power(steps):
        s = steps[None, :, None, None]
        mag = jnp.exp(lrdt * s)
        return (mag * jnp.cos(lidt * s))[:, :, :, None, :], (mag * jnp.sin(lidt * s))[:, :, :, None, :]

    tok = jnp.arange(q, dtype=F32)
    p1r, p1i = power(jnp.ones((1,), F32))
    nr, ni = p1r[:, 0, :, 0] - 1.0, p1i[:, 0, :, 0]
    den = lr * lr + li * li
    fr, fi = (nr * lr + ni * li) / den, (ni * lr - nr * li) / den
    bbr = jnp.transpose(fr[..., None] * b_re - fi[..., None] * b_im, (0, 1, 3, 2))[:, None]
    bbi = jnp.transpose(fr[..., None] * b_im + fi[..., None] * b_re, (0, 1, 3, 2))[:, None]
    cr, ci = c_re[:, None], c_im[:, None]

    rr, ri = power(q - 1.0 - tok)
    wc = jnp.concatenate([rr * bbr - ri * bbi, rr * bbi + ri * bbr], axis=-1)
    qr, qi = power(tok + 1.0)
    mc = jnp.concatenate([cr * qr - ci * qi, -(cr * qi + ci * qr)], axis=-1)
    er, ei = power(tok)
    cpr = jnp.transpose(cr * er - ci * ei, (0, 2, 1, 3, 4)).reshape(dep, g, q * h, p)
    cpi = jnp.transpose(cr * ei + ci * er, (0, 2, 1, 3, 4)).reshape(dep, g, q * h, p)
    kd = (jnp.einsum("dgcp,dgyp->dgcy", bbr[:, 0], cpr, precision=HIGHEST)
          - jnp.einsum("dgcp,dgyp->dgcy", bbi[:, 0], cpi, precision=HIGHEST))
    lane = jnp.arange(q * h)
    tk = jnp.stack([jnp.where(lane >= j * h, jnp.roll(kd, j * h, axis=-1), 0.0) for j in range(q)],
                   axis=1)

    flat = lambda t: t.reshape(dep, q * g * h, t.shape[-1])
    ar, ai = power(jnp.full((1,), float(q), F32))
    a = jnp.concatenate([ar.reshape(dep, 1, g * p), ai.reshape(dep, 1, g * p),
                         jnp.zeros((dep, 6, g * p), F32)], axis=1)
    return _s5_tiles(flat(wc), True), _s5_tiles(flat(tk), False), _s5_tiles(flat(mc), True), a


def _w_in_tail(w_in):
    n = MAIN_BLOCKS * BRANCH
    pad = jnp.zeros(w_in.shape[:-1] + (128 - 2 * DN_HEADS,), w_in.dtype)
    return jnp.concatenate([w_in[..., n + 2 * DN_HEADS:], w_in[..., n:n + 2 * DN_HEADS], pad],
                           axis=-1).astype(BF16)


def kernel(x, norm_g, w_in, a_conv_w, a_conv_b, a_ln_g, a_ln_b, a_pw_w, a_pw_b, s5_lambda_re, s5_lambda_im, s5_b_re, s5_b_im, s5_c_re, s5_c_im, s5_d, s5_log_dt, s5_glu_w, s5_glu_b, c_conv_w, d_conv_w, d_a_log, d_dt_bias, d_norm_g, w_out, final_g):
    bsz, seq, d = x.shape
    depth = w_in.shape[0]
    w = BRANCH
    tm = min(512, seq)
    tc = min(1024, seq)
    dn_tc = min(256, seq)
    s5_rows = min(512, seq // S5_Q)

    w_bf = w_in.astype(BF16)
    w_tail = _w_in_tail(w_in)
    s5_wd, s5_td, s5_vd, s5_a = _s5_tables(s5_lambda_re, s5_lambda_im, s5_b_re, s5_b_im,
                                           s5_c_re, s5_c_im, s5_log_dt)
    acw = _pad_rows(a_conv_w, 32)
    zeros = jnp.zeros_like(a_conv_b)
    avec = jnp.stack([a_conv_b, a_ln_g, a_ln_b, a_pw_b, zeros, zeros, zeros, zeros], axis=1)
    bvec = jnp.stack([s5_d, s5_glu_b, zeros, zeros, zeros, zeros, zeros, zeros], axis=1)
    ccw = _pad_rows(c_conv_w, 8)
    dcw = _pad_rows(d_conv_w, 8)
    alog = jnp.repeat(d_a_log, DN_HEAD_DIM, axis=-1)[:, None, :]
    dtb = jnp.repeat(d_dt_bias, DN_HEAD_DIM, axis=-1)[:, None, :]
    ng = jnp.tile(d_norm_g, (1, DN_HEADS))[:, None, :]
    apw = a_pw_w.astype(BF16)
    bglu = s5_glu_w.astype(BF16)
    wout = w_out.astype(BF16)
    fg = final_g[None, :]

    def seq3(a):
        return a.reshape(bsz, seq, a.shape[-1])

    ng3 = norm_g[:, None, :]
    for l in range(depth):
        act, dn, ulo, uhi = _inproj(l, x.reshape(bsz * seq, d), ng3, w_bf, w_tail, dcw, alog, dtb,
                                    tm, seq)
        ulo, uhi = seq3(ulo), seq3(uhi)
        ylo, yhi = _s5(l, ulo, uhi, s5_wd, s5_td, s5_vd, s5_a, s5_rows)
        od = _dnet(l, seq3(dn), ng, dn_tc)
        x = _mixout(l, seq3(act), ulo, uhi, ylo, yhi, od, x, acw, avec, apw, bvec, bglu, ccw, wout,
                    fg, tc, l == depth - 1)
    return x
```

```python
import functools

import jax
import jax.numpy as jnp
from jax import lax
from jax.experimental import pallas as pl
from jax.experimental.pallas import tpu as pltpu

F32 = jnp.float32
BF16 = jnp.bfloat16
HIGHEST = lax.Precision.HIGHEST

NORM_EPS = 1e-6
BRANCH = 256
CONF_KERNEL = 31
SC_KERNEL = 3
DN_HEADS = 4
DN_HEAD_DIM = 64
DN_CONV = 4
DN_CHUNK = 64
DN_GROUP = 16
S5_GROUPS = 16
S5_GROUP = 16
S5_STATE = 64
S5_Q = 8
S5_ROW = S5_Q * BRANCH
S5_NSTATE = S5_GROUPS * S5_STATE

COL_A, COL_B, COL_C, COL_QKV, MAIN_BLOCKS = 0, 3, 5, 9, 12
IN_COLS = MAIN_BLOCKS * BRANCH + 2 * DN_HEADS + BRANCH
TAIL_COLS = BRANCH + 128
HALF = BRANCH // 2
ACT_A, ACT_AZ, ACT_BZ, ACT_CB, ACT_CX, ACT_DZ, ACT_COLS = 0, 1, 2, 3, 4, 5, 6
DN_Q, DN_K, DN_V, DN_BETA, DN_G, DN_COLS = 0, 1, 2, 3, 4, 5

VMEM_LIMIT_BYTES = 56 * 1024 * 1024


def _mm(a, b):
    return jnp.dot(a.astype(BF16), b.astype(BF16), preferred_element_type=F32)


def _layer_spec(layer, shape):
    zeros = (0,) * len(shape)
    return pl.BlockSpec((None,) + tuple(shape), lambda *_: (layer,) + zeros)


def _sigmoid(x):
    return 1.0 / (1.0 + jnp.exp(-x))


def _silu(x):
    return x * _sigmoid(x)


def _inproj_kernel(x_ref, g_ref, w_ref, wt_ref, dcw_ref, alog_ref, dtb_ref, act_ref, dn_ref, ulo_ref,
                   uhi_ref, pad_scr, *, steps_per_seq, sub):
    tm = x_ref.shape[0]
    w = BRANCH
    first = pl.program_id(0) % steps_per_seq == 0

    @pl.when(first)
    def _():
        pad_scr[tm:tm + 8, :] = jnp.zeros((8, 3 * w), F32)

    pad_scr[0:8, :] = pad_scr[tm:tm + 8, :]

    row_h = lax.broadcasted_iota(jnp.int32, (w, w), 0) // DN_HEAD_DIM
    col_h = lax.broadcasted_iota(jnp.int32, (w, w), 1) // DN_HEAD_DIM
    ones_bd = jnp.where(row_h == col_h, 1.0, 0.0).astype(BF16)

    def project(hb, col, n):
        return jnp.dot(hb, w_ref[:, col * w:(col + n) * w], preferred_element_type=F32)

    def put(ref, rows, j, val):
        ref[rows, j * w:(j + 1) * w] = val

    def deltanet_inputs(rows, r0):
        xp = pad_scr[r0:r0 + sub + 8, :]
        acc = dcw_ref[DN_CONV - 1:DN_CONV, :] * xp[8:, :]
        for kk in range(DN_CONV - 1):
            acc = acc + dcw_ref[kk:kk + 1, :] * pltpu.roll(xp, DN_CONV - 1 - kk, 0)[8:, :]
        qkv = _silu(acc)
        q = qkv[:, 0:w]
        k = qkv[:, w:2 * w]
        ssq = jnp.dot((q * q).astype(BF16), ones_bd, preferred_element_type=F32)
        ssk = jnp.dot((k * k).astype(BF16), ones_bd, preferred_element_type=F32)
        put(dn_ref, rows, DN_Q, q * lax.rsqrt(ssq + NORM_EPS) * (DN_HEAD_DIM ** -0.5))
        put(dn_ref, rows, DN_K, k * lax.rsqrt(ssk + NORM_EPS))
        put(dn_ref, rows, DN_V, qkv[:, 2 * w:3 * w] * dn_ref[rows, DN_BETA * w:(DN_BETA + 1) * w])

    for r in range(tm // sub):
        r0 = r * sub
        rows = slice(r0, r0 + sub)
        x = x_ref[rows, :]
        h = x * lax.rsqrt(jnp.mean(x * x, axis=-1, keepdims=True) + NORM_EPS) * g_ref[...]
        hb = h.astype(BF16)
        pad_scr[r0 + 8:r0 + 8 + sub, :] = project(hb, COL_QKV, 3)
        p = project(hb, COL_A, 3)
        put(act_ref, rows, ACT_A, p[:, 0:w] * _sigmoid(p[:, w:2 * w]))
        put(act_ref, rows, ACT_AZ, _silu(p[:, 2 * w:3 * w]))
        p = project(hb, COL_B, 2)
        ulo_ref[rows, :] = p[:, 0:HALF]
        uhi_ref[rows, :] = p[:, HALF:w]
        put(act_ref, rows, ACT_BZ, _silu(p[:, w:2 * w]))
        p = project(hb, COL_C, 4)
        put(act_ref, rows, ACT_CB, p[:, 0:w] * _silu(p[:, 3 * w:4 * w]))
        put(act_ref, rows, ACT_CX, p[:, w:2 * w] * p[:, 2 * w:3 * w])
        p = jnp.dot(hb, wt_ref[...], preferred_element_type=F32)
        put(act_ref, rows, ACT_DZ, _silu(p[:, 0:w]))
        p = jnp.concatenate(
            [jnp.broadcast_to(p[:, w + j:w + j + 1], (sub, DN_HEAD_DIM)) for j in range(2 * DN_HEADS)],
            axis=1)
        sp_in = p[:, 0:w] + dtb_ref[...]
        softplus = jnp.maximum(sp_in, 0.0) + jnp.log1p(jnp.exp(-jnp.abs(sp_in)))
        put(dn_ref, rows, DN_G, -jnp.exp(alog_ref[...]) * softplus)
        put(dn_ref, rows, DN_BETA, _sigmoid(p[:, w:2 * w]))
        if r > 0:
            deltanet_inputs(slice(r0 - sub, r0), r0 - sub)
    deltanet_inputs(slice(tm - sub, tm), tm - sub)


def _inproj(layer, x2d, g, w, wt, dcw, alog, dtb, tm, seq):
    t, d = x2d.shape
    const = functools.partial(_layer_spec, layer)
    out = lambda n: pl.BlockSpec((tm, n * BRANCH), lambda i: (i, 0))
    return pl.pallas_call(
        functools.partial(_inproj_kernel, steps_per_seq=seq // tm, sub=min(256, tm)),
        out_shape=(jax.ShapeDtypeStruct((t, ACT_COLS * BRANCH), F32),
                   jax.ShapeDtypeStruct((t, DN_COLS * BRANCH), F32),
                   jax.ShapeDtypeStruct((t, HALF), F32),
                   jax.ShapeDtypeStruct((t, HALF), F32)),
        grid=(t // tm,),
        in_specs=[pl.BlockSpec((tm, d), lambda i: (i, 0)),
                  const((1, d)), const((d, IN_COLS)), const((d, TAIL_COLS)),
                  const((8, 3 * BRANCH)), const((1, BRANCH)), const((1, BRANCH))],
        out_specs=(out(ACT_COLS), out(DN_COLS), pl.BlockSpec((tm, HALF), lambda i: (i, 0)),
                   pl.BlockSpec((tm, HALF), lambda i: (i, 0))),
        scratch_shapes=[pltpu.VMEM((tm + 8, 3 * BRANCH), F32)],
        compiler_params=pltpu.CompilerParams(
            dimension_semantics=("arbitrary",), vmem_limit_bytes=VMEM_LIMIT_BYTES),
        name="inproj",
    )(x2d, g, w, wt, dcw, alog, dtb)


S5_TILES = S5_GROUPS // 2


def _block_transpose(pieces, block):
    lane = lax.broadcasted_iota(jnp.int32, pieces[0].shape, 1)
    n = len(pieces)
    s = n // 2
    while s:
        width = s * block
        lower = lane % (2 * width) < width
        out = list(pieces)
        for j in range(n):
            if j & s == 0:
                a, b = pieces[j], pieces[j + s]
                out[j] = jnp.where(lower, a, pltpu.roll(b, width, 1))
                out[j + s] = jnp.where(lower, pltpu.roll(a, 128 - width, 1), b)
        pieces = out
        s //= 2
    return pieces


def _s5_kernel(ulo_ref, uhi_ref, wd_ref, td_ref, vdt_ref, a_ref, ylo_ref, yhi_ref,
               carry_scr, z_scr, xp_scr):
    rows = ulo_ref.shape[0] // S5_Q
    n = S5_NSTATE
    tiles_per_half = S5_TILES // 2
    lanes = 2 * S5_GROUP

    @pl.when(pl.program_id(1) == 0)
    def _():
        carry_scr[...] = jnp.zeros_like(carry_scr)

    ub = []
    for half in (ulo_ref, uhi_ref):
        tok = [half[pl.ds(j, rows, stride=S5_Q), :] for j in range(S5_Q)]
        first = _block_transpose(tok[0:4], lanes)
        second = _block_transpose(tok[4:8], lanes)
        ub += [jnp.concatenate([first[k], second[k]], axis=1).astype(BF16)
               for k in range(tiles_per_half)]
    for k in range(S5_TILES):
        z = jnp.dot(ub[k], wd_ref[k], preferred_element_type=F32)
        z_scr[:, k * 128:(k + 1) * 128] = z[:, 0:128]
        z_scr[:, n + k * 128:n + (k + 1) * 128] = z[:, 128:256]

    ar = a_ref[0:1, :]
    ai = a_ref[1:2, :]

    def body(c, carry):
        xr, xi = carry
        xp_scr[pl.ds(c, 1), 0:n] = xr
        xp_scr[pl.ds(c, 1), n:2 * n] = xi
        zr = z_scr[pl.ds(c, 1), 0:n]
        zi = z_scr[pl.ds(c, 1), n:2 * n]
        return ar * xr - ai * xi + zr, ar * xi + ai * xr + zi

    xr, xi = lax.fori_loop(0, rows, body, (carry_scr[0:1, 0:n], carry_scr[0:1, n:2 * n]),
                           unroll=4)
    carry_scr[0:1, 0:n] = xr
    carry_scr[0:1, n:2 * n] = xi

    ys = []
    for k in range(S5_TILES):
        xp = jnp.concatenate([xp_scr[:, k * 128:(k + 1) * 128],
                              xp_scr[:, n + k * 128:n + (k + 1) * 128]], axis=1).astype(BF16)
        ys.append(jnp.dot(ub[k], td_ref[k], preferred_element_type=F32)
                  + lax.dot_general(xp, vdt_ref[k], (((1,), (1,)), ((), ())),
                                    preferred_element_type=F32))
    for h, out_ref in enumerate((ylo_ref, yhi_ref)):
        tiles = ys[h * tiles_per_half:(h + 1) * tiles_per_half]
        for m in range(2):
            tok = _block_transpose([y[:, m * 128:(m + 1) * 128] for y in tiles], lanes)
            for b in range(4):
                out_ref[pl.ds(4 * m + b, rows, stride=S5_Q), :] = tok[b]


def _s5(layer, ulo, uhi, wd, td, vd, a, rows):
    b, l, _ = ulo.shape
    const = functools.partial(_layer_spec, layer)
    half = lambda: pl.BlockSpec((None, rows * S5_Q, HALF), lambda bi, j: (bi, j, 0))
    tiles = (S5_TILES, BRANCH, BRANCH)
    return pl.pallas_call(
        _s5_kernel,
        out_shape=(jax.ShapeDtypeStruct((b, l, HALF), F32), jax.ShapeDtypeStruct((b, l, HALF), F32)),
        grid=(b, l // (rows * S5_Q)),
        in_specs=[half(), half(), const(tiles), const(tiles), const(tiles), const((8, S5_NSTATE))],
        out_specs=(half(), half()),
        scratch_shapes=[pltpu.VMEM((8, 2 * S5_NSTATE), F32),
                        pltpu.VMEM((rows, 2 * S5_NSTATE), F32),
                        pltpu.VMEM((rows, 2 * S5_NSTATE), F32)],
        compiler_params=pltpu.CompilerParams(
            dimension_semantics=("parallel", "arbitrary"), vmem_limit_bytes=VMEM_LIMIT_BYTES),
        name="s5",
    )(ulo, uhi, wd, td, vd, a)


def _tile_heads(x):
    return jnp.concatenate([x] * DN_HEADS, axis=0)


def _dnet_kernel(dn_ref, ng_ref, o_ref, s_scr, o_scr, u_scr, wq_scr, al_scr, egl_scr):
    nb, tc, _ = dn_ref.shape
    c64 = DN_CHUNK
    w = DN_HEADS * DN_HEAD_DIM
    nck = tc // c64
    gb = DN_GROUP // nck

    @pl.when(pl.program_id(0) == 0)
    def _():
        s_scr[...] = jnp.zeros_like(s_scr)

    row_h = lax.broadcasted_iota(jnp.int32, (w, w), 0) // DN_HEAD_DIM
    col_h = lax.broadcasted_iota(jnp.int32, (w, w), 1) // DN_HEAD_DIM
    bd = row_h == col_h
    bd_bf = jnp.where(bd, 1.0, 0.0).astype(BF16)

    def mmb(a, b_bf):
        return jnp.dot(a.astype(BF16), b_bf, preferred_element_type=F32)

    def mmb_nt(a, b_bf):
        return lax.dot_general(a.astype(BF16), b_bf, (((1,), (1,)), ((), ())),
                               preferred_element_type=F32)

    ri = lax.broadcasted_iota(jnp.int32, (c64, w), 0)
    cj = lax.broadcasted_iota(jnp.int32, (c64, w), 1) % DN_HEAD_DIM
    causal = ri >= cj
    strict = ri > cj
    diag = ri == cj
    eye_all = jnp.where(diag, 1.0, 0.0).astype(F32)
    lt_bf = jnp.where(lax.broadcasted_iota(jnp.int32, (c64, c64), 0)
                      >= lax.broadcasted_iota(jnp.int32, (c64, c64), 1), 1.0, 0.0).astype(BF16)

    def blockdiag(x):
        return _tile_heads(x.astype(BF16)) * bd_bf

    grp = range(DN_GROUP)

    def prepare(gi, carry):
        def col(i, j):
            return dn_ref[gi * gb + i // nck, (i % nck) * c64:(i % nck + 1) * c64, j * w:(j + 1) * w]

        rows = [pl.ds(pl.multiple_of((gi * gb + i // nck) * tc + (i % nck) * c64, c64), c64)
                for i in grp]
        qc = [col(i, DN_Q) for i in grp]
        kc = [col(i, DN_K) for i in grp]
        bc = [col(i, DN_BETA) for i in grp]
        g = [col(i, DN_G) for i in grp]
        g_hi = [x.astype(BF16) for x in g]
        g_lo = [(g[i] - g_hi[i].astype(F32)).astype(BF16) for i in grp]
        gc = [jnp.dot(lt_bf, g_hi[i], preferred_element_type=F32)
              + jnp.dot(lt_bf, g_lo[i], preferred_element_type=F32) for i in grp]
        egc = [jnp.exp(x) for x in gc]
        gl = [x[c64 - 1:c64, :] for x in gc]
        kb = [kc[i] * bc[i] for i in grp]
        grow = [jnp.sum(jnp.where(diag, x, 0.0), axis=0, keepdims=True) for x in gc]
        decay = [jnp.exp(jnp.where(causal, gc[i] - grow[i], -jnp.inf)) for i in grp]
        k_bd = [blockdiag(x) for x in kc]
        kq = [mmb_nt(jnp.concatenate([kb[i], qc[i]], axis=0), k_bd[i]) for i in grp]
        for i in grp:
            al_scr[gi * DN_GROUP + i, 0:c64, :] = kq[i][c64:2 * c64, :] * decay[i]
        x = [jnp.where(strict, -kq[i][0:c64, :] * decay[i], 0.0) for i in grp]
        p = [eye_all + x[i] for i in grp]
        x = [mmb(x[i], blockdiag(x[i])) for i in grp]
        for _ in range(4):
            xp = [mmb(jnp.concatenate([x[i], p[i]], axis=0), blockdiag(x[i])) for i in grp]
            x = [xp[i][0:c64, :] for i in grp]
            p = [p[i] + xp[i][c64:2 * c64, :] for i in grp]
        p = [p[i] + mmb(p[i], blockdiag(x[i])) for i in grp]
        for i in grp:
            ci = gi * DN_GROUP + i
            u_scr[rows[i], :] = mmb(p[i], blockdiag(col(i, DN_V)))
            wq_scr[ci, 0:c64, :] = mmb(p[i], blockdiag(kb[i] * egc[i]))
            wq_scr[ci, c64:2 * c64, :] = qc[i] * egc[i]
            kd_t = (kc[i] * jnp.exp(gl[i] - gc[i])).T
            al_scr[ci, c64:2 * c64, :] = jnp.concatenate(
                [kd_t[h * c64:(h + 1) * c64, :] for h in range(DN_HEADS)], axis=1)
            egl_scr[ci] = jnp.broadcast_to(jnp.exp(gl[i]), (8, w))
        return carry

    lax.fori_loop(0, nb // gb, prepare, 0)

    bs = range(nb)

    def recur(c, carry):
        rows = [pl.ds(pl.multiple_of(b * tc + c * c64, c64), c64) for b in bs]
        s = [s_scr[b] for b in bs]
        ws_qs = [mmb(wq_scr[b * nck + c], blockdiag(s[b])) for b in bs]
        v_new = [u_scr[rows[b], :] - ws_qs[b][0:c64, :] for b in bs]
        od = [mmb(al_scr[b * nck + c], blockdiag(v_new[b])) for b in bs]
        for b in bs:
            o_scr[rows[b], :] = ws_qs[b][c64:2 * c64, :] + od[b][0:c64, :]
        for b in bs:
            s_scr[b] = s[b] * egl_scr[b * nck + c][0:1, :] + od[b][c64:2 * c64, :]
        return carry

    lax.fori_loop(0, nck, recur, 0)

    o = o_scr[...]
    ms = mmb(o * o, bd_bf) * (1.0 / DN_HEAD_DIM)
    o_ref[...] = (o * lax.rsqrt(ms + NORM_EPS) * ng_ref[...]).reshape(nb, tc, w)


def _dnet(layer, dn3, ng, tc):
    b, l, _ = dn3.shape
    w = BRANCH
    nchunk = b * tc // DN_CHUNK
    assert DN_GROUP % (tc // DN_CHUNK) == 0 and nchunk % DN_GROUP == 0
    act = lambda: pltpu.VMEM((b * tc, w), F32)
    return pl.pallas_call(
        _dnet_kernel,
        out_shape=jax.ShapeDtypeStruct((b, l, w), F32),
        grid=(l // tc,),
        in_specs=[pl.BlockSpec((b, tc, DN_COLS * w), lambda i: (0, i, 0)),
                  _layer_spec(layer, (1, w))],
        out_specs=pl.BlockSpec((b, tc, w), lambda i: (0, i, 0)),
        scratch_shapes=[pltpu.VMEM((b, DN_CHUNK, w), F32),
                        act(), act(),
                        pltpu.VMEM((nchunk, 2 * DN_CHUNK, w), F32),
                        pltpu.VMEM((nchunk, 2 * DN_CHUNK, w), F32),
                        pltpu.VMEM((nchunk, 8, w), F32)],
        compiler_params=pltpu.CompilerParams(
            dimension_semantics=("arbitrary",), vmem_limit_bytes=VMEM_LIMIT_BYTES),
        name="dnet",
    )(dn3, ng)


def _mixout_kernel(act_ref, ahalo_ref, chalo_ref, ulo_ref, uhi_ref, ylo_ref, yhi_ref, od_ref, x_ref,
                   acw_ref, avec_ref, apw_ref, bvec_ref, bglu_ref, ccw_ref, wout_ref, fg_ref,
                   out_ref, ash_scr, cpad_scr, *, final, sub):
    tc = x_ref.shape[0]
    w = BRANCH
    first = pl.program_id(1) == 0

    def act(j, rows=slice(None)):
        return act_ref[rows, j * w:(j + 1) * w]

    ash_scr[0, 0:32, :] = jnp.where(first, 0.0, ahalo_ref[...])
    ash_scr[0, 32:32 + tc, :] = act(ACT_A)
    apad = ash_scr[0]
    for r in range(1, 8):
        ash_scr[r] = pltpu.roll(apad, tc + 32 - r, 0)
    cpad_scr[0:8, :] = jnp.where(first, 0.0, chalo_ref[...])
    cpad_scr[8:8 + tc, :] = act(ACT_CX)

    conv_b, ln_g, ln_b, pw_b = (avec_ref[0:1, :], avec_ref[1:2, :], avec_ref[2:3, :],
                                avec_ref[3:4, :])
    d_skip, glu_b = bvec_ref[0:1, :], bvec_ref[1:2, :]

    for r in range(tc // sub):
        r0 = r * sub
        rows = slice(r0, r0 + sub)
        acc = conv_b
        for kk in range(CONF_KERNEL):
            off = 32 - (CONF_KERNEL - 1) + kk
            acc = acc + acw_ref[kk:kk + 1, :] * ash_scr[off % 8, pl.ds(r0 + off - off % 8, sub), :]
        mu = jnp.mean(acc, axis=-1, keepdims=True)
        xc = acc - mu
        ln = xc * lax.rsqrt(jnp.mean(xc * xc, axis=-1, keepdims=True) + NORM_EPS) * ln_g + ln_b
        ya = (_mm(_silu(ln), apw_ref[...]) + pw_b) * act(ACT_AZ, rows)
        ub = jnp.concatenate([ulo_ref[rows, :], uhi_ref[rows, :]], axis=1)
        ys = jnp.concatenate([ylo_ref[rows, :], yhi_ref[rows, :]], axis=1)
        yb = jax.nn.gelu(ys + d_skip * ub)
        yb = yb * _sigmoid(_mm(yb, bglu_ref[...]) + glu_b) * act(ACT_BZ, rows)
        cp = cpad_scr[r0:r0 + sub + 8, :]
        cacc = ccw_ref[SC_KERNEL - 1:SC_KERNEL, :] * cp[8:, :]
        for kk in range(SC_KERNEL - 1):
            cacc = cacc + ccw_ref[kk:kk + 1, :] * pltpu.roll(cp, SC_KERNEL - 1 - kk, 0)[8:, :]
        yc = act(ACT_CB, rows) * cacc
        yd = od_ref[rows, :] * act(ACT_DZ, rows)
        mixed = jnp.concatenate([ya, yb, yc, yd], axis=1)
        y = x_ref[rows, :] + _mm(mixed, wout_ref[:, 0:4 * w])
        if final:
            y = y * lax.rsqrt(jnp.mean(y * y, axis=-1, keepdims=True) + NORM_EPS) * fg_ref[...]
        out_ref[rows, :] = y


def _mixout(layer, act3, ulo, uhi, ylo, yhi, od, x3, acw, avec, apw, bvec, bglu, ccw, wout, fg, tc,
            final):
    b, l, d = x3.shape
    w = BRANCH
    blk = lambda width, col: pl.BlockSpec((None, tc, width), lambda bi, i: (bi, i, col))
    halo = lambda nrow, col: pl.BlockSpec(
        (None, nrow, w), lambda bi, i: (bi, jnp.maximum(i * (tc // nrow) - 1, 0), col))
    const = functools.partial(_layer_spec, layer)
    return pl.pallas_call(
        functools.partial(_mixout_kernel, final=final, sub=256),
        out_shape=jax.ShapeDtypeStruct((b, l, d), F32),
        grid=(b, l // tc),
        in_specs=[blk(ACT_COLS * w, 0), halo(32, ACT_A), halo(8, ACT_CX),
                  blk(HALF, 0), blk(HALF, 0), blk(HALF, 0), blk(HALF, 0), blk(w, 0), blk(d, 0),
                  const((32, w)), const((8, w)), const((w, w)), const((8, w)), const((w, w)),
                  const((8, w)), const((4 * w, d + 128)), pl.BlockSpec((1, d), lambda bi, i: (0, 0))],
        out_specs=pl.BlockSpec((None, tc, d), lambda bi, i: (bi, i, 0)),
        scratch_shapes=[pltpu.VMEM((8, tc + 32, w), F32), pltpu.VMEM((tc + 8, w), F32)],
        compiler_params=pltpu.CompilerParams(
            dimension_semantics=("parallel", "parallel"), vmem_limit_bytes=VMEM_LIMIT_BYTES),
        name="mixout_final" if final else "mixout",
    )(act3, act3, act3, ulo, uhi, ylo, yhi, od, x3, acw, avec, apw, bvec, bglu, ccw, wout, fg)


def _pad_rows(a, n):
    return jnp.pad(a, ((0, 0), (0, n - a.shape[1]), (0, 0)))


def _s5_tiles(tab, state_cols):
    dep = tab.shape[0]
    nt, rb = S5_TILES, 2 * S5_GROUP
    t = tab.reshape(dep, S5_Q, nt, rb, 128)
    t = jnp.transpose(t, (0, 2, 1, 3, 4)).reshape(dep, nt, S5_Q * rb, 128)
    per = S5_STATE if state_cols else S5_GROUP
    src = jnp.arange(128)
    dst = jnp.arange(2 * 128)
    spread = ((src[:, None] // per == dst[None, :] // (2 * per))
              & (src[:, None] % per == dst[None, :] % per)).astype(F32)
    row_group = (jnp.arange(S5_Q * rb) % rb) // S5_GROUP
    col_group = (dst % (2 * per)) // per
    mask = (row_group[:, None] == col_group[None, :]).astype(F32)
    return (jnp.einsum("dkrc,cn->dkrn", t, spread, precision=HIGHEST) * mask).astype(BF16)


def _s5_tables(lam_re, lam_im, b_re, b_im, c_re, c_im, log_dt):
    q, g, p, h = S5_Q, S5_GROUPS, S5_STATE, S5_GROUP
    dep = lam_re.shape[0]
    lr = jnp.minimum(lam_re, -1e-4)
    li = lam_im
    dt = jnp.exp(log_dt)[..., None]
    lrdt, lidt = (lr * dt)[:, None], (li * dt)[:, None]

    def power(steps):
        s = steps[None, :, None, None]
        mag = jnp.exp(lrdt * s)
        return (mag * jnp.cos(lidt * s))[:, :, :, None, :], (mag * jnp.sin(lidt * s))[:, :, :, None, :]

    tok = jnp.arange(q, dtype=F32)
    p1r, p1i = power(jnp.ones((1,), F32))
    nr, ni = p1r[:, 0, :, 0] - 1.0, p1i[:, 0, :, 0]
    den = lr * lr + li * li
    fr, fi = (nr * lr + ni * li) / den, (ni * lr - nr * li) / den
    bbr = jnp.transpose(fr[..., None] * b_re - fi[..., None] * b_im, (0, 1, 3, 2))[:, None]
    bbi = jnp.transpose(fr[..., None] * b_im + fi[..., None] * b_re, (0, 1, 3, 2))[:, None]
    cr, ci = c_re[:, None], c_im[:, None]

    rr, ri = power(q - 1.0 - tok)
    wc = jnp.concatenate([rr * bbr - ri * bbi, rr * bbi + ri * bbr], axis=-1)
    qr, qi = power(tok + 1.0)
    mc = jnp.concatenate([cr * qr - ci * qi, -(cr * qi + ci * qr)], axis=-1)
    er, ei = power(tok)
    cpr = jnp.transpose(cr * er - ci * ei, (0, 2, 1, 3, 4)).reshape(dep, g, q * h, p)
    cpi = jnp.transpose(cr * ei + ci * er, (0, 2, 1, 3, 4)).reshape(dep, g, q * h, p)
    kd = (jnp.einsum("dgcp,dgyp->dgcy", bbr[:, 0], cpr, precision=HIGHEST)
          - jnp.einsum("dgcp,dgyp->dgcy", bbi[:, 0], cpi, precision=HIGHEST))
    lane = jnp.arange(q * h)
    tk = jnp.stack([jnp.where(lane >= j * h, jnp.roll(kd, j * h, axis=-1), 0.0) for j in range(q)],
                   axis=1)

    flat = lambda t: t.reshape(dep, q * g * h, t.shape[-1])
    ar, ai = power(jnp.full((1,), float(q), F32))
    a = jnp.concatenate([ar.reshape(dep, 1, g * p), ai.reshape(dep, 1, g * p),
                         jnp.zeros((dep, 6, g * p), F32)], axis=1)
    return _s5_tiles(flat(wc), True), _s5_tiles(flat(tk), False), _s5_tiles(flat(mc), True), a


def _w_in_tail(w_in):
    n = MAIN_BLOCKS * BRANCH
    pad = jnp.zeros(w_in.shape[:-1] + (128 - 2 * DN_HEADS,), w_in.dtype)
    return jnp.concatenate([w_in[..., n + 2 * DN_HEADS:], w_in[..., n:n + 2 * DN_HEADS], pad],
                           axis=-1).astype(BF16)


def kernel(x, norm_g, w_in, a_conv_w, a_conv_b, a_ln_g, a_ln_b, a_pw_w, a_pw_b, s5_lambda_re, s5_lambda_im, s5_b_re, s5_b_im, s5_c_re, s5_c_im, s5_d, s5_log_dt, s5_glu_w, s5_glu_b, c_conv_w, d_conv_w, d_a_log, d_dt_bias, d_norm_g, w_out, final_g):
    bsz, seq, d = x.shape
    depth = w_in.shape[0]
    w = BRANCH
    tm = min(512, seq)
    tc = min(1024, seq)
    dn_tc = min(256, seq)
    s5_rows = min(512, seq // S5_Q)

    w_bf = w_in.astype(BF16)
    w_tail = _w_in_tail(w_in)
    s5_wd, s5_td, s5_vd, s5_a = _s5_tables(s5_lambda_re, s5_lambda_im, s5_b_re, s5_b_im,
                                           s5_c_re, s5_c_im, s5_log_dt)
    acw = _pad_rows(a_conv_w, 32)
    zeros = jnp.zeros_like(a_conv_b)
    avec = jnp.stack([a_conv_b, a_ln_g, a_ln_b, a_pw_b, zeros, zeros, zeros, zeros], axis=1)
    bvec = jnp.stack([s5_d, s5_glu_b, zeros, zeros, zeros, zeros, zeros, zeros], axis=1)
    ccw = _pad_rows(c_conv_w, 8)
    dcw = _pad_rows(d_conv_w, 8)
    alog = jnp.repeat(d_a_log, DN_HEAD_DIM, axis=-1)[:, None, :]
    dtb = jnp.repeat(d_dt_bias, DN_HEAD_DIM, axis=-1)[:, None, :]
    ng = jnp.tile(d_norm_g, (1, DN_HEADS))[:, None, :]
    apw = a_pw_w.astype(BF16)
    bglu = s5_glu_w.astype(BF16)
    wout = jnp.pad(w_out, ((0, 0), (0, 0), (0, 128))).astype(BF16)
    fg = final_g[None, :]

    def seq3(a):
        return a.reshape(bsz, seq, a.shape[-1])

    ng3 = norm_g[:, None, :]
    for l in range(depth):
        act, dn, ulo, uhi = _inproj(l, x.reshape(bsz * seq, d), ng3, w_bf, w_tail, dcw, alog, dtb,
                                    tm, seq)
        ulo, uhi = seq3(ulo), seq3(uhi)
        ylo, yhi = _s5(l, ulo, uhi, s5_wd, s5_td, s5_vd, s5_a, s5_rows)
        od = _dnet(l, seq3(dn), ng, dn_tc)
        x = _mixout(l, seq3(act), ulo, uhi, ylo, yhi, od, x, acw, avec, apw, bvec, bglu, ccw, wout,
                    fg, tc, l == depth - 1)
    return x
```

```python
import functools

import jax
import jax.numpy as jnp
from jax import lax
from jax.experimental import pallas as pl
from jax.experimental.pallas import tpu as pltpu

F32 = jnp.float32
BF16 = jnp.bfloat16
HIGHEST = lax.Precision.HIGHEST

NORM_EPS = 1e-6
BRANCH = 256
CONF_KERNEL = 31
SC_KERNEL = 3
DN_HEADS = 4
DN_HEAD_DIM = 64
DN_CONV = 4
DN_CHUNK = 64
DN_GROUP = 16
S5_GROUPS = 16
S5_GROUP = 16
S5_STATE = 64
S5_Q = 8
S5_ROW = S5_Q * BRANCH
S5_NSTATE = S5_GROUPS * S5_STATE

COL_A, COL_B, COL_C, COL_QKV, MAIN_BLOCKS = 0, 3, 5, 9, 12
IN_COLS = MAIN_BLOCKS * BRANCH + 2 * DN_HEADS + BRANCH
TAIL_COLS = BRANCH + 128
HALF = BRANCH // 2
ACT_A, ACT_CX, ACT_COLS = 0, 1, 2
GATE_AZ, GATE_BZ, GATE_CB, GATE_DZ, GATE_COLS = 0, 1, 2, 3, 4
DN_Q, DN_K, DN_V, DN_BETA, DN_G, DN_COLS = 0, 1, 2, 3, 4, 5

VMEM_LIMIT_BYTES = 56 * 1024 * 1024


def _mm(a, b):
    return jnp.dot(a.astype(BF16), b.astype(BF16), preferred_element_type=F32)


def _layer_spec(layer, shape):
    zeros = (0,) * len(shape)
    return pl.BlockSpec((None,) + tuple(shape), lambda *_: (layer,) + zeros)


def _sigmoid(x):
    return 1.0 / (1.0 + jnp.exp(-x))


def _silu(x):
    return x * _sigmoid(x)


def _inproj_kernel(x_ref, g_ref, w_ref, wt_ref, dcw_ref, alog_ref, dtb_ref, act_ref, gate_ref, dn_ref,
                   ulo_ref, uhi_ref, pad_scr, *, steps_per_seq, sub):
    tm = x_ref.shape[0]
    w = BRANCH
    first = pl.program_id(0) % steps_per_seq == 0

    @pl.when(first)
    def _():
        pad_scr[tm:tm + 8, :] = jnp.zeros((8, 3 * w), F32)

    pad_scr[0:8, :] = pad_scr[tm:tm + 8, :]

    row_h = lax.broadcasted_iota(jnp.int32, (w, w), 0) // DN_HEAD_DIM
    col_h = lax.broadcasted_iota(jnp.int32, (w, w), 1) // DN_HEAD_DIM
    ones_bd = jnp.where(row_h == col_h, 1.0, 0.0).astype(BF16)

    def project(hb, col, n):
        return jnp.dot(hb, w_ref[:, col * w:(col + n) * w], preferred_element_type=F32)

    def put(ref, rows, j, val):
        ref[rows, j * w:(j + 1) * w] = val.astype(ref.dtype)

    def deltanet_inputs(rows, r0):
        xp = pad_scr[r0:r0 + sub + 8, :]
        acc = dcw_ref[DN_CONV - 1:DN_CONV, :] * xp[8:, :]
        for kk in range(DN_CONV - 1):
            acc = acc + dcw_ref[kk:kk + 1, :] * pltpu.roll(xp, DN_CONV - 1 - kk, 0)[8:, :]
        qkv = _silu(acc)
        q = qkv[:, 0:w]
        k = qkv[:, w:2 * w]
        ssq = jnp.dot((q * q).astype(BF16), ones_bd, preferred_element_type=F32)
        ssk = jnp.dot((k * k).astype(BF16), ones_bd, preferred_element_type=F32)
        put(dn_ref, rows, DN_Q, q * lax.rsqrt(ssq + NORM_EPS) * (DN_HEAD_DIM ** -0.5))
        put(dn_ref, rows, DN_K, k * lax.rsqrt(ssk + NORM_EPS))
        put(dn_ref, rows, DN_V, qkv[:, 2 * w:3 * w] * dn_ref[rows, DN_BETA * w:(DN_BETA + 1) * w])

    for r in range(tm // sub):
        r0 = r * sub
        rows = slice(r0, r0 + sub)
        x = x_ref[rows, :]
        h = x * lax.rsqrt(jnp.mean(x * x, axis=-1, keepdims=True) + NORM_EPS) * g_ref[...]
        hb = h.astype(BF16)
        pad_scr[r0 + 8:r0 + 8 + sub, :] = project(hb, COL_QKV, 3)
        p = project(hb, COL_A, 3)
        put(act_ref, rows, ACT_A, p[:, 0:w] * _sigmoid(p[:, w:2 * w]))
        put(gate_ref, rows, GATE_AZ, _silu(p[:, 2 * w:3 * w]))
        p = project(hb, COL_B, 2)
        ulo_ref[rows, :] = p[:, 0:HALF]
        uhi_ref[rows, :] = p[:, HALF:w]
        put(gate_ref, rows, GATE_BZ, _silu(p[:, w:2 * w]))
        p = project(hb, COL_C, 4)
        put(gate_ref, rows, GATE_CB, p[:, 0:w] * _silu(p[:, 3 * w:4 * w]))
        put(act_ref, rows, ACT_CX, p[:, w:2 * w] * p[:, 2 * w:3 * w])
        p = jnp.dot(hb, wt_ref[...], preferred_element_type=F32)
        put(gate_ref, rows, GATE_DZ, _silu(p[:, 0:w]))
        p = jnp.concatenate(
            [jnp.broadcast_to(p[:, w + j:w + j + 1], (sub, DN_HEAD_DIM)) for j in range(2 * DN_HEADS)],
            axis=1)
        sp_in = p[:, 0:w] + dtb_ref[...]
        softplus = jnp.maximum(sp_in, 0.0) + jnp.log1p(jnp.exp(-jnp.abs(sp_in)))
        put(dn_ref, rows, DN_G, -jnp.exp(alog_ref[...]) * softplus)
        put(dn_ref, rows, DN_BETA, _sigmoid(p[:, w:2 * w]))
        if r > 0:
            deltanet_inputs(slice(r0 - sub, r0), r0 - sub)
    deltanet_inputs(slice(tm - sub, tm), tm - sub)


def _inproj(layer, x2d, g, w, wt, dcw, alog, dtb, tm, seq):
    t, d = x2d.shape
    const = functools.partial(_layer_spec, layer)
    out = lambda n: pl.BlockSpec((tm, n * BRANCH), lambda i: (i, 0))
    return pl.pallas_call(
        functools.partial(_inproj_kernel, steps_per_seq=seq // tm, sub=min(256, tm)),
        out_shape=(jax.ShapeDtypeStruct((t, ACT_COLS * BRANCH), F32),
                   jax.ShapeDtypeStruct((t, GATE_COLS * BRANCH), BF16),
                   jax.ShapeDtypeStruct((t, DN_COLS * BRANCH), F32),
                   jax.ShapeDtypeStruct((t, HALF), F32),
                   jax.ShapeDtypeStruct((t, HALF), F32)),
        grid=(t // tm,),
        in_specs=[pl.BlockSpec((tm, d), lambda i: (i, 0)),
                  const((1, d)), const((d, IN_COLS)), const((d, TAIL_COLS)),
                  const((8, 3 * BRANCH)), const((1, BRANCH)), const((1, BRANCH))],
        out_specs=(out(ACT_COLS), out(GATE_COLS), out(DN_COLS), pl.BlockSpec((tm, HALF), lambda i: (i, 0)),
                   pl.BlockSpec((tm, HALF), lambda i: (i, 0))),
        scratch_shapes=[pltpu.VMEM((tm + 8, 3 * BRANCH), F32)],
        compiler_params=pltpu.CompilerParams(
            dimension_semantics=("arbitrary",), vmem_limit_bytes=VMEM_LIMIT_BYTES),
        name="inproj",
    )(x2d, g, w, wt, dcw, alog, dtb)


S5_TILES = S5_GROUPS // 2


def _block_transpose(pieces, block):
    lane = lax.broadcasted_iota(jnp.int32, pieces[0].shape, 1)
    n = len(pieces)
    s = n // 2
    while s:
        width = s * block
        lower = lane % (2 * width) < width
        out = list(pieces)
        for j in range(n):
            if j & s == 0:
                a, b = pieces[j], pieces[j + s]
                out[j] = jnp.where(lower, a, pltpu.roll(b, width, 1))
                out[j + s] = jnp.where(lower, pltpu.roll(a, 128 - width, 1), b)
        pieces = out
        s //= 2
    return pieces


def _s5_kernel(ulo_ref, uhi_ref, wd_ref, td_ref, vdt_ref, a_ref, ylo_ref, yhi_ref,
               carry_scr, z_scr, xp_scr):
    rows = ulo_ref.shape[0] // S5_Q
    n = S5_NSTATE
    tiles_per_half = S5_TILES // 2
    lanes = 2 * S5_GROUP

    @pl.when(pl.program_id(1) == 0)
    def _():
        carry_scr[...] = jnp.zeros_like(carry_scr)

    ub = []
    for half in (ulo_ref, uhi_ref):
        tok = [half[pl.ds(j, rows, stride=S5_Q), :] for j in range(S5_Q)]
        first = _block_transpose(tok[0:4], lanes)
        second = _block_transpose(tok[4:8], lanes)
        ub += [jnp.concatenate([first[k], second[k]], axis=1).astype(BF16)
               for k in range(tiles_per_half)]
    for k in range(S5_TILES):
        z = jnp.dot(ub[k], wd_ref[k], preferred_element_type=F32)
        z_scr[:, k * 128:(k + 1) * 128] = z[:, 0:128]
        z_scr[:, n + k * 128:n + (k + 1) * 128] = z[:, 128:256]

    ar = a_ref[0:1, :]
    ai = a_ref[1:2, :]

    def body(c, carry):
        xr, xi = carry
        xp_scr[pl.ds(c, 1), 0:n] = xr
        xp_scr[pl.ds(c, 1), n:2 * n] = xi
        zr = z_scr[pl.ds(c, 1), 0:n]
        zi = z_scr[pl.ds(c, 1), n:2 * n]
        return ar * xr - ai * xi + zr, ar * xi + ai * xr + zi

    xr, xi = lax.fori_loop(0, rows, body, (carry_scr[0:1, 0:n], carry_scr[0:1, n:2 * n]),
                           unroll=4)
    carry_scr[0:1, 0:n] = xr
    carry_scr[0:1, n:2 * n] = xi

    ys = []
    for k in range(S5_TILES):
        xp = jnp.concatenate([xp_scr[:, k * 128:(k + 1) * 128],
                              xp_scr[:, n + k * 128:n + (k + 1) * 128]], axis=1).astype(BF16)
        ys.append(jnp.dot(ub[k], td_ref[k], preferred_element_type=F32)
                  + lax.dot_general(xp, vdt_ref[k], (((1,), (1,)), ((), ())),
                                    preferred_element_type=F32))
    for h, out_ref in enumerate((ylo_ref, yhi_ref)):
        tiles = ys[h * tiles_per_half:(h + 1) * tiles_per_half]
        for m in range(2):
            tok = _block_transpose([y[:, m * 128:(m + 1) * 128] for y in tiles], lanes)
            for b in range(4):
                out_ref[pl.ds(4 * m + b, rows, stride=S5_Q), :] = tok[b]


def _s5(layer, ulo, uhi, wd, td, vd, a, rows):
    b, l, _ = ulo.shape
    const = functools.partial(_layer_spec, layer)
    half = lambda: pl.BlockSpec((None, rows * S5_Q, HALF), lambda bi, j: (bi, j, 0))
    tiles = (S5_TILES, BRANCH, BRANCH)
    return pl.pallas_call(
        _s5_kernel,
        out_shape=(jax.ShapeDtypeStruct((b, l, HALF), F32), jax.ShapeDtypeStruct((b, l, HALF), F32)),
        grid=(b, l // (rows * S5_Q)),
        in_specs=[half(), half(), const(tiles), const(tiles), const(tiles), const((8, S5_NSTATE))],
        out_specs=(half(), half()),
        scratch_shapes=[pltpu.VMEM((8, 2 * S5_NSTATE), F32),
                        pltpu.VMEM((rows, 2 * S5_NSTATE), F32),
                        pltpu.VMEM((rows, 2 * S5_NSTATE), F32)],
        compiler_params=pltpu.CompilerParams(
            dimension_semantics=("parallel", "arbitrary"), vmem_limit_bytes=VMEM_LIMIT_BYTES),
        name="s5",
    )(ulo, uhi, wd, td, vd, a)


def _tile_heads(x):
    return jnp.concatenate([x] * DN_HEADS, axis=0)


def _dnet_kernel(dn_ref, ng_ref, o_ref, s_scr, o_scr, u_scr, wq_scr, al_scr, egl_scr):
    nb, tc, _ = dn_ref.shape
    c64 = DN_CHUNK
    w = DN_HEADS * DN_HEAD_DIM
    nck = tc // c64
    gb = DN_GROUP // nck

    @pl.when(pl.program_id(0) == 0)
    def _():
        s_scr[...] = jnp.zeros_like(s_scr)

    row_h = lax.broadcasted_iota(jnp.int32, (w, w), 0) // DN_HEAD_DIM
    col_h = lax.broadcasted_iota(jnp.int32, (w, w), 1) // DN_HEAD_DIM
    bd = row_h == col_h
    bd_bf = jnp.where(bd, 1.0, 0.0).astype(BF16)

    def mmb(a, b_bf):
        return jnp.dot(a.astype(BF16), b_bf, preferred_element_type=F32)

    def mmb_nt(a, b_bf):
        return lax.dot_general(a.astype(BF16), b_bf, (((1,), (1,)), ((), ())),
                               preferred_element_type=F32)

    ri = lax.broadcasted_iota(jnp.int32, (c64, w), 0)
    cj = lax.broadcasted_iota(jnp.int32, (c64, w), 1) % DN_HEAD_DIM
    causal = ri >= cj
    strict = ri > cj
    diag = ri == cj
    eye_all = jnp.where(diag, 1.0, 0.0).astype(F32)
    lt_bf = jnp.where(lax.broadcasted_iota(jnp.int32, (c64, c64), 0)
                      >= lax.broadcasted_iota(jnp.int32, (c64, c64), 1), 1.0, 0.0).astype(BF16)

    def blockdiag(x):
        return _tile_heads(x.astype(BF16)) * bd_bf

    grp = range(DN_GROUP)

    def prepare(gi, carry):
        def col(i, j):
            return dn_ref[gi * gb + i // nck, (i % nck) * c64:(i % nck + 1) * c64, j * w:(j + 1) * w]

        rows = [pl.ds(pl.multiple_of((gi * gb + i // nck) * tc + (i % nck) * c64, c64), c64)
                for i in grp]
        qc = [col(i, DN_Q) for i in grp]
        kc = [col(i, DN_K) for i in grp]
        bc = [col(i, DN_BETA) for i in grp]
        g = [col(i, DN_G) for i in grp]
        g_hi = [x.astype(BF16) for x in g]
        g_lo = [(g[i] - g_hi[i].astype(F32)).astype(BF16) for i in grp]
        gc = [jnp.dot(lt_bf, g_hi[i], preferred_element_type=F32)
              + jnp.dot(lt_bf, g_lo[i], preferred_element_type=F32) for i in grp]
        egc = [jnp.exp(x) for x in gc]
        gl = [x[c64 - 1:c64, :] for x in gc]
        kb = [kc[i] * bc[i] for i in grp]
        grow = [jnp.sum(jnp.where(diag, x, 0.0), axis=0, keepdims=True) for x in gc]
        decay = [jnp.exp(jnp.where(causal, gc[i] - grow[i], -jnp.inf)) for i in grp]
        k_bd = [blockdiag(x) for x in kc]
        kq = [mmb_nt(jnp.concatenate([kb[i], qc[i]], axis=0), k_bd[i]) for i in grp]
        for i in grp:
            al_scr[gi * DN_GROUP + i, 0:c64, :] = kq[i][c64:2 * c64, :] * decay[i]
        x = [jnp.where(strict, -kq[i][0:c64, :] * decay[i], 0.0) for i in grp]
        p = [eye_all + x[i] for i in grp]
        x = [mmb(x[i], blockdiag(x[i])) for i in grp]
        for _ in range(4):
            xp = [mmb(jnp.concatenate([x[i], p[i]], axis=0), blockdiag(x[i])) for i in grp]
            x = [xp[i][0:c64, :] for i in grp]
            p = [p[i] + xp[i][c64:2 * c64, :] for i in grp]
        p = [p[i] + mmb(p[i], blockdiag(x[i])) for i in grp]
        for i in grp:
            ci = gi * DN_GROUP + i
            u_scr[rows[i], :] = mmb(p[i], blockdiag(col(i, DN_V)))
            wq_scr[ci, 0:c64, :] = mmb(p[i], blockdiag(kb[i] * egc[i]))
            wq_scr[ci, c64:2 * c64, :] = qc[i] * egc[i]
            kd_t = (kc[i] * jnp.exp(gl[i] - gc[i])).T
            al_scr[ci, c64:2 * c64, :] = jnp.concatenate(
                [kd_t[h * c64:(h + 1) * c64, :] for h in range(DN_HEADS)], axis=1)
            egl_scr[ci] = jnp.broadcast_to(jnp.exp(gl[i]), (8, w))
        return carry

    lax.fori_loop(0, nb // gb, prepare, 0)

    bs = range(nb)

    def recur(c, carry):
        rows = [pl.ds(pl.multiple_of(b * tc + c * c64, c64), c64) for b in bs]
        s = [s_scr[b] for b in bs]
        ws_qs = [mmb(wq_scr[b * nck + c], blockdiag(s[b])) for b in bs]
        v_new = [u_scr[rows[b], :] - ws_qs[b][0:c64, :] for b in bs]
        od = [mmb(al_scr[b * nck + c], blockdiag(v_new[b])) for b in bs]
        for b in bs:
            o_scr[rows[b], :] = ws_qs[b][c64:2 * c64, :] + od[b][0:c64, :]
        for b in bs:
            s_scr[b] = s[b] * egl_scr[b * nck + c][0:1, :] + od[b][c64:2 * c64, :]
        return carry

    lax.fori_loop(0, nck, recur, 0)

    o = o_scr[...]
    ms = mmb(o * o, bd_bf) * (1.0 / DN_HEAD_DIM)
    o_ref[...] = (o * lax.rsqrt(ms + NORM_EPS) * ng_ref[...]).reshape(nb, tc, w).astype(o_ref.dtype)


def _dnet(layer, dn3, ng, tc):
    b, l, _ = dn3.shape
    w = BRANCH
    nchunk = b * tc // DN_CHUNK
    assert DN_GROUP % (tc // DN_CHUNK) == 0 and nchunk % DN_GROUP == 0
    act = lambda: pltpu.VMEM((b * tc, w), F32)
    return pl.pallas_call(
        _dnet_kernel,
        out_shape=jax.ShapeDtypeStruct((b, l, w), BF16),
        grid=(l // tc,),
        in_specs=[pl.BlockSpec((b, tc, DN_COLS * w), lambda i: (0, i, 0)),
                  _layer_spec(layer, (1, w))],
        out_specs=pl.BlockSpec((b, tc, w), lambda i: (0, i, 0)),
        scratch_shapes=[pltpu.VMEM((b, DN_CHUNK, w), F32),
                        act(), act(),
                        pltpu.VMEM((nchunk, 2 * DN_CHUNK, w), F32),
                        pltpu.VMEM((nchunk, 2 * DN_CHUNK, w), F32),
                        pltpu.VMEM((nchunk, 8, w), F32)],
        compiler_params=pltpu.CompilerParams(
            dimension_semantics=("arbitrary",), vmem_limit_bytes=VMEM_LIMIT_BYTES),
        name="dnet",
    )(dn3, ng)


def _mixout_kernel(act_ref, gate_ref, ahalo_ref, chalo_ref, ulo_ref, uhi_ref, ylo_ref, yhi_ref, od_ref,
                   x_ref,
                   acw_ref, avec_ref, apw_ref, bvec_ref, bglu_ref, ccw_ref, wout_ref, fg_ref,
                   out_ref, ash_scr, cpad_scr, *, final, sub):
    tc = x_ref.shape[0]
    w = BRANCH
    first = pl.program_id(1) == 0

    def act(j, rows=slice(None)):
        return act_ref[rows, j * w:(j + 1) * w]

    def gate(j, rows):
        return gate_ref[rows, j * w:(j + 1) * w].astype(F32)

    ash_scr[0, 0:32, :] = jnp.where(first, 0.0, ahalo_ref[...])
    ash_scr[0, 32:32 + tc, :] = act(ACT_A)
    apad = ash_scr[0]
    for r in range(1, 8):
        ash_scr[r] = pltpu.roll(apad, tc + 32 - r, 0)
    cpad_scr[0:8, :] = jnp.where(first, 0.0, chalo_ref[...])
    cpad_scr[8:8 + tc, :] = act(ACT_CX)

    conv_b, ln_g, ln_b, pw_b = (avec_ref[0:1, :], avec_ref[1:2, :], avec_ref[2:3, :],
                                avec_ref[3:4, :])
    d_skip, glu_b = bvec_ref[0:1, :], bvec_ref[1:2, :]

    for r in range(tc // sub):
        r0 = r * sub
        rows = slice(r0, r0 + sub)
        acc = conv_b
        for kk in range(CONF_KERNEL):
            off = 32 - (CONF_KERNEL - 1) + kk
            acc = acc + acw_ref[kk:kk + 1, :] * ash_scr[off % 8, pl.ds(r0 + off - off % 8, sub), :]
        mu = jnp.mean(acc, axis=-1, keepdims=True)
        xc = acc - mu
        ln = xc * lax.rsqrt(jnp.mean(xc * xc, axis=-1, keepdims=True) + NORM_EPS) * ln_g + ln_b
        ya = (_mm(_silu(ln), apw_ref[...]) + pw_b) * gate(GATE_AZ, rows)
        ub = jnp.concatenate([ulo_ref[rows, :], uhi_ref[rows, :]], axis=1)
        ys = jnp.concatenate([ylo_ref[rows, :], yhi_ref[rows, :]], axis=1)
        yb = jax.nn.gelu(ys + d_skip * ub)
        yb = yb * _sigmoid(_mm(yb, bglu_ref[...]) + glu_b) * gate(GATE_BZ, rows)
        cp = cpad_scr[r0:r0 + sub + 8, :]
        cacc = ccw_ref[SC_KERNEL - 1:SC_KERNEL, :] * cp[8:, :]
        for kk in range(SC_KERNEL - 1):
            cacc = cacc + ccw_ref[kk:kk + 1, :] * pltpu.roll(cp, SC_KERNEL - 1 - kk, 0)[8:, :]
        yc = gate(GATE_CB, rows) * cacc
        yd = od_ref[rows, :].astype(F32) * gate(GATE_DZ, rows)
        mixed = jnp.concatenate([ya, yb, yc, yd], axis=1)
        y = x_ref[rows, :] + _mm(mixed, wout_ref[:, 0:4 * w])
        if final:
            y = y * lax.rsqrt(jnp.mean(y * y, axis=-1, keepdims=True) + NORM_EPS) * fg_ref[...]
        out_ref[rows, :] = y


def _mixout(layer, act3, gate3, ulo, uhi, ylo, yhi, od, x3, acw, avec, apw, bvec, bglu, ccw, wout, fg, tc,
            final):
    b, l, d = x3.shape
    w = BRANCH
    blk = lambda width, col: pl.BlockSpec((None, tc, width), lambda bi, i: (bi, i, col))
    halo = lambda nrow, col: pl.BlockSpec(
        (None, nrow, w), lambda bi, i: (bi, jnp.maximum(i * (tc // nrow) - 1, 0), col))
    const = functools.partial(_layer_spec, layer)
    return pl.pallas_call(
        functools.partial(_mixout_kernel, final=final, sub=256),
        out_shape=jax.ShapeDtypeStruct((b, l, d), F32),
        grid=(b, l // tc),
        in_specs=[blk(ACT_COLS * w, 0), blk(GATE_COLS * w, 0), halo(32, ACT_A), halo(8, ACT_CX),
                  blk(HALF, 0), blk(HALF, 0), blk(HALF, 0), blk(HALF, 0), blk(w, 0), blk(d, 0),
                  const((32, w)), const((8, w)), const((w, w)), const((8, w)), const((w, w)),
                  const((8, w)), const((4 * w, d + 128)), pl.BlockSpec((1, d), lambda bi, i: (0, 0))],
        out_specs=pl.BlockSpec((None, tc, d), lambda bi, i: (bi, i, 0)),
        scratch_shapes=[pltpu.VMEM((8, tc + 32, w), F32), pltpu.VMEM((tc + 8, w), F32)],
        compiler_params=pltpu.CompilerParams(
            dimension_semantics=("parallel", "parallel"), vmem_limit_bytes=VMEM_LIMIT_BYTES),
        name="mixout_final" if final else "mixout",
    )(act3, gate3, act3, act3, ulo, uhi, ylo, yhi, od, x3, acw, avec, apw, bvec, bglu, ccw, wout, fg)


def _pad_rows(a, n):
    return jnp.pad(a, ((0, 0), (0, n - a.shape[1]), (0, 0)))


def _s5_tiles(tab, state_cols):
    dep = tab.shape[0]
    nt, rb = S5_TILES, 2 * S5_GROUP
    t = tab.reshape(dep, S5_Q, nt, rb, 128)
    t = jnp.transpose(t, (0, 2, 1, 3, 4)).reshape(dep, nt, S5_Q * rb, 128)
    per = S5_STATE if state_cols else S5_GROUP
    src = jnp.arange(128)
    dst = jnp.arange(2 * 128)
    spread = ((src[:, None] // per == dst[None, :] // (2 * per))
              & (src[:, None] % per == dst[None, :] % per)).astype(F32)
    row_group = (jnp.arange(S5_Q * rb) % rb) // S5_GROUP
    col_group = (dst % (2 * per)) // per
    mask = (row_group[:, None] == col_group[None, :]).astype(F32)
    return (jnp.einsum("dkrc,cn->dkrn", t, spread, precision=HIGHEST) * mask).astype(BF16)


def _s5_tables(lam_re, lam_im, b_re, b_im, c_re, c_im, log_dt):
    q, g, p, h = S5_Q, S5_GROUPS, S5_STATE, S5_GROUP
    dep = lam_re.shape[0]
    lr = jnp.minimum(lam_re, -1e-4)
    li = lam_im
    dt = jnp.exp(log_dt)[..., None]
    lrdt, lidt = (lr * dt)[:, None], (li * dt)[:, None]

    def power(steps):
        s = steps[None, :, None, None]
        mag = jnp.exp(lrdt * s)
        return (mag * jnp.cos(lidt * s))[:, :, :, None, :], (mag * jnp.sin(lidt * s))[:, :, :, None, :]

    tok = jnp.arange(q, dtype=F32)
    p1r, p1i = power(jnp.ones((1,), F32))
    nr, ni = p1r[:, 0, :, 0] - 1.0, p1i[:, 0, :, 0]
    den = lr * lr + li * li
    fr, fi = (nr * lr + ni * li) / den, (ni * lr - nr * li) / den
    bbr = jnp.transpose(fr[..., None] * b_re - fi[..., None] * b_im, (0, 1, 3, 2))[:, None]
    bbi = jnp.transpose(fr[..., None] * b_im + fi[..., None] * b_re, (0, 1, 3, 2))[:, None]
    cr, ci = c_re[:, None], c_im[:, None]

    rr, ri = power(q - 1.0 - tok)
    wc = jnp.concatenate([rr * bbr - ri * bbi, rr * bbi + ri * bbr], axis=-1)
    qr, qi = power(tok + 1.0)
    mc = jnp.concatenate([cr * qr - ci * qi, -(cr * qi + ci * qr)], axis=-1)
    er, ei = power(tok)
    cpr = jnp.transpose(cr * er - ci * ei, (0, 2, 1, 3, 4)).reshape(dep, g, q * h, p)
    cpi = jnp.transpose(cr * ei + ci * er, (0, 2, 1, 3, 4)).reshape(dep, g, q * h, p)
    kd = (jnp.einsum("dgcp,dgyp->dgcy", bbr[:, 0], cpr, precision=HIGHEST)
          - jnp.einsum("dgcp,dgyp->dgcy", bbi[:, 0], cpi, precision=HIGHEST))
    lane = jnp.arange(q * h)
    tk = jnp.stack([jnp.where(lane >= j * h, jnp.roll(kd, j * h, axis=-1), 0.0) for j in range(q)],
                   axis=1)

    flat = lambda t: t.reshape(dep, q * g * h, t.shape[-1])
    ar, ai = power(jnp.full((1,), float(q), F32))
    a = jnp.concatenate([ar.reshape(dep, 1, g * p), ai.reshape(dep, 1, g * p),
                         jnp.zeros((dep, 6, g * p), F32)], axis=1)
    return _s5_tiles(flat(wc), True), _s5_tiles(flat(tk), False), _s5_tiles(flat(mc), True), a


def _w_in_tail(w_in):
    n = MAIN_BLOCKS * BRANCH
    pad = jnp.zeros(w_in.shape[:-1] + (128 - 2 * DN_HEADS,), w_in.dtype)
    return jnp.concatenate([w_in[..., n + 2 * DN_HEADS:], w_in[..., n:n + 2 * DN_HEADS], pad],
                           axis=-1).astype(BF16)


def kernel(x, norm_g, w_in, a_conv_w, a_conv_b, a_ln_g, a_ln_b, a_pw_w, a_pw_b, s5_lambda_re, s5_lambda_im, s5_b_re, s5_b_im, s5_c_re, s5_c_im, s5_d, s5_log_dt, s5_glu_w, s5_glu_b, c_conv_w, d_conv_w, d_a_log, d_dt_bias, d_norm_g, w_out, final_g):
    bsz, seq, d = x.shape
    depth = w_in.shape[0]
    w = BRANCH
    tm = min(512, seq)
    tc = min(1024, seq)
    dn_tc = min(256, seq)
    s5_rows = min(512, seq // S5_Q)

    w_bf = w_in.astype(BF16)
    w_tail = _w_in_tail(w_in)
    s5_wd, s5_td, s5_vd, s5_a = _s5_tables(s5_lambda_re, s5_lambda_im, s5_b_re, s5_b_im,
                                           s5_c_re, s5_c_im, s5_log_dt)
    acw = _pad_rows(a_conv_w, 32)
    zeros = jnp.zeros_like(a_conv_b)
    avec = jnp.stack([a_conv_b, a_ln_g, a_ln_b, a_pw_b, zeros, zeros, zeros, zeros], axis=1)
    bvec = jnp.stack([s5_d, s5_glu_b, zeros, zeros, zeros, zeros, zeros, zeros], axis=1)
    ccw = _pad_rows(c_conv_w, 8)
    dcw = _pad_rows(d_conv_w, 8)
    alog = jnp.repeat(d_a_log, DN_HEAD_DIM, axis=-1)[:, None, :]
    dtb = jnp.repeat(d_dt_bias, DN_HEAD_DIM, axis=-1)[:, None, :]
    ng = jnp.tile(d_norm_g, (1, DN_HEADS))[:, None, :]
    apw = a_pw_w.astype(BF16)
    bglu = s5_glu_w.astype(BF16)
    wout = jnp.pad(w_out, ((0, 0), (0, 0), (0, 128))).astype(BF16)
    fg = final_g[None, :]

    def seq3(a):
        return a.reshape(bsz, seq, a.shape[-1])

    ng3 = norm_g[:, None, :]
    for l in range(depth):
        act, gate, dn, ulo, uhi = _inproj(l, x.reshape(bsz * seq, d), ng3, w_bf, w_tail, dcw, alog, dtb,
                                    tm, seq)
        ulo, uhi = seq3(ulo), seq3(uhi)
        ylo, yhi = _s5(l, ulo, uhi, s5_wd, s5_td, s5_vd, s5_a, s5_rows)
        od = _dnet(l, seq3(dn), ng, dn_tc)
        x = _mixout(l, seq3(act), seq3(gate), ulo, uhi, ylo, yhi, od, x, acw, avec, apw, bvec, bglu, ccw, wout,
                    fg, tc, l == depth - 1)
    return x
```

```python
import functools

import jax
import jax.numpy as jnp
from jax import lax
from jax.experimental import pallas as pl
from jax.experimental.pallas import tpu as pltpu

F32 = jnp.float32
BF16 = jnp.bfloat16
HIGHEST = lax.Precision.HIGHEST

NORM_EPS = 1e-6
BRANCH = 256
CONF_KERNEL = 31
PITCH_PAD = 8
SC_KERNEL = 3
DN_HEADS = 4
DN_HEAD_DIM = 64
DN_CONV = 4
DN_CHUNK = 64
DN_GROUP = 16
S5_GROUPS = 16
S5_GROUP = 16
S5_STATE = 64
S5_Q = 8
S5_ROW = S5_Q * BRANCH
S5_NSTATE = S5_GROUPS * S5_STATE

COL_A, COL_B, COL_C, COL_QKV, MAIN_BLOCKS = 0, 3, 5, 9, 12
IN_COLS = MAIN_BLOCKS * BRANCH + 2 * DN_HEADS + BRANCH
TAIL_COLS = BRANCH + 128
HALF = BRANCH // 2
ACT_A, ACT_AZ, ACT_BZ, ACT_CB, ACT_CX, ACT_DZ, ACT_COLS = 0, 1, 2, 3, 4, 5, 6
DN_Q, DN_K, DN_V, DN_BETA, DN_G, DN_COLS = 0, 1, 2, 3, 4, 5

VMEM_LIMIT_BYTES = 56 * 1024 * 1024


def _mm(a, b):
    return jnp.dot(a.astype(BF16), b.astype(BF16), preferred_element_type=F32)


def _layer_spec(layer, shape):
    zeros = (0,) * len(shape)
    return pl.BlockSpec((None,) + tuple(shape), lambda *_: (layer,) + zeros)


def _sigmoid(x):
    return 1.0 / (1.0 + jnp.exp(-x))


def _silu(x):
    return x * _sigmoid(x)


def _inproj_kernel(x_ref, g_ref, w_ref, wt_ref, dcw_ref, alog_ref, dtb_ref, act_ref, dn_ref, ulo_ref,
                   uhi_ref, pad_scr, *, steps_per_seq, sub):
    tm = x_ref.shape[0]
    w = BRANCH
    first = pl.program_id(0) % steps_per_seq == 0

    @pl.when(first)
    def _():
        pad_scr[tm:tm + 8, :] = jnp.zeros((8, 3 * w), F32)

    pad_scr[0:8, :] = pad_scr[tm:tm + 8, :]

    row_h = lax.broadcasted_iota(jnp.int32, (w, w), 0) // DN_HEAD_DIM
    col_h = lax.broadcasted_iota(jnp.int32, (w, w), 1) // DN_HEAD_DIM
    ones_bd = jnp.where(row_h == col_h, 1.0, 0.0).astype(BF16)

    def project(hb, col, n):
        return jnp.dot(hb, w_ref[:, col * w:(col + n) * w], preferred_element_type=F32)

    def put(ref, rows, j, val):
        ref[rows, j * w:(j + 1) * w] = val

    def deltanet_inputs(rows, r0):
        xp = pad_scr[r0:r0 + sub + 8, :]
        acc = dcw_ref[DN_CONV - 1:DN_CONV, :] * xp[8:, :]
        for kk in range(DN_CONV - 1):
            acc = acc + dcw_ref[kk:kk + 1, :] * pltpu.roll(xp, DN_CONV - 1 - kk, 0)[8:, :]
        qkv = _silu(acc)
        q = qkv[:, 0:w]
        k = qkv[:, w:2 * w]
        ssq = jnp.dot((q * q).astype(BF16), ones_bd, preferred_element_type=F32)
        ssk = jnp.dot((k * k).astype(BF16), ones_bd, preferred_element_type=F32)
        put(dn_ref, rows, DN_Q, q * lax.rsqrt(ssq + NORM_EPS) * (DN_HEAD_DIM ** -0.5))
        put(dn_ref, rows, DN_K, k * lax.rsqrt(ssk + NORM_EPS))
        put(dn_ref, rows, DN_V, qkv[:, 2 * w:3 * w] * dn_ref[rows, DN_BETA * w:(DN_BETA + 1) * w])

    for r in range(tm // sub):
        r0 = r * sub
        rows = slice(r0, r0 + sub)
        x = x_ref[rows, :]
        h = x * lax.rsqrt(jnp.mean(x * x, axis=-1, keepdims=True) + NORM_EPS) * g_ref[...]
        hb = h.astype(BF16)
        pad_scr[r0 + 8:r0 + 8 + sub, :] = project(hb, COL_QKV, 3)
        p = project(hb, COL_A, 3)
        put(act_ref, rows, ACT_A, p[:, 0:w] * _sigmoid(p[:, w:2 * w]))
        put(act_ref, rows, ACT_AZ, _silu(p[:, 2 * w:3 * w]))
        p = project(hb, COL_B, 2)
        ulo_ref[rows, :] = p[:, 0:HALF]
        uhi_ref[rows, :] = p[:, HALF:w]
        put(act_ref, rows, ACT_BZ, _silu(p[:, w:2 * w]))
        p = project(hb, COL_C, 4)
        put(act_ref, rows, ACT_CB, p[:, 0:w] * _silu(p[:, 3 * w:4 * w]))
        put(act_ref, rows, ACT_CX, p[:, w:2 * w] * p[:, 2 * w:3 * w])
        p = jnp.dot(hb, wt_ref[...], preferred_element_type=F32)
        put(act_ref, rows, ACT_DZ, _silu(p[:, 0:w]))
        p = jnp.concatenate(
            [jnp.broadcast_to(p[:, w + j:w + j + 1], (sub, DN_HEAD_DIM)) for j in range(2 * DN_HEADS)],
            axis=1)
        sp_in = p[:, 0:w] + dtb_ref[...]
        softplus = jnp.maximum(sp_in, 0.0) + jnp.log1p(jnp.exp(-jnp.abs(sp_in)))
        put(dn_ref, rows, DN_G, -jnp.exp(alog_ref[...]) * softplus)
        put(dn_ref, rows, DN_BETA, _sigmoid(p[:, w:2 * w]))
        if r > 0:
            deltanet_inputs(slice(r0 - sub, r0), r0 - sub)
    deltanet_inputs(slice(tm - sub, tm), tm - sub)


def _inproj(layer, x2d, g, w, wt, dcw, alog, dtb, tm, seq):
    t, d = x2d.shape
    const = functools.partial(_layer_spec, layer)
    out = lambda n: pl.BlockSpec((tm, n * BRANCH), lambda i: (i, 0))
    return pl.pallas_call(
        functools.partial(_inproj_kernel, steps_per_seq=seq // tm, sub=min(256, tm)),
        out_shape=(jax.ShapeDtypeStruct((t, ACT_COLS * BRANCH), F32),
                   jax.ShapeDtypeStruct((t, DN_COLS * BRANCH), F32),
                   jax.ShapeDtypeStruct((t, HALF), F32),
                   jax.ShapeDtypeStruct((t, HALF), F32)),
        grid=(t // tm,),
        in_specs=[pl.BlockSpec((tm, d), lambda i: (i, 0)),
                  const((1, d)), const((d, IN_COLS)), const((d, TAIL_COLS)),
                  const((8, 3 * BRANCH)), const((1, BRANCH)), const((1, BRANCH))],
        out_specs=(out(ACT_COLS), out(DN_COLS), pl.BlockSpec((tm, HALF), lambda i: (i, 0)),
                   pl.BlockSpec((tm, HALF), lambda i: (i, 0))),
        scratch_shapes=[pltpu.VMEM((tm + 8, 3 * BRANCH), F32)],
        compiler_params=pltpu.CompilerParams(
            dimension_semantics=("arbitrary",), vmem_limit_bytes=VMEM_LIMIT_BYTES),
        name="inproj",
    )(x2d, g, w, wt, dcw, alog, dtb)


S5_TILES = S5_GROUPS // 2


def _block_transpose(pieces, block):
    lane = lax.broadcasted_iota(jnp.int32, pieces[0].shape, 1)
    n = len(pieces)
    s = n // 2
    while s:
        width = s * block
        lower = lane % (2 * width) < width
        out = list(pieces)
        for j in range(n):
            if j & s == 0:
                a, b = pieces[j], pieces[j + s]
                out[j] = jnp.where(lower, a, pltpu.roll(b, width, 1))
                out[j + s] = jnp.where(lower, pltpu.roll(a, 128 - width, 1), b)
        pieces = out
        s //= 2
    return pieces


def _s5_kernel(ulo_ref, uhi_ref, wd_ref, td_ref, vdt_ref, a_ref, ylo_ref, yhi_ref,
               carry_scr, z_scr, xp_scr):
    rows = ulo_ref.shape[0] // S5_Q
    n = S5_NSTATE
    tiles_per_half = S5_TILES // 2
    lanes = 2 * S5_GROUP

    @pl.when(pl.program_id(1) == 0)
    def _():
        carry_scr[...] = jnp.zeros_like(carry_scr)

    ub = []
    for half in (ulo_ref, uhi_ref):
        tok = [half[pl.ds(j, rows, stride=S5_Q), :] for j in range(S5_Q)]
        first = _block_transpose(tok[0:4], lanes)
        second = _block_transpose(tok[4:8], lanes)
        ub += [jnp.concatenate([first[k], second[k]], axis=1).astype(BF16)
               for k in range(tiles_per_half)]
    for k in range(S5_TILES):
        z = jnp.dot(ub[k], wd_ref[k], preferred_element_type=F32)
        z_scr[:, k * 128:(k + 1) * 128] = z[:, 0:128]
        z_scr[:, n + k * 128:n + (k + 1) * 128] = z[:, 128:256]

    ar = a_ref[0:1, :]
    ai = a_ref[1:2, :]

    def body(c, carry):
        xr, xi = carry
        xp_scr[pl.ds(c, 1), 0:n] = xr
        xp_scr[pl.ds(c, 1), n:2 * n] = xi
        zr = z_scr[pl.ds(c, 1), 0:n]
        zi = z_scr[pl.ds(c, 1), n:2 * n]
        return ar * xr - ai * xi + zr, ar * xi + ai * xr + zi

    xr, xi = lax.fori_loop(0, rows, body, (carry_scr[0:1, 0:n], carry_scr[0:1, n:2 * n]),
                           unroll=4)
    carry_scr[0:1, 0:n] = xr
    carry_scr[0:1, n:2 * n] = xi

    ys = []
    for k in range(S5_TILES):
        xp = jnp.concatenate([xp_scr[:, k * 128:(k + 1) * 128],
                              xp_scr[:, n + k * 128:n + (k + 1) * 128]], axis=1).astype(BF16)
        ys.append(jnp.dot(ub[k], td_ref[k], preferred_element_type=F32)
                  + lax.dot_general(xp, vdt_ref[k], (((1,), (1,)), ((), ())),
                                    preferred_element_type=F32))
    for h, out_ref in enumerate((ylo_ref, yhi_ref)):
        tiles = ys[h * tiles_per_half:(h + 1) * tiles_per_half]
        for m in range(2):
            tok = _block_transpose([y[:, m * 128:(m + 1) * 128] for y in tiles], lanes)
            for b in range(4):
                out_ref[pl.ds(4 * m + b, rows, stride=S5_Q), :] = tok[b]


def _s5(layer, ulo, uhi, wd, td, vd, a, rows):
    b, l, _ = ulo.shape
    const = functools.partial(_layer_spec, layer)
    half = lambda: pl.BlockSpec((None, rows * S5_Q, HALF), lambda bi, j: (bi, j, 0))
    tiles = (S5_TILES, BRANCH, BRANCH)
    return pl.pallas_call(
        _s5_kernel,
        out_shape=(jax.ShapeDtypeStruct((b, l, HALF), F32), jax.ShapeDtypeStruct((b, l, HALF), F32)),
        grid=(b, l // (rows * S5_Q)),
        in_specs=[half(), half(), const(tiles), const(tiles), const(tiles), const((8, S5_NSTATE))],
        out_specs=(half(), half()),
        scratch_shapes=[pltpu.VMEM((8, 2 * S5_NSTATE), F32),
                        pltpu.VMEM((rows, 2 * S5_NSTATE), F32),
                        pltpu.VMEM((rows, 2 * S5_NSTATE), F32)],
        compiler_params=pltpu.CompilerParams(
            dimension_semantics=("parallel", "arbitrary"), vmem_limit_bytes=VMEM_LIMIT_BYTES),
        name="s5",
    )(ulo, uhi, wd, td, vd, a)


def _tile_heads(x):
    return jnp.concatenate([x] * DN_HEADS, axis=0)


def _dnet_kernel(dn_ref, ng_ref, o_ref, s_scr, o_scr, u_scr, wq_scr, al_scr, egl_scr):
    nb, tc, _ = dn_ref.shape
    c64 = DN_CHUNK
    w = DN_HEADS * DN_HEAD_DIM
    nck = tc // c64
    gb = DN_GROUP // nck

    @pl.when(pl.program_id(0) == 0)
    def _():
        s_scr[...] = jnp.zeros_like(s_scr)

    row_h = lax.broadcasted_iota(jnp.int32, (w, w), 0) // DN_HEAD_DIM
    col_h = lax.broadcasted_iota(jnp.int32, (w, w), 1) // DN_HEAD_DIM
    bd = row_h == col_h
    bd_bf = jnp.where(bd, 1.0, 0.0).astype(BF16)

    def mmb(a, b_bf):
        return jnp.dot(a.astype(BF16), b_bf, preferred_element_type=F32)

    def mmb_nt(a, b_bf):
        return lax.dot_general(a.astype(BF16), b_bf, (((1,), (1,)), ((), ())),
                               preferred_element_type=F32)

    ri = lax.broadcasted_iota(jnp.int32, (c64, w), 0)
    cj = lax.broadcasted_iota(jnp.int32, (c64, w), 1) % DN_HEAD_DIM
    causal = ri >= cj
    strict = ri > cj
    diag = ri == cj
    eye_all = jnp.where(diag, 1.0, 0.0).astype(F32)
    lt_bf = jnp.where(lax.broadcasted_iota(jnp.int32, (c64, c64), 0)
                      >= lax.broadcasted_iota(jnp.int32, (c64, c64), 1), 1.0, 0.0).astype(BF16)

    def blockdiag(x):
        return _tile_heads(x.astype(BF16)) * bd_bf

    grp = range(DN_GROUP)

    def prepare(gi, carry):
        def col(i, j):
            return dn_ref[gi * gb + i // nck, (i % nck) * c64:(i % nck + 1) * c64, j * w:(j + 1) * w]

        rows = [pl.ds(pl.multiple_of((gi * gb + i // nck) * tc + (i % nck) * c64, c64), c64)
                for i in grp]
        qc = [col(i, DN_Q) for i in grp]
        kc = [col(i, DN_K) for i in grp]
        bc = [col(i, DN_BETA) for i in grp]
        g = [col(i, DN_G) for i in grp]
        g_hi = [x.astype(BF16) for x in g]
        g_lo = [(g[i] - g_hi[i].astype(F32)).astype(BF16) for i in grp]
        gc = [jnp.dot(lt_bf, g_hi[i], preferred_element_type=F32)
              + jnp.dot(lt_bf, g_lo[i], preferred_element_type=F32) for i in grp]
        egc = [jnp.exp(x) for x in gc]
        gl = [x[c64 - 1:c64, :] for x in gc]
        kb = [kc[i] * bc[i] for i in grp]
        grow = [jnp.sum(jnp.where(diag, x, 0.0), axis=0, keepdims=True) for x in gc]
        decay = [jnp.exp(jnp.where(causal, gc[i] - grow[i], -jnp.inf)) for i in grp]
        k_bd = [blockdiag(x) for x in kc]
        kq = [mmb_nt(jnp.concatenate([kb[i], qc[i]], axis=0), k_bd[i]) for i in grp]
        for i in grp:
            al_scr[gi * DN_GROUP + i, 0:c64, :] = kq[i][c64:2 * c64, :] * decay[i]
        x = [jnp.where(strict, -kq[i][0:c64, :] * decay[i], 0.0) for i in grp]
        p = [eye_all + x[i] for i in grp]
        x = [mmb(x[i], blockdiag(x[i])) for i in grp]
        for _ in range(4):
            xp = [mmb(jnp.concatenate([x[i], p[i]], axis=0), blockdiag(x[i])) for i in grp]
            x = [xp[i][0:c64, :] for i in grp]
            p = [p[i] + xp[i][c64:2 * c64, :] for i in grp]
        p = [p[i] + mmb(p[i], blockdiag(x[i])) for i in grp]
        for i in grp:
            ci = gi * DN_GROUP + i
            u_scr[rows[i], :] = mmb(p[i], blockdiag(col(i, DN_V)))
            wq_scr[ci, 0:c64, :] = mmb(p[i], blockdiag(kb[i] * egc[i]))
            wq_scr[ci, c64:2 * c64, :] = qc[i] * egc[i]
            kd_t = (kc[i] * jnp.exp(gl[i] - gc[i])).T
            al_scr[ci, c64:2 * c64, :] = jnp.concatenate(
                [kd_t[h * c64:(h + 1) * c64, :] for h in range(DN_HEADS)], axis=1)
            egl_scr[ci] = jnp.broadcast_to(jnp.exp(gl[i]), (8, w))
        return carry

    lax.fori_loop(0, nb // gb, prepare, 0)

    bs = range(nb)

    def recur(c, carry):
        rows = [pl.ds(pl.multiple_of(b * tc + c * c64, c64), c64) for b in bs]
        s = [s_scr[b] for b in bs]
        ws_qs = [mmb(wq_scr[b * nck + c], blockdiag(s[b])) for b in bs]
        v_new = [u_scr[rows[b], :] - ws_qs[b][0:c64, :] for b in bs]
        od = [mmb(al_scr[b * nck + c], blockdiag(v_new[b])) for b in bs]
        for b in bs:
            o_scr[rows[b], :] = ws_qs[b][c64:2 * c64, :] + od[b][0:c64, :]
        for b in bs:
            s_scr[b] = s[b] * egl_scr[b * nck + c][0:1, :] + od[b][c64:2 * c64, :]
        return carry

    lax.fori_loop(0, nck, recur, 0)

    o = o_scr[...]
    ms = mmb(o * o, bd_bf) * (1.0 / DN_HEAD_DIM)
    o_ref[...] = (o * lax.rsqrt(ms + NORM_EPS) * ng_ref[...]).reshape(nb, tc, w)


def _dnet(layer, dn3, ng, tc):
    b, l, _ = dn3.shape
    w = BRANCH
    nchunk = b * tc // DN_CHUNK
    assert DN_GROUP % (tc // DN_CHUNK) == 0 and nchunk % DN_GROUP == 0
    act = lambda: pltpu.VMEM((b * tc, w), F32)
    return pl.pallas_call(
        _dnet_kernel,
        out_shape=jax.ShapeDtypeStruct((b, l, w), F32),
        grid=(l // tc,),
        in_specs=[pl.BlockSpec((b, tc, DN_COLS * w), lambda i: (0, i, 0)),
                  _layer_spec(layer, (1, w))],
        out_specs=pl.BlockSpec((b, tc, w), lambda i: (0, i, 0)),
        scratch_shapes=[pltpu.VMEM((b, DN_CHUNK, w), F32),
                        act(), act(),
                        pltpu.VMEM((nchunk, 2 * DN_CHUNK, w), F32),
                        pltpu.VMEM((nchunk, 2 * DN_CHUNK, w), F32),
                        pltpu.VMEM((nchunk, 8, w), F32)],
        compiler_params=pltpu.CompilerParams(
            dimension_semantics=("arbitrary",), vmem_limit_bytes=VMEM_LIMIT_BYTES),
        name="dnet",
    )(dn3, ng)


def _mixout_kernel(act_ref, ahalo_ref, chalo_ref, ulo_ref, uhi_ref, ylo_ref, yhi_ref, od_ref, x_ref,
                   acw_ref, avec_ref, apw_ref, bvec_ref, bglu_ref, ccw_ref, wout_ref, fg_ref,
                   out_ref, ash_scr, cpad_scr, *, final, sub):
    tc = x_ref.shape[0]
    w = BRANCH
    first = pl.program_id(1) == 0

    def act(j, rows=slice(None)):
        return act_ref[rows, j * w:(j + 1) * w]

    ash_scr[0, 0:32, :] = jnp.where(first, 0.0, ahalo_ref[...])
    ash_scr[0, 32:32 + tc, :] = act(ACT_A)
    apad = ash_scr[0, 0:tc + 32, :]
    for r in range(1, 8):
        ash_scr[r, 0:tc + 32, :] = pltpu.roll(apad, tc + 32 - r, 0)
    cpad_scr[0:8, :] = jnp.where(first, 0.0, chalo_ref[...])
    cpad_scr[8:8 + tc, :] = act(ACT_CX)

    conv_b, ln_g, ln_b, pw_b = (avec_ref[0:1, :], avec_ref[1:2, :], avec_ref[2:3, :],
                                avec_ref[3:4, :])
    d_skip, glu_b = bvec_ref[0:1, :], bvec_ref[1:2, :]

    for r in range(tc // sub):
        r0 = r * sub
        rows = slice(r0, r0 + sub)
        acc = conv_b
        for kk in range(CONF_KERNEL):
            off = 32 - (CONF_KERNEL - 1) + kk
            acc = acc + acw_ref[kk:kk + 1, :] * ash_scr[off % 8, pl.ds(r0 + off - off % 8, sub), :]
        mu = jnp.mean(acc, axis=-1, keepdims=True)
        xc = acc - mu
        ln = xc * lax.rsqrt(jnp.mean(xc * xc, axis=-1, keepdims=True) + NORM_EPS) * ln_g + ln_b
        ya = (_mm(_silu(ln), apw_ref[...]) + pw_b) * act(ACT_AZ, rows)
        ub = jnp.concatenate([ulo_ref[rows, :], uhi_ref[rows, :]], axis=1)
        ys = jnp.concatenate([ylo_ref[rows, :], yhi_ref[rows, :]], axis=1)
        yb = jax.nn.gelu(ys + d_skip * ub)
        yb = yb * _sigmoid(_mm(yb, bglu_ref[...]) + glu_b) * act(ACT_BZ, rows)
        cp = cpad_scr[r0:r0 + sub + 8, :]
        cacc = ccw_ref[SC_KERNEL - 1:SC_KERNEL, :] * cp[8:, :]
        for kk in range(SC_KERNEL - 1):
            cacc = cacc + ccw_ref[kk:kk + 1, :] * pltpu.roll(cp, SC_KERNEL - 1 - kk, 0)[8:, :]
        yc = act(ACT_CB, rows) * cacc
        yd = od_ref[rows, :] * act(ACT_DZ, rows)
        mixed = jnp.concatenate([ya, yb, yc, yd], axis=1)
        y = x_ref[rows, :] + _mm(mixed, wout_ref[...])
        if final:
            y = y * lax.rsqrt(jnp.mean(y * y, axis=-1, keepdims=True) + NORM_EPS) * fg_ref[...]
        out_ref[rows, :] = y


def _mixout(layer, act3, ulo, uhi, ylo, yhi, od, x3, acw, avec, apw, bvec, bglu, ccw, wout, fg, tc,
            final):
    b, l, d = x3.shape
    w = BRANCH
    blk = lambda width, col: pl.BlockSpec((None, tc, width), lambda bi, i: (bi, i, col))
    halo = lambda nrow, col: pl.BlockSpec(
        (None, nrow, w), lambda bi, i: (bi, jnp.maximum(i * (tc // nrow) - 1, 0), col))
    const = functools.partial(_layer_spec, layer)
    return pl.pallas_call(
        functools.partial(_mixout_kernel, final=final, sub=256),
        out_shape=jax.ShapeDtypeStruct((b, l, d), F32),
        grid=(b, l // tc),
        in_specs=[blk(ACT_COLS * w, 0), halo(32, ACT_A), halo(8, ACT_CX),
                  blk(HALF, 0), blk(HALF, 0), blk(HALF, 0), blk(HALF, 0), blk(w, 0), blk(d, 0),
                  const((32, w)), const((8, w)), const((w, w)), const((8, w)), const((w, w)),
                  const((8, w)), const((4 * w, d)), pl.BlockSpec((1, d), lambda bi, i: (0, 0))],
        out_specs=pl.BlockSpec((None, tc, d), lambda bi, i: (bi, i, 0)),
        scratch_shapes=[pltpu.VMEM((8, tc + 32 + PITCH_PAD, w), F32), pltpu.VMEM((tc + 8, w), F32)],
        compiler_params=pltpu.CompilerParams(
            dimension_semantics=("parallel", "parallel"), vmem_limit_bytes=VMEM_LIMIT_BYTES),
        name="mixout_final" if final else "mixout",
    )(act3, act3, act3, ulo, uhi, ylo, yhi, od, x3, acw, avec, apw, bvec, bglu, ccw, wout, fg)


def _pad_rows(a, n):
    return jnp.pad(a, ((0, 0), (0, n - a.shape[1]), (0, 0)))


def _s5_tiles(tab, state_cols):
    dep = tab.shape[0]
    nt, rb = S5_TILES, 2 * S5_GROUP
    t = tab.reshape(dep, S5_Q, nt, rb, 128)
    t = jnp.transpose(t, (0, 2, 1, 3, 4)).reshape(dep, nt, S5_Q * rb, 128)
    per = S5_STATE if state_cols else S5_GROUP
    src = jnp.arange(128)
    dst = jnp.arange(2 * 128)
    spread = ((src[:, None] // per == dst[None, :] // (2 * per))
              & (src[:, None] % per == dst[None, :] % per)).astype(F32)
    row_group = (jnp.arange(S5_Q * rb) % rb) // S5_GROUP
    col_group = (dst % (2 * per)) // per
    mask = (row_group[:, None] == col_group[None, :]).astype(F32)
    return (jnp.einsum("dkrc,cn->dkrn", t, spread, precision=HIGHEST) * mask).astype(BF16)


def _s5_tables(lam_re, lam_im, b_re, b_im, c_re, c_im, log_dt):
    q, g, p, h = S5_Q, S5_GROUPS, S5_STATE, S5_GROUP
    dep = lam_re.shape[0]
    lr = jnp.minimum(lam_re, -1e-4)
    li = lam_im
    dt = jnp.exp(log_dt)[..., None]
    lrdt, lidt = (lr * dt)[:, None], (li * dt)[:, None]

    def power(steps):
        s = steps[None, :, None, None]
        mag = jnp.exp(lrdt * s)
        return (mag * jnp.cos(lidt * s))[:, :, :, None, :], (mag * jnp.sin(lidt * s))[:, :, :, None, :]

    tok = jnp.arange(q, dtype=F32)
    p1r, p1i = power(jnp.ones((1,), F32))
    nr, ni = p1r[:, 0, :, 0] - 1.0, p1i[:, 0, :, 0]
    den = lr * lr + li * li
    fr, fi = (nr * lr + ni * li) / den, (ni * lr - nr * li) / den
    bbr = jnp.transpose(fr[..., None] * b_re - fi[..., None] * b_im, (0, 1, 3, 2))[:, None]
    bbi = jnp.transpose(fr[..., None] * b_im + fi[..., None] * b_re, (0, 1, 3, 2))[:, None]
    cr, ci = c_re[:, None], c_im[:, None]

    rr, ri = power(q - 1.0 - tok)
    wc = jnp.concatenate([rr * bbr - ri * bbi, rr * bbi + ri * bbr], axis=-1)
    qr, qi = power(tok + 1.0)
    mc = jnp.concatenate([cr * qr - ci * qi, -(cr * qi + ci * qr)], axis=-1)
    er, ei = power(tok)
    cpr = jnp.transpose(cr * er - ci * ei, (0, 2, 1, 3, 4)).reshape(dep, g, q * h, p)
    cpi = jnp.transpose(cr * ei + ci * er, (0, 2, 1, 3, 4)).reshape(dep, g, q * h, p)
    kd = (jnp.einsum("dgcp,dgyp->dgcy", bbr[:, 0], cpr, precision=HIGHEST)
          - jnp.einsum("dgcp,dgyp->dgcy", bbi[:, 0], cpi, precision=HIGHEST))
    lane = jnp.arange(q * h)
    tk = jnp.stack([jnp.where(lane >= j * h, jnp.roll(kd, j * h, axis=-1), 0.0) for j in range(q)],
                   axis=1)

    flat = lambda t: t.reshape(dep, q * g * h, t.shape[-1])
    ar, ai = power(jnp.full((1,), float(q), F32))
    a = jnp.concatenate([ar.reshape(dep, 1, g * p), ai.reshape(dep, 1, g * p),
                         jnp.zeros((dep, 6, g * p), F32)], axis=1)
    return _s5_tiles(flat(wc), True), _s5_tiles(flat(tk), False), _s5_tiles(flat(mc), True), a


def _w_in_tail(w_in):
    n = MAIN_BLOCKS * BRANCH
    pad = jnp.zeros(w_in.shape[:-1] + (128 - 2 * DN_HEADS,), w_in.dtype)
    return jnp.concatenate([w_in[..., n + 2 * DN_HEADS:], w_in[..., n:n + 2 * DN_HEADS], pad],
                           axis=-1).astype(BF16)


def kernel(x, norm_g, w_in, a_conv_w, a_conv_b, a_ln_g, a_ln_b, a_pw_w, a_pw_b, s5_lambda_re, s5_lambda_im, s5_b_re, s5_b_im, s5_c_re, s5_c_im, s5_d, s5_log_dt, s5_glu_w, s5_glu_b, c_conv_w, d_conv_w, d_a_log, d_dt_bias, d_norm_g, w_out, final_g):
    bsz, seq, d = x.shape
    depth = w_in.shape[0]
    w = BRANCH
    tm = min(512, seq)
    tc = min(1024, seq)
    dn_tc = min(256, seq)
    s5_rows = min(512, seq // S5_Q)

    w_bf = w_in.astype(BF16)
    w_tail = _w_in_tail(w_in)
    s5_wd, s5_td, s5_vd, s5_a = _s5_tables(s5_lambda_re, s5_lambda_im, s5_b_re, s5_b_im,
                                           s5_c_re, s5_c_im, s5_log_dt)
    acw = _pad_rows(a_conv_w, 32)
    zeros = jnp.zeros_like(a_conv_b)
    avec = jnp.stack([a_conv_b, a_ln_g, a_ln_b, a_pw_b, zeros, zeros, zeros, zeros], axis=1)
    bvec = jnp.stack([s5_d, s5_glu_b, zeros, zeros, zeros, zeros, zeros, zeros], axis=1)
    ccw = _pad_rows(c_conv_w, 8)
    dcw = _pad_rows(d_conv_w, 8)
    alog = jnp.repeat(d_a_log, DN_HEAD_DIM, axis=-1)[:, None, :]
    dtb = jnp.repeat(d_dt_bias, DN_HEAD_DIM, axis=-1)[:, None, :]
    ng = jnp.tile(d_norm_g, (1, DN_HEADS))[:, None, :]
    apw = a_pw_w.astype(BF16)
    bglu = s5_glu_w.astype(BF16)
    wout = w_out.astype(BF16)
    fg = final_g[None, :]

    def seq3(a):
        return a.reshape(bsz, seq, a.shape[-1])

    ng3 = norm_g[:, None, :]
    for l in range(depth):
        act, dn, ulo, uhi = _inproj(l, x.reshape(bsz * seq, d), ng3, w_bf, w_tail, dcw, alog, dtb,
                                    tm, seq)
        ulo, uhi = seq3(ulo), seq3(uhi)
        ylo, yhi = _s5(l, ulo, uhi, s5_wd, s5_td, s5_vd, s5_a, s5_rows)
        od = _dnet(l, seq3(dn), ng, dn_tc)
        x = _mixout(l, seq3(act), ulo, uhi, ylo, yhi, od, x, acw, avec, apw, bvec, bglu, ccw, wout,
                    fg, tc, l == depth - 1)
    return x
```

```python
import functools

import jax
import jax.numpy as jnp
from jax import lax
from jax.experimental import pallas as pl
from jax.experimental.pallas import tpu as pltpu

F32 = jnp.float32
BF16 = jnp.bfloat16
HIGHEST = lax.Precision.HIGHEST

NORM_EPS = 1e-6
BRANCH = 256
CONF_KERNEL = 31
SC_KERNEL = 3
DN_HEADS = 4
DN_HEAD_DIM = 64
DN_CONV = 4
DN_CHUNK = 64
DN_GROUP = 16
CONV_GROUP = 4
BF16_ROWS = 16
S5_GROUPS = 16
S5_GROUP = 16
S5_STATE = 64
S5_Q = 8
S5_ROW = S5_Q * BRANCH
S5_NSTATE = S5_GROUPS * S5_STATE

COL_A, COL_B, COL_C, COL_QKV, MAIN_BLOCKS = 0, 3, 5, 9, 12
IN_COLS = MAIN_BLOCKS * BRANCH + 2 * DN_HEADS + BRANCH
TAIL_COLS = BRANCH + 128
HALF = BRANCH // 2
ACT_A, ACT_AZ, ACT_BZ, ACT_CB, ACT_CX, ACT_DZ, ACT_COLS = 0, 1, 2, 3, 4, 5, 6
DN_Q, DN_K, DN_V, DN_BETA, DN_G, DN_COLS = 0, 1, 2, 3, 4, 5

VMEM_LIMIT_BYTES = 56 * 1024 * 1024


def _mm(a, b):
    return jnp.dot(a.astype(BF16), b.astype(BF16), preferred_element_type=F32)


def _layer_spec(layer, shape):
    zeros = (0,) * len(shape)
    return pl.BlockSpec((None,) + tuple(shape), lambda *_: (layer,) + zeros)


def _sigmoid(x):
    return 1.0 / (1.0 + jnp.exp(-x))


def _silu(x):
    return x * _sigmoid(x)


def _inproj_kernel(x_ref, g_ref, w_ref, wt_ref, dcw_ref, alog_ref, dtb_ref, act_ref, dn_ref, ulo_ref,
                   uhi_ref, pad_scr, *, steps_per_seq, sub):
    tm = x_ref.shape[0]
    w = BRANCH
    first = pl.program_id(0) % steps_per_seq == 0

    @pl.when(first)
    def _():
        pad_scr[tm:tm + 8, :] = jnp.zeros((8, 3 * w), F32)

    pad_scr[0:8, :] = pad_scr[tm:tm + 8, :]

    row_h = lax.broadcasted_iota(jnp.int32, (w, w), 0) // DN_HEAD_DIM
    col_h = lax.broadcasted_iota(jnp.int32, (w, w), 1) // DN_HEAD_DIM
    ones_bd = jnp.where(row_h == col_h, 1.0, 0.0).astype(BF16)

    def project(hb, col, n):
        return jnp.dot(hb, w_ref[:, col * w:(col + n) * w], preferred_element_type=F32)

    def put(ref, rows, j, val):
        ref[rows, j * w:(j + 1) * w] = val

    def deltanet_inputs(rows, r0):
        xp = pad_scr[r0:r0 + sub + 8, :]
        acc = dcw_ref[DN_CONV - 1:DN_CONV, :] * xp[8:, :]
        for kk in range(DN_CONV - 1):
            acc = acc + dcw_ref[kk:kk + 1, :] * pltpu.roll(xp, DN_CONV - 1 - kk, 0)[8:, :]
        qkv = _silu(acc)
        q = qkv[:, 0:w]
        k = qkv[:, w:2 * w]
        ssq = jnp.dot((q * q).astype(BF16), ones_bd, preferred_element_type=F32)
        ssk = jnp.dot((k * k).astype(BF16), ones_bd, preferred_element_type=F32)
        put(dn_ref, rows, DN_Q, q * lax.rsqrt(ssq + NORM_EPS) * (DN_HEAD_DIM ** -0.5))
        put(dn_ref, rows, DN_K, k * lax.rsqrt(ssk + NORM_EPS))
        put(dn_ref, rows, DN_V, qkv[:, 2 * w:3 * w] * dn_ref[rows, DN_BETA * w:(DN_BETA + 1) * w])

    for r in range(tm // sub):
        r0 = r * sub
        rows = slice(r0, r0 + sub)
        x = x_ref[rows, :]
        h = x * lax.rsqrt(jnp.mean(x * x, axis=-1, keepdims=True) + NORM_EPS) * g_ref[...]
        hb = h.astype(BF16)
        pad_scr[r0 + 8:r0 + 8 + sub, :] = project(hb, COL_QKV, 3)
        p = project(hb, COL_A, 3)
        put(act_ref, rows, ACT_A, p[:, 0:w] * _sigmoid(p[:, w:2 * w]))
        put(act_ref, rows, ACT_AZ, _silu(p[:, 2 * w:3 * w]))
        p = project(hb, COL_B, 2)
        ulo_ref[rows, :] = p[:, 0:HALF]
        uhi_ref[rows, :] = p[:, HALF:w]
        put(act_ref, rows, ACT_BZ, _silu(p[:, w:2 * w]))
        p = project(hb, COL_C, 4)
        put(act_ref, rows, ACT_CB, p[:, 0:w] * _silu(p[:, 3 * w:4 * w]))
        put(act_ref, rows, ACT_CX, p[:, w:2 * w] * p[:, 2 * w:3 * w])
        p = jnp.dot(hb, wt_ref[...], preferred_element_type=F32)
        put(act_ref, rows, ACT_DZ, _silu(p[:, 0:w]))
        p = jnp.concatenate(
            [jnp.broadcast_to(p[:, w + j:w + j + 1], (sub, DN_HEAD_DIM)) for j in range(2 * DN_HEADS)],
            axis=1)
        sp_in = p[:, 0:w] + dtb_ref[...]
        softplus = jnp.maximum(sp_in, 0.0) + jnp.log1p(jnp.exp(-jnp.abs(sp_in)))
        put(dn_ref, rows, DN_G, -jnp.exp(alog_ref[...]) * softplus)
        put(dn_ref, rows, DN_BETA, _sigmoid(p[:, w:2 * w]))
        if r > 0:
            deltanet_inputs(slice(r0 - sub, r0), r0 - sub)
    deltanet_inputs(slice(tm - sub, tm), tm - sub)


def _inproj(layer, x2d, g, w, wt, dcw, alog, dtb, tm, seq):
    t, d = x2d.shape
    const = functools.partial(_layer_spec, layer)
    out = lambda n: pl.BlockSpec((tm, n * BRANCH), lambda i: (i, 0))
    return pl.pallas_call(
        functools.partial(_inproj_kernel, steps_per_seq=seq // tm, sub=min(256, tm)),
        out_shape=(jax.ShapeDtypeStruct((t, ACT_COLS * BRANCH), F32),
                   jax.ShapeDtypeStruct((t, DN_COLS * BRANCH), F32),
                   jax.ShapeDtypeStruct((t, HALF), F32),
                   jax.ShapeDtypeStruct((t, HALF), F32)),
        grid=(t // tm,),
        in_specs=[pl.BlockSpec((tm, d), lambda i: (i, 0)),
                  const((1, d)), const((d, IN_COLS)), const((d, TAIL_COLS)),
                  const((8, 3 * BRANCH)), const((1, BRANCH)), const((1, BRANCH))],
        out_specs=(out(ACT_COLS), out(DN_COLS), pl.BlockSpec((tm, HALF), lambda i: (i, 0)),
                   pl.BlockSpec((tm, HALF), lambda i: (i, 0))),
        scratch_shapes=[pltpu.VMEM((tm + 8, 3 * BRANCH), F32)],
        compiler_params=pltpu.CompilerParams(
            dimension_semantics=("arbitrary",), vmem_limit_bytes=VMEM_LIMIT_BYTES),
        name="inproj",
    )(x2d, g, w, wt, dcw, alog, dtb)


S5_TILES = S5_GROUPS // 2


def _block_transpose(pieces, block):
    lane = lax.broadcasted_iota(jnp.int32, pieces[0].shape, 1)
    n = len(pieces)
    s = n // 2
    while s:
        width = s * block
        lower = lane % (2 * width) < width
        out = list(pieces)
        for j in range(n):
            if j & s == 0:
                a, b = pieces[j], pieces[j + s]
                out[j] = jnp.where(lower, a, pltpu.roll(b, width, 1))
                out[j + s] = jnp.where(lower, pltpu.roll(a, 128 - width, 1), b)
        pieces = out
        s //= 2
    return pieces


def _s5_kernel(ulo_ref, uhi_ref, wd_ref, td_ref, vdt_ref, a_ref, ylo_ref, yhi_ref,
               carry_scr, z_scr, xp_scr):
    rows = ulo_ref.shape[0] // S5_Q
    n = S5_NSTATE
    tiles_per_half = S5_TILES // 2
    lanes = 2 * S5_GROUP

    @pl.when(pl.program_id(1) == 0)
    def _():
        carry_scr[...] = jnp.zeros_like(carry_scr)

    ub = []
    for half in (ulo_ref, uhi_ref):
        tok = [half[pl.ds(j, rows, stride=S5_Q), :] for j in range(S5_Q)]
        first = _block_transpose(tok[0:4], lanes)
        second = _block_transpose(tok[4:8], lanes)
        ub += [jnp.concatenate([first[k], second[k]], axis=1).astype(BF16)
               for k in range(tiles_per_half)]
    for k in range(S5_TILES):
        z = jnp.dot(ub[k], wd_ref[k], preferred_element_type=F32)
        z_scr[:, k * 128:(k + 1) * 128] = z[:, 0:128]
        z_scr[:, n + k * 128:n + (k + 1) * 128] = z[:, 128:256]

    ar = a_ref[0:1, :]
    ai = a_ref[1:2, :]

    def body(c, carry):
        xr, xi = carry
        xp_scr[pl.ds(c, 1), 0:n] = xr
        xp_scr[pl.ds(c, 1), n:2 * n] = xi
        zr = z_scr[pl.ds(c, 1), 0:n]
        zi = z_scr[pl.ds(c, 1), n:2 * n]
        return ar * xr - ai * xi + zr, ar * xi + ai * xr + zi

    xr, xi = lax.fori_loop(0, rows, body, (carry_scr[0:1, 0:n], carry_scr[0:1, n:2 * n]),
                           unroll=4)
    carry_scr[0:1, 0:n] = xr
    carry_scr[0:1, n:2 * n] = xi

    ys = []
    for k in range(S5_TILES):
        xp = jnp.concatenate([xp_scr[:, k * 128:(k + 1) * 128],
                              xp_scr[:, n + k * 128:n + (k + 1) * 128]], axis=1).astype(BF16)
        ys.append(jnp.dot(ub[k], td_ref[k], preferred_element_type=F32)
                  + lax.dot_general(xp, vdt_ref[k], (((1,), (1,)), ((), ())),
                                    preferred_element_type=F32))
    for h, out_ref in enumerate((ylo_ref, yhi_ref)):
        tiles = ys[h * tiles_per_half:(h + 1) * tiles_per_half]
        for m in range(2):
            tok = _block_transpose([y[:, m * 128:(m + 1) * 128] for y in tiles], lanes)
            for b in range(4):
                out_ref[pl.ds(4 * m + b, rows, stride=S5_Q), :] = tok[b]


def _s5(layer, ulo, uhi, wd, td, vd, a, rows):
    b, l, _ = ulo.shape
    const = functools.partial(_layer_spec, layer)
    half = lambda: pl.BlockSpec((None, rows * S5_Q, HALF), lambda bi, j: (bi, j, 0))
    tiles = (S5_TILES, BRANCH, BRANCH)
    return pl.pallas_call(
        _s5_kernel,
        out_shape=(jax.ShapeDtypeStruct((b, l, HALF), F32), jax.ShapeDtypeStruct((b, l, HALF), F32)),
        grid=(b, l // (rows * S5_Q)),
        in_specs=[half(), half(), const(tiles), const(tiles), const(tiles), const((8, S5_NSTATE))],
        out_specs=(half(), half()),
        scratch_shapes=[pltpu.VMEM((8, 2 * S5_NSTATE), F32),
                        pltpu.VMEM((rows, 2 * S5_NSTATE), F32),
                        pltpu.VMEM((rows, 2 * S5_NSTATE), F32)],
        compiler_params=pltpu.CompilerParams(
            dimension_semantics=("parallel", "arbitrary"), vmem_limit_bytes=VMEM_LIMIT_BYTES),
        name="s5",
    )(ulo, uhi, wd, td, vd, a)


def _tile_heads(x):
    return jnp.concatenate([x] * DN_HEADS, axis=0)


def _dnet_kernel(dn_ref, ng_ref, o_ref, s_scr, o_scr, u_scr, wq_scr, al_scr, egl_scr):
    nb, tc, _ = dn_ref.shape
    c64 = DN_CHUNK
    w = DN_HEADS * DN_HEAD_DIM
    nck = tc // c64
    gb = DN_GROUP // nck

    @pl.when(pl.program_id(0) == 0)
    def _():
        s_scr[...] = jnp.zeros_like(s_scr)

    row_h = lax.broadcasted_iota(jnp.int32, (w, w), 0) // DN_HEAD_DIM
    col_h = lax.broadcasted_iota(jnp.int32, (w, w), 1) // DN_HEAD_DIM
    bd = row_h == col_h
    bd_bf = jnp.where(bd, 1.0, 0.0).astype(BF16)

    def mmb(a, b_bf):
        return jnp.dot(a.astype(BF16), b_bf, preferred_element_type=F32)

    def mmb_nt(a, b_bf):
        return lax.dot_general(a.astype(BF16), b_bf, (((1,), (1,)), ((), ())),
                               preferred_element_type=F32)

    ri = lax.broadcasted_iota(jnp.int32, (c64, w), 0)
    cj = lax.broadcasted_iota(jnp.int32, (c64, w), 1) % DN_HEAD_DIM
    causal = ri >= cj
    strict = ri > cj
    diag = ri == cj
    eye_all = jnp.where(diag, 1.0, 0.0).astype(F32)
    lt_bf = jnp.where(lax.broadcasted_iota(jnp.int32, (c64, c64), 0)
                      >= lax.broadcasted_iota(jnp.int32, (c64, c64), 1), 1.0, 0.0).astype(BF16)

    def blockdiag(x):
        return _tile_heads(x.astype(BF16)) * bd_bf

    grp = range(DN_GROUP)

    def prepare(gi, carry):
        def col(i, j):
            return dn_ref[gi * gb + i // nck, (i % nck) * c64:(i % nck + 1) * c64, j * w:(j + 1) * w]

        rows = [pl.ds(pl.multiple_of((gi * gb + i // nck) * tc + (i % nck) * c64, c64), c64)
                for i in grp]
        qc = [col(i, DN_Q) for i in grp]
        kc = [col(i, DN_K) for i in grp]
        bc = [col(i, DN_BETA) for i in grp]
        g = [col(i, DN_G) for i in grp]
        g_hi = [x.astype(BF16) for x in g]
        g_lo = [(g[i] - g_hi[i].astype(F32)).astype(BF16) for i in grp]
        gc = [jnp.dot(lt_bf, g_hi[i], preferred_element_type=F32)
              + jnp.dot(lt_bf, g_lo[i], preferred_element_type=F32) for i in grp]
        egc = [jnp.exp(x) for x in gc]
        gl = [x[c64 - 1:c64, :] for x in gc]
        kb = [kc[i] * bc[i] for i in grp]
        grow = [jnp.sum(jnp.where(diag, x, 0.0), axis=0, keepdims=True) for x in gc]
        decay = [jnp.exp(jnp.where(causal, gc[i] - grow[i], -jnp.inf)) for i in grp]
        k_bd = [blockdiag(x) for x in kc]
        kq = [mmb_nt(jnp.concatenate([kb[i], qc[i]], axis=0), k_bd[i]) for i in grp]
        for i in grp:
            al_scr[gi * DN_GROUP + i, 0:c64, :] = kq[i][c64:2 * c64, :] * decay[i]
        x = [jnp.where(strict, -kq[i][0:c64, :] * decay[i], 0.0) for i in grp]
        p = [eye_all + x[i] for i in grp]
        x = [mmb(x[i], blockdiag(x[i])) for i in grp]
        for _ in range(4):
            xp = [mmb(jnp.concatenate([x[i], p[i]], axis=0), blockdiag(x[i])) for i in grp]
            x = [xp[i][0:c64, :] for i in grp]
            p = [p[i] + xp[i][c64:2 * c64, :] for i in grp]
        p = [p[i] + mmb(p[i], blockdiag(x[i])) for i in grp]
        for i in grp:
            ci = gi * DN_GROUP + i
            u_scr[rows[i], :] = mmb(p[i], blockdiag(col(i, DN_V)))
            wq_scr[ci, 0:c64, :] = mmb(p[i], blockdiag(kb[i] * egc[i]))
            wq_scr[ci, c64:2 * c64, :] = qc[i] * egc[i]
            kd_t = (kc[i] * jnp.exp(gl[i] - gc[i])).T
            al_scr[ci, c64:2 * c64, :] = jnp.concatenate(
                [kd_t[h * c64:(h + 1) * c64, :] for h in range(DN_HEADS)], axis=1)
            egl_scr[ci] = jnp.broadcast_to(jnp.exp(gl[i]), (8, w))
        return carry

    lax.fori_loop(0, nb // gb, prepare, 0)

    bs = range(nb)

    def recur(c, carry):
        rows = [pl.ds(pl.multiple_of(b * tc + c * c64, c64), c64) for b in bs]
        s = [s_scr[b] for b in bs]
        ws_qs = [mmb(wq_scr[b * nck + c], blockdiag(s[b])) for b in bs]
        v_new = [u_scr[rows[b], :] - ws_qs[b][0:c64, :] for b in bs]
        od = [mmb(al_scr[b * nck + c], blockdiag(v_new[b])) for b in bs]
        for b in bs:
            o_scr[rows[b], :] = ws_qs[b][c64:2 * c64, :] + od[b][0:c64, :]
        for b in bs:
            s_scr[b] = s[b] * egl_scr[b * nck + c][0:1, :] + od[b][c64:2 * c64, :]
        return carry

    lax.fori_loop(0, nck, recur, 0)

    o = o_scr[...]
    ms = mmb(o * o, bd_bf) * (1.0 / DN_HEAD_DIM)
    o_ref[...] = (o * lax.rsqrt(ms + NORM_EPS) * ng_ref[...]).reshape(nb, tc, w)


def _dnet(layer, dn3, ng, tc):
    b, l, _ = dn3.shape
    w = BRANCH
    nchunk = b * tc // DN_CHUNK
    assert DN_GROUP % (tc // DN_CHUNK) == 0 and nchunk % DN_GROUP == 0
    act = lambda: pltpu.VMEM((b * tc, w), F32)
    return pl.pallas_call(
        _dnet_kernel,
        out_shape=jax.ShapeDtypeStruct((b, l, w), F32),
        grid=(l // tc,),
        in_specs=[pl.BlockSpec((b, tc, DN_COLS * w), lambda i: (0, i, 0)),
                  _layer_spec(layer, (1, w))],
        out_specs=pl.BlockSpec((b, tc, w), lambda i: (0, i, 0)),
        scratch_shapes=[pltpu.VMEM((b, DN_CHUNK, w), F32),
                        act(), act(),
                        pltpu.VMEM((nchunk, 2 * DN_CHUNK, w), F32),
                        pltpu.VMEM((nchunk, 2 * DN_CHUNK, w), F32),
                        pltpu.VMEM((nchunk, 8, w), F32)],
        compiler_params=pltpu.CompilerParams(
            dimension_semantics=("arbitrary",), vmem_limit_bytes=VMEM_LIMIT_BYTES),
        name="dnet",
    )(dn3, ng)


def _mixout_kernel(act_ref, ahalo_ref, chalo_ref, ulo_ref, uhi_ref, ylo_ref, yhi_ref, od_ref, x_ref,
                   acw_ref, avec_ref, apw_ref, bvec_ref, bglu_ref, ccw_ref, wout_ref, fg_ref,
                   out_ref, ash_scr, cpad_scr, *, final, sub):
    tc = x_ref.shape[0]
    w = BRANCH
    first = pl.program_id(1) == 0

    def act(j, rows=slice(None)):
        return act_ref[rows, j * w:(j + 1) * w]

    apad = jnp.concatenate([jnp.where(first, 0.0, ahalo_ref[...]), act(ACT_A)], axis=0)
    for r in range(8):
        shifted = apad if r == 0 else pltpu.roll(apad, tc + 32 - r, 0)
        ash_scr[r] = shifted.astype(BF16)
        ash_scr[r + 8, 0:tc + 16, :] = shifted[8:tc + 24, :].astype(BF16)
    cpad_scr[0:8, :] = jnp.where(first, 0.0, chalo_ref[...])
    cpad_scr[8:8 + tc, :] = act(ACT_CX)

    acw_bf = acw_ref[...].astype(BF16)
    conv_b, ln_g, ln_b, pw_b = (avec_ref[0:1, :], avec_ref[1:2, :], avec_ref[2:3, :],
                                avec_ref[3:4, :])
    d_skip, glu_b = bvec_ref[0:1, :], bvec_ref[1:2, :]

    for r in range(tc // sub):
        r0 = r * sub
        rows = slice(r0, r0 + sub)
        acc = conv_b
        for k0 in range(0, CONF_KERNEL, CONV_GROUP):
            part = None
            for kk in range(k0, min(k0 + CONV_GROUP, CONF_KERNEL)):
                off = 32 - (CONF_KERNEL - 1) + kk
                shift = off % BF16_ROWS
                term = acw_bf[kk:kk + 1, :] * ash_scr[shift, pl.ds(r0 + off - shift, sub), :]
                part = term if part is None else part + term
            acc = acc + part.astype(F32)
        mu = jnp.mean(acc, axis=-1, keepdims=True)
        xc = acc - mu
        ln = xc * lax.rsqrt(jnp.mean(xc * xc, axis=-1, keepdims=True) + NORM_EPS) * ln_g + ln_b
        ya = (_mm(_silu(ln), apw_ref[...]) + pw_b) * act(ACT_AZ, rows)
        ub = jnp.concatenate([ulo_ref[rows, :], uhi_ref[rows, :]], axis=1)
        ys = jnp.concatenate([ylo_ref[rows, :], yhi_ref[rows, :]], axis=1)
        yb = jax.nn.gelu(ys + d_skip * ub)
        yb = yb * _sigmoid(_mm(yb, bglu_ref[...]) + glu_b) * act(ACT_BZ, rows)
        cp = cpad_scr[r0:r0 + sub + 8, :]
        cacc = ccw_ref[SC_KERNEL - 1:SC_KERNEL, :] * cp[8:, :]
        for kk in range(SC_KERNEL - 1):
            cacc = cacc + ccw_ref[kk:kk + 1, :] * pltpu.roll(cp, SC_KERNEL - 1 - kk, 0)[8:, :]
        yc = act(ACT_CB, rows) * cacc
        yd = od_ref[rows, :] * act(ACT_DZ, rows)
        mixed = jnp.concatenate([ya, yb, yc, yd], axis=1)
        y = x_ref[rows, :] + _mm(mixed, wout_ref[...])
        if final:
            y = y * lax.rsqrt(jnp.mean(y * y, axis=-1, keepdims=True) + NORM_EPS) * fg_ref[...]
        out_ref[rows, :] = y


def _mixout(layer, act3, ulo, uhi, ylo, yhi, od, x3, acw, avec, apw, bvec, bglu, ccw, wout, fg, tc,
            final):
    b, l, d = x3.shape
    w = BRANCH
    blk = lambda width, col: pl.BlockSpec((None, tc, width), lambda bi, i: (bi, i, col))
    halo = lambda nrow, col: pl.BlockSpec(
        (None, nrow, w), lambda bi, i: (bi, jnp.maximum(i * (tc // nrow) - 1, 0), col))
    const = functools.partial(_layer_spec, layer)
    return pl.pallas_call(
        functools.partial(_mixout_kernel, final=final, sub=256),
        out_shape=jax.ShapeDtypeStruct((b, l, d), F32),
        grid=(b, l // tc),
        in_specs=[blk(ACT_COLS * w, 0), halo(32, ACT_A), halo(8, ACT_CX),
                  blk(HALF, 0), blk(HALF, 0), blk(HALF, 0), blk(HALF, 0), blk(w, 0), blk(d, 0),
                  const((32, w)), const((8, w)), const((w, w)), const((8, w)), const((w, w)),
                  const((8, w)), const((4 * w, d)), pl.BlockSpec((1, d), lambda bi, i: (0, 0))],
        out_specs=pl.BlockSpec((None, tc, d), lambda bi, i: (bi, i, 0)),
        scratch_shapes=[pltpu.VMEM((BF16_ROWS, tc + 32, w), BF16), pltpu.VMEM((tc + 8, w), F32)],
        compiler_params=pltpu.CompilerParams(
            dimension_semantics=("parallel", "parallel"), vmem_limit_bytes=VMEM_LIMIT_BYTES),
        name="mixout_final" if final else "mixout",
    )(act3, act3, act3, ulo, uhi, ylo, yhi, od, x3, acw, avec, apw, bvec, bglu, ccw, wout, fg)


def _pad_rows(a, n):
    return jnp.pad(a, ((0, 0), (0, n - a.shape[1]), (0, 0)))


def _s5_tiles(tab, state_cols):
    dep = tab.shape[0]
    nt, rb = S5_TILES, 2 * S5_GROUP
    t = tab.reshape(dep, S5_Q, nt, rb, 128)
    t = jnp.transpose(t, (0, 2, 1, 3, 4)).reshape(dep, nt, S5_Q * rb, 128)
    per = S5_STATE if state_cols else S5_GROUP
    src = jnp.arange(128)
    dst = jnp.arange(2 * 128)
    spread = ((src[:, None] // per == dst[None, :] // (2 * per))
              & (src[:, None] % per == dst[None, :] % per)).astype(F32)
    row_group = (jnp.arange(S5_Q * rb) % rb) // S5_GROUP
    col_group = (dst % (2 * per)) // per
    mask = (row_group[:, None] == col_group[None, :]).astype(F32)
    return (jnp.einsum("dkrc,cn->dkrn", t, spread, precision=HIGHEST) * mask).astype(BF16)


def _s5_tables(lam_re, lam_im, b_re, b_im, c_re, c_im, log_dt):
    q, g, p, h = S5_Q, S5_GROUPS, S5_STATE, S5_GROUP
    dep = lam_re.shape[0]
    lr = jnp.minimum(lam_re, -1e-4)
    li = lam_im
    dt = jnp.exp(log_dt)[..., None]
    lrdt, lidt = (lr * dt)[:, None], (li * dt)[:, None]

    def power(steps):
        s = steps[None, :, None, None]
        mag = jnp.exp(lrdt * s)
        return (mag * jnp.cos(lidt * s))[:, :, :, None, :], (mag * jnp.sin(lidt * s))[:, :, :, None, :]

    tok = jnp.arange(q, dtype=F32)
    p1r, p1i = power(jnp.ones((1,), F32))
    nr, ni = p1r[:, 0, :, 0] - 1.0, p1i[:, 0, :, 0]
    den = lr * lr + li * li
    fr, fi = (nr * lr + ni * li) / den, (ni * lr - nr * li) / den
    bbr = jnp.transpose(fr[..., None] * b_re - fi[..., None] * b_im, (0, 1, 3, 2))[:, None]
    bbi = jnp.transpose(fr[..., None] * b_im + fi[..., None] * b_re, (0, 1, 3, 2))[:, None]
    cr, ci = c_re[:, None], c_im[:, None]

    rr, ri = power(q - 1.0 - tok)
    wc = jnp.concatenate([rr * bbr - ri * bbi, rr * bbi + ri * bbr], axis=-1)
    qr, qi = power(tok + 1.0)
    mc = jnp.concatenate([cr * qr - ci * qi, -(cr * qi + ci * qr)], axis=-1)
    er, ei = power(tok)
    cpr = jnp.transpose(cr * er - ci * ei, (0, 2, 1, 3, 4)).reshape(dep, g, q * h, p)
    cpi = jnp.transpose(cr * ei + ci * er, (0, 2, 1, 3, 4)).reshape(dep, g, q * h, p)
    kd = (jnp.einsum("dgcp,dgyp->dgcy", bbr[:, 0], cpr, precision=HIGHEST)
          - jnp.einsum("dgcp,dgyp->dgcy", bbi[:, 0], cpi, precision=HIGHEST))
    lane = jnp.arange(q * h)
    tk = jnp.stack([jnp.where(lane >= j * h, jnp.roll(kd, j * h, axis=-1), 0.0) for j in range(q)],
                   axis=1)

    flat = lambda t: t.reshape(dep, q * g * h, t.shape[-1])
    ar, ai = power(jnp.full((1,), float(q), F32))
    a = jnp.concatenate([ar.reshape(dep, 1, g * p), ai.reshape(dep, 1, g * p),
                         jnp.zeros((dep, 6, g * p), F32)], axis=1)
    return _s5_tiles(flat(wc), True), _s5_tiles(flat(tk), False), _s5_tiles(flat(mc), True), a


def _w_in_tail(w_in):
    n = MAIN_BLOCKS * BRANCH
    pad = jnp.zeros(w_in.shape[:-1] + (128 - 2 * DN_HEADS,), w_in.dtype)
    return jnp.concatenate([w_in[..., n + 2 * DN_HEADS:], w_in[..., n:n + 2 * DN_HEADS], pad],
                           axis=-1).astype(BF16)


def kernel(x, norm_g, w_in, a_conv_w, a_conv_b, a_ln_g, a_ln_b, a_pw_w, a_pw_b, s5_lambda_re, s5_lambda_im, s5_b_re, s5_b_im, s5_c_re, s5_c_im, s5_d, s5_log_dt, s5_glu_w, s5_glu_b, c_conv_w, d_conv_w, d_a_log, d_dt_bias, d_norm_g, w_out, final_g):
    bsz, seq, d = x.shape
    depth = w_in.shape[0]
    w = BRANCH
    tm = min(512, seq)
    tc = min(1024, seq)
    dn_tc = min(256, seq)
    s5_rows = min(512, seq // S5_Q)

    w_bf = w_in.astype(BF16)
    w_tail = _w_in_tail(w_in)
    s5_wd, s5_td, s5_vd, s5_a = _s5_tables(s5_lambda_re, s5_lambda_im, s5_b_re, s5_b_im,
                                           s5_c_re, s5_c_im, s5_log_dt)
    acw = _pad_rows(a_conv_w, 32)
    zeros = jnp.zeros_like(a_conv_b)
    avec = jnp.stack([a_conv_b, a_ln_g, a_ln_b, a_pw_b, zeros, zeros, zeros, zeros], axis=1)
    bvec = jnp.stack([s5_d, s5_glu_b, zeros, zeros, zeros, zeros, zeros, zeros], axis=1)
    ccw = _pad_rows(c_conv_w, 8)
    dcw = _pad_rows(d_conv_w, 8)
    alog = jnp.repeat(d_a_log, DN_HEAD_DIM, axis=-1)[:, None, :]
    dtb = jnp.repeat(d_dt_bias, DN_HEAD_DIM, axis=-1)[:, None, :]
    ng = jnp.tile(d_norm_g, (1, DN_HEADS))[:, None, :]
    apw = a_pw_w.astype(BF16)
    bglu = s5_glu_w.astype(BF16)
    wout = w_out.astype(BF16)
    fg = final_g[None, :]

    def seq3(a):
        return a.reshape(bsz, seq, a.shape[-1])

    ng3 = norm_g[:, None, :]
    for l in range(depth):
        act, dn, ulo, uhi = _inproj(l, x.reshape(bsz * seq, d), ng3, w_bf, w_tail, dcw, alog, dtb,
                                    tm, seq)
        ulo, uhi = seq3(ulo), seq3(uhi)
        ylo, yhi = _s5(l, ulo, uhi, s5_wd, s5_td, s5_vd, s5_a, s5_rows)
        od = _dnet(l, seq3(dn), ng, dn_tc)
        x = _mixout(l, seq3(act), ulo, uhi, ylo, yhi, od, x, acw, avec, apw, bvec, bglu, ccw, wout,
                    fg, tc, l == depth - 1)
    return x
```

```python
import functools

import jax
import jax.numpy as jnp
from jax import lax
from jax.experimental import pallas as pl
from jax.experimental.pallas import tpu as pltpu

F32 = jnp.float32
BF16 = jnp.bfloat16
HIGHEST = lax.Precision.HIGHEST

NORM_EPS = 1e-6
BRANCH = 256
CONF_KERNEL = 31
SC_KERNEL = 3
DN_HEADS = 4
DN_HEAD_DIM = 64
DN_CONV = 4
DN_CHUNK = 64
DN_GROUP = 16
CONV_GROUP = 8
BF16_ROWS = 16
S5_GROUPS = 16
S5_GROUP = 16
S5_STATE = 64
S5_Q = 8
S5_ROW = S5_Q * BRANCH
S5_NSTATE = S5_GROUPS * S5_STATE

COL_A, COL_B, COL_C, COL_QKV, MAIN_BLOCKS = 0, 3, 5, 9, 12
IN_COLS = MAIN_BLOCKS * BRANCH + 2 * DN_HEADS + BRANCH
TAIL_COLS = BRANCH + 128
HALF = BRANCH // 2
ACT_A, ACT_AZ, ACT_BZ, ACT_CB, ACT_CX, ACT_DZ, ACT_COLS = 0, 1, 2, 3, 4, 5, 6
DN_Q, DN_K, DN_V, DN_BETA, DN_G, DN_COLS = 0, 1, 2, 3, 4, 5

VMEM_LIMIT_BYTES = 56 * 1024 * 1024


def _mm(a, b):
    return jnp.dot(a.astype(BF16), b.astype(BF16), preferred_element_type=F32)


def _layer_spec(layer, shape):
    zeros = (0,) * len(shape)
    return pl.BlockSpec((None,) + tuple(shape), lambda *_: (layer,) + zeros)


def _sigmoid(x):
    return 1.0 / (1.0 + jnp.exp(-x))


def _silu(x):
    return x * _sigmoid(x)


def _inproj_kernel(x_ref, g_ref, w_ref, wt_ref, dcw_ref, alog_ref, dtb_ref, act_ref, dn_ref, ulo_ref,
                   uhi_ref, pad_scr, *, steps_per_seq, sub):
    tm = x_ref.shape[0]
    w = BRANCH
    first = pl.program_id(0) % steps_per_seq == 0

    @pl.when(first)
    def _():
        pad_scr[tm:tm + 8, :] = jnp.zeros((8, 3 * w), F32)

    pad_scr[0:8, :] = pad_scr[tm:tm + 8, :]

    row_h = lax.broadcasted_iota(jnp.int32, (w, w), 0) // DN_HEAD_DIM
    col_h = lax.broadcasted_iota(jnp.int32, (w, w), 1) // DN_HEAD_DIM
    ones_bd = jnp.where(row_h == col_h, 1.0, 0.0).astype(BF16)

    def project(hb, col, n):
        return jnp.dot(hb, w_ref[:, col * w:(col + n) * w], preferred_element_type=F32)

    def put(ref, rows, j, val):
        ref[rows, j * w:(j + 1) * w] = val

    def deltanet_inputs(rows, r0):
        xp = pad_scr[r0:r0 + sub + 8, :]
        acc = dcw_ref[DN_CONV - 1:DN_CONV, :] * xp[8:, :]
        for kk in range(DN_CONV - 1):
            acc = acc + dcw_ref[kk:kk + 1, :] * pltpu.roll(xp, DN_CONV - 1 - kk, 0)[8:, :]
        qkv = _silu(acc)
        q = qkv[:, 0:w]
        k = qkv[:, w:2 * w]
        ssq = jnp.dot((q * q).astype(BF16), ones_bd, preferred_element_type=F32)
        ssk = jnp.dot((k * k).astype(BF16), ones_bd, preferred_element_type=F32)
        put(dn_ref, rows, DN_Q, q * lax.rsqrt(ssq + NORM_EPS) * (DN_HEAD_DIM ** -0.5))
        put(dn_ref, rows, DN_K, k * lax.rsqrt(ssk + NORM_EPS))
        put(dn_ref, rows, DN_V, qkv[:, 2 * w:3 * w] * dn_ref[rows, DN_BETA * w:(DN_BETA + 1) * w])

    for r in range(tm // sub):
        r0 = r * sub
        rows = slice(r0, r0 + sub)
        x = x_ref[rows, :]
        h = x * lax.rsqrt(jnp.mean(x * x, axis=-1, keepdims=True) + NORM_EPS) * g_ref[...]
        hb = h.astype(BF16)
        pad_scr[r0 + 8:r0 + 8 + sub, :] = project(hb, COL_QKV, 3)
        p = project(hb, COL_A, 3)
        put(act_ref, rows, ACT_A, p[:, 0:w] * _sigmoid(p[:, w:2 * w]))
        put(act_ref, rows, ACT_AZ, _silu(p[:, 2 * w:3 * w]))
        p = project(hb, COL_B, 2)
        ulo_ref[rows, :] = p[:, 0:HALF]
        uhi_ref[rows, :] = p[:, HALF:w]
        put(act_ref, rows, ACT_BZ, _silu(p[:, w:2 * w]))
        p = project(hb, COL_C, 4)
        put(act_ref, rows, ACT_CB, p[:, 0:w] * _silu(p[:, 3 * w:4 * w]))
        put(act_ref, rows, ACT_CX, p[:, w:2 * w] * p[:, 2 * w:3 * w])
        p = jnp.dot(hb, wt_ref[...], preferred_element_type=F32)
        put(act_ref, rows, ACT_DZ, _silu(p[:, 0:w]))
        p = jnp.concatenate(
            [jnp.broadcast_to(p[:, w + j:w + j + 1], (sub, DN_HEAD_DIM)) for j in range(2 * DN_HEADS)],
            axis=1)
        sp_in = p[:, 0:w] + dtb_ref[...]
        softplus = jnp.maximum(sp_in, 0.0) + jnp.log1p(jnp.exp(-jnp.abs(sp_in)))
        put(dn_ref, rows, DN_G, -jnp.exp(alog_ref[...]) * softplus)
        put(dn_ref, rows, DN_BETA, _sigmoid(p[:, w:2 * w]))
        if r > 0:
            deltanet_inputs(slice(r0 - sub, r0), r0 - sub)
    deltanet_inputs(slice(tm - sub, tm), tm - sub)


def _inproj(layer, x2d, g, w, wt, dcw, alog, dtb, tm, seq):
    t, d = x2d.shape
    const = functools.partial(_layer_spec, layer)
    out = lambda n: pl.BlockSpec((tm, n * BRANCH), lambda i: (i, 0))
    return pl.pallas_call(
        functools.partial(_inproj_kernel, steps_per_seq=seq // tm, sub=min(256, tm)),
        out_shape=(jax.ShapeDtypeStruct((t, ACT_COLS * BRANCH), F32),
                   jax.ShapeDtypeStruct((t, DN_COLS * BRANCH), F32),
                   jax.ShapeDtypeStruct((t, HALF), F32),
                   jax.ShapeDtypeStruct((t, HALF), F32)),
        grid=(t // tm,),
        in_specs=[pl.BlockSpec((tm, d), lambda i: (i, 0)),
                  const((1, d)), const((d, IN_COLS)), const((d, TAIL_COLS)),
                  const((8, 3 * BRANCH)), const((1, BRANCH)), const((1, BRANCH))],
        out_specs=(out(ACT_COLS), out(DN_COLS), pl.BlockSpec((tm, HALF), lambda i: (i, 0)),
                   pl.BlockSpec((tm, HALF), lambda i: (i, 0))),
        scratch_shapes=[pltpu.VMEM((tm + 8, 3 * BRANCH), F32)],
        compiler_params=pltpu.CompilerParams(
            dimension_semantics=("arbitrary",), vmem_limit_bytes=VMEM_LIMIT_BYTES),
        name="inproj",
    )(x2d, g, w, wt, dcw, alog, dtb)


S5_TILES = S5_GROUPS // 2


def _block_transpose(pieces, block):
    lane = lax.broadcasted_iota(jnp.int32, pieces[0].shape, 1)
    n = len(pieces)
    s = n // 2
    while s:
        width = s * block
        lower = lane % (2 * width) < width
        out = list(pieces)
        for j in range(n):
            if j & s == 0:
                a, b = pieces[j], pieces[j + s]
                out[j] = jnp.where(lower, a, pltpu.roll(b, width, 1))
                out[j + s] = jnp.where(lower, pltpu.roll(a, 128 - width, 1), b)
        pieces = out
        s //= 2
    return pieces


def _s5_kernel(ulo_ref, uhi_ref, wd_ref, td_ref, vdt_ref, a_ref, ylo_ref, yhi_ref,
               carry_scr, z_scr, xp_scr):
    rows = ulo_ref.shape[0] // S5_Q
    n = S5_NSTATE
    tiles_per_half = S5_TILES // 2
    lanes = 2 * S5_GROUP

    @pl.when(pl.program_id(1) == 0)
    def _():
        carry_scr[...] = jnp.zeros_like(carry_scr)

    ub = []
    for half in (ulo_ref, uhi_ref):
        tok = [half[pl.ds(j, rows, stride=S5_Q), :] for j in range(S5_Q)]
        first = _block_transpose(tok[0:4], lanes)
        second = _block_transpose(tok[4:8], lanes)
        ub += [jnp.concatenate([first[k], second[k]], axis=1).astype(BF16)
               for k in range(tiles_per_half)]
    for k in range(S5_TILES):
        z = jnp.dot(ub[k], wd_ref[k], preferred_element_type=F32)
        z_scr[:, k * 128:(k + 1) * 128] = z[:, 0:128]
        z_scr[:, n + k * 128:n + (k + 1) * 128] = z[:, 128:256]

    ar = a_ref[0:1, :]
    ai = a_ref[1:2, :]

    def body(c, carry):
        xr, xi = carry
        xp_scr[pl.ds(c, 1), 0:n] = xr
        xp_scr[pl.ds(c, 1), n:2 * n] = xi
        zr = z_scr[pl.ds(c, 1), 0:n]
        zi = z_scr[pl.ds(c, 1), n:2 * n]
        return ar * xr - ai * xi + zr, ar * xi + ai * xr + zi

    xr, xi = lax.fori_loop(0, rows, body, (carry_scr[0:1, 0:n], carry_scr[0:1, n:2 * n]),
                           unroll=4)
    carry_scr[0:1, 0:n] = xr
    carry_scr[0:1, n:2 * n] = xi

    ys = []
    for k in range(S5_TILES):
        xp = jnp.concatenate([xp_scr[:, k * 128:(k + 1) * 128],
                              xp_scr[:, n + k * 128:n + (k + 1) * 128]], axis=1).astype(BF16)
        ys.append(jnp.dot(ub[k], td_ref[k], preferred_element_type=F32)
                  + lax.dot_general(xp, vdt_ref[k], (((1,), (1,)), ((), ())),
                                    preferred_element_type=F32))
    for h, out_ref in enumerate((ylo_ref, yhi_ref)):
        tiles = ys[h * tiles_per_half:(h + 1) * tiles_per_half]
        for m in range(2):
            tok = _block_transpose([y[:, m * 128:(m + 1) * 128] for y in tiles], lanes)
            for b in range(4):
                out_ref[pl.ds(4 * m + b, rows, stride=S5_Q), :] = tok[b]


def _s5(layer, ulo, uhi, wd, td, vd, a, rows):
    b, l, _ = ulo.shape
    const = functools.partial(_layer_spec, layer)
    half = lambda: pl.BlockSpec((None, rows * S5_Q, HALF), lambda bi, j: (bi, j, 0))
    tiles = (S5_TILES, BRANCH, BRANCH)
    return pl.pallas_call(
        _s5_kernel,
        out_shape=(jax.ShapeDtypeStruct((b, l, HALF), F32), jax.ShapeDtypeStruct((b, l, HALF), F32)),
        grid=(b, l // (rows * S5_Q)),
        in_specs=[half(), half(), const(tiles), const(tiles), const(tiles), const((8, S5_NSTATE))],
        out_specs=(half(), half()),
        scratch_shapes=[pltpu.VMEM((8, 2 * S5_NSTATE), F32),
                        pltpu.VMEM((rows, 2 * S5_NSTATE), F32),
                        pltpu.VMEM((rows, 2 * S5_NSTATE), F32)],
        compiler_params=pltpu.CompilerParams(
            dimension_semantics=("parallel", "arbitrary"), vmem_limit_bytes=VMEM_LIMIT_BYTES),
        name="s5",
    )(ulo, uhi, wd, td, vd, a)


def _tile_heads(x):
    return jnp.concatenate([x] * DN_HEADS, axis=0)


def _dnet_kernel(dn_ref, ng_ref, o_ref, s_scr, o_scr, u_scr, wq_scr, al_scr, egl_scr):
    nb, tc, _ = dn_ref.shape
    c64 = DN_CHUNK
    w = DN_HEADS * DN_HEAD_DIM
    nck = tc // c64
    gb = DN_GROUP // nck

    @pl.when(pl.program_id(0) == 0)
    def _():
        s_scr[...] = jnp.zeros_like(s_scr)

    row_h = lax.broadcasted_iota(jnp.int32, (w, w), 0) // DN_HEAD_DIM
    col_h = lax.broadcasted_iota(jnp.int32, (w, w), 1) // DN_HEAD_DIM
    bd = row_h == col_h
    bd_bf = jnp.where(bd, 1.0, 0.0).astype(BF16)

    def mmb(a, b_bf):
        return jnp.dot(a.astype(BF16), b_bf, preferred_element_type=F32)

    def mmb_nt(a, b_bf):
        return lax.dot_general(a.astype(BF16), b_bf, (((1,), (1,)), ((), ())),
                               preferred_element_type=F32)

    ri = lax.broadcasted_iota(jnp.int32, (c64, w), 0)
    cj = lax.broadcasted_iota(jnp.int32, (c64, w), 1) % DN_HEAD_DIM
    causal = ri >= cj
    strict = ri > cj
    diag = ri == cj
    eye_all = jnp.where(diag, 1.0, 0.0).astype(F32)
    lt_bf = jnp.where(lax.broadcasted_iota(jnp.int32, (c64, c64), 0)
                      >= lax.broadcasted_iota(jnp.int32, (c64, c64), 1), 1.0, 0.0).astype(BF16)

    def blockdiag(x):
        return _tile_heads(x.astype(BF16)) * bd_bf

    grp = range(DN_GROUP)

    def prepare(gi, carry):
        def col(i, j):
            return dn_ref[gi * gb + i // nck, (i % nck) * c64:(i % nck + 1) * c64, j * w:(j + 1) * w]

        rows = [pl.ds(pl.multiple_of((gi * gb + i // nck) * tc + (i % nck) * c64, c64), c64)
                for i in grp]
        qc = [col(i, DN_Q) for i in grp]
        kc = [col(i, DN_K) for i in grp]
        bc = [col(i, DN_BETA) for i in grp]
        g = [col(i, DN_G) for i in grp]
        g_hi = [x.astype(BF16) for x in g]
        g_lo = [(g[i] - g_hi[i].astype(F32)).astype(BF16) for i in grp]
        gc = [jnp.dot(lt_bf, g_hi[i], preferred_element_type=F32)
              + jnp.dot(lt_bf, g_lo[i], preferred_element_type=F32) for i in grp]
        egc = [jnp.exp(x) for x in gc]
        gl = [x[c64 - 1:c64, :] for x in gc]
        kb = [kc[i] * bc[i] for i in grp]
        grow = [jnp.sum(jnp.where(diag, x, 0.0), axis=0, keepdims=True) for x in gc]
        decay = [jnp.exp(jnp.where(causal, gc[i] - grow[i], -jnp.inf)) for i in grp]
        k_bd = [blockdiag(x) for x in kc]
        kq = [mmb_nt(jnp.concatenate([kb[i], qc[i]], axis=0), k_bd[i]) for i in grp]
        for i in grp:
            al_scr[gi * DN_GROUP + i, 0:c64, :] = kq[i][c64:2 * c64, :] * decay[i]
        x = [jnp.where(strict, -kq[i][0:c64, :] * decay[i], 0.0) for i in grp]
        p = [eye_all + x[i] for i in grp]
        x = [mmb(x[i], blockdiag(x[i])) for i in grp]
        for _ in range(4):
            xp = [mmb(jnp.concatenate([x[i], p[i]], axis=0), blockdiag(x[i])) for i in grp]
            x = [xp[i][0:c64, :] for i in grp]
            p = [p[i] + xp[i][c64:2 * c64, :] for i in grp]
        p = [p[i] + mmb(p[i], blockdiag(x[i])) for i in grp]
        for i in grp:
            ci = gi * DN_GROUP + i
            u_scr[rows[i], :] = mmb(p[i], blockdiag(col(i, DN_V)))
            wq_scr[ci, 0:c64, :] = mmb(p[i], blockdiag(kb[i] * egc[i]))
            wq_scr[ci, c64:2 * c64, :] = qc[i] * egc[i]
            kd_t = (kc[i] * jnp.exp(gl[i] - gc[i])).T
            al_scr[ci, c64:2 * c64, :] = jnp.concatenate(
                [kd_t[h * c64:(h + 1) * c64, :] for h in range(DN_HEADS)], axis=1)
            egl_scr[ci] = jnp.broadcast_to(jnp.exp(gl[i]), (8, w))
        return carry

    lax.fori_loop(0, nb // gb, prepare, 0)

    bs = range(nb)

    def recur(c, carry):
        rows = [pl.ds(pl.multiple_of(b * tc + c * c64, c64), c64) for b in bs]
        s = [s_scr[b] for b in bs]
        ws_qs = [mmb(wq_scr[b * nck + c], blockdiag(s[b])) for b in bs]
        v_new = [u_scr[rows[b], :] - ws_qs[b][0:c64, :] for b in bs]
        od = [mmb(al_scr[b * nck + c], blockdiag(v_new[b])) for b in bs]
        for b in bs:
            o_scr[rows[b], :] = ws_qs[b][c64:2 * c64, :] + od[b][0:c64, :]
        for b in bs:
            s_scr[b] = s[b] * egl_scr[b * nck + c][0:1, :] + od[b][c64:2 * c64, :]
        return carry

    lax.fori_loop(0, nck, recur, 0)

    o = o_scr[...]
    ms = mmb(o * o, bd_bf) * (1.0 / DN_HEAD_DIM)
    o_ref[...] = (o * lax.rsqrt(ms + NORM_EPS) * ng_ref[...]).reshape(nb, tc, w)


def _dnet(layer, dn3, ng, tc):
    b, l, _ = dn3.shape
    w = BRANCH
    nchunk = b * tc // DN_CHUNK
    assert DN_GROUP % (tc // DN_CHUNK) == 0 and nchunk % DN_GROUP == 0
    act = lambda: pltpu.VMEM((b * tc, w), F32)
    return pl.pallas_call(
        _dnet_kernel,
        out_shape=jax.ShapeDtypeStruct((b, l, w), F32),
        grid=(l // tc,),
        in_specs=[pl.BlockSpec((b, tc, DN_COLS * w), lambda i: (0, i, 0)),
                  _layer_spec(layer, (1, w))],
        out_specs=pl.BlockSpec((b, tc, w), lambda i: (0, i, 0)),
        scratch_shapes=[pltpu.VMEM((b, DN_CHUNK, w), F32),
                        act(), act(),
                        pltpu.VMEM((nchunk, 2 * DN_CHUNK, w), F32),
                        pltpu.VMEM((nchunk, 2 * DN_CHUNK, w), F32),
                        pltpu.VMEM((nchunk, 8, w), F32)],
        compiler_params=pltpu.CompilerParams(
            dimension_semantics=("arbitrary",), vmem_limit_bytes=VMEM_LIMIT_BYTES),
        name="dnet",
    )(dn3, ng)


def _mixout_kernel(act_ref, ahalo_ref, chalo_ref, ulo_ref, uhi_ref, ylo_ref, yhi_ref, od_ref, x_ref,
                   acw_ref, avec_ref, apw_ref, bvec_ref, bglu_ref, ccw_ref, wout_ref, fg_ref,
                   out_ref, ash_scr, cpad_scr, *, final, sub):
    tc = x_ref.shape[0]
    w = BRANCH
    first = pl.program_id(1) == 0

    def act(j, rows=slice(None)):
        return act_ref[rows, j * w:(j + 1) * w]

    apad = jnp.concatenate([jnp.where(first, 0.0, ahalo_ref[...]), act(ACT_A)], axis=0)
    for r in range(8):
        shifted = apad if r == 0 else pltpu.roll(apad, tc + 32 - r, 0)
        ash_scr[r] = shifted.astype(BF16)
        ash_scr[r + 8, 0:tc + 16, :] = shifted[8:tc + 24, :].astype(BF16)
    cpad_scr[0:8, :] = jnp.where(first, 0.0, chalo_ref[...])
    cpad_scr[8:8 + tc, :] = act(ACT_CX)

    acw_bf = acw_ref[...].astype(BF16)
    conv_b, ln_g, ln_b, pw_b = (avec_ref[0:1, :], avec_ref[1:2, :], avec_ref[2:3, :],
                                avec_ref[3:4, :])
    d_skip, glu_b = bvec_ref[0:1, :], bvec_ref[1:2, :]

    for r in range(tc // sub):
        r0 = r * sub
        rows = slice(r0, r0 + sub)
        acc = conv_b
        for k0 in range(0, CONF_KERNEL, CONV_GROUP):
            part = None
            for kk in range(k0, min(k0 + CONV_GROUP, CONF_KERNEL)):
                off = 32 - (CONF_KERNEL - 1) + kk
                shift = off % BF16_ROWS
                term = acw_bf[kk:kk + 1, :] * ash_scr[shift, pl.ds(r0 + off - shift, sub), :]
                part = term if part is None else part + term
            acc = acc + part.astype(F32)
        mu = jnp.mean(acc, axis=-1, keepdims=True)
        xc = acc - mu
        ln = xc * lax.rsqrt(jnp.mean(xc * xc, axis=-1, keepdims=True) + NORM_EPS) * ln_g + ln_b
        ya = (_mm(_silu(ln), apw_ref[...]) + pw_b) * act(ACT_AZ, rows)
        ub = jnp.concatenate([ulo_ref[rows, :], uhi_ref[rows, :]], axis=1)
        ys = jnp.concatenate([ylo_ref[rows, :], yhi_ref[rows, :]], axis=1)
        yb = jax.nn.gelu(ys + d_skip * ub)
        yb = yb * _sigmoid(_mm(yb, bglu_ref[...]) + glu_b) * act(ACT_BZ, rows)
        cp = cpad_scr[r0:r0 + sub + 8, :]
        cacc = ccw_ref[SC_KERNEL - 1:SC_KERNEL, :] * cp[8:, :]
        for kk in range(SC_KERNEL - 1):
            cacc = cacc + ccw_ref[kk:kk + 1, :] * pltpu.roll(cp, SC_KERNEL - 1 - kk, 0)[8:, :]
        yc = act(ACT_CB, rows) * cacc
        yd = od_ref[rows, :] * act(ACT_DZ, rows)
        mixed = jnp.concatenate([ya, yb, yc, yd], axis=1)
        y = x_ref[rows, :] + _mm(mixed, wout_ref[...])
        if final:
            y = y * lax.rsqrt(jnp.mean(y * y, axis=-1, keepdims=True) + NORM_EPS) * fg_ref[...]
        out_ref[rows, :] = y


def _mixout(layer, act3, ulo, uhi, ylo, yhi, od, x3, acw, avec, apw, bvec, bglu, ccw, wout, fg, tc,
            final):
    b, l, d = x3.shape
    w = BRANCH
    blk = lambda width, col: pl.BlockSpec((None, tc, width), lambda bi, i: (bi, i, col))
    halo = lambda nrow, col: pl.BlockSpec(
        (None, nrow, w), lambda bi, i: (bi, jnp.maximum(i * (tc // nrow) - 1, 0), col))
    const = functools.partial(_layer_spec, layer)
    return pl.pallas_call(
        functools.partial(_mixout_kernel, final=final, sub=256),
        out_shape=jax.ShapeDtypeStruct((b, l, d), F32),
        grid=(b, l // tc),
        in_specs=[blk(ACT_COLS * w, 0), halo(32, ACT_A), halo(8, ACT_CX),
                  blk(HALF, 0), blk(HALF, 0), blk(HALF, 0), blk(HALF, 0), blk(w, 0), blk(d, 0),
                  const((32, w)), const((8, w)), const((w, w)), const((8, w)), const((w, w)),
                  const((8, w)), const((4 * w, d)), pl.BlockSpec((1, d), lambda bi, i: (0, 0))],
        out_specs=pl.BlockSpec((None, tc, d), lambda bi, i: (bi, i, 0)),
        scratch_shapes=[pltpu.VMEM((BF16_ROWS, tc + 32, w), BF16), pltpu.VMEM((tc + 8, w), F32)],
        compiler_params=pltpu.CompilerParams(
            dimension_semantics=("parallel", "parallel"), vmem_limit_bytes=VMEM_LIMIT_BYTES),
        name="mixout_final" if final else "mixout",
    )(act3, act3, act3, ulo, uhi, ylo, yhi, od, x3, acw, avec, apw, bvec, bglu, ccw, wout, fg)


def _pad_rows(a, n):
    return jnp.pad(a, ((0, 0), (0, n - a.shape[1]), (0, 0)))


def _s5_tiles(tab, state_cols):
    dep = tab.shape[0]
    nt, rb = S5_TILES, 2 * S5_GROUP
    t = tab.reshape(dep, S5_Q, nt, rb, 128)
    t = jnp.transpose(t, (0, 2, 1, 3, 4)).reshape(dep, nt, S5_Q * rb, 128)
    per = S5_STATE if state_cols else S5_GROUP
    src = jnp.arange(128)
    dst = jnp.arange(2 * 128)
    spread = ((src[:, None] // per == dst[None, :] // (2 * per))
              & (src[:, None] % per == dst[None, :] % per)).astype(F32)
    row_group = (jnp.arange(S5_Q * rb) % rb) // S5_GROUP
    col_group = (dst % (2 * per)) // per
    mask = (row_group[:, None] == col_group[None, :]).astype(F32)
    return (jnp.einsum("dkrc,cn->dkrn", t, spread, precision=HIGHEST) * mask).astype(BF16)


def _s5_tables(lam_re, lam_im, b_re, b_im, c_re, c_im, log_dt):
    q, g, p, h = S5_Q, S5_GROUPS, S5_STATE, S5_GROUP
    dep = lam_re.shape[0]
    lr = jnp.minimum(lam_re, -1e-4)
    li = lam_im
    dt = jnp.exp(log_dt)[..., None]
    lrdt, lidt = (lr * dt)[:, None], (li * dt)[:, None]

    def power(steps):
        s = steps[None, :, None, None]
        mag = jnp.exp(lrdt * s)
        return (mag * jnp.cos(lidt * s))[:, :, :, None, :], (mag * jnp.sin(lidt * s))[:, :, :, None, :]

    tok = jnp.arange(q, dtype=F32)
    p1r, p1i = power(jnp.ones((1,), F32))
    nr, ni = p1r[:, 0, :, 0] - 1.0, p1i[:, 0, :, 0]
    den = lr * lr + li * li
    fr, fi = (nr * lr + ni * li) / den, (ni * lr - nr * li) / den
    bbr = jnp.transpose(fr[..., None] * b_re - fi[..., None] * b_im, (0, 1, 3, 2))[:, None]
    bbi = jnp.transpose(fr[..., None] * b_im + fi[..., None] * b_re, (0, 1, 3, 2))[:, None]
    cr, ci = c_re[:, None], c_im[:, None]

    rr, ri = power(q - 1.0 - tok)
    wc = jnp.concatenate([rr * bbr - ri * bbi, rr * bbi + ri * bbr], axis=-1)
    qr, qi = power(tok + 1.0)
    mc = jnp.concatenate([cr * qr - ci * qi, -(cr * qi + ci * qr)], axis=-1)
    er, ei = power(tok)
    cpr = jnp.transpose(cr * er - ci * ei, (0, 2, 1, 3, 4)).reshape(dep, g, q * h, p)
    cpi = jnp.transpose(cr * ei + ci * er, (0, 2, 1, 3, 4)).reshape(dep, g, q * h, p)
    kd = (jnp.einsum("dgcp,dgyp->dgcy", bbr[:, 0], cpr, precision=HIGHEST)
          - jnp.einsum("dgcp,dgyp->dgcy", bbi[:, 0], cpi, precision=HIGHEST))
    lane = jnp.arange(q * h)
    tk = jnp.stack([jnp.where(lane >= j * h, jnp.roll(kd, j * h, axis=-1), 0.0) for j in range(q)],
                   axis=1)

    flat = lambda t: t.reshape(dep, q * g * h, t.shape[-1])
    ar, ai = power(jnp.full((1,), float(q), F32))
    a = jnp.concatenate([ar.reshape(dep, 1, g * p), ai.reshape(dep, 1, g * p),
                         jnp.zeros((dep, 6, g * p), F32)], axis=1)
    return _s5_tiles(flat(wc), True), _s5_tiles(flat(tk), False), _s5_tiles(flat(mc), True), a


def _w_in_tail(w_in):
    n = MAIN_BLOCKS * BRANCH
    pad = jnp.zeros(w_in.shape[:-1] + (128 - 2 * DN_HEADS,), w_in.dtype)
    return jnp.concatenate([w_in[..., n + 2 * DN_HEADS:], w_in[..., n:n + 2 * DN_HEADS], pad],
                           axis=-1).astype(BF16)


def kernel(x, norm_g, w_in, a_conv_w, a_conv_b, a_ln_g, a_ln_b, a_pw_w, a_pw_b, s5_lambda_re, s5_lambda_im, s5_b_re, s5_b_im, s5_c_re, s5_c_im, s5_d, s5_log_dt, s5_glu_w, s5_glu_b, c_conv_w, d_conv_w, d_a_log, d_dt_bias, d_norm_g, w_out, final_g):
    bsz, seq, d = x.shape
    depth = w_in.shape[0]
    w = BRANCH
    tm = min(512, seq)
    tc = min(1024, seq)
    dn_tc = min(256, seq)
    s5_rows = min(512, seq // S5_Q)

    w_bf = w_in.astype(BF16)
    w_tail = _w_in_tail(w_in)
    s5_wd, s5_td, s5_vd, s5_a = _s5_tables(s5_lambda_re, s5_lambda_im, s5_b_re, s5_b_im,
                                           s5_c_re, s5_c_im, s5_log_dt)
    acw = _pad_rows(a_conv_w, 32)
    zeros = jnp.zeros_like(a_conv_b)
    avec = jnp.stack([a_conv_b, a_ln_g, a_ln_b, a_pw_b, zeros, zeros, zeros, zeros], axis=1)
    bvec = jnp.stack([s5_d, s5_glu_b, zeros, zeros, zeros, zeros, zeros, zeros], axis=1)
    ccw = _pad_rows(c_conv_w, 8)
    dcw = _pad_rows(d_conv_w, 8)
    alog = jnp.repeat(d_a_log, DN_HEAD_DIM, axis=-1)[:, None, :]
    dtb = jnp.repeat(d_dt_bias, DN_HEAD_DIM, axis=-1)[:, None, :]
    ng = jnp.tile(d_norm_g, (1, DN_HEADS))[:, None, :]
    apw = a_pw_w.astype(BF16)
    bglu = s5_glu_w.astype(BF16)
    wout = w_out.astype(BF16)
    fg = final_g[None, :]

    def seq3(a):
        return a.reshape(bsz, seq, a.shape[-1])

    ng3 = norm_g[:, None, :]
    for l in range(depth):
        act, dn, ulo, uhi = _inproj(l, x.reshape(bsz * seq, d), ng3, w_bf, w_tail, dcw, alog, dtb,
                                    tm, seq)
        ulo, uhi = seq3(ulo), seq3(uhi)
        ylo, yhi = _s5(l, ulo, uhi, s5_wd, s5_td, s5_vd, s5_a, s5_rows)
        od = _dnet(l, seq3(dn), ng, dn_tc)
        x = _mixout(l, seq3(act), ulo, uhi, ylo, yhi, od, x, acw, avec, apw, bvec, bglu, ccw, wout,
                    fg, tc, l == depth - 1)
    return x
```

```python
import functools

import jax
import jax.numpy as jnp
from jax import lax
from jax.experimental import pallas as pl
from jax.experimental.pallas import tpu as pltpu

F32 = jnp.float32
BF16 = jnp.bfloat16
HIGHEST = lax.Precision.HIGHEST

NORM_EPS = 1e-6
BRANCH = 256
CONF_KERNEL = 31
SC_KERNEL = 3
DN_HEADS = 4
DN_HEAD_DIM = 64
DN_CONV = 4
DN_CHUNK = 64
DN_GROUP = 16
CONV_GROUP = 4
BF16_ROWS = 16
S5_GROUPS = 16
S5_GROUP = 16
S5_STATE = 64
S5_Q = 8
S5_ROW = S5_Q * BRANCH
S5_NSTATE = S5_GROUPS * S5_STATE

COL_A, COL_B, COL_C, COL_QKV, MAIN_BLOCKS = 0, 3, 5, 9, 12
IN_COLS = MAIN_BLOCKS * BRANCH + 2 * DN_HEADS + BRANCH
TAIL_COLS = BRANCH + 128
HALF = BRANCH // 2
GATE_AZ, GATE_BZ, GATE_CB, GATE_DZ, GATE_A, GATE_COLS = 0, 1, 2, 3, 4, 5
DN_Q, DN_K, DN_V, DN_BETA, DN_G, DN_COLS = 0, 1, 2, 3, 4, 5

VMEM_LIMIT_BYTES = 56 * 1024 * 1024


def _mm(a, b):
    return jnp.dot(a.astype(BF16), b.astype(BF16), preferred_element_type=F32)


def _layer_spec(layer, shape):
    zeros = (0,) * len(shape)
    return pl.BlockSpec((None,) + tuple(shape), lambda *_: (layer,) + zeros)


def _sigmoid(x):
    return 1.0 / (1.0 + jnp.exp(-x))


def _silu(x):
    return x * _sigmoid(x)


def _inproj_kernel(x_ref, g_ref, w_ref, wt_ref, dcw_ref, alog_ref, dtb_ref, act_ref, gate_ref, dn_ref,
                   ulo_ref, uhi_ref, pad_scr, *, steps_per_seq, sub):
    tm = x_ref.shape[0]
    w = BRANCH
    first = pl.program_id(0) % steps_per_seq == 0

    @pl.when(first)
    def _():
        pad_scr[tm:tm + 8, :] = jnp.zeros((8, 3 * w), F32)

    pad_scr[0:8, :] = pad_scr[tm:tm + 8, :]

    row_h = lax.broadcasted_iota(jnp.int32, (w, w), 0) // DN_HEAD_DIM
    col_h = lax.broadcasted_iota(jnp.int32, (w, w), 1) // DN_HEAD_DIM
    ones_bd = jnp.where(row_h == col_h, 1.0, 0.0).astype(BF16)

    def project(hb, col, n):
        return jnp.dot(hb, w_ref[:, col * w:(col + n) * w], preferred_element_type=F32)

    def put(ref, rows, j, val):
        ref[rows, j * w:(j + 1) * w] = val.astype(ref.dtype)

    def deltanet_inputs(rows, r0):
        xp = pad_scr[r0:r0 + sub + 8, :]
        acc = dcw_ref[DN_CONV - 1:DN_CONV, :] * xp[8:, :]
        for kk in range(DN_CONV - 1):
            acc = acc + dcw_ref[kk:kk + 1, :] * pltpu.roll(xp, DN_CONV - 1 - kk, 0)[8:, :]
        qkv = _silu(acc)
        q = qkv[:, 0:w]
        k = qkv[:, w:2 * w]
        ssq = jnp.dot((q * q).astype(BF16), ones_bd, preferred_element_type=F32)
        ssk = jnp.dot((k * k).astype(BF16), ones_bd, preferred_element_type=F32)
        put(dn_ref, rows, DN_Q, q * lax.rsqrt(ssq + NORM_EPS) * (DN_HEAD_DIM ** -0.5))
        put(dn_ref, rows, DN_K, k * lax.rsqrt(ssk + NORM_EPS))
        put(dn_ref, rows, DN_V, qkv[:, 2 * w:3 * w] * dn_ref[rows, DN_BETA * w:(DN_BETA + 1) * w])

    for r in range(tm // sub):
        r0 = r * sub
        rows = slice(r0, r0 + sub)
        x = x_ref[rows, :]
        h = x * lax.rsqrt(jnp.mean(x * x, axis=-1, keepdims=True) + NORM_EPS) * g_ref[...]
        hb = h.astype(BF16)
        pad_scr[r0 + 8:r0 + 8 + sub, :] = project(hb, COL_QKV, 3)
        p = project(hb, COL_A, 3)
        put(gate_ref, rows, GATE_A, p[:, 0:w] * _sigmoid(p[:, w:2 * w]))
        put(gate_ref, rows, GATE_AZ, _silu(p[:, 2 * w:3 * w]))
        p = project(hb, COL_B, 2)
        ulo_ref[rows, :] = p[:, 0:HALF]
        uhi_ref[rows, :] = p[:, HALF:w]
        put(gate_ref, rows, GATE_BZ, _silu(p[:, w:2 * w]))
        p = project(hb, COL_C, 4)
        put(gate_ref, rows, GATE_CB, p[:, 0:w] * _silu(p[:, 3 * w:4 * w]))
        act_ref[rows, :] = p[:, w:2 * w] * p[:, 2 * w:3 * w]
        p = jnp.dot(hb, wt_ref[...], preferred_element_type=F32)
        put(gate_ref, rows, GATE_DZ, _silu(p[:, 0:w]))
        p = jnp.concatenate(
            [jnp.broadcast_to(p[:, w + j:w + j + 1], (sub, DN_HEAD_DIM)) for j in range(2 * DN_HEADS)],
            axis=1)
        sp_in = p[:, 0:w] + dtb_ref[...]
        softplus = jnp.maximum(sp_in, 0.0) + jnp.log1p(jnp.exp(-jnp.abs(sp_in)))
        put(dn_ref, rows, DN_G, -jnp.exp(alog_ref[...]) * softplus)
        put(dn_ref, rows, DN_BETA, _sigmoid(p[:, w:2 * w]))
        if r > 0:
            deltanet_inputs(slice(r0 - sub, r0), r0 - sub)
    deltanet_inputs(slice(tm - sub, tm), tm - sub)


def _inproj(layer, x2d, g, w, wt, dcw, alog, dtb, tm, seq):
    t, d = x2d.shape
    const = functools.partial(_layer_spec, layer)
    out = lambda n: pl.BlockSpec((tm, n * BRANCH), lambda i: (i, 0))
    return pl.pallas_call(
        functools.partial(_inproj_kernel, steps_per_seq=seq // tm, sub=min(256, tm)),
        out_shape=(jax.ShapeDtypeStruct((t, BRANCH), F32),
                   jax.ShapeDtypeStruct((t, GATE_COLS * BRANCH), BF16),
                   jax.ShapeDtypeStruct((t, DN_COLS * BRANCH), F32),
                   jax.ShapeDtypeStruct((t, HALF), F32),
                   jax.ShapeDtypeStruct((t, HALF), F32)),
        grid=(t // tm,),
        in_specs=[pl.BlockSpec((tm, d), lambda i: (i, 0)),
                  const((1, d)), const((d, IN_COLS)), const((d, TAIL_COLS)),
                  const((8, 3 * BRANCH)), const((1, BRANCH)), const((1, BRANCH))],
        out_specs=(out(1), out(GATE_COLS), out(DN_COLS), pl.BlockSpec((tm, HALF), lambda i: (i, 0)),
                   pl.BlockSpec((tm, HALF), lambda i: (i, 0))),
        scratch_shapes=[pltpu.VMEM((tm + 8, 3 * BRANCH), F32)],
        compiler_params=pltpu.CompilerParams(
            dimension_semantics=("arbitrary",), vmem_limit_bytes=VMEM_LIMIT_BYTES),
        name="inproj",
    )(x2d, g, w, wt, dcw, alog, dtb)


S5_TILES = S5_GROUPS // 2


def _block_transpose(pieces, block):
    lane = lax.broadcasted_iota(jnp.int32, pieces[0].shape, 1)
    n = len(pieces)
    s = n // 2
    while s:
        width = s * block
        lower = lane % (2 * width) < width
        out = list(pieces)
        for j in range(n):
            if j & s == 0:
                a, b = pieces[j], pieces[j + s]
                out[j] = jnp.where(lower, a, pltpu.roll(b, width, 1))
                out[j + s] = jnp.where(lower, pltpu.roll(a, 128 - width, 1), b)
        pieces = out
        s //= 2
    return pieces


def _s5_kernel(ulo_ref, uhi_ref, wd_ref, td_ref, vdt_ref, a_ref, ylo_ref, yhi_ref,
               carry_scr, z_scr, xp_scr):
    rows = ulo_ref.shape[0] // S5_Q
    n = S5_NSTATE
    tiles_per_half = S5_TILES // 2
    lanes = 2 * S5_GROUP

    @pl.when(pl.program_id(1) == 0)
    def _():
        carry_scr[...] = jnp.zeros_like(carry_scr)

    ub = []
    for half in (ulo_ref, uhi_ref):
        tok = [half[pl.ds(j, rows, stride=S5_Q), :] for j in range(S5_Q)]
        first = _block_transpose(tok[0:4], lanes)
        second = _block_transpose(tok[4:8], lanes)
        ub += [jnp.concatenate([first[k], second[k]], axis=1).astype(BF16)
               for k in range(tiles_per_half)]
    for k in range(S5_TILES):
        z = jnp.dot(ub[k], wd_ref[k], preferred_element_type=F32)
        z_scr[:, k * 128:(k + 1) * 128] = z[:, 0:128]
        z_scr[:, n + k * 128:n + (k + 1) * 128] = z[:, 128:256]

    ar = a_ref[0:1, :]
    ai = a_ref[1:2, :]

    def body(c, carry):
        xr, xi = carry
        xp_scr[pl.ds(c, 1), 0:n] = xr
        xp_scr[pl.ds(c, 1), n:2 * n] = xi
        zr = z_scr[pl.ds(c, 1), 0:n]
        zi = z_scr[pl.ds(c, 1), n:2 * n]
        return ar * xr - ai * xi + zr, ar * xi + ai * xr + zi

    xr, xi = lax.fori_loop(0, rows, body, (carry_scr[0:1, 0:n], carry_scr[0:1, n:2 * n]),
                           unroll=4)
    carry_scr[0:1, 0:n] = xr
    carry_scr[0:1, n:2 * n] = xi

    ys = []
    for k in range(S5_TILES):
        xp = jnp.concatenate([xp_scr[:, k * 128:(k + 1) * 128],
                              xp_scr[:, n + k * 128:n + (k + 1) * 128]], axis=1).astype(BF16)
        ys.append(jnp.dot(ub[k], td_ref[k], preferred_element_type=F32)
                  + lax.dot_general(xp, vdt_ref[k], (((1,), (1,)), ((), ())),
                                    preferred_element_type=F32))
    for h, out_ref in enumerate((ylo_ref, yhi_ref)):
        tiles = ys[h * tiles_per_half:(h + 1) * tiles_per_half]
        for m in range(2):
            tok = _block_transpose([y[:, m * 128:(m + 1) * 128] for y in tiles], lanes)
            for b in range(4):
                out_ref[pl.ds(4 * m + b, rows, stride=S5_Q), :] = tok[b]


def _s5(layer, ulo, uhi, wd, td, vd, a, rows):
    b, l, _ = ulo.shape
    const = functools.partial(_layer_spec, layer)
    half = lambda: pl.BlockSpec((None, rows * S5_Q, HALF), lambda bi, j: (bi, j, 0))
    tiles = (S5_TILES, BRANCH, BRANCH)
    return pl.pallas_call(
        _s5_kernel,
        out_shape=(jax.ShapeDtypeStruct((b, l, HALF), F32), jax.ShapeDtypeStruct((b, l, HALF), F32)),
        grid=(b, l // (rows * S5_Q)),
        in_specs=[half(), half(), const(tiles), const(tiles), const(tiles), const((8, S5_NSTATE))],
        out_specs=(half(), half()),
        scratch_shapes=[pltpu.VMEM((8, 2 * S5_NSTATE), F32),
                        pltpu.VMEM((rows, 2 * S5_NSTATE), F32),
                        pltpu.VMEM((rows, 2 * S5_NSTATE), F32)],
        compiler_params=pltpu.CompilerParams(
            dimension_semantics=("parallel", "arbitrary"), vmem_limit_bytes=VMEM_LIMIT_BYTES),
        name="s5",
    )(ulo, uhi, wd, td, vd, a)


def _tile_heads(x):
    return jnp.concatenate([x] * DN_HEADS, axis=0)


def _dnet_kernel(dn_ref, ng_ref, o_ref, s_scr, o_scr, u_scr, wq_scr, al_scr, egl_scr):
    nb, tc, _ = dn_ref.shape
    c64 = DN_CHUNK
    w = DN_HEADS * DN_HEAD_DIM
    nck = tc // c64
    gb = DN_GROUP // nck

    @pl.when(pl.program_id(0) == 0)
    def _():
        s_scr[...] = jnp.zeros_like(s_scr)

    row_h = lax.broadcasted_iota(jnp.int32, (w, w), 0) // DN_HEAD_DIM
    col_h = lax.broadcasted_iota(jnp.int32, (w, w), 1) // DN_HEAD_DIM
    bd = row_h == col_h
    bd_bf = jnp.where(bd, 1.0, 0.0).astype(BF16)

    def mmb(a, b_bf):
        return jnp.dot(a.astype(BF16), b_bf, preferred_element_type=F32)

    def mmb_nt(a, b_bf):
        return lax.dot_general(a.astype(BF16), b_bf, (((1,), (1,)), ((), ())),
                               preferred_element_type=F32)

    ri = lax.broadcasted_iota(jnp.int32, (c64, w), 0)
    cj = lax.broadcasted_iota(jnp.int32, (c64, w), 1) % DN_HEAD_DIM
    causal = ri >= cj
    strict = ri > cj
    diag = ri == cj
    eye_all = jnp.where(diag, 1.0, 0.0).astype(F32)
    lt_bf = jnp.where(lax.broadcasted_iota(jnp.int32, (c64, c64), 0)
                      >= lax.broadcasted_iota(jnp.int32, (c64, c64), 1), 1.0, 0.0).astype(BF16)

    def blockdiag(x):
        return _tile_heads(x.astype(BF16)) * bd_bf

    grp = range(DN_GROUP)

    def prepare(gi, carry):
        def col(i, j):
            return dn_ref[gi * gb + i // nck, (i % nck) * c64:(i % nck + 1) * c64, j * w:(j + 1) * w]

        rows = [pl.ds(pl.multiple_of((gi * gb + i // nck) * tc + (i % nck) * c64, c64), c64)
                for i in grp]
        qc = [col(i, DN_Q) for i in grp]
        kc = [col(i, DN_K) for i in grp]
        bc = [col(i, DN_BETA) for i in grp]
        g = [col(i, DN_G) for i in grp]
        g_hi = [x.astype(BF16) for x in g]
        g_lo = [(g[i] - g_hi[i].astype(F32)).astype(BF16) for i in grp]
        gc = [jnp.dot(lt_bf, g_hi[i], preferred_element_type=F32)
              + jnp.dot(lt_bf, g_lo[i], preferred_element_type=F32) for i in grp]
        egc = [jnp.exp(x) for x in gc]
        gl = [x[c64 - 1:c64, :] for x in gc]
        kb = [kc[i] * bc[i] for i in grp]
        grow = [jnp.sum(jnp.where(diag, x, 0.0), axis=0, keepdims=True) for x in gc]
        decay = [jnp.exp(jnp.where(causal, gc[i] - grow[i], -jnp.inf)) for i in grp]
        k_bd = [blockdiag(x) for x in kc]
        kq = [mmb_nt(jnp.concatenate([kb[i], qc[i]], axis=0), k_bd[i]) for i in grp]
        for i in grp:
            al_scr[gi * DN_GROUP + i, 0:c64, :] = kq[i][c64:2 * c64, :] * decay[i]
        x = [jnp.where(strict, -kq[i][0:c64, :] * decay[i], 0.0) for i in grp]
        p = [eye_all + x[i] for i in grp]
        x = [mmb(x[i], blockdiag(x[i])) for i in grp]
        for _ in range(4):
            xp = [mmb(jnp.concatenate([x[i], p[i]], axis=0), blockdiag(x[i])) for i in grp]
            x = [xp[i][0:c64, :] for i in grp]
            p = [p[i] + xp[i][c64:2 * c64, :] for i in grp]
        p = [p[i] + mmb(p[i], blockdiag(x[i])) for i in grp]
        for i in grp:
            ci = gi * DN_GROUP + i
            u_scr[rows[i], :] = mmb(p[i], blockdiag(col(i, DN_V)))
            wq_scr[ci, 0:c64, :] = mmb(p[i], blockdiag(kb[i] * egc[i]))
            wq_scr[ci, c64:2 * c64, :] = qc[i] * egc[i]
            kd_t = (kc[i] * jnp.exp(gl[i] - gc[i])).T
            al_scr[ci, c64:2 * c64, :] = jnp.concatenate(
                [kd_t[h * c64:(h + 1) * c64, :] for h in range(DN_HEADS)], axis=1)
            egl_scr[ci] = jnp.broadcast_to(jnp.exp(gl[i]), (8, w))
        return carry

    lax.fori_loop(0, nb // gb, prepare, 0)

    bs = range(nb)

    def recur(c, carry):
        rows = [pl.ds(pl.multiple_of(b * tc + c * c64, c64), c64) for b in bs]
        s = [s_scr[b] for b in bs]
        ws_qs = [mmb(wq_scr[b * nck + c], blockdiag(s[b])) for b in bs]
        v_new = [u_scr[rows[b], :] - ws_qs[b][0:c64, :] for b in bs]
        od = [mmb(al_scr[b * nck + c], blockdiag(v_new[b])) for b in bs]
        for b in bs:
            o_scr[rows[b], :] = ws_qs[b][c64:2 * c64, :] + od[b][0:c64, :]
        for b in bs:
            s_scr[b] = s[b] * egl_scr[b * nck + c][0:1, :] + od[b][c64:2 * c64, :]
        return carry

    lax.fori_loop(0, nck, recur, 0)

    o = o_scr[...]
    ms = mmb(o * o, bd_bf) * (1.0 / DN_HEAD_DIM)
    o_ref[...] = (o * lax.rsqrt(ms + NORM_EPS) * ng_ref[...]).reshape(nb, tc, w).astype(o_ref.dtype)


def _dnet(layer, dn3, ng, tc):
    b, l, _ = dn3.shape
    w = BRANCH
    nchunk = b * tc // DN_CHUNK
    assert DN_GROUP % (tc // DN_CHUNK) == 0 and nchunk % DN_GROUP == 0
    act = lambda: pltpu.VMEM((b * tc, w), F32)
    return pl.pallas_call(
        _dnet_kernel,
        out_shape=jax.ShapeDtypeStruct((b, l, w), BF16),
        grid=(l // tc,),
        in_specs=[pl.BlockSpec((b, tc, DN_COLS * w), lambda i: (0, i, 0)),
                  _layer_spec(layer, (1, w))],
        out_specs=pl.BlockSpec((b, tc, w), lambda i: (0, i, 0)),
        scratch_shapes=[pltpu.VMEM((b, DN_CHUNK, w), F32),
                        act(), act(),
                        pltpu.VMEM((nchunk, 2 * DN_CHUNK, w), F32),
                        pltpu.VMEM((nchunk, 2 * DN_CHUNK, w), F32),
                        pltpu.VMEM((nchunk, 8, w), F32)],
        compiler_params=pltpu.CompilerParams(
            dimension_semantics=("arbitrary",), vmem_limit_bytes=VMEM_LIMIT_BYTES),
        name="dnet",
    )(dn3, ng)


def _mixout_kernel(act_ref, gate_ref, ahalo_ref, chalo_ref, ulo_ref, uhi_ref, ylo_ref, yhi_ref, od_ref,
                   x_ref,
                   acw_ref, avec_ref, apw_ref, bvec_ref, bglu_ref, ccw_ref, wout_ref, fg_ref,
                   out_ref, ash_scr, cpad_scr, *, final, sub):
    tc = x_ref.shape[0]
    w = BRANCH
    first = pl.program_id(1) == 0

    def gate(j, rows=slice(None)):
        return gate_ref[rows, j * w:(j + 1) * w].astype(F32)

    apad = jnp.concatenate([jnp.where(first, 0.0, ahalo_ref[...].astype(F32)), gate(GATE_A)], axis=0)
    for r in range(8):
        shifted = apad if r == 0 else pltpu.roll(apad, tc + 32 - r, 0)
        ash_scr[r] = shifted.astype(BF16)
        ash_scr[r + 8, 0:tc + 16, :] = shifted[8:tc + 24, :].astype(BF16)
    cpad_scr[0:8, :] = jnp.where(first, 0.0, chalo_ref[...])
    cpad_scr[8:8 + tc, :] = act_ref[...]

    acw_bf = acw_ref[...].astype(BF16)
    conv_b, ln_g, ln_b, pw_b = (avec_ref[0:1, :], avec_ref[1:2, :], avec_ref[2:3, :],
                                avec_ref[3:4, :])
    d_skip, glu_b = bvec_ref[0:1, :], bvec_ref[1:2, :]

    for r in range(tc // sub):
        r0 = r * sub
        rows = slice(r0, r0 + sub)
        acc = conv_b
        for k0 in range(0, CONF_KERNEL, CONV_GROUP):
            part = None
            for kk in range(k0, min(k0 + CONV_GROUP, CONF_KERNEL)):
                off = 32 - (CONF_KERNEL - 1) + kk
                shift = off % BF16_ROWS
                term = acw_bf[kk:kk + 1, :] * ash_scr[shift, pl.ds(r0 + off - shift, sub), :]
                part = term if part is None else part + term
            acc = acc + part.astype(F32)
        mu = jnp.mean(acc, axis=-1, keepdims=True)
        xc = acc - mu
        ln = xc * lax.rsqrt(jnp.mean(xc * xc, axis=-1, keepdims=True) + NORM_EPS) * ln_g + ln_b
        ya = (_mm(_silu(ln), apw_ref[...]) + pw_b) * gate(GATE_AZ, rows)
        ub = jnp.concatenate([ulo_ref[rows, :], uhi_ref[rows, :]], axis=1)
        ys = jnp.concatenate([ylo_ref[rows, :], yhi_ref[rows, :]], axis=1)
        yb = jax.nn.gelu(ys + d_skip * ub)
        yb = yb * _sigmoid(_mm(yb, bglu_ref[...]) + glu_b) * gate(GATE_BZ, rows)
        cp = cpad_scr[r0:r0 + sub + 8, :]
        cacc = ccw_ref[SC_KERNEL - 1:SC_KERNEL, :] * cp[8:, :]
        for kk in range(SC_KERNEL - 1):
            cacc = cacc + ccw_ref[kk:kk + 1, :] * pltpu.roll(cp, SC_KERNEL - 1 - kk, 0)[8:, :]
        yc = gate(GATE_CB, rows) * cacc
        yd = od_ref[rows, :].astype(F32) * gate(GATE_DZ, rows)
        mixed = jnp.concatenate([ya, yb, yc, yd], axis=1)
        y = x_ref[rows, :] + _mm(mixed, wout_ref[...])
        if final:
            y = y * lax.rsqrt(jnp.mean(y * y, axis=-1, keepdims=True) + NORM_EPS) * fg_ref[...]
        out_ref[rows, :] = y


def _mixout(layer, act3, gate3, ulo, uhi, ylo, yhi, od, x3, acw, avec, apw, bvec, bglu, ccw, wout, fg, tc,
            final):
    b, l, d = x3.shape
    w = BRANCH
    blk = lambda width, col: pl.BlockSpec((None, tc, width), lambda bi, i: (bi, i, col))
    halo = lambda nrow, col: pl.BlockSpec(
        (None, nrow, w), lambda bi, i: (bi, jnp.maximum(i * (tc // nrow) - 1, 0), col))
    const = functools.partial(_layer_spec, layer)
    return pl.pallas_call(
        functools.partial(_mixout_kernel, final=final, sub=256),
        out_shape=jax.ShapeDtypeStruct((b, l, d), F32),
        grid=(b, l // tc),
        in_specs=[blk(w, 0), blk(GATE_COLS * w, 0), halo(32, GATE_A), halo(8, 0),
                  blk(HALF, 0), blk(HALF, 0), blk(HALF, 0), blk(HALF, 0), blk(w, 0), blk(d, 0),
                  const((32, w)), const((8, w)), const((w, w)), const((8, w)), const((w, w)),
                  const((8, w)), const((4 * w, d)), pl.BlockSpec((1, d), lambda bi, i: (0, 0))],
        out_specs=pl.BlockSpec((None, tc, d), lambda bi, i: (bi, i, 0)),
        scratch_shapes=[pltpu.VMEM((BF16_ROWS, tc + 32, w), BF16), pltpu.VMEM((tc + 8, w), F32)],
        compiler_params=pltpu.CompilerParams(
            dimension_semantics=("parallel", "parallel"), vmem_limit_bytes=VMEM_LIMIT_BYTES),
        name="mixout_final" if final else "mixout",
    )(act3, gate3, gate3, act3, ulo, uhi, ylo, yhi, od, x3, acw, avec, apw, bvec, bglu, ccw, wout, fg)


def _pad_rows(a, n):
    return jnp.pad(a, ((0, 0), (0, n - a.shape[1]), (0, 0)))


def _s5_tiles(tab, state_cols):
    dep = tab.shape[0]
    nt, rb = S5_TILES, 2 * S5_GROUP
    t = tab.reshape(dep, S5_Q, nt, rb, 128)
    t = jnp.transpose(t, (0, 2, 1, 3, 4)).reshape(dep, nt, S5_Q * rb, 128)
    per = S5_STATE if state_cols else S5_GROUP
    src = jnp.arange(128)
    dst = jnp.arange(2 * 128)
    spread = ((src[:, None] // per == dst[None, :] // (2 * per))
              & (src[:, None] % per == dst[None, :] % per)).astype(F32)
    row_group = (jnp.arange(S5_Q * rb) % rb) // S5_GROUP
    col_group = (dst % (2 * per)) // per
    mask = (row_group[:, None] == col_group[None, :]).astype(F32)
    return (jnp.einsum("dkrc,cn->dkrn", t, spread, precision=HIGHEST) * mask).astype(BF16)


def _s5_tables(lam_re, lam_im, b_re, b_im, c_re, c_im, log_dt):
    q, g, p, h = S5_Q, S5_GROUPS, S5_STATE, S5_GROUP
    dep = lam_re.shape[0]
    lr = jnp.minimum(lam_re, -1e-4)
    li = lam_im
    dt = jnp.exp(log_dt)[..., None]
    lrdt, lidt = (lr * dt)[:, None], (li * dt)[:, None]

    def power(steps):
        s = steps[None, :, None, None]
        mag = jnp.exp(lrdt * s)
        return (mag * jnp.cos(lidt * s))[:, :, :, None, :], (mag * jnp.sin(lidt * s))[:, :, :, None, :]

    tok = jnp.arange(q, dtype=F32)
    p1r, p1i = power(jnp.ones((1,), F32))
    nr, ni = p1r[:, 0, :, 0] - 1.0, p1i[:, 0, :, 0]
    den = lr * lr + li * li
    fr, fi = (nr * lr + ni * li) / den, (ni * lr - nr * li) / den
    bbr = jnp.transpose(fr[..., None] * b_re - fi[..., None] * b_im, (0, 1, 3, 2))[:, None]
    bbi = jnp.transpose(fr[..., None] * b_im + fi[..., None] * b_re, (0, 1, 3, 2))[:, None]
    cr, ci = c_re[:, None], c_im[:, None]

    rr, ri = power(q - 1.0 - tok)
    wc = jnp.concatenate([rr * bbr - ri * bbi, rr * bbi + ri * bbr], axis=-1)
    qr, qi = power(tok + 1.0)
    mc = jnp.concatenate([cr * qr - ci * qi, -(cr * qi + ci * qr)], axis=-1)
    er, ei = power(tok)
    cpr = jnp.transpose(cr * er - ci * ei, (0, 2, 1, 3, 4)).reshape(dep, g, q * h, p)
    cpi = jnp.transpose(cr * ei + ci * er, (0, 2, 1, 3, 4)).reshape(dep, g, q * h, p)
    kd = (jnp.einsum("dgcp,dgyp->dgcy", bbr[:, 0], cpr, precision=HIGHEST)
          - jnp.einsum("dgcp,dgyp->dgcy", bbi[:, 0], cpi, precision=HIGHEST))
    lane = jnp.arange(q * h)
    tk = jnp.stack([jnp.where(lane >= j * h, jnp.roll(kd, j * h, axis=-1), 0.0) for j in range(q)],
                   axis=1)

    flat = lambda t: t.reshape(dep, q * g * h, t.shape[-1])
    ar, ai = power(jnp.full((1,), float(q), F32))
    a = jnp.concatenate([ar.reshape(dep, 1, g * p), ai.reshape(dep, 1, g * p),
                         jnp.zeros((dep, 6, g * p), F32)], axis=1)
    return _s5_tiles(flat(wc), True), _s5_tiles(flat(tk), False), _s5_tiles(flat(mc), True), a


def _w_in_tail(w_in):
    n = MAIN_BLOCKS * BRANCH
    pad = jnp.zeros(w_in.shape[:-1] + (128 - 2 * DN_HEADS,), w_in.dtype)
    return jnp.concatenate([w_in[..., n + 2 * DN_HEADS:], w_in[..., n:n + 2 * DN_HEADS], pad],
                           axis=-1).astype(BF16)


def kernel(x, norm_g, w_in, a_conv_w, a_conv_b, a_ln_g, a_ln_b, a_pw_w, a_pw_b, s5_lambda_re, s5_lambda_im, s5_b_re, s5_b_im, s5_c_re, s5_c_im, s5_d, s5_log_dt, s5_glu_w, s5_glu_b, c_conv_w, d_conv_w, d_a_log, d_dt_bias, d_norm_g, w_out, final_g):
    bsz, seq, d = x.shape
    depth = w_in.shape[0]
    w = BRANCH
    tm = min(512, seq)
    tc = min(1024, seq)
    dn_tc = min(256, seq)
    s5_rows = min(512, seq // S5_Q)

    w_bf = w_in.astype(BF16)
    w_tail = _w_in_tail(w_in)
    s5_wd, s5_td, s5_vd, s5_a = _s5_tables(s5_lambda_re, s5_lambda_im, s5_b_re, s5_b_im,
                                           s5_c_re, s5_c_im, s5_log_dt)
    acw = _pad_rows(a_conv_w, 32)
    zeros = jnp.zeros_like(a_conv_b)
    avec = jnp.stack([a_conv_b, a_ln_g, a_ln_b, a_pw_b, zeros, zeros, zeros, zeros], axis=1)
    bvec = jnp.stack([s5_d, s5_glu_b, zeros, zeros, zeros, zeros, zeros, zeros], axis=1)
    ccw = _pad_rows(c_conv_w, 8)
    dcw = _pad_rows(d_conv_w, 8)
    alog = jnp.repeat(d_a_log, DN_HEAD_DIM, axis=-1)[:, None, :]
    dtb = jnp.repeat(d_dt_bias, DN_HEAD_DIM, axis=-1)[:, None, :]
    ng = jnp.tile(d_norm_g, (1, DN_HEADS))[:, None, :]
    apw = a_pw_w.astype(BF16)
    bglu = s5_glu_w.astype(BF16)
    wout = w_out.astype(BF16)
    fg = final_g[None, :]

    def seq3(a):
        return a.reshape(bsz, seq, a.shape[-1])

    ng3 = norm_g[:, None, :]
    for l in range(depth):
        act, gate, dn, ulo, uhi = _inproj(l, x.reshape(bsz * seq, d), ng3, w_bf, w_tail, dcw, alog, dtb,
                                    tm, seq)
        ulo, uhi = seq3(ulo), seq3(uhi)
        ylo, yhi = _s5(l, ulo, uhi, s5_wd, s5_td, s5_vd, s5_a, s5_rows)
        od = _dnet(l, seq3(dn), ng, dn_tc)
        x = _mixout(l, seq3(act), seq3(gate), ulo, uhi, ylo, yhi, od, x, acw, avec, apw, bvec, bglu, ccw, wout,
                    fg, tc, l == depth - 1)
    return x
```

```python
import functools

import jax
import jax.numpy as jnp
from jax import lax
from jax.experimental import pallas as pl
from jax.experimental.pallas import tpu as pltpu

F32 = jnp.float32
BF16 = jnp.bfloat16
HIGHEST = lax.Precision.HIGHEST

NORM_EPS = 1e-6
BRANCH = 256
CONF_KERNEL = 31
SC_KERNEL = 3
DN_HEADS = 4
DN_HEAD_DIM = 64
DN_CONV = 4
DN_CHUNK = 64
DN_GROUP = 16
CONV_GROUP = 16
BF16_ROWS = 16
S5_GROUPS = 16
S5_GROUP = 16
S5_STATE = 64
S5_Q = 8
S5_ROW = S5_Q * BRANCH
S5_NSTATE = S5_GROUPS * S5_STATE

COL_A, COL_B, COL_C, COL_QKV, MAIN_BLOCKS = 0, 3, 5, 9, 12
IN_COLS = MAIN_BLOCKS * BRANCH + 2 * DN_HEADS + BRANCH
TAIL_COLS = BRANCH + 128
HALF = BRANCH // 2
ACT_A, ACT_AZ, ACT_BZ, ACT_CB, ACT_CX, ACT_DZ, ACT_COLS = 0, 1, 2, 3, 4, 5, 6
DN_Q, DN_K, DN_V, DN_BETA, DN_G, DN_COLS = 0, 1, 2, 3, 4, 5

VMEM_LIMIT_BYTES = 56 * 1024 * 1024


def _mm(a, b):
    return jnp.dot(a.astype(BF16), b.astype(BF16), preferred_element_type=F32)


def _layer_spec(layer, shape):
    zeros = (0,) * len(shape)
    return pl.BlockSpec((None,) + tuple(shape), lambda *_: (layer,) + zeros)


def _sigmoid(x):
    return 1.0 / (1.0 + jnp.exp(-x))


def _silu(x):
    return x * _sigmoid(x)


def _inproj_kernel(x_ref, g_ref, w_ref, wt_ref, dcw_ref, alog_ref, dtb_ref, act_ref, dn_ref, ulo_ref,
                   uhi_ref, pad_scr, *, steps_per_seq, sub):
    tm = x_ref.shape[0]
    w = BRANCH
    first = pl.program_id(0) % steps_per_seq == 0

    @pl.when(first)
    def _():
        pad_scr[tm:tm + 8, :] = jnp.zeros((8, 3 * w), F32)

    pad_scr[0:8, :] = pad_scr[tm:tm + 8, :]

    row_h = lax.broadcasted_iota(jnp.int32, (w, w), 0) // DN_HEAD_DIM
    col_h = lax.broadcasted_iota(jnp.int32, (w, w), 1) // DN_HEAD_DIM
    ones_bd = jnp.where(row_h == col_h, 1.0, 0.0).astype(BF16)

    def project(hb, col, n):
        return jnp.dot(hb, w_ref[:, col * w:(col + n) * w], preferred_element_type=F32)

    def put(ref, rows, j, val):
        ref[rows, j * w:(j + 1) * w] = val

    def deltanet_inputs(rows, r0):
        xp = pad_scr[r0:r0 + sub + 8, :]
        acc = dcw_ref[DN_CONV - 1:DN_CONV, :] * xp[8:, :]
        for kk in range(DN_CONV - 1):
            acc = acc + dcw_ref[kk:kk + 1, :] * pltpu.roll(xp, DN_CONV - 1 - kk, 0)[8:, :]
        qkv = _silu(acc)
        q = qkv[:, 0:w]
        k = qkv[:, w:2 * w]
        ssq = jnp.dot((q * q).astype(BF16), ones_bd, preferred_element_type=F32)
        ssk = jnp.dot((k * k).astype(BF16), ones_bd, preferred_element_type=F32)
        put(dn_ref, rows, DN_Q, q * lax.rsqrt(ssq + NORM_EPS) * (DN_HEAD_DIM ** -0.5))
        put(dn_ref, rows, DN_K, k * lax.rsqrt(ssk + NORM_EPS))
        put(dn_ref, rows, DN_V, qkv[:, 2 * w:3 * w] * dn_ref[rows, DN_BETA * w:(DN_BETA + 1) * w])

    for r in range(tm // sub):
        r0 = r * sub
        rows = slice(r0, r0 + sub)
        x = x_ref[rows, :]
        h = x * lax.rsqrt(jnp.mean(x * x, axis=-1, keepdims=True) + NORM_EPS) * g_ref[...]
        hb = h.astype(BF16)
        pad_scr[r0 + 8:r0 + 8 + sub, :] = project(hb, COL_QKV, 3)
        p = project(hb, COL_A, 3)
        put(act_ref, rows, ACT_A, p[:, 0:w] * _sigmoid(p[:, w:2 * w]))
        put(act_ref, rows, ACT_AZ, _silu(p[:, 2 * w:3 * w]))
        p = project(hb, COL_B, 2)
        ulo_ref[rows, :] = p[:, 0:HALF]
        uhi_ref[rows, :] = p[:, HALF:w]
        put(act_ref, rows, ACT_BZ, _silu(p[:, w:2 * w]))
        p = project(hb, COL_C, 4)
        put(act_ref, rows, ACT_CB, p[:, 0:w] * _silu(p[:, 3 * w:4 * w]))
        put(act_ref, rows, ACT_CX, p[:, w:2 * w] * p[:, 2 * w:3 * w])
        p = jnp.dot(hb, wt_ref[...], preferred_element_type=F32)
        put(act_ref, rows, ACT_DZ, _silu(p[:, 0:w]))
        p = jnp.concatenate(
            [jnp.broadcast_to(p[:, w + j:w + j + 1], (sub, DN_HEAD_DIM)) for j in range(2 * DN_HEADS)],
            axis=1)
        sp_in = p[:, 0:w] + dtb_ref[...]
        softplus = jnp.maximum(sp_in, 0.0) + jnp.log1p(jnp.exp(-jnp.abs(sp_in)))
        put(dn_ref, rows, DN_G, -jnp.exp(alog_ref[...]) * softplus)
        put(dn_ref, rows, DN_BETA, _sigmoid(p[:, w:2 * w]))
        if r > 0:
            deltanet_inputs(slice(r0 - sub, r0), r0 - sub)
    deltanet_inputs(slice(tm - sub, tm), tm - sub)


def _inproj(layer, x2d, g, w, wt, dcw, alog, dtb, tm, seq):
    t, d = x2d.shape
    const = functools.partial(_layer_spec, layer)
    out = lambda n: pl.BlockSpec((tm, n * BRANCH), lambda i: (i, 0))
    return pl.pallas_call(
        functools.partial(_inproj_kernel, steps_per_seq=seq // tm, sub=min(256, tm)),
        out_shape=(jax.ShapeDtypeStruct((t, ACT_COLS * BRANCH), F32),
                   jax.ShapeDtypeStruct((t, DN_COLS * BRANCH), F32),
                   jax.ShapeDtypeStruct((t, HALF), F32),
                   jax.ShapeDtypeStruct((t, HALF), F32)),
        grid=(t // tm,),
        in_specs=[pl.BlockSpec((tm, d), lambda i: (i, 0)),
                  const((1, d)), const((d, IN_COLS)), const((d, TAIL_COLS)),
                  const((8, 3 * BRANCH)), const((1, BRANCH)), const((1, BRANCH))],
        out_specs=(out(ACT_COLS), out(DN_COLS), pl.BlockSpec((tm, HALF), lambda i: (i, 0)),
                   pl.BlockSpec((tm, HALF), lambda i: (i, 0))),
        scratch_shapes=[pltpu.VMEM((tm + 8, 3 * BRANCH), F32)],
        compiler_params=pltpu.CompilerParams(
            dimension_semantics=("arbitrary",), vmem_limit_bytes=VMEM_LIMIT_BYTES),
        name="inproj",
    )(x2d, g, w, wt, dcw, alog, dtb)


S5_TILES = S5_GROUPS // 2


def _block_transpose(pieces, block):
    lane = lax.broadcasted_iota(jnp.int32, pieces[0].shape, 1)
    n = len(pieces)
    s = n // 2
    while s:
        width = s * block
        lower = lane % (2 * width) < width
        out = list(pieces)
        for j in range(n):
            if j & s == 0:
                a, b = pieces[j], pieces[j + s]
                out[j] = jnp.where(lower, a, pltpu.roll(b, width, 1))
                out[j + s] = jnp.where(lower, pltpu.roll(a, 128 - width, 1), b)
        pieces = out
        s //= 2
    return pieces


def _s5_kernel(ulo_ref, uhi_ref, wd_ref, td_ref, vdt_ref, a_ref, ylo_ref, yhi_ref,
               carry_scr, z_scr, xp_scr):
    rows = ulo_ref.shape[0] // S5_Q
    n = S5_NSTATE
    tiles_per_half = S5_TILES // 2
    lanes = 2 * S5_GROUP

    @pl.when(pl.program_id(1) == 0)
    def _():
        carry_scr[...] = jnp.zeros_like(carry_scr)

    ub = []
    for half in (ulo_ref, uhi_ref):
        tok = [half[pl.ds(j, rows, stride=S5_Q), :] for j in range(S5_Q)]
        first = _block_transpose(tok[0:4], lanes)
        second = _block_transpose(tok[4:8], lanes)
        ub += [jnp.concatenate([first[k], second[k]], axis=1).astype(BF16)
               for k in range(tiles_per_half)]
    for k in range(S5_TILES):
        z = jnp.dot(ub[k], wd_ref[k], preferred_element_type=F32)
        z_scr[:, k * 128:(k + 1) * 128] = z[:, 0:128]
        z_scr[:, n + k * 128:n + (k + 1) * 128] = z[:, 128:256]

    ar = a_ref[0:1, :]
    ai = a_ref[1:2, :]

    def body(c, carry):
        xr, xi = carry
        xp_scr[pl.ds(c, 1), 0:n] = xr
        xp_scr[pl.ds(c, 1), n:2 * n] = xi
        zr = z_scr[pl.ds(c, 1), 0:n]
        zi = z_scr[pl.ds(c, 1), n:2 * n]
        return ar * xr - ai * xi + zr, ar * xi + ai * xr + zi

    xr, xi = lax.fori_loop(0, rows, body, (carry_scr[0:1, 0:n], carry_scr[0:1, n:2 * n]),
                           unroll=4)
    carry_scr[0:1, 0:n] = xr
    carry_scr[0:1, n:2 * n] = xi

    ys = []
    for k in range(S5_TILES):
        xp = jnp.concatenate([xp_scr[:, k * 128:(k + 1) * 128],
                              xp_scr[:, n + k * 128:n + (k + 1) * 128]], axis=1).astype(BF16)
        ys.append(jnp.dot(ub[k], td_ref[k], preferred_element_type=F32)
                  + lax.dot_general(xp, vdt_ref[k], (((1,), (1,)), ((), ())),
                                    preferred_element_type=F32))
    for h, out_ref in enumerate((ylo_ref, yhi_ref)):
        tiles = ys[h * tiles_per_half:(h + 1) * tiles_per_half]
        for m in range(2):
            tok = _block_transpose([y[:, m * 128:(m + 1) * 128] for y in tiles], lanes)
            for b in range(4):
                out_ref[pl.ds(4 * m + b, rows, stride=S5_Q), :] = tok[b]


def _s5(layer, ulo, uhi, wd, td, vd, a, rows):
    b, l, _ = ulo.shape
    const = functools.partial(_layer_spec, layer)
    half = lambda: pl.BlockSpec((None, rows * S5_Q, HALF), lambda bi, j: (bi, j, 0))
    tiles = (S5_TILES, BRANCH, BRANCH)
    return pl.pallas_call(
        _s5_kernel,
        out_shape=(jax.ShapeDtypeStruct((b, l, HALF), F32), jax.ShapeDtypeStruct((b, l, HALF), F32)),
        grid=(b, l // (rows * S5_Q)),
        in_specs=[half(), half(), const(tiles), const(tiles), const(tiles), const((8, S5_NSTATE))],
        out_specs=(half(), half()),
        scratch_shapes=[pltpu.VMEM((8, 2 * S5_NSTATE), F32),
                        pltpu.VMEM((rows, 2 * S5_NSTATE), F32),
                        pltpu.VMEM((rows, 2 * S5_NSTATE), F32)],
        compiler_params=pltpu.CompilerParams(
            dimension_semantics=("parallel", "arbitrary"), vmem_limit_bytes=VMEM_LIMIT_BYTES),
        name="s5",
    )(ulo, uhi, wd, td, vd, a)


def _tile_heads(x):
    return jnp.concatenate([x] * DN_HEADS, axis=0)


def _dnet_kernel(dn_ref, ng_ref, o_ref, s_scr, o_scr, u_scr, wq_scr, al_scr, egl_scr):
    nb, tc, _ = dn_ref.shape
    c64 = DN_CHUNK
    w = DN_HEADS * DN_HEAD_DIM
    nck = tc // c64
    gb = DN_GROUP // nck

    @pl.when(pl.program_id(0) == 0)
    def _():
        s_scr[...] = jnp.zeros_like(s_scr)

    row_h = lax.broadcasted_iota(jnp.int32, (w, w), 0) // DN_HEAD_DIM
    col_h = lax.broadcasted_iota(jnp.int32, (w, w), 1) // DN_HEAD_DIM
    bd = row_h == col_h
    bd_bf = jnp.where(bd, 1.0, 0.0).astype(BF16)

    def mmb(a, b_bf):
        return jnp.dot(a.astype(BF16), b_bf, preferred_element_type=F32)

    def mmb_nt(a, b_bf):
        return lax.dot_general(a.astype(BF16), b_bf, (((1,), (1,)), ((), ())),
                               preferred_element_type=F32)

    ri = lax.broadcasted_iota(jnp.int32, (c64, w), 0)
    cj = lax.broadcasted_iota(jnp.int32, (c64, w), 1) % DN_HEAD_DIM
    causal = ri >= cj
    strict = ri > cj
    diag = ri == cj
    eye_all = jnp.where(diag, 1.0, 0.0).astype(F32)
    lt_bf = jnp.where(lax.broadcasted_iota(jnp.int32, (c64, c64), 0)
                      >= lax.broadcasted_iota(jnp.int32, (c64, c64), 1), 1.0, 0.0).astype(BF16)

    def blockdiag(x):
        return _tile_heads(x.astype(BF16)) * bd_bf

    grp = range(DN_GROUP)

    def prepare(gi, carry):
        def col(i, j):
            return dn_ref[gi * gb + i // nck, (i % nck) * c64:(i % nck + 1) * c64, j * w:(j + 1) * w]

        rows = [pl.ds(pl.multiple_of((gi * gb + i // nck) * tc + (i % nck) * c64, c64), c64)
                for i in grp]
        qc = [col(i, DN_Q) for i in grp]
        kc = [col(i, DN_K) for i in grp]
        bc = [col(i, DN_BETA) for i in grp]
        g = [col(i, DN_G) for i in grp]
        g_hi = [x.astype(BF16) for x in g]
        g_lo = [(g[i] - g_hi[i].astype(F32)).astype(BF16) for i in grp]
        gc = [jnp.dot(lt_bf, g_hi[i], preferred_element_type=F32)
              + jnp.dot(lt_bf, g_lo[i], preferred_element_type=F32) for i in grp]
        egc = [jnp.exp(x) for x in gc]
        gl = [x[c64 - 1:c64, :] for x in gc]
        kb = [kc[i] * bc[i] for i in grp]
        grow = [jnp.sum(jnp.where(diag, x, 0.0), axis=0, keepdims=True) for x in gc]
        decay = [jnp.exp(jnp.where(causal, gc[i] - grow[i], -jnp.inf)) for i in grp]
        k_bd = [blockdiag(x) for x in kc]
        kq = [mmb_nt(jnp.concatenate([kb[i], qc[i]], axis=0), k_bd[i]) for i in grp]
        for i in grp:
            al_scr[gi * DN_GROUP + i, 0:c64, :] = kq[i][c64:2 * c64, :] * decay[i]
        x = [jnp.where(strict, -kq[i][0:c64, :] * decay[i], 0.0) for i in grp]
        p = [eye_all + x[i] for i in grp]
        x = [mmb(x[i], blockdiag(x[i])) for i in grp]
        for _ in range(4):
            xp = [mmb(jnp.concatenate([x[i], p[i]], axis=0), blockdiag(x[i])) for i in grp]
            x = [xp[i][0:c64, :] for i in grp]
            p = [p[i] + xp[i][c64:2 * c64, :] for i in grp]
        p = [p[i] + mmb(p[i], blockdiag(x[i])) for i in grp]
        for i in grp:
            ci = gi * DN_GROUP + i
            u_scr[rows[i], :] = mmb(p[i], blockdiag(col(i, DN_V)))
            wq_scr[ci, 0:c64, :] = mmb(p[i], blockdiag(kb[i] * egc[i]))
            wq_scr[ci, c64:2 * c64, :] = qc[i] * egc[i]
            kd_t = (kc[i] * jnp.exp(gl[i] - gc[i])).T
            al_scr[ci, c64:2 * c64, :] = jnp.concatenate(
                [kd_t[h * c64:(h + 1) * c64, :] for h in range(DN_HEADS)], axis=1)
            egl_scr[ci] = jnp.broadcast_to(jnp.exp(gl[i]), (8, w))
        return carry

    lax.fori_loop(0, nb // gb, prepare, 0)

    bs = range(nb)

    def recur(c, carry):
        rows = [pl.ds(pl.multiple_of(b * tc + c * c64, c64), c64) for b in bs]
        s = [s_scr[b] for b in bs]
        ws_qs = [mmb(wq_scr[b * nck + c], blockdiag(s[b])) for b in bs]
        v_new = [u_scr[rows[b], :] - ws_qs[b][0:c64, :] for b in bs]
        od = [mmb(al_scr[b * nck + c], blockdiag(v_new[b])) for b in bs]
        for b in bs:
            o_scr[rows[b], :] = ws_qs[b][c64:2 * c64, :] + od[b][0:c64, :]
        for b in bs:
            s_scr[b] = s[b] * egl_scr[b * nck + c][0:1, :] + od[b][c64:2 * c64, :]
        return carry

    lax.fori_loop(0, nck, recur, 0)

    o = o_scr[...]
    ms = mmb(o * o, bd_bf) * (1.0 / DN_HEAD_DIM)
    o_ref[...] = (o * lax.rsqrt(ms + NORM_EPS) * ng_ref[...]).reshape(nb, tc, w)


def _dnet(layer, dn3, ng, tc):
    b, l, _ = dn3.shape
    w = BRANCH
    nchunk = b * tc // DN_CHUNK
    assert DN_GROUP % (tc // DN_CHUNK) == 0 and nchunk % DN_GROUP == 0
    act = lambda: pltpu.VMEM((b * tc, w), F32)
    return pl.pallas_call(
        _dnet_kernel,
        out_shape=jax.ShapeDtypeStruct((b, l, w), F32),
        grid=(l // tc,),
        in_specs=[pl.BlockSpec((b, tc, DN_COLS * w), lambda i: (0, i, 0)),
                  _layer_spec(layer, (1, w))],
        out_specs=pl.BlockSpec((b, tc, w), lambda i: (0, i, 0)),
        scratch_shapes=[pltpu.VMEM((b, DN_CHUNK, w), F32),
                        act(), act(),
                        pltpu.VMEM((nchunk, 2 * DN_CHUNK, w), F32),
                        pltpu.VMEM((nchunk, 2 * DN_CHUNK, w), F32),
                        pltpu.VMEM((nchunk, 8, w), F32)],
        compiler_params=pltpu.CompilerParams(
            dimension_semantics=("arbitrary",), vmem_limit_bytes=VMEM_LIMIT_BYTES),
        name="dnet",
    )(dn3, ng)


def _mixout_kernel(act_ref, ahalo_ref, chalo_ref, ulo_ref, uhi_ref, ylo_ref, yhi_ref, od_ref, x_ref,
                   acw_ref, avec_ref, apw_ref, bvec_ref, bglu_ref, ccw_ref, wout_ref, fg_ref,
                   out_ref, ash_scr, cpad_scr, *, final, sub):
    tc = x_ref.shape[0]
    w = BRANCH
    first = pl.program_id(1) == 0

    def act(j, rows=slice(None)):
        return act_ref[rows, j * w:(j + 1) * w]

    apad = jnp.concatenate([jnp.where(first, 0.0, ahalo_ref[...]), act(ACT_A)], axis=0)
    for r in range(8):
        shifted = apad if r == 0 else pltpu.roll(apad, tc + 32 - r, 0)
        ash_scr[r] = shifted.astype(BF16)
        ash_scr[r + 8, 0:tc + 16, :] = shifted[8:tc + 24, :].astype(BF16)
    cpad_scr[0:8, :] = jnp.where(first, 0.0, chalo_ref[...])
    cpad_scr[8:8 + tc, :] = act(ACT_CX)

    acw_bf = acw_ref[...].astype(BF16)
    conv_b, ln_g, ln_b, pw_b = (avec_ref[0:1, :], avec_ref[1:2, :], avec_ref[2:3, :],
                                avec_ref[3:4, :])
    d_skip, glu_b = bvec_ref[0:1, :], bvec_ref[1:2, :]

    for r in range(tc // sub):
        r0 = r * sub
        rows = slice(r0, r0 + sub)
        acc = conv_b
        for k0 in range(0, CONF_KERNEL, CONV_GROUP):
            part = None
            for kk in range(k0, min(k0 + CONV_GROUP, CONF_KERNEL)):
                off = 32 - (CONF_KERNEL - 1) + kk
                shift = off % BF16_ROWS
                term = acw_bf[kk:kk + 1, :] * ash_scr[shift, pl.ds(r0 + off - shift, sub), :]
                part = term if part is None else part + term
            acc = acc + part.astype(F32)
        mu = jnp.mean(acc, axis=-1, keepdims=True)
        xc = acc - mu
        ln = xc * lax.rsqrt(jnp.mean(xc * xc, axis=-1, keepdims=True) + NORM_EPS) * ln_g + ln_b
        ya = (_mm(_silu(ln), apw_ref[...]) + pw_b) * act(ACT_AZ, rows)
        ub = jnp.concatenate([ulo_ref[rows, :], uhi_ref[rows, :]], axis=1)
        ys = jnp.concatenate([ylo_ref[rows, :], yhi_ref[rows, :]], axis=1)
        yb = jax.nn.gelu(ys + d_skip * ub)
        yb = yb * _sigmoid(_mm(yb, bglu_ref[...]) + glu_b) * act(ACT_BZ, rows)
        cp = cpad_scr[r0:r0 + sub + 8, :]
        cacc = ccw_ref[SC_KERNEL - 1:SC_KERNEL, :] * cp[8:, :]
        for kk in range(SC_KERNEL - 1):
            cacc = cacc + ccw_ref[kk:kk + 1, :] * pltpu.roll(cp, SC_KERNEL - 1 - kk, 0)[8:, :]
        yc = act(ACT_CB, rows) * cacc
        yd = od_ref[rows, :] * act(ACT_DZ, rows)
        mixed = jnp.concatenate([ya, yb, yc, yd], axis=1)
        y = x_ref[rows, :] + _mm(mixed, wout_ref[...])
        if final:
            y = y * lax.rsqrt(jnp.mean(y * y, axis=-1, keepdims=True) + NORM_EPS) * fg_ref[...]
        out_ref[rows, :] = y


def _mixout(layer, act3, ulo, uhi, ylo, yhi, od, x3, acw, avec, apw, bvec, bglu, ccw, wout, fg, tc,
            final):
    b, l, d = x3.shape
    w = BRANCH
    blk = lambda width, col: pl.BlockSpec((None, tc, width), lambda bi, i: (bi, i, col))
    halo = lambda nrow, col: pl.BlockSpec(
        (None, nrow, w), lambda bi, i: (bi, jnp.maximum(i * (tc // nrow) - 1, 0), col))
    const = functools.partial(_layer_spec, layer)
    return pl.pallas_call(
        functools.partial(_mixout_kernel, final=final, sub=256),
        out_shape=jax.ShapeDtypeStruct((b, l, d), F32),
        grid=(b, l // tc),
        in_specs=[blk(ACT_COLS * w, 0), halo(32, ACT_A), halo(8, ACT_CX),
                  blk(HALF, 0), blk(HALF, 0), blk(HALF, 0), blk(HALF, 0), blk(w, 0), blk(d, 0),
                  const((32, w)), const((8, w)), const((w, w)), const((8, w)), const((w, w)),
                  const((8, w)), const((4 * w, d)), pl.BlockSpec((1, d), lambda bi, i: (0, 0))],
        out_specs=pl.BlockSpec((None, tc, d), lambda bi, i: (bi, i, 0)),
        scratch_shapes=[pltpu.VMEM((BF16_ROWS, tc + 32, w), BF16), pltpu.VMEM((tc + 8, w), F32)],
        compiler_params=pltpu.CompilerParams(
            dimension_semantics=("parallel", "parallel"), vmem_limit_bytes=VMEM_LIMIT_BYTES),
        name="mixout_final" if final else "mixout",
    )(act3, act3, act3, ulo, uhi, ylo, yhi, od, x3, acw, avec, apw, bvec, bglu, ccw, wout, fg)


def _pad_rows(a, n):
    return jnp.pad(a, ((0, 0), (0, n - a.shape[1]), (0, 0)))


def _s5_tiles(tab, state_cols):
    dep = tab.shape[0]
    nt, rb = S5_TILES, 2 * S5_GROUP
    t = tab.reshape(dep, S5_Q, nt, rb, 128)
    t = jnp.transpose(t, (0, 2, 1, 3, 4)).reshape(dep, nt, S5_Q * rb, 128)
    per = S5_STATE if state_cols else S5_GROUP
    src = jnp.arange(128)
    dst = jnp.arange(2 * 128)
    spread = ((src[:, None] // per == dst[None, :] // (2 * per))
              & (src[:, None] % per == dst[None, :] % per)).astype(F32)
    row_group = (jnp.arange(S5_Q * rb) % rb) // S5_GROUP
    col_group = (dst % (2 * per)) // per
    mask = (row_group[:, None] == col_group[None, :]).astype(F32)
    return (jnp.einsum("dkrc,cn->dkrn", t, spread, precision=HIGHEST) * mask).astype(BF16)


def _s5_tables(lam_re, lam_im, b_re, b_im, c_re, c_im, log_dt):
    q, g, p, h = S5_Q, S5_GROUPS, S5_STATE, S5_GROUP
    dep = lam_re.shape[0]
    lr = jnp.minimum(lam_re, -1e-4)
    li = lam_im
    dt = jnp.exp(log_dt)[..., None]
    lrdt, lidt = (lr * dt)[:, None], (li * dt)[:, None]

    def power(steps):
        s = steps[None, :, None, None]
        mag = jnp.exp(lrdt * s)
        return (mag * jnp.cos(lidt * s))[:, :, :, None, :], (mag * jnp.sin(lidt * s))[:, :, :, None, :]

    tok = jnp.arange(q, dtype=F32)
    p1r, p1i = power(jnp.ones((1,), F32))
    nr, ni = p1r[:, 0, :, 0] - 1.0, p1i[:, 0, :, 0]
    den = lr * lr + li * li
    fr, fi = (nr * lr + ni * li) / den, (ni * lr - nr * li) / den
    bbr = jnp.transpose(fr[..., None] * b_re - fi[..., None] * b_im, (0, 1, 3, 2))[:, None]
    bbi = jnp.transpose(fr[..., None] * b_im + fi[..., None] * b_re, (0, 1, 3, 2))[:, None]
    cr, ci = c_re[:, None], c_im[:, None]

    rr, ri = power(q - 1.0 - tok)
    wc = jnp.concatenate([rr * bbr - ri * bbi, rr * bbi + ri * bbr], axis=-1)
    qr, qi = power(tok + 1.0)
    mc = jnp.concatenate([cr * qr - ci * qi, -(cr * qi + ci * qr)], axis=-1)
    er, ei = power(tok)
    cpr = jnp.transpose(cr * er - ci * ei, (0, 2, 1, 3, 4)).reshape(dep, g, q * h, p)
    cpi = jnp.transpose(cr * ei + ci * er, (0, 2, 1, 3, 4)).reshape(dep, g, q * h, p)
    kd = (jnp.einsum("dgcp,dgyp->dgcy", bbr[:, 0], cpr, precision=HIGHEST)
          - jnp.einsum("dgcp,dgyp->dgcy", bbi[:, 0], cpi, precision=HIGHEST))
    lane = jnp.arange(q * h)
    tk = jnp.stack([jnp.where(lane >= j * h, jnp.roll(kd, j * h, axis=-1), 0.0) for j in range(q)],
                   axis=1)

    flat = lambda t: t.reshape(dep, q * g * h, t.shape[-1])
    ar, ai = power(jnp.full((1,), float(q), F32))
    a = jnp.concatenate([ar.reshape(dep, 1, g * p), ai.reshape(dep, 1, g * p),
                         jnp.zeros((dep, 6, g * p), F32)], axis=1)
    return _s5_tiles(flat(wc), True), _s5_tiles(flat(tk), False), _s5_tiles(flat(mc), True), a


def _w_in_tail(w_in):
    n = MAIN_BLOCKS * BRANCH
    pad = jnp.zeros(w_in.shape[:-1] + (128 - 2 * DN_HEADS,), w_in.dtype)
    return jnp.concatenate([w_in[..., n + 2 * DN_HEADS:], w_in[..., n:n + 2 * DN_HEADS], pad],
                           axis=-1).astype(BF16)


def kernel(x, norm_g, w_in, a_conv_w, a_conv_b, a_ln_g, a_ln_b, a_pw_w, a_pw_b, s5_lambda_re, s5_lambda_im, s5_b_re, s5_b_im, s5_c_re, s5_c_im, s5_d, s5_log_dt, s5_glu_w, s5_glu_b, c_conv_w, d_conv_w, d_a_log, d_dt_bias, d_norm_g, w_out, final_g):
    bsz, seq, d = x.shape
    depth = w_in.shape[0]
    w = BRANCH
    tm = min(512, seq)
    tc = min(1024, seq)
    dn_tc = min(256, seq)
    s5_rows = min(512, seq // S5_Q)

    w_bf = w_in.astype(BF16)
    w_tail = _w_in_tail(w_in)
    s5_wd, s5_td, s5_vd, s5_a = _s5_tables(s5_lambda_re, s5_lambda_im, s5_b_re, s5_b_im,
                                           s5_c_re, s5_c_im, s5_log_dt)
    acw = _pad_rows(a_conv_w, 32)
    zeros = jnp.zeros_like(a_conv_b)
    avec = jnp.stack([a_conv_b, a_ln_g, a_ln_b, a_pw_b, zeros, zeros, zeros, zeros], axis=1)
    bvec = jnp.stack([s5_d, s5_glu_b, zeros, zeros, zeros, zeros, zeros, zeros], axis=1)
    ccw = _pad_rows(c_conv_w, 8)
    dcw = _pad_rows(d_conv_w, 8)
    alog = jnp.repeat(d_a_log, DN_HEAD_DIM, axis=-1)[:, None, :]
    dtb = jnp.repeat(d_dt_bias, DN_HEAD_DIM, axis=-1)[:, None, :]
    ng = jnp.tile(d_norm_g, (1, DN_HEADS))[:, None, :]
    apw = a_pw_w.astype(BF16)
    bglu = s5_glu_w.astype(BF16)
    wout = w_out.astype(BF16)
    fg = final_g[None, :]

    def seq3(a):
        return a.reshape(bsz, seq, a.shape[-1])

    ng3 = norm_g[:, None, :]
    for l in range(depth):
        act, dn, ulo, uhi = _inproj(l, x.reshape(bsz * seq, d), ng3, w_bf, w_tail, dcw, alog, dtb,
                                    tm, seq)
        ulo, uhi = seq3(ulo), seq3(uhi)
        ylo, yhi = _s5(l, ulo, uhi, s5_wd, s5_td, s5_vd, s5_a, s5_rows)
        od = _dnet(l, seq3(dn), ng, dn_tc)
        x = _mixout(l, seq3(act), ulo, uhi, ylo, yhi, od, x, acw, avec, apw, bvec, bglu, ccw, wout,
                    fg, tc, l == depth - 1)
    return x
```
